```python
import math
import jax, jax.numpy as jnp
from jax import lax
import numpy as np

D_MODEL = 1024
BATCH = 8
SEQ = 2048
DEPTH = 2
DEC_BATCH = 128
DEC_SEQ = 1
PAST_LEN = 16384
PAGE_SIZE = 128

D_MIX = D_MODEL
W_A = D_MIX // 4
G_A = 4
CONV_A = 31
W_B = D_MIX // 2
H_B = 4
DH_B = W_B // H_B
CONV_B = 4
MLSTM_CHUNK = 64
W_C = D_MIX // 4
H_C = 4
DH_C = W_C // H_C
CONV_C = 4
RG_C = 8.0
D_IN = 2 * W_A + 2 * W_B + 2 * W_C
SPLITS = [W_A, 2 * W_A, 2 * W_A + W_B, 2 * W_A + 2 * W_B, 2 * W_A + 2 * W_B + W_C]
D_FF = 11 * D_MODEL // 4
N_EXPERTS = 8
TOP_K = 2
D_FF_EXPERT = D_FF // 2
N_DENSE = (DEPTH + 1) // 2
N_MOE = DEPTH // 2
NORM_EPS = 1e-6
LN_EPS = 1e-5

kernel_name = 'hymba_conformer_mlstm_rglru_moe_step'


def rmsnorm(x, g):
    xf = x.astype(jnp.float32)
    y = xf * lax.rsqrt(jnp.mean(xf * xf, axis=-1, keepdims=True) + NORM_EPS)
    return (y * g.astype(jnp.float32)).astype(x.dtype)


def group_layernorm(x, g, b, groups):
    shp = x.shape
    xf = x.astype(jnp.float32).reshape(shp[:-1] + (groups, shp[-1] // groups))
    mu = jnp.mean(xf, axis=-1, keepdims=True)
    var = jnp.mean(jnp.square(xf - mu), axis=-1, keepdims=True)
    y = ((xf - mu) * lax.rsqrt(var + LN_EPS)).reshape(shp)
    return (y * g.astype(jnp.float32) + b.astype(jnp.float32)).astype(x.dtype)


def head_layernorm(h, g):
    mu = jnp.mean(h, axis=-1, keepdims=True)
    var = jnp.mean(jnp.square(h - mu), axis=-1, keepdims=True)
    return (h - mu) * lax.rsqrt(var + LN_EPS) * g.astype(jnp.float32).reshape(h.shape[-2:])


def causal_dwconv(x, buf, w, b):
    xp = jnp.concatenate([buf.astype(x.dtype), x], axis=1)
    k = w.shape[0]
    y = lax.conv_general_dilated(xp, w[:, None, :].astype(x.dtype), window_strides=(1,), padding='VALID',
                                 dimension_numbers=('NWC', 'WIO', 'NWC'), feature_group_count=x.shape[-1])
    return y + b, xp[:, xp.shape[1] - (k - 1):]


def headwise(x, w):
    bn, l, _ = x.shape
    return jnp.einsum('blhi,hij->blhj', x.reshape(bn, l, w.shape[0], w.shape[1]), w)


def _mlstm_chunk_step(carry, chunk):
    C, n, m = carry
    q, k, v, ig, lf = chunk
    L = q.shape[1]
    b = jnp.cumsum(lf, axis=1)
    causal = jnp.tril(jnp.ones((L, L), dtype=bool))
    dmat = b[:, :, None, :] - b[:, None, :, :] + ig[:, None, :, :]
    dmat = jnp.where(causal[None, :, :, None], dmat, -jnp.inf)
    inter = b + m[:, None, :]
    m_t = jnp.maximum(inter, jnp.max(dmat, axis=2))
    s = jnp.einsum('bthd,bshd->btsh', q, k) * jnp.exp(dmat - m_t[:, :, None, :])
    w_inter = jnp.exp(inter - m_t)
    num = jnp.einsum('btsh,bshd->bthd', s, v) + w_inter[..., None] * jnp.einsum('bthk,bhkv->bthv', q, C)
    den = jnp.sum(s, axis=2) + w_inter * jnp.einsum('bthk,bhk->bth', q, n)
    h = num / jnp.maximum(jnp.abs(den), jnp.exp(-m_t))[..., None]
    m_new = m_t[:, -1]
    g = jnp.exp(b[:, -1:, :] - b + ig - m_new[:, None, :])
    decay = jnp.exp(b[:, -1] + m - m_new)
    C_new = decay[..., None, None] * C + jnp.einsum('bsh,bshk,bshv->bhkv', g, k, v)
    n_new = decay[..., None] * n + jnp.einsum('bsh,bshk->bhk', g, k)
    return (C_new, n_new, m_new), h


def mlstm_chunkwise(q, k, v, ig, lf, C, n, m):
    bn, L = q.shape[:2]
    chunk = MLSTM_CHUNK if L % MLSTM_CHUNK == 0 else L
    nc = L // chunk

    def to_chunks(t):
        return jnp.moveaxis(t.reshape((bn, nc, chunk) + t.shape[2:]), 1, 0)

    (C, n, m), hs = lax.scan(_mlstm_chunk_step, (C, n, m),
                             (to_chunks(q), to_chunks(k), to_chunks(v), to_chunks(ig), to_chunks(lf)))
    h = jnp.moveaxis(hs, 0, 1).reshape((bn, L) + hs.shape[3:])
    return h, C, n, m


def rglru(x, h0, w_ra, b_ra, w_ix, b_ix, lam):
    bn, L, _ = x.shape
    r = jax.nn.sigmoid((headwise(x, w_ra).reshape(bn, L, W_C) + b_ra).astype(jnp.float32))
    i = jax.nn.sigmoid((headwise(x, w_ix).reshape(bn, L, W_C) + b_ix).astype(jnp.float32))
    log_a = RG_C * r * jax.nn.log_sigmoid(lam.astype(jnp.float32))
    a = jnp.exp(log_a)
    u = jnp.sqrt(-jnp.expm1(2.0 * log_a)) * i * x.astype(jnp.float32)

    def combine(e1, e2):
        a1, u1 = e1
        a2, u2 = e2
        return a1 * a2, a2 * u1 + u2

    a_cum, u_cum = lax.associative_scan(combine, (a, u), axis=1)
    h = a_cum * h0[:, None, :] + u_cum
    return h, h[:, -1]


def token_mixers(h, st, p, l):
    buf_a, buf_b, C0, n0, m0, buf_c, hc0 = st
    bn, L, _ = h.shape
    f32 = jnp.float32
    u = h @ p['w_in'][l]
    xa, ga, xb, zb, xc, gc = jnp.split(u, SPLITS, axis=-1)
    a = xa * jax.nn.sigmoid(ga)
    a, new_buf_a = causal_dwconv(a, buf_a, p['conv_a_w'][l], p['conv_a_b'][l])
    a_out = jax.nn.silu(group_layernorm(a, p['ln_a_g'][l], p['ln_a_b'][l], G_A))
    cb, new_buf_b = causal_dwconv(xb, buf_b, p['conv_b_w'][l], p['conv_b_b'][l])
    cb = jax.nn.silu(cb)
    q = headwise(cb, p['w_q'][l])
    k = headwise(cb, p['w_k'][l]) * (DH_B ** -0.5)
    v = headwise(xb, p['w_v'][l])
    qkv = jnp.concatenate([q, k, v], axis=-1).reshape(bn, L, 3 * W_B)
    ig = (qkv @ p['w_ig'][l] + p['b_ig'][l]).astype(f32)
    lf = jax.nn.log_sigmoid((qkv @ p['w_fg'][l] + p['b_fg'][l]).astype(f32))
    hb, C1, n1, m1 = mlstm_chunkwise(q.astype(f32), k.astype(f32), v.astype(f32), ig, lf,
                                     C0.astype(f32), n0.astype(f32), m0.astype(f32))
    hb = head_layernorm(hb, p['gn_b_g'][l]).reshape(bn, L, W_B).astype(h.dtype)
    b_out = jax.nn.sigmoid(zb) * (hb + p['skip_b'][l] * cb)
    xcv, new_buf_c = causal_dwconv(xc, buf_c, p['conv_c_w'][l], p['conv_c_b'][l])
    hc, hc1 = rglru(xcv, hc0.astype(f32), p['w_ra'][l], p['b_ra'][l], p['w_ix'][l], p['b_ix'][l], p['lam'][l])
    c_out = hc.astype(h.dtype) * jax.nn.gelu(gc)
    y = jnp.concatenate([a_out, b_out, c_out], axis=-1) @ p['w_out'][l]
    return y, (new_buf_a, new_buf_b, C1, n1, m1, new_buf_c, hc1)


def dense_ffn(h, p, j):
    return (jax.nn.silu(h @ p['w_gate'][j]) * (h @ p['w_up'][j])) @ p['w_down'][j]


def moe_ffn(h, p, j):
    logits = (h @ p['w_router'][j] + p['b_router'][j]).astype(jnp.float32)
    top_v, top_i = lax.top_k(logits, TOP_K)
    gates = jax.nn.softmax(top_v, axis=-1)
    combine = jnp.einsum('blk,blke->ble', gates,
                         jax.nn.one_hot(top_i, N_EXPERTS, dtype=jnp.float32)).astype(h.dtype)
    out = jnp.zeros_like(h)
    for e in range(N_EXPERTS):
        act = jax.nn.silu(h @ p['we_gate'][j, e]) * (h @ p['we_up'][j, e])
        out = out + combine[..., e:e + 1] * (act @ p['we_down'][j, e])
    return out


def trunk(x, states, p):
    new = []
    for l in range(DEPTH):
        st = tuple(s[l] for s in states)
        y, st_new = token_mixers(rmsnorm(x, p['norm_mix_g'][l]), st, p, l)
        x = x + y.astype(x.dtype)
        hn = rmsnorm(x, p['norm_ffn_g'][l])
        f = dense_ffn(hn, p, l // 2) if l % 2 == 0 else moe_ffn(hn, p, l // 2)
        x = x + f.astype(x.dtype)
        new.append(st_new)
    y = rmsnorm(x, p['norm_final_g'])
    new_states = tuple(jnp.stack([s[i] for s in new]).astype(x.dtype) for i in range(7))
    return y, new_states


def setup_inputs(seed: int = 0) -> dict:
    key = jax.random.key(seed)
    ks = iter(jax.random.split(key, 64))
    nrm = lambda shape, s: jax.random.normal(next(ks), shape, jnp.float32) * s
    gain = lambda shape: 1.0 + nrm(shape, 0.02)
    a_init = jax.random.uniform(next(ks), (DEPTH, W_C), jnp.float32, 0.9, 0.999)
    return {
        'x_prompt': nrm((BATCH, SEQ, D_MODEL), 1.0),
        'x_sample': nrm((DEC_BATCH, DEC_SEQ, D_MODEL), 1.0),
        'state_conv_a': nrm((DEPTH, DEC_BATCH, CONV_A - 1, W_A), 0.5),
        'state_conv_b': nrm((DEPTH, DEC_BATCH, CONV_B - 1, W_B), 0.5),
        'state_mlstm_C': nrm((DEPTH, DEC_BATCH, H_B, DH_B, DH_B), 0.05),
        'state_mlstm_n': nrm((DEPTH, DEC_BATCH, H_B, DH_B), 0.5),
        'state_mlstm_m': nrm((DEPTH, DEC_BATCH, H_B), 1.0),
        'state_conv_c': nrm((DEPTH, DEC_BATCH, CONV_C - 1, W_C), 0.5),
        'state_rglru_h': nrm((DEPTH, DEC_BATCH, W_C), 0.5),
        'norm_mix_g': gain((DEPTH, D_MODEL)),
        'w_in': nrm((DEPTH, D_MODEL, D_IN), D_MODEL ** -0.5),
        'conv_a_w': nrm((DEPTH, CONV_A, W_A), CONV_A ** -0.5),
        'conv_a_b': nrm((DEPTH, W_A), 0.02),
        'ln_a_g': gain((DEPTH, W_A)),
        'ln_a_b': nrm((DEPTH, W_A), 0.02),
        'conv_b_w': nrm((DEPTH, CONV_B, W_B), CONV_B ** -0.5),
        'conv_b_b': nrm((DEPTH, W_B), 0.02),
        'w_q': nrm((DEPTH, H_B, DH_B, DH_B), DH_B ** -0.5),
        'w_k': nrm((DEPTH, H_B, DH_B, DH_B), DH_B ** -0.5),
        'w_v': nrm((DEPTH, H_B, DH_B, DH_B), DH_B ** -0.5),
        'w_ig': nrm((DEPTH, 3 * W_B, H_B), 0.3 * (3 * W_B) ** -0.5),
        'b_ig': nrm((DEPTH, H_B), 0.1),
        'w_fg': nrm((DEPTH, 3 * W_B, H_B), 0.3 * (3 * W_B) ** -0.5),
        'b_fg': jnp.linspace(3.0, 6.0, H_B, dtype=jnp.float32)[None, :] + nrm((DEPTH, H_B), 0.1),
        'gn_b_g': gain((DEPTH, W_B)),
        'skip_b': gain((DEPTH, W_B)),
        'conv_c_w': nrm((DEPTH, CONV_C, W_C), CONV_C ** -0.5),
        'conv_c_b': nrm((DEPTH, W_C), 0.02),
        'w_ra': nrm((DEPTH, H_C, DH_C, DH_C), DH_C ** -0.5),
        'b_ra': nrm((DEPTH, W_C), 0.02),
        'w_ix': nrm((DEPTH, H_C, DH_C, DH_C), DH_C ** -0.5),
        'b_ix': nrm((DEPTH, W_C), 0.02),
        'lam': jnp.log(a_init) - jnp.log1p(-a_init),
        'w_out': nrm((DEPTH, D_MIX, D_MODEL), D_MIX ** -0.5),
        'norm_ffn_g': gain((DEPTH, D_MODEL)),
        'w_gate': nrm((N_DENSE, D_MODEL, D_FF), D_MODEL ** -0.5),
        'w_up': nrm((N_DENSE, D_MODEL, D_FF), D_MODEL ** -0.5),
        'w_down': nrm((N_DENSE, D_FF, D_MODEL), D_FF ** -0.5),
        'w_router': nrm((N_MOE, D_MODEL, N_EXPERTS), D_MODEL ** -0.5),
        'b_router': nrm((N_MOE, N_EXPERTS), 0.01),
        'we_gate': nrm((N_MOE, N_EXPERTS, D_MODEL, D_FF_EXPERT), D_MODEL ** -0.5),
        'we_up': nrm((N_MOE, N_EXPERTS, D_MODEL, D_FF_EXPERT), D_MODEL ** -0.5),
        'we_down': nrm((N_MOE, N_EXPERTS, D_FF_EXPERT, D_MODEL), D_FF_EXPERT ** -0.5),
        'norm_final_g': gain((D_MODEL,)),
    }


def reference(x_prompt, x_sample, state_conv_a, state_conv_b, state_mlstm_C, state_mlstm_n, state_mlstm_m,
              state_conv_c, state_rglru_h, norm_mix_g, w_in, conv_a_w, conv_a_b, ln_a_g, ln_a_b,
              conv_b_w, conv_b_b, w_q, w_k, w_v, w_ig, b_ig, w_fg, b_fg, gn_b_g, skip_b,
              conv_c_w, conv_c_b, w_ra, b_ra, w_ix, b_ix, lam, w_out, norm_ffn_g,
              w_gate, w_up, w_down, w_router, b_router, we_gate, we_up, we_down, norm_final_g):
    p = dict(norm_mix_g=norm_mix_g, w_in=w_in, conv_a_w=conv_a_w, conv_a_b=conv_a_b, ln_a_g=ln_a_g,
             ln_a_b=ln_a_b, conv_b_w=conv_b_w, conv_b_b=conv_b_b, w_q=w_q, w_k=w_k, w_v=w_v,
             w_ig=w_ig, b_ig=b_ig, w_fg=w_fg, b_fg=b_fg, gn_b_g=gn_b_g, skip_b=skip_b,
             conv_c_w=conv_c_w, conv_c_b=conv_c_b, w_ra=w_ra, b_ra=b_ra, w_ix=w_ix, b_ix=b_ix, lam=lam,
             w_out=w_out, norm_ffn_g=norm_ffn_g, w_gate=w_gate, w_up=w_up, w_down=w_down,
             w_router=w_router, b_router=b_router, we_gate=we_gate, we_up=we_up, we_down=we_down,
             norm_final_g=norm_final_g)
    bp = x_prompt.shape[0]
    dt = x_prompt.dtype
    zero_states = (jnp.zeros((DEPTH, bp, CONV_A - 1, W_A), dt),
                   jnp.zeros((DEPTH, bp, CONV_B - 1, W_B), dt),
                   jnp.zeros((DEPTH, bp, H_B, DH_B, DH_B), jnp.float32),
                   jnp.zeros((DEPTH, bp, H_B, DH_B), jnp.float32),
                   jnp.zeros((DEPTH, bp, H_B), jnp.float32),
                   jnp.zeros((DEPTH, bp, CONV_C - 1, W_C), dt),
                   jnp.zeros((DEPTH, bp, W_C), jnp.float32))
    y_prompt, new_p = trunk(x_prompt, zero_states, p)
    sample_states = (state_conv_a, state_conv_b, state_mlstm_C, state_mlstm_n, state_mlstm_m,
                     state_conv_c, state_rglru_h)
    y_sample, new_s = trunk(x_sample, sample_states, p)
    conv_a_p, conv_b_p, C_p, n_p, m_p, conv_c_p, h_p = new_p
    conv_a_s, conv_b_s, C_s, n_s, m_s, conv_c_s, h_s = new_s
    return (y_prompt, y_sample, conv_a_p, conv_b_p, C_p, n_p, m_p, conv_c_p, h_p,
            conv_a_s, conv_b_s, C_s, n_s, m_s, conv_c_s, h_s)
```

```python
import functools

import jax
import jax.numpy as jnp
from jax import lax
from jax.experimental import pallas as pl
from jax.experimental.pallas import tpu as pltpu

f32 = jnp.float32
bf16 = jnp.bfloat16

D_MODEL = 1024
W_A, W_B, W_C = 256, 512, 256
G_A = 4
CONV_A, CONV_B, CONV_C = 31, 4, 4
H_B, DH_B = 4, 128
H_C, DH_C = 4, 64
RG_C = 8.0
D_IN = 2048
N_EXPERTS = 8
NORM_EPS = 1e-6
LN_EPS = 1e-5

LANES = 128
SUBLANES = 8
VMEM_LIMIT = 56 * 1024 * 1024

T_MIX = 256
L_CHUNK = 128
CONV_ROWS = 64
A_HIST = 32
S_HIST = 8
TM_FFN = 512
STEP_BB = 8


def _dot(a, b):
    return jnp.dot(a, b, preferred_element_type=f32)


def _dot_nt(a, b):
    return lax.dot_general(a, b, (((1,), (1,)), ((), ())), preferred_element_type=f32)


def _dot_tn(a, b):
    return lax.dot_general(a, b, (((0,), (0,)), ((), ())), preferred_element_type=f32)


def _split_terms(x, terms):
    out = []
    r = x
    for _ in range(terms - 1):
        p = r.astype(bf16)
        out.append(p)
        r = r - p.astype(f32)
    out.append(r.astype(bf16))
    return out


def _dot_split_lhs(x, m, terms):
    acc = None
    for p in _split_terms(x, terms):
        d = _dot(p, m)
        acc = d if acc is None else acc + d
    return acc


def _dot_split_rhs(m, x, terms):
    acc = None
    for p in _split_terms(x, terms):
        d = _dot(m, p)
        acc = d if acc is None else acc + d
    return acc


def _rmsnorm(x, g):
    return x * lax.rsqrt(jnp.mean(x * x, axis=-1, keepdims=True) + NORM_EPS) * g


def _group_mask_a():
    r = lax.broadcasted_iota(jnp.int32, (W_A, W_A), 0) // (W_A // G_A)
    c = lax.broadcasted_iota(jnp.int32, (W_A, W_A), 1) // (W_A // G_A)
    return jnp.where(r == c, 1.0, 0.0).astype(bf16)


def _group_layernorm_silu(a, gm, g, b):
    inv = 1.0 / (W_A // G_A)
    mu = _dot_split_lhs(a, gm, 2) * inv
    d = a - mu
    var = _dot_split_lhs(d * d, gm, 2) * inv
    y = d * lax.rsqrt(var + LN_EPS) * g + b
    return y * jax.nn.sigmoid(y)


def _head_layernorm(h, g):
    mu = jnp.mean(h, axis=-1, keepdims=True)
    d = h - mu
    var = jnp.mean(d * d, axis=-1, keepdims=True)
    return d * lax.rsqrt(var + LN_EPS) * g


def _rglru_gates(xcv, wrg, brg, lam):
    gates = _dot(xcv.astype(bf16), wrg) + brg
    r = jax.nn.sigmoid(gates[:, :W_C])
    i = jax.nn.sigmoid(gates[:, W_C:])
    log_a = RG_C * r * jax.nn.log_sigmoid(lam)
    a = jnp.exp(log_a)
    t = jnp.tanh(log_a)
    one_minus_a2 = -2.0 * t / (1.0 - t)
    u = jnp.sqrt(one_minus_a2) * i * xcv
    return a, u


def _mix_prompt_body(x_ref, g_ref, win_ref, caw_ref, cab_ref, lag_ref, lab_ref, cbw_ref, cbb_ref,
                     wq_ref, wk_ref, wv_ref, wg_ref, bg_ref, gn_ref, skip_ref, ccw_ref, ccb_ref,
                     wrg_ref, brg_ref, lam_ref, wout_ref,
                     x1_ref, sa_ref, sb_ref, sC_ref, sn_ref, sm_ref, sc_ref, sh_ref,
                     ahist, bhist, chist, conv_scr, C_scr, n_scr, m_scr, h_scr, qkv_scr, mix_scr):
    T, L = T_MIX, L_CHUNK
    s = pl.program_id(1)
    last = pl.num_programs(1) - 1

    @pl.when(s == 0)
    def _():
        ahist[0:A_HIST, :] = jnp.zeros((A_HIST, W_A), f32)
        bhist[0:S_HIST, :] = jnp.zeros((S_HIST, W_B), f32)
        chist[0:S_HIST, :] = jnp.zeros((S_HIST, W_C), f32)
        C_scr[...] = jnp.zeros(C_scr.shape, f32)
        n_scr[...] = jnp.zeros(n_scr.shape, f32)
        m_scr[...] = jnp.zeros(m_scr.shape, f32)
        h_scr[...] = jnp.zeros(h_scr.shape, f32)

    x = x_ref[...]
    hn = _rmsnorm(x, g_ref[...])
    u = _dot(hn.astype(bf16), win_ref[...])
    xa, ga = u[:, 0:256], u[:, 256:512]
    xb, zb = u[:, 512:1024], u[:, 1024:1536]
    xc, gc = u[:, 1536:1792], u[:, 1792:2048]

    ahist[pl.ds(A_HIST, T), :] = xa * jax.nn.sigmoid(ga)
    caw = caw_ref[...]
    cab = cab_ref[...]
    first = A_HIST - (CONV_A - 1)

    def conv_block(r, carry):
        base = pl.multiple_of(r * CONV_ROWS, CONV_ROWS)
        blk = ahist[pl.ds(base, CONV_ROWS + A_HIST), :]
        acc = jnp.broadcast_to(cab, (CONV_ROWS, W_A))
        for rr in range(SUBLANES):
            shifted = blk if rr == 0 else blk[rr:rr + CONV_ROWS + A_HIST - SUBLANES, :]
            for q in range(A_HIST // SUBLANES + 1):
                j = SUBLANES * q + rr - first
                if 0 <= j < CONV_A:
                    acc = acc + caw[j:j + 1, :] * shifted[SUBLANES * q:SUBLANES * q + CONV_ROWS, :]
        conv_scr[pl.ds(base, CONV_ROWS), :] = acc
        return carry

    lax.fori_loop(0, T // CONV_ROWS, conv_block, 0)
    a_out = _group_layernorm_silu(conv_scr[...], _group_mask_a(), lag_ref[...], lab_ref[...])
    mix_scr[:, 0:W_A] = a_out.astype(bf16)

    bhist[pl.ds(S_HIST, T), :] = xb
    cbw = cbw_ref[...]
    cb = cbb_ref[...] + cbw[3:4, :] * xb
    for j in range(CONV_B - 1):
        cb = cb + cbw[j:j + 1, :] * bhist[pl.ds(S_HIST - (CONV_B - 1) + j, T), :]
    cb = cb * jax.nn.sigmoid(cb)
    cb_bf = cb.astype(bf16)
    xb_bf = xb.astype(bf16)
    qs, ks, vs = [], [], []
    for h in range(H_B):
        hs = slice(h * DH_B, (h + 1) * DH_B)
        q = _dot(cb_bf[:, hs], wq_ref[h])
        k = _dot(cb_bf[:, hs], wk_ref[h]) * (DH_B ** -0.5)
        v = _dot(xb_bf[:, hs], wv_ref[h])
        qs.append(q)
        ks.append(k)
        vs.append(v)
        qkv_scr[:, (3 * h) * DH_B:(3 * h + 1) * DH_B] = q.astype(bf16)
        qkv_scr[:, (3 * h + 1) * DH_B:(3 * h + 2) * DH_B] = k.astype(bf16)
        qkv_scr[:, (3 * h + 2) * DH_B:(3 * h + 3) * DH_B] = v.astype(bf16)
    gates = _dot(qkv_scr[...], wg_ref[...]) + bg_ref[...]
    ig = gates[:, 0:LANES]
    lf = jax.nn.log_sigmoid(gates[:, LANES:2 * LANES])
    ri = lax.broadcasted_iota(jnp.int32, (L, L), 0)
    ci = lax.broadcasted_iota(jnp.int32, (L, L), 1)
    causal = ri >= ci
    tri = jnp.where(causal, 1.0, 0.0).astype(bf16)
    gn = gn_ref[...]
    skip = skip_ref[...]
    for c in range(T // L):
        rows = slice(c * L, (c + 1) * L)
        b_all = _dot_split_rhs(tri, lf[rows], 3)
        c_all = ig[rows] - b_all
        c_all_t = c_all.T
        for h in range(H_B):
            hs = slice(h * DH_B, (h + 1) * DH_B)
            q, k, v = qs[h][rows], ks[h][rows], vs[h][rows]
            q_bf, v_bf = q.astype(bf16), v.astype(bf16)
            m_prev = m_scr[h:h + 1, 0:1]
            b_col = b_all[:, h:h + 1]
            c_col = c_all[:, h:h + 1]
            dm = jnp.where(causal, c_all_t[h:h + 1, :], -jnp.inf)
            mx = jnp.maximum(m_prev, jnp.max(dm, axis=1, keepdims=True))
            sc = _dot_nt(q_bf, k.astype(bf16)) * jnp.exp(dm - mx)
            w_inter = jnp.exp(m_prev - mx)
            C_h = C_scr[h]
            n_h = n_scr[h:h + 1, :]
            num = _dot(sc.astype(bf16), v_bf) + w_inter * _dot(q_bf, C_h.astype(bf16))
            den = jnp.sum(sc, axis=1, keepdims=True) + w_inter * jnp.sum(q * n_h, axis=1, keepdims=True)
            hb = num / jnp.maximum(jnp.abs(den), jnp.exp(-(b_col + mx)))
            hb = _head_layernorm(hb, gn[:, hs])
            b_out = jax.nn.sigmoid(zb[rows, hs]) * (hb + skip[:, hs] * cb[rows, hs])
            mix_scr[rows, W_A + h * DH_B:W_A + (h + 1) * DH_B] = b_out.astype(bf16)
            mx_last = mx[L - 1:L, :]
            gk = jnp.exp(c_col - mx_last) * k
            decay = jnp.exp(m_prev - mx_last)
            C_scr[h] = decay * C_h + _dot_tn(gk.astype(bf16), v_bf)
            n_scr[h:h + 1, :] = decay * n_h + jnp.sum(gk, axis=0, keepdims=True)
            m_scr[h:h + 1, :] = jnp.broadcast_to(b_col[L - 1:L, :] + mx_last, (1, LANES))

    chist[pl.ds(S_HIST, T), :] = xc
    ccw = ccw_ref[...]
    xcv = ccb_ref[...] + ccw[3:4, :] * xc
    for j in range(CONV_C - 1):
        xcv = xcv + ccw[j:j + 1, :] * chist[pl.ds(S_HIST - (CONV_C - 1) + j, T), :]
    a_t, u_t = _rglru_gates(xcv, wrg_ref[...], brg_ref[...], lam_ref[...])
    row = lax.broadcasted_iota(jnp.int32, (T, W_C), 0)
    d = 1
    while d < T:
        a_sh = jnp.where(row >= d, pltpu.roll(a_t, d, axis=0), 1.0)
        u_sh = jnp.where(row >= d, pltpu.roll(u_t, d, axis=0), 0.0)
        u_t = a_t * u_sh + u_t
        a_t = a_t * a_sh
        d *= 2
    hc = a_t * h_scr[...] + u_t
    h_scr[...] = hc[T - 1:T, :]
    mix_scr[:, W_A + W_B:] = (hc * jax.nn.gelu(gc)).astype(bf16)

    x1_ref[...] = x + _dot(mix_scr[...], wout_ref[...])

    ahist[0:A_HIST, :] = ahist[pl.ds(T, A_HIST), :]
    bhist[0:S_HIST, :] = bhist[pl.ds(T, S_HIST), :]
    chist[0:S_HIST, :] = chist[pl.ds(T, S_HIST), :]

    @pl.when(s == last)
    def _():
        sa_ref[...] = ahist[pl.ds(first, CONV_A - 1), :]
        sb_ref[...] = bhist[pl.ds(S_HIST - (CONV_B - 1), CONV_B - 1), :]
        sc_ref[...] = chist[pl.ds(S_HIST - (CONV_C - 1), CONV_C - 1), :]
        sC_ref[...] = C_scr[...]
        sn_ref[...] = n_scr[0:H_B, :]
        sm_ref[...] = m_scr[0:H_B, :]
        sh_ref[...] = h_scr[...]


def _full(shape):
    nd = len(shape)
    return pl.BlockSpec(shape, lambda *_: (0,) * nd)


def _mix_prompt(x, lw):
    B, S, _ = x.shape
    T = T_MIX
    weights = [lw[k] for k in ("norm_mix_g", "w_in", "conv_a_w", "conv_a_b", "ln_a_g", "ln_a_b", "conv_b_w", "conv_b_b",
                               "w_q", "w_k", "w_v", "w_gates", "b_gates", "gn_b_g", "skip_b", "conv_c_w", "conv_c_b",
                               "w_rg", "b_rg", "lam", "w_out")]
    out_shape = (
        jax.ShapeDtypeStruct((B, S, D_MODEL), f32),
        jax.ShapeDtypeStruct((B, CONV_A - 1, W_A), f32),
        jax.ShapeDtypeStruct((B, CONV_B - 1, W_B), f32),
        jax.ShapeDtypeStruct((B, H_B, DH_B, DH_B), f32),
        jax.ShapeDtypeStruct((B, H_B, DH_B), f32),
        jax.ShapeDtypeStruct((B, H_B, LANES), f32),
        jax.ShapeDtypeStruct((B, CONV_C - 1, W_C), f32),
        jax.ShapeDtypeStruct((B, 1, W_C), f32),
    )
    per_b = lambda shp: pl.BlockSpec((None,) + shp, lambda b, s: (b,) + (0,) * len(shp))
    out_specs = (
        pl.BlockSpec((None, T, D_MODEL), lambda b, s: (b, s, 0)),
        per_b((CONV_A - 1, W_A)), per_b((CONV_B - 1, W_B)), per_b((H_B, DH_B, DH_B)), per_b((H_B, DH_B)),
        per_b((H_B, LANES)), per_b((CONV_C - 1, W_C)), per_b((1, W_C)),
    )
    scratch = [
        pltpu.VMEM((A_HIST + T, W_A), f32), pltpu.VMEM((S_HIST + T, W_B), f32), pltpu.VMEM((S_HIST + T, W_C), f32),
        pltpu.VMEM((T, W_A), f32),
        pltpu.VMEM((H_B, DH_B, DH_B), f32), pltpu.VMEM((SUBLANES, DH_B), f32), pltpu.VMEM((SUBLANES, LANES), f32),
        pltpu.VMEM((1, W_C), f32),
        pltpu.VMEM((T, 3 * W_B), bf16), pltpu.VMEM((T, D_MODEL), bf16),
    ]
    return pl.pallas_call(
        _mix_prompt_body,
        grid=(B, S // T),
        in_specs=[pl.BlockSpec((None, T, D_MODEL), lambda b, s: (b, s, 0))] + [_full(w.shape) for w in weights],
        out_specs=out_specs,
        out_shape=out_shape,
        scratch_shapes=scratch,
        compiler_params=pltpu.CompilerParams(dimension_semantics=("arbitrary", "arbitrary"),
                                             vmem_limit_bytes=VMEM_LIMIT),
        name="mix_prompt",
    )(x, *weights)


def _step_pre_body(x_ref, g_ref, win_ref, sa_ref, caw_ref, cab_ref, lag_ref, lab_ref, sb_ref, cbw_ref, cbb_ref,
                   wq_ref, wk_ref, wv_ref, wg_ref, bg_ref, n_ref, m_ref, sc_ref, ccw_ref, ccb_ref,
                   wrg_ref, brg_ref, lam_ref, h_ref, skip_ref,
                   sa_o, sb_o, sc_o, n_o, m_o, h_o, q_o, gk_o, v_o, dec_o, den_o, sv_o, zsig_o, skcb_o, ac_o,
                   qkv_scr):
    x = x_ref[...]
    hn = _rmsnorm(x, g_ref[...])
    u = _dot(hn.astype(bf16), win_ref[...])
    xa, ga = u[:, 0:256], u[:, 256:512]
    xb, zb = u[:, 512:1024], u[:, 1024:1536]
    xc, gc = u[:, 1536:1792], u[:, 1792:2048]

    a_new = xa * jax.nn.sigmoid(ga)
    caw = caw_ref[...]
    conv = cab_ref[...] + caw[CONV_A - 1:CONV_A, :] * a_new
    for j in range(CONV_A - 1):
        conv = conv + caw[j:j + 1, :] * sa_ref[:, j * W_A:(j + 1) * W_A]
    sa_o[:, 0:(CONV_A - 2) * W_A] = sa_ref[:, W_A:]
    sa_o[:, (CONV_A - 2) * W_A:] = a_new
    ac_o[:, 0:W_A] = _group_layernorm_silu(conv, _group_mask_a(), lag_ref[...], lab_ref[...])

    cbw = cbw_ref[...]
    cb = cbb_ref[...] + cbw[CONV_B - 1:CONV_B, :] * xb
    for j in range(CONV_B - 1):
        cb = cb + cbw[j:j + 1, :] * sb_ref[:, j * W_B:(j + 1) * W_B]
    sb_o[:, 0:(CONV_B - 2) * W_B] = sb_ref[:, W_B:]
    sb_o[:, (CONV_B - 2) * W_B:] = xb
    cb = cb * jax.nn.sigmoid(cb)
    cb_bf = cb.astype(bf16)
    xb_bf = xb.astype(bf16)
    qs, ks, vs = [], [], []
    for h in range(H_B):
        hs = slice(h * DH_B, (h + 1) * DH_B)
        q = _dot(cb_bf[:, hs], wq_ref[h])
        k = _dot(cb_bf[:, hs], wk_ref[h]) * (DH_B ** -0.5)
        v = _dot(xb_bf[:, hs], wv_ref[h])
        qs.append(q)
        ks.append(k)
        vs.append(v)
        qkv_scr[:, (3 * h) * DH_B:(3 * h + 1) * DH_B] = q.astype(bf16)
        qkv_scr[:, (3 * h + 1) * DH_B:(3 * h + 2) * DH_B] = k.astype(bf16)
        qkv_scr[:, (3 * h + 2) * DH_B:(3 * h + 3) * DH_B] = v.astype(bf16)
    gates = _dot(qkv_scr[...], wg_ref[...]) + bg_ref[...]
    ig = gates[:, 0:LANES]
    lf = jax.nn.log_sigmoid(gates[:, LANES:2 * LANES])
    m0 = m_ref[...]
    m_t = jnp.maximum(lf + m0, ig)
    g_in = jnp.exp(ig - m_t)
    decay = jnp.exp(lf + m0 - m_t)
    lane = lax.broadcasted_iota(jnp.int32, ig.shape, 1)
    qk = jnp.zeros(ig.shape, f32)
    qn = jnp.zeros(ig.shape, f32)
    for h in range(H_B):
        hs = slice(h * DH_B, (h + 1) * DH_B)
        n_h = n_ref[:, hs]
        qk = jnp.where(lane == h, jnp.sum(qs[h] * ks[h], axis=-1, keepdims=True), qk)
        qn = jnp.where(lane == h, jnp.sum(qs[h] * n_h, axis=-1, keepdims=True), qn)
        gk = g_in[:, h:h + 1] * ks[h]
        n_o[:, hs] = decay[:, h:h + 1] * n_h + gk
        q_o[:, hs] = qs[h]
        gk_o[:, hs] = gk
        v_o[:, hs] = vs[h]
    s_t = qk * g_in
    den = s_t + decay * qn
    den_o[...] = jnp.maximum(jnp.abs(den), jnp.exp(-m_t))
    dec_o[...] = decay
    m_o[...] = m_t
    for h in range(H_B):
        hs = slice(h * DH_B, (h + 1) * DH_B)
        sv_o[:, hs] = s_t[:, h:h + 1] * vs[h]
    zsig_o[...] = jax.nn.sigmoid(zb)
    skcb_o[...] = skip_ref[...] * cb

    ccw = ccw_ref[...]
    xcv = ccb_ref[...] + ccw[CONV_C - 1:CONV_C, :] * xc
    for j in range(CONV_C - 1):
        xcv = xcv + ccw[j:j + 1, :] * sc_ref[:, j * W_C:(j + 1) * W_C]
    sc_o[:, 0:(CONV_C - 2) * W_C] = sc_ref[:, W_C:]
    sc_o[:, (CONV_C - 2) * W_C:] = xc
    a_t, u_t = _rglru_gates(xcv, wrg_ref[...], brg_ref[...], lam_ref[...])
    hc = a_t * h_ref[...] + u_t
    h_o[...] = hc
    ac_o[:, W_A:] = hc * jax.nn.gelu(gc)


def _step_state_body(C_ref, qT_ref, gkT_ref, v_ref, dec_ref, Cn_ref, qc_ref):
    for bb in range(STEP_BB):
        for h in range(H_B):
            hs = slice(h * DH_B, (h + 1) * DH_B)
            C = C_ref[bb, h]
            qc_ref[bb:bb + 1, hs] = jnp.sum(C * qT_ref[h, :, bb:bb + 1], axis=0, keepdims=True)
            Cn_ref[bb, h] = dec_ref[bb:bb + 1, h:h + 1] * C + gkT_ref[h, :, bb:bb + 1] * v_ref[bb:bb + 1, hs]


def _step_post_body(x_ref, ac_ref, zsig_ref, skcb_ref, sv_ref, qc_ref, dec_ref, den_ref, gn_ref, wout_ref,
                    x1_ref, mix_scr):
    gn = gn_ref[...]
    dec = dec_ref[...]
    den = den_ref[...]
    mix_scr[:, 0:W_A] = ac_ref[:, 0:W_A].astype(bf16)
    mix_scr[:, W_A + W_B:] = ac_ref[:, W_A:].astype(bf16)
    for h in range(H_B):
        hs = slice(h * DH_B, (h + 1) * DH_B)
        num = sv_ref[:, hs] + dec[:, h:h + 1] * qc_ref[:, hs]
        hb = _head_layernorm(num / den[:, h:h + 1], gn[:, hs])
        mix_scr[:, W_A + h * DH_B:W_A + (h + 1) * DH_B] = (zsig_ref[:, hs] * (hb + skcb_ref[:, hs])).astype(bf16)
    x1_ref[...] = x_ref[...] + _dot(mix_scr[...], wout_ref[...])


def _mix_step(x, st, lw):
    buf_a, buf_b, C0, n0, m0, buf_c, hc0 = st
    Bs = x.shape[0]
    sa = buf_a.reshape(Bs, (CONV_A - 1) * W_A)
    sb = buf_b.reshape(Bs, (CONV_B - 1) * W_B)
    sc = buf_c.reshape(Bs, (CONV_C - 1) * W_C)
    n_in = n0.reshape(Bs, W_B)
    m_in = jnp.pad(m0, ((0, 0), (0, LANES - H_B)))
    sds = lambda *shape: jax.ShapeDtypeStruct(shape, f32)
    pre_in = [x, lw["norm_mix_g"], lw["w_in"], sa, lw["conv_a_w"], lw["conv_a_b"], lw["ln_a_g"], lw["ln_a_b"],
              sb, lw["conv_b_w"], lw["conv_b_b"], lw["w_q"], lw["w_k"], lw["w_v"], lw["w_gates"], lw["b_gates"],
              n_in, m_in, sc, lw["conv_c_w"], lw["conv_c_b"], lw["w_rg"], lw["b_rg"], lw["lam"], hc0, lw["skip_b"]]
    (sa_n, sb_n, sc_n, n_n, m_n, h_n, q, gk, v, dec, den, sv, zsig, skcb, ac) = pl.pallas_call(
        _step_pre_body,
        out_shape=(sds(*sa.shape), sds(*sb.shape), sds(*sc.shape), sds(Bs, W_B), sds(Bs, LANES), sds(Bs, W_C),
                   sds(Bs, W_B), sds(Bs, W_B), sds(Bs, W_B), sds(Bs, LANES), sds(Bs, LANES), sds(Bs, W_B),
                   sds(Bs, W_B), sds(Bs, W_B), sds(Bs, W_A + W_C)),
        scratch_shapes=[pltpu.VMEM((Bs, 3 * W_B), bf16)],
        compiler_params=pltpu.CompilerParams(vmem_limit_bytes=VMEM_LIMIT),
        name="step_pre",
    )(*pre_in)

    nblk = Bs // STEP_BB
    to_cols = lambda t: t.reshape(nblk, STEP_BB, H_B, DH_B).transpose(0, 2, 3, 1)
    C_n, qc = pl.pallas_call(
        _step_state_body,
        grid=(nblk,),
        in_specs=[pl.BlockSpec((STEP_BB, H_B, DH_B, DH_B), lambda i: (i, 0, 0, 0)),
                  pl.BlockSpec((None, H_B, DH_B, STEP_BB), lambda i: (i, 0, 0, 0)),
                  pl.BlockSpec((None, H_B, DH_B, STEP_BB), lambda i: (i, 0, 0, 0)),
                  pl.BlockSpec((STEP_BB, W_B), lambda i: (i, 0)),
                  pl.BlockSpec((STEP_BB, LANES), lambda i: (i, 0))],
        out_specs=(pl.BlockSpec((STEP_BB, H_B, DH_B, DH_B), lambda i: (i, 0, 0, 0)),
                   pl.BlockSpec((STEP_BB, W_B), lambda i: (i, 0))),
        out_shape=(sds(Bs, H_B, DH_B, DH_B), sds(Bs, W_B)),
        compiler_params=pltpu.CompilerParams(dimension_semantics=("arbitrary",), vmem_limit_bytes=VMEM_LIMIT),
        name="step_state",
    )(C0, to_cols(q), to_cols(gk), v, dec)

    x1 = pl.pallas_call(
        _step_post_body,
        out_shape=sds(Bs, D_MODEL),
        scratch_shapes=[pltpu.VMEM((Bs, D_MODEL), bf16)],
        compiler_params=pltpu.CompilerParams(vmem_limit_bytes=VMEM_LIMIT),
        name="step_post",
    )(x, ac, zsig, skcb, sv, qc, dec, den, lw["gn_b_g"], lw["w_out"])
    new = (sa_n.reshape(buf_a.shape), sb_n.reshape(buf_b.shape), C_n, n_n.reshape(n0.shape), m_n[:, :H_B],
           sc_n.reshape(buf_c.shape), h_n)
    return x1, new


def _ffn_body(*refs, moe, final_norm):
    it = iter(refs)
    x_ref, g_ref, wg_ref, wu_ref, wd_ref = next(it), next(it), next(it), next(it), next(it)
    wr_ref = next(it) if moe else None
    br_ref = next(it) if moe else None
    gf_ref = next(it) if final_norm else None
    o_ref, hn_scr, acc_scr = next(it), next(it), next(it)
    comb_scr = next(it) if moe else None
    j = pl.program_id(1)

    @pl.when(j == 0)
    def _():
        hn = _rmsnorm(x_ref[...], g_ref[...])
        hn_scr[...] = hn.astype(bf16)
        acc_scr[...] = jnp.zeros(acc_scr.shape, f32)
        if moe:
            h1, h2 = _split_terms(hn, 2)
            w1, w2 = _split_terms(wr_ref[...], 2)
            logits = _dot(h1, w1) + (_dot(h1, w2) + _dot(h2, w1)) + br_ref[...]
            lane = lax.broadcasted_iota(jnp.int32, logits.shape, 1).astype(f32)
            lg = jnp.where(lane < N_EXPERTS, logits, -jnp.inf)
            m1 = jnp.max(lg, axis=-1, keepdims=True)
            i1 = jnp.min(jnp.where(lg == m1, lane, float(LANES)), axis=-1, keepdims=True)
            lg2 = jnp.where(lane == i1, -jnp.inf, lg)
            m2 = jnp.max(lg2, axis=-1, keepdims=True)
            i2 = jnp.min(jnp.where(lg2 == m2, lane, float(LANES)), axis=-1, keepdims=True)
            e2 = jnp.exp(m2 - m1)
            comb_scr[...] = jnp.where(lane == i1, 1.0 / (1.0 + e2), 0.0) + jnp.where(lane == i2, e2 / (1.0 + e2), 0.0)

    hb = hn_scr[...]
    gt = _dot(hb, wg_ref[...])
    up = _dot(hb, wu_ref[...])
    act = (gt * jax.nn.sigmoid(gt) * up).astype(bf16)
    y = _dot(act, wd_ref[...])
    if moe:
        lane = lax.broadcasted_iota(jnp.int32, comb_scr.shape, 1)
        y = y * jnp.sum(jnp.where(lane == j, comb_scr[...], 0.0), axis=-1, keepdims=True)
    acc_scr[...] += y

    @pl.when(j == pl.num_programs(1) - 1)
    def _():
        out = x_ref[...] + acc_scr[...]
        if final_norm:
            out = _rmsnorm(out, gf_ref[...])
        o_ref[...] = out


def _ffn(x, g, wg, wu, wd, router=None, final_g=None):
    N = x.shape[0]
    E, _, F = wg.shape
    tm = min(TM_FFN, N)
    moe = router is not None
    final_norm = final_g is not None
    ins = [x, g, wg, wu, wd]
    in_specs = [pl.BlockSpec((tm, D_MODEL), lambda i, j: (i, 0)), _full(g.shape),
                pl.BlockSpec((None, D_MODEL, F), lambda i, j: (j, 0, 0)),
                pl.BlockSpec((None, D_MODEL, F), lambda i, j: (j, 0, 0)),
                pl.BlockSpec((None, F, D_MODEL), lambda i, j: (j, 0, 0))]
    scratch = [pltpu.VMEM((tm, D_MODEL), bf16), pltpu.VMEM((tm, D_MODEL), f32)]
    if moe:
        ins += list(router)
        in_specs += [_full(router[0].shape), _full(router[1].shape)]
        scratch.append(pltpu.VMEM((tm, LANES), f32))
    if final_norm:
        ins.append(final_g)
        in_specs.append(_full(final_g.shape))
    return pl.pallas_call(
        functools.partial(_ffn_body, moe=moe, final_norm=final_norm),
        grid=(N // tm, E),
        in_specs=in_specs,
        out_specs=pl.BlockSpec((tm, D_MODEL), lambda i, j: (i, 0)),
        out_shape=jax.ShapeDtypeStruct((N, D_MODEL), f32),
        scratch_shapes=scratch,
        compiler_params=pltpu.CompilerParams(dimension_semantics=("arbitrary", "arbitrary"),
                                             vmem_limit_bytes=VMEM_LIMIT),
        name="ffn_moe" if moe else "ffn_dense",
    )(*ins)


def _block_diag(w):
    h, di, do = w.shape
    return jnp.einsum("hij,hg->higj", w, jnp.eye(h, dtype=w.dtype)).reshape(h * di, h * do)


def _layer_weights(l, p):
    row = lambda v: v.reshape(1, -1)
    w_gates = jnp.zeros((3 * W_B, 2 * LANES), f32)
    w_gates = w_gates.at[:, 0:H_B].set(p["w_ig"][l]).at[:, LANES:LANES + H_B].set(p["w_fg"][l])
    b_gates = jnp.zeros((1, 2 * LANES), f32)
    b_gates = b_gates.at[0, 0:H_B].set(p["b_ig"][l]).at[0, LANES:LANES + H_B].set(p["b_fg"][l])
    return dict(
        norm_mix_g=row(p["norm_mix_g"][l]), w_in=p["w_in"][l].astype(bf16),
        conv_a_w=p["conv_a_w"][l], conv_a_b=row(p["conv_a_b"][l]), ln_a_g=row(p["ln_a_g"][l]), ln_a_b=row(p["ln_a_b"][l]),
        conv_b_w=p["conv_b_w"][l], conv_b_b=row(p["conv_b_b"][l]),
        w_q=p["w_q"][l].astype(bf16), w_k=p["w_k"][l].astype(bf16), w_v=p["w_v"][l].astype(bf16),
        w_gates=w_gates.astype(bf16), b_gates=b_gates, gn_b_g=row(p["gn_b_g"][l]), skip_b=row(p["skip_b"][l]),
        conv_c_w=p["conv_c_w"][l], conv_c_b=row(p["conv_c_b"][l]),
        w_rg=jnp.concatenate([_block_diag(p["w_ra"][l]), _block_diag(p["w_ix"][l])], axis=1).astype(bf16),
        b_rg=jnp.concatenate([p["b_ra"][l], p["b_ix"][l]]).reshape(1, -1), lam=row(p["lam"][l]),
        w_out=p["w_out"][l].astype(bf16),
    )


def kernel(x_prompt, x_sample, state_conv_a, state_conv_b, state_mlstm_C, state_mlstm_n, state_mlstm_m, state_conv_c, state_rglru_h, norm_mix_g, w_in, conv_a_w, conv_a_b, ln_a_g, ln_a_b, conv_b_w, conv_b_b, w_q, w_k, w_v, w_ig, b_ig, w_fg, b_fg, gn_b_g, skip_b, conv_c_w, conv_c_b, w_ra, b_ra, w_ix, b_ix, lam, w_out, norm_ffn_g, w_gate, w_up, w_down, w_router, b_router, we_gate, we_up, we_down, norm_final_g):
    p = dict(norm_mix_g=norm_mix_g, w_in=w_in, conv_a_w=conv_a_w, conv_a_b=conv_a_b, ln_a_g=ln_a_g, ln_a_b=ln_a_b,
             conv_b_w=conv_b_w, conv_b_b=conv_b_b, w_q=w_q, w_k=w_k, w_v=w_v, w_ig=w_ig, b_ig=b_ig, w_fg=w_fg,
             b_fg=b_fg, gn_b_g=gn_b_g, skip_b=skip_b, conv_c_w=conv_c_w, conv_c_b=conv_c_b, w_ra=w_ra, b_ra=b_ra,
             w_ix=w_ix, b_ix=b_ix, lam=lam, w_out=w_out)
    depth = w_in.shape[0]
    Bp, S, _ = x_prompt.shape
    Bs = x_sample.shape[0]
    sample_states = (state_conv_a, state_conv_b, state_mlstm_C, state_mlstm_n, state_mlstm_m, state_conv_c,
                     state_rglru_h)
    xp = x_prompt
    xs = x_sample.reshape(Bs, D_MODEL)
    new_p, new_s = [], []
    for l in range(depth):
        lw = _layer_weights(l, p)
        xp, sa, sb, sC, sn, sm, sc, sh = _mix_prompt(xp, lw)
        new_p.append((sa, sb, sC, sn, sm[:, :, 0], sc, sh.reshape(Bp, W_C)))
        xs, st = _mix_step(xs, tuple(s[l] for s in sample_states), lw)
        new_s.append(st)
        g = norm_ffn_g[l].reshape(1, -1)
        final_g = norm_final_g.reshape(1, -1) if l == depth - 1 else None
        j = l // 2
        if l % 2 == 0:
            half = w_gate.shape[2] // 2
            wg = w_gate[j].reshape(D_MODEL, 2, half).transpose(1, 0, 2).astype(bf16)
            wu = w_up[j].reshape(D_MODEL, 2, half).transpose(1, 0, 2).astype(bf16)
            wd = w_down[j].reshape(2, half, D_MODEL).astype(bf16)
            router = None
        else:
            wg, wu, wd = we_gate[j].astype(bf16), we_up[j].astype(bf16), we_down[j].astype(bf16)
            router = (jnp.pad(w_router[j], ((0, 0), (0, LANES - N_EXPERTS))),
                      jnp.pad(b_router[j], (0, LANES - N_EXPERTS)).reshape(1, -1))
        xp = _ffn(xp.reshape(Bp * S, D_MODEL), g, wg, wu, wd, router, final_g).reshape(Bp, S, D_MODEL)
        xs = _ffn(xs, g, wg, wu, wd, router, final_g)
    stack = lambda states, i: jnp.stack([st[i] for st in states])
    out_p = tuple(stack(new_p, i) for i in range(7))
    out_s = tuple(stack(new_s, i) for i in range(7))
    return (xp, xs.reshape(Bs, 1, D_MODEL)) + out_p + out_s
```

```python
import functools

import jax
import jax.numpy as jnp
from jax import lax
from jax.experimental import pallas as pl
from jax.experimental.pallas import tpu as pltpu

f32 = jnp.float32
bf16 = jnp.bfloat16

D_MODEL = 1024
W_A, W_B, W_C = 256, 512, 256
G_A = 4
CONV_A, CONV_B, CONV_C = 31, 4, 4
H_B, DH_B = 4, 128
H_C, DH_C = 4, 64
RG_C = 8.0
D_IN = 2048
N_EXPERTS = 8
NORM_EPS = 1e-6
LN_EPS = 1e-5

LANES = 128
SUBLANES = 8
VMEM_LIMIT = 56 * 1024 * 1024

T_MIX = 256
L_CHUNK = 128
CONV_ROWS = 128
A_HIST = 32
S_HIST = 8
TM_FFN = 512
STEP_BB = 8


def _dot(a, b):
    return jnp.dot(a, b, preferred_element_type=f32)


def _dot_nt(a, b):
    return lax.dot_general(a, b, (((1,), (1,)), ((), ())), preferred_element_type=f32)


def _dot_tn(a, b):
    return lax.dot_general(a, b, (((0,), (0,)), ((), ())), preferred_element_type=f32)


def _split_terms(x, terms):
    out = []
    r = x
    for _ in range(terms - 1):
        p = r.astype(bf16)
        out.append(p)
        r = r - p.astype(f32)
    out.append(r.astype(bf16))
    return out


def _dot_split_lhs(x, m, terms):
    acc = None
    for p in _split_terms(x, terms):
        d = _dot(p, m)
        acc = d if acc is None else acc + d
    return acc


def _dot_split_rhs(m, x, terms):
    acc = None
    for p in _split_terms(x, terms):
        d = _dot(m, p)
        acc = d if acc is None else acc + d
    return acc


def _sigmoid(x):
    return 0.5 * (1.0 + jnp.tanh(0.5 * x))


def _rmsnorm(x, g):
    return x * lax.rsqrt(jnp.mean(x * x, axis=-1, keepdims=True) + NORM_EPS) * g


def _group_mask_a():
    r = lax.broadcasted_iota(jnp.int32, (W_A, W_A), 0) // (W_A // G_A)
    c = lax.broadcasted_iota(jnp.int32, (W_A, W_A), 1) // (W_A // G_A)
    return jnp.where(r == c, 1.0, 0.0).astype(bf16)


def _group_layernorm_silu(a, gm, g, b):
    inv = 1.0 / (W_A // G_A)
    mu = _dot_split_lhs(a, gm, 2) * inv
    d = a - mu
    var = _dot((d * d).astype(bf16), gm) * inv
    y = d * lax.rsqrt(var + LN_EPS) * g + b
    return y * _sigmoid(y)


def _head_layernorm(h, g):
    mu = jnp.mean(h, axis=-1, keepdims=True)
    d = h - mu
    var = jnp.mean(d * d, axis=-1, keepdims=True)
    return d * lax.rsqrt(var + LN_EPS) * g


def _rglru_gates(xcv, wrg, brg, lam):
    gates = _dot(xcv.astype(bf16), wrg) + brg
    r = _sigmoid(gates[:, :W_C])
    i = _sigmoid(gates[:, W_C:])
    log_a = RG_C * r * jax.nn.log_sigmoid(lam)
    a = jnp.exp(log_a)
    t = jnp.tanh(log_a)
    one_minus_a2 = -2.0 * t / (1.0 - t)
    u = jnp.sqrt(one_minus_a2) * i * xcv
    return a, u


def _mix_prompt_body(x_ref, g_ref, win_ref, caw_ref, cab_ref, lag_ref, lab_ref, cbw_ref, cbb_ref,
                     wq_ref, wk_ref, wv_ref, wg_ref, bg_ref, gn_ref, skip_ref, ccw_ref, ccb_ref,
                     wrg_ref, brg_ref, lam_ref, wout_ref,
                     x1_ref, sa_ref, sb_ref, sC_ref, sn_ref, sm_ref, sc_ref, sh_ref,
                     ahist, bhist, chist, conv_scr, C_scr, n_scr, m_scr, h_scr, qkv_scr, mix_scr):
    T, L = T_MIX, L_CHUNK
    s = pl.program_id(1)
    last = pl.num_programs(1) - 1

    @pl.when(s == 0)
    def _():
        ahist[0:A_HIST, :] = jnp.zeros((A_HIST, W_A), f32)
        bhist[0:S_HIST, :] = jnp.zeros((S_HIST, W_B), f32)
        chist[0:S_HIST, :] = jnp.zeros((S_HIST, W_C), f32)
        C_scr[...] = jnp.zeros(C_scr.shape, f32)
        n_scr[...] = jnp.zeros(n_scr.shape, f32)
        m_scr[...] = jnp.zeros(m_scr.shape, f32)
        h_scr[...] = jnp.zeros(h_scr.shape, f32)

    x = x_ref[...]
    hn = _rmsnorm(x, g_ref[...])
    u = _dot(hn.astype(bf16), win_ref[...])
    xa, ga = u[:, 0:256], u[:, 256:512]
    xb, zb = u[:, 512:1024], u[:, 1024:1536]
    xc, gc = u[:, 1536:1792], u[:, 1792:2048]

    ahist[pl.ds(A_HIST, T), :] = xa * _sigmoid(ga)
    caw = caw_ref[...]
    cab = cab_ref[...]
    first = A_HIST - (CONV_A - 1)

    def conv_block(r, carry):
        base = pl.multiple_of(r * CONV_ROWS, CONV_ROWS)
        for lo in range(0, W_A, LANES):
            blk = ahist[pl.ds(base, CONV_ROWS + A_HIST), lo:lo + LANES]
            acc = jnp.broadcast_to(cab[:, lo:lo + LANES], (CONV_ROWS, LANES))
            for rr in range(SUBLANES):
                n = CONV_ROWS if rr == 0 else CONV_ROWS + SUBLANES
                z = None
                for q in range(A_HIST // SUBLANES + 1):
                    j = SUBLANES * q + rr - first
                    if 0 <= j < CONV_A:
                        term = caw[j:j + 1, lo:lo + LANES] * blk[SUBLANES * q:SUBLANES * q + n, :]
                        z = term if z is None else z + term
                acc = acc + (z if rr == 0 else z[rr:rr + CONV_ROWS, :])
            conv_scr[pl.ds(base, CONV_ROWS), lo:lo + LANES] = acc
        return carry

    lax.fori_loop(0, T // CONV_ROWS, conv_block, 0)
    a_out = _group_layernorm_silu(conv_scr[...], _group_mask_a(), lag_ref[...], lab_ref[...])
    mix_scr[:, 0:W_A] = a_out.astype(bf16)

    bhist[pl.ds(S_HIST, T), :] = xb
    cbw = cbw_ref[...]
    cb = cbb_ref[...] + cbw[3:4, :] * xb
    for j in range(CONV_B - 1):
        cb = cb + cbw[j:j + 1, :] * bhist[pl.ds(S_HIST - (CONV_B - 1) + j, T), :]
    cb = cb * _sigmoid(cb)
    cb_bf = cb.astype(bf16)
    xb_bf = xb.astype(bf16)
    qs, ks, vs = [], [], []
    for h in range(H_B):
        hs = slice(h * DH_B, (h + 1) * DH_B)
        q = _dot(cb_bf[:, hs], wq_ref[h])
        k = _dot(cb_bf[:, hs], wk_ref[h]) * (DH_B ** -0.5)
        v = _dot(xb_bf[:, hs], wv_ref[h])
        qs.append(q)
        ks.append(k)
        vs.append(v)
        qkv_scr[:, (3 * h) * DH_B:(3 * h + 1) * DH_B] = q.astype(bf16)
        qkv_scr[:, (3 * h + 1) * DH_B:(3 * h + 2) * DH_B] = k.astype(bf16)
        qkv_scr[:, (3 * h + 2) * DH_B:(3 * h + 3) * DH_B] = v.astype(bf16)
    gates = _dot(qkv_scr[...], wg_ref[...]) + bg_ref[...]
    ig = gates[:, 0:LANES]
    lf = jax.nn.log_sigmoid(gates[:, LANES:2 * LANES])
    ri = lax.broadcasted_iota(jnp.int32, (L, L), 0)
    ci = lax.broadcasted_iota(jnp.int32, (L, L), 1)
    causal = ri >= ci
    tri = jnp.where(causal, 1.0, 0.0).astype(bf16)
    gn = gn_ref[...]
    skip = skip_ref[...]
    heads = range(H_B)
    stack = lambda parts: jnp.concatenate(parts, axis=0)
    gn_st = stack([jnp.broadcast_to(gn[:, h * DH_B:(h + 1) * DH_B], (L, DH_B)) for h in heads])
    for c in range(T // L):
        rows = slice(c * L, (c + 1) * L)
        b_all = _dot_split_rhs(tri, lf[rows], 3)
        c_all = ig[rows] - b_all
        c_all_t = c_all.T
        q_bf = [qs[h][rows].astype(bf16) for h in heads]
        k_bf = [ks[h][rows].astype(bf16) for h in heads]
        v_bf = [vs[h][rows].astype(bf16) for h in heads]
        C_old = [C_scr[h] for h in heads]
        n_old = [n_scr[h:h + 1, :] for h in heads]
        m_old = [m_scr[h:h + 1, 0:1] for h in heads]
        m_prev = stack([jnp.broadcast_to(m_old[h], (L, 1)) for h in heads])
        b_col = stack([b_all[:, h:h + 1] for h in heads])
        c_col = stack([c_all[:, h:h + 1] for h in heads])
        dm = stack([jnp.where(causal, c_all_t[h:h + 1, :], -jnp.inf) for h in heads])
        mx = jnp.maximum(m_prev, jnp.max(dm, axis=1, keepdims=True))
        sc = stack([_dot_nt(q_bf[h], k_bf[h]) for h in heads]) * jnp.exp(dm - mx)
        sc_bf = sc.astype(bf16)
        w_inter = jnp.exp(m_prev - mx)
        sv = stack([_dot(sc_bf[h * L:(h + 1) * L], v_bf[h]) for h in heads])
        qC = stack([_dot(q_bf[h], C_old[h].astype(bf16)) for h in heads])
        qn = jnp.sum(stack([qs[h][rows] * n_old[h] for h in heads]), axis=1, keepdims=True)
        num = sv + w_inter * qC
        den = jnp.sum(sc, axis=1, keepdims=True) + w_inter * qn
        hb = num / jnp.maximum(jnp.abs(den), jnp.exp(-(b_col + mx)))
        hb = _head_layernorm(hb, gn_st)
        mx_last = [mx[(h + 1) * L - 1:(h + 1) * L, :] for h in heads]
        g_in = jnp.exp(c_col - stack([jnp.broadcast_to(mx_last[h], (L, 1)) for h in heads]))
        for h in heads:
            hs = slice(h * DH_B, (h + 1) * DH_B)
            hr = slice(h * L, (h + 1) * L)
            b_out = _sigmoid(zb[rows, hs]) * (hb[hr] + skip[:, hs] * cb[rows, hs])
            mix_scr[rows, W_A + h * DH_B:W_A + (h + 1) * DH_B] = b_out.astype(bf16)
            gk = g_in[hr] * ks[h][rows]
            decay = jnp.exp(m_old[h] - mx_last[h])
            C_scr[h] = decay * C_old[h] + _dot_tn(gk.astype(bf16), v_bf[h])
            n_scr[h:h + 1, :] = decay * n_old[h] + jnp.sum(gk, axis=0, keepdims=True)
            m_scr[h:h + 1, :] = jnp.broadcast_to(b_all[L - 1:L, h:h + 1] + mx_last[h], (1, LANES))

    chist[pl.ds(S_HIST, T), :] = xc
    ccw = ccw_ref[...]
    xcv = ccb_ref[...] + ccw[3:4, :] * xc
    for j in range(CONV_C - 1):
        xcv = xcv + ccw[j:j + 1, :] * chist[pl.ds(S_HIST - (CONV_C - 1) + j, T), :]
    a_t, u_t = _rglru_gates(xcv, wrg_ref[...], brg_ref[...], lam_ref[...])
    row = lax.broadcasted_iota(jnp.int32, (T, W_C), 0)
    d = 1
    while d < T:
        if d < SUBLANES:
            a_sh = jnp.where(row >= d, pltpu.roll(a_t, d, axis=0), 1.0)
            u_sh = jnp.where(row >= d, pltpu.roll(u_t, d, axis=0), 0.0)
        else:
            a_sh = jnp.concatenate([jnp.ones((d, W_C), f32), a_t[:T - d]], axis=0)
            u_sh = jnp.concatenate([jnp.zeros((d, W_C), f32), u_t[:T - d]], axis=0)
        u_t = a_t * u_sh + u_t
        a_t = a_t * a_sh
        d *= 2
    hc = a_t * h_scr[...] + u_t
    h_scr[...] = hc[T - 1:T, :]
    mix_scr[:, W_A + W_B:] = (hc * jax.nn.gelu(gc)).astype(bf16)

    x1_ref[...] = x + _dot(mix_scr[...], wout_ref[...])

    ahist[0:A_HIST, :] = ahist[pl.ds(T, A_HIST), :]
    bhist[0:S_HIST, :] = bhist[pl.ds(T, S_HIST), :]
    chist[0:S_HIST, :] = chist[pl.ds(T, S_HIST), :]

    @pl.when(s == last)
    def _():
        sa_ref[...] = ahist[pl.ds(first, CONV_A - 1), :]
        sb_ref[...] = bhist[pl.ds(S_HIST - (CONV_B - 1), CONV_B - 1), :]
        sc_ref[...] = chist[pl.ds(S_HIST - (CONV_C - 1), CONV_C - 1), :]
        sC_ref[...] = C_scr[...]
        sn_ref[...] = n_scr[0:H_B, :]
        sm_ref[...] = m_scr[0:H_B, :]
        sh_ref[...] = h_scr[...]


def _full(shape):
    nd = len(shape)
    return pl.BlockSpec(shape, lambda *_: (0,) * nd)


def _mix_prompt(x, lw):
    B, S, _ = x.shape
    T = T_MIX
    weights = [lw[k] for k in ("norm_mix_g", "w_in", "conv_a_w", "conv_a_b", "ln_a_g", "ln_a_b", "conv_b_w", "conv_b_b",
                               "w_q", "w_k", "w_v", "w_gates", "b_gates", "gn_b_g", "skip_b", "conv_c_w", "conv_c_b",
                               "w_rg", "b_rg", "lam", "w_out")]
    out_shape = (
        jax.ShapeDtypeStruct((B, S, D_MODEL), f32),
        jax.ShapeDtypeStruct((B, CONV_A - 1, W_A), f32),
        jax.ShapeDtypeStruct((B, CONV_B - 1, W_B), f32),
        jax.ShapeDtypeStruct((B, H_B, DH_B, DH_B), f32),
        jax.ShapeDtypeStruct((B, H_B, DH_B), f32),
        jax.ShapeDtypeStruct((B, H_B, LANES), f32),
        jax.ShapeDtypeStruct((B, CONV_C - 1, W_C), f32),
        jax.ShapeDtypeStruct((B, 1, W_C), f32),
    )
    per_b = lambda shp: pl.BlockSpec((None,) + shp, lambda b, s: (b,) + (0,) * len(shp))
    out_specs = (
        pl.BlockSpec((None, T, D_MODEL), lambda b, s: (b, s, 0)),
        per_b((CONV_A - 1, W_A)), per_b((CONV_B - 1, W_B)), per_b((H_B, DH_B, DH_B)), per_b((H_B, DH_B)),
        per_b((H_B, LANES)), per_b((CONV_C - 1, W_C)), per_b((1, W_C)),
    )
    scratch = [
        pltpu.VMEM((A_HIST + T, W_A), f32), pltpu.VMEM((S_HIST + T, W_B), f32), pltpu.VMEM((S_HIST + T, W_C), f32),
        pltpu.VMEM((T, W_A), f32),
        pltpu.VMEM((H_B, DH_B, DH_B), f32), pltpu.VMEM((SUBLANES, DH_B), f32), pltpu.VMEM((SUBLANES, LANES), f32),
        pltpu.VMEM((1, W_C), f32),
        pltpu.VMEM((T, 3 * W_B), bf16), pltpu.VMEM((T, D_MODEL), bf16),
    ]
    return pl.pallas_call(
        _mix_prompt_body,
        grid=(B, S // T),
        in_specs=[pl.BlockSpec((None, T, D_MODEL), lambda b, s: (b, s, 0))] + [_full(w.shape) for w in weights],
        out_specs=out_specs,
        out_shape=out_shape,
        scratch_shapes=scratch,
        compiler_params=pltpu.CompilerParams(dimension_semantics=("arbitrary", "arbitrary"),
                                             vmem_limit_bytes=VMEM_LIMIT),
        name="mix_prompt",
    )(x, *weights)


def _step_pre_body(x_ref, g_ref, win_ref, sa_ref, caw_ref, cab_ref, lag_ref, lab_ref, sb_ref, cbw_ref, cbb_ref,
                   wq_ref, wk_ref, wv_ref, wg_ref, bg_ref, n_ref, m_ref, sc_ref, ccw_ref, ccb_ref,
                   wrg_ref, brg_ref, lam_ref, h_ref, skip_ref,
                   sa_o, sb_o, sc_o, n_o, m_o, h_o, q_o, gk_o, v_o, dec_o, den_o, sv_o, zsig_o, skcb_o, ac_o,
                   qkv_scr):
    x = x_ref[...]
    hn = _rmsnorm(x, g_ref[...])
    u = _dot(hn.astype(bf16), win_ref[...])
    xa, ga = u[:, 0:256], u[:, 256:512]
    xb, zb = u[:, 512:1024], u[:, 1024:1536]
    xc, gc = u[:, 1536:1792], u[:, 1792:2048]

    a_new = xa * _sigmoid(ga)
    caw = caw_ref[...]
    conv = cab_ref[...] + caw[CONV_A - 1:CONV_A, :] * a_new
    for j in range(CONV_A - 1):
        conv = conv + caw[j:j + 1, :] * sa_ref[:, j * W_A:(j + 1) * W_A]
    sa_o[:, 0:(CONV_A - 2) * W_A] = sa_ref[:, W_A:]
    sa_o[:, (CONV_A - 2) * W_A:] = a_new
    ac_o[:, 0:W_A] = _group_layernorm_silu(conv, _group_mask_a(), lag_ref[...], lab_ref[...])

    cbw = cbw_ref[...]
    cb = cbb_ref[...] + cbw[CONV_B - 1:CONV_B, :] * xb
    for j in range(CONV_B - 1):
        cb = cb + cbw[j:j + 1, :] * sb_ref[:, j * W_B:(j + 1) * W_B]
    sb_o[:, 0:(CONV_B - 2) * W_B] = sb_ref[:, W_B:]
    sb_o[:, (CONV_B - 2) * W_B:] = xb
    cb = cb * _sigmoid(cb)
    cb_bf = cb.astype(bf16)
    xb_bf = xb.astype(bf16)
    qs, ks, vs = [], [], []
    for h in range(H_B):
        hs = slice(h * DH_B, (h + 1) * DH_B)
        q = _dot(cb_bf[:, hs], wq_ref[h])
        k = _dot(cb_bf[:, hs], wk_ref[h]) * (DH_B ** -0.5)
        v = _dot(xb_bf[:, hs], wv_ref[h])
        qs.append(q)
        ks.append(k)
        vs.append(v)
        qkv_scr[:, (3 * h) * DH_B:(3 * h + 1) * DH_B] = q.astype(bf16)
        qkv_scr[:, (3 * h + 1) * DH_B:(3 * h + 2) * DH_B] = k.astype(bf16)
        qkv_scr[:, (3 * h + 2) * DH_B:(3 * h + 3) * DH_B] = v.astype(bf16)
    gates = _dot(qkv_scr[...], wg_ref[...]) + bg_ref[...]
    ig = gates[:, 0:LANES]
    lf = jax.nn.log_sigmoid(gates[:, LANES:2 * LANES])
    m0 = m_ref[...]
    m_t = jnp.maximum(lf + m0, ig)
    g_in = jnp.exp(ig - m_t)
    decay = jnp.exp(lf + m0 - m_t)
    lane = lax.broadcasted_iota(jnp.int32, ig.shape, 1)
    qk = jnp.zeros(ig.shape, f32)
    qn = jnp.zeros(ig.shape, f32)
    for h in range(H_B):
        hs = slice(h * DH_B, (h + 1) * DH_B)
        n_h = n_ref[:, hs]
        qk = jnp.where(lane == h, jnp.sum(qs[h] * ks[h], axis=-1, keepdims=True), qk)
        qn = jnp.where(lane == h, jnp.sum(qs[h] * n_h, axis=-1, keepdims=True), qn)
        gk = g_in[:, h:h + 1] * ks[h]
        n_o[:, hs] = decay[:, h:h + 1] * n_h + gk
        q_o[:, hs] = qs[h]
        gk_o[:, hs] = gk
        v_o[:, hs] = vs[h]
    s_t = qk * g_in
    den = s_t + decay * qn
    den_o[...] = jnp.maximum(jnp.abs(den), jnp.exp(-m_t))
    dec_o[...] = decay
    m_o[...] = m_t
    for h in range(H_B):
        hs = slice(h * DH_B, (h + 1) * DH_B)
        sv_o[:, hs] = s_t[:, h:h + 1] * vs[h]
    zsig_o[...] = _sigmoid(zb)
    skcb_o[...] = skip_ref[...] * cb

    ccw = ccw_ref[...]
    xcv = ccb_ref[...] + ccw[CONV_C - 1:CONV_C, :] * xc
    for j in range(CONV_C - 1):
        xcv = xcv + ccw[j:j + 1, :] * sc_ref[:, j * W_C:(j + 1) * W_C]
    sc_o[:, 0:(CONV_C - 2) * W_C] = sc_ref[:, W_C:]
    sc_o[:, (CONV_C - 2) * W_C:] = xc
    a_t, u_t = _rglru_gates(xcv, wrg_ref[...], brg_ref[...], lam_ref[...])
    hc = a_t * h_ref[...] + u_t
    h_o[...] = hc
    ac_o[:, W_A:] = hc * jax.nn.gelu(gc)


def _step_read_body(C_ref, qT_ref, qc_ref):
    for bb in range(STEP_BB):
        for h in range(H_B):
            hs = slice(h * DH_B, (h + 1) * DH_B)
            qc_ref[bb:bb + 1, hs] = jnp.sum(C_ref[bb, h] * qT_ref[h, :, bb:bb + 1], axis=0, keepdims=True)


def _step_state_body(C_ref, qT_ref, gkT_ref, v_ref, dec_ref, Cn_ref, qc_ref):
    for bb in range(STEP_BB):
        for h in range(H_B):
            hs = slice(h * DH_B, (h + 1) * DH_B)
            C = C_ref[bb, h]
            qc_ref[bb:bb + 1, hs] = jnp.sum(C * qT_ref[h, :, bb:bb + 1], axis=0, keepdims=True)
            Cn_ref[bb, h] = dec_ref[bb:bb + 1, h:h + 1] * C + gkT_ref[h, :, bb:bb + 1] * v_ref[bb:bb + 1, hs]


def _to_cols(t):
    return t.reshape(t.shape[0] // STEP_BB, STEP_BB, H_B, DH_B).transpose(0, 2, 3, 1)


def _step_read(C_all, layer, q):
    Bs = q.shape[0]
    return pl.pallas_call(
        _step_read_body,
        grid=(Bs // STEP_BB,),
        in_specs=[pl.BlockSpec((None, STEP_BB, H_B, DH_B, DH_B), lambda i: (layer, i, 0, 0, 0)),
                  pl.BlockSpec((None, H_B, DH_B, STEP_BB), lambda i: (i, 0, 0, 0))],
        out_specs=pl.BlockSpec((STEP_BB, W_B), lambda i: (i, 0)),
        out_shape=jax.ShapeDtypeStruct((Bs, W_B), f32),
        compiler_params=pltpu.CompilerParams(dimension_semantics=("arbitrary",), vmem_limit_bytes=VMEM_LIMIT),
        name="step_read",
    )(C_all, _to_cols(q))


def _step_state(C_all, q, gk, v, dec):
    depth, Bs = q.shape[0], q.shape[1]
    cols = lambda t: jnp.stack([_to_cols(t[l]) for l in range(depth)])
    blk5 = pl.BlockSpec((None, STEP_BB, H_B, DH_B, DH_B), lambda l, i: (l, i, 0, 0, 0))
    colspec = pl.BlockSpec((None, None, H_B, DH_B, STEP_BB), lambda l, i: (l, i, 0, 0, 0))
    return pl.pallas_call(
        _step_state_body,
        grid=(depth, Bs // STEP_BB),
        in_specs=[blk5, colspec, colspec,
                  pl.BlockSpec((None, STEP_BB, W_B), lambda l, i: (l, i, 0)),
                  pl.BlockSpec((None, STEP_BB, LANES), lambda l, i: (l, i, 0))],
        out_specs=(blk5, pl.BlockSpec((None, STEP_BB, W_B), lambda l, i: (l, i, 0))),
        out_shape=(jax.ShapeDtypeStruct(C_all.shape, f32), jax.ShapeDtypeStruct((depth, Bs, W_B), f32)),
        compiler_params=pltpu.CompilerParams(dimension_semantics=("arbitrary", "arbitrary"),
                                             vmem_limit_bytes=VMEM_LIMIT),
        name="step_state",
    )(C_all, cols(q), cols(gk), v, dec)


def _step_post_body(x_ref, ac_ref, zsig_ref, skcb_ref, sv_ref, qc_ref, dec_ref, den_ref, gn_ref, wout_ref,
                    x1_ref, mix_scr):
    gn = gn_ref[...]
    dec = dec_ref[...]
    den = den_ref[...]
    mix_scr[:, 0:W_A] = ac_ref[:, 0:W_A].astype(bf16)
    mix_scr[:, W_A + W_B:] = ac_ref[:, W_A:].astype(bf16)
    for h in range(H_B):
        hs = slice(h * DH_B, (h + 1) * DH_B)
        num = sv_ref[:, hs] + dec[:, h:h + 1] * qc_ref[:, hs]
        hb = _head_layernorm(num / den[:, h:h + 1], gn[:, hs])
        mix_scr[:, W_A + h * DH_B:W_A + (h + 1) * DH_B] = (zsig_ref[:, hs] * (hb + skcb_ref[:, hs])).astype(bf16)
    x1_ref[...] = x_ref[...] + _dot(mix_scr[...], wout_ref[...])


def _step_pre(x, st, lw):
    buf_a, buf_b, n0, m0, buf_c, hc0 = st
    Bs = x.shape[0]
    sa = buf_a.reshape(Bs, (CONV_A - 1) * W_A)
    sb = buf_b.reshape(Bs, (CONV_B - 1) * W_B)
    sc = buf_c.reshape(Bs, (CONV_C - 1) * W_C)
    n_in = n0.reshape(Bs, W_B)
    m_in = jnp.pad(m0, ((0, 0), (0, LANES - H_B)))
    sds = lambda *shape: jax.ShapeDtypeStruct(shape, f32)
    pre_in = [x, lw["norm_mix_g"], lw["w_in"], sa, lw["conv_a_w"], lw["conv_a_b"], lw["ln_a_g"], lw["ln_a_b"],
              sb, lw["conv_b_w"], lw["conv_b_b"], lw["w_q"], lw["w_k"], lw["w_v"], lw["w_gates"], lw["b_gates"],
              n_in, m_in, sc, lw["conv_c_w"], lw["conv_c_b"], lw["w_rg"], lw["b_rg"], lw["lam"], hc0, lw["skip_b"]]
    (sa_n, sb_n, sc_n, n_n, m_n, h_n, q, gk, v, dec, den, sv, zsig, skcb, ac) = pl.pallas_call(
        _step_pre_body,
        out_shape=(sds(*sa.shape), sds(*sb.shape), sds(*sc.shape), sds(Bs, W_B), sds(Bs, LANES), sds(Bs, W_C),
                   sds(Bs, W_B), sds(Bs, W_B), sds(Bs, W_B), sds(Bs, LANES), sds(Bs, LANES), sds(Bs, W_B),
                   sds(Bs, W_B), sds(Bs, W_B), sds(Bs, W_A + W_C)),
        scratch_shapes=[pltpu.VMEM((Bs, 3 * W_B), bf16)],
        compiler_params=pltpu.CompilerParams(vmem_limit_bytes=VMEM_LIMIT),
        name="step_pre",
    )(*pre_in)
    new = (sa_n.reshape(buf_a.shape), sb_n.reshape(buf_b.shape), n_n.reshape(n0.shape), m_n[:, :H_B],
           sc_n.reshape(buf_c.shape), h_n)
    return new, (q, gk, v, dec), (ac, zsig, skcb, sv, dec, den)


def _step_post(x, post_in, qc, lw):
    ac, zsig, skcb, sv, dec, den = post_in
    return pl.pallas_call(
        _step_post_body,
        out_shape=jax.ShapeDtypeStruct(x.shape, f32),
        scratch_shapes=[pltpu.VMEM(x.shape, bf16)],
        compiler_params=pltpu.CompilerParams(vmem_limit_bytes=VMEM_LIMIT),
        name="step_post",
    )(x, ac, zsig, skcb, sv, qc, dec, den, lw["gn_b_g"], lw["w_out"])


def _ffn_body(*refs, moe, final_norm):
    it = iter(refs)
    x_ref, g_ref, wg_ref, wu_ref, wd_ref = next(it), next(it), next(it), next(it), next(it)
    wr_ref = next(it) if moe else None
    br_ref = next(it) if moe else None
    gf_ref = next(it) if final_norm else None
    o_ref, hn_scr, acc_scr = next(it), next(it), next(it)
    comb_scr = next(it) if moe else None
    j = pl.program_id(1)

    @pl.when(j == 0)
    def _():
        hn = _rmsnorm(x_ref[...], g_ref[...])
        hn_scr[...] = hn.astype(bf16)
        acc_scr[...] = jnp.zeros(acc_scr.shape, f32)
        if moe:
            h1, h2 = _split_terms(hn, 2)
            w1, w2 = _split_terms(wr_ref[...], 2)
            logits = _dot(h1, w1) + (_dot(h1, w2) + _dot(h2, w1)) + br_ref[...]
            lane = lax.broadcasted_iota(jnp.int32, logits.shape, 1).astype(f32)
            lg = jnp.where(lane < N_EXPERTS, logits, -jnp.inf)
            m1 = jnp.max(lg, axis=-1, keepdims=True)
            i1 = jnp.min(jnp.where(lg == m1, lane, float(LANES)), axis=-1, keepdims=True)
            lg2 = jnp.where(lane == i1, -jnp.inf, lg)
            m2 = jnp.max(lg2, axis=-1, keepdims=True)
            i2 = jnp.min(jnp.where(lg2 == m2, lane, float(LANES)), axis=-1, keepdims=True)
            e2 = jnp.exp(m2 - m1)
            comb_scr[...] = jnp.where(lane == i1, 1.0 / (1.0 + e2), 0.0) + jnp.where(lane == i2, e2 / (1.0 + e2), 0.0)

    hb = hn_scr[...]
    gt = _dot(hb, wg_ref[...])
    up = _dot(hb, wu_ref[...])
    act = (gt * _sigmoid(gt) * up).astype(bf16)
    y = _dot(act, wd_ref[...])
    if moe:
        lane = lax.broadcasted_iota(jnp.int32, comb_scr.shape, 1)
        y = y * jnp.sum(jnp.where(lane == j, comb_scr[...], 0.0), axis=-1, keepdims=True)
    acc_scr[...] += y

    @pl.when(j == pl.num_programs(1) - 1)
    def _():
        out = x_ref[...] + acc_scr[...]
        if final_norm:
            out = _rmsnorm(out, gf_ref[...])
        o_ref[...] = out


def _ffn(x, g, wg, wu, wd, chunks, router=None, final_g=None):
    N = x.shape[0]
    E = wg.shape[0]
    F = wg.shape[2] // chunks
    tm = min(TM_FFN, N)
    moe = router is not None
    assert not moe or chunks == 1
    final_norm = final_g is not None
    ins = [x, g, wg, wu, wd]
    in_specs = [pl.BlockSpec((tm, D_MODEL), lambda i, j: (i, 0)), _full(g.shape),
                pl.BlockSpec((None, D_MODEL, F), lambda i, j: (j // chunks, 0, j % chunks)),
                pl.BlockSpec((None, D_MODEL, F), lambda i, j: (j // chunks, 0, j % chunks)),
                pl.BlockSpec((None, F, D_MODEL), lambda i, j: (j // chunks, j % chunks, 0))]
    scratch = [pltpu.VMEM((tm, D_MODEL), bf16), pltpu.VMEM((tm, D_MODEL), f32)]
    if moe:
        ins += list(router)
        in_specs += [_full(router[0].shape), _full(router[1].shape)]
        scratch.append(pltpu.VMEM((tm, LANES), f32))
    if final_norm:
        ins.append(final_g)
        in_specs.append(_full(final_g.shape))
    return pl.pallas_call(
        functools.partial(_ffn_body, moe=moe, final_norm=final_norm),
        grid=(N // tm, E * chunks),
        in_specs=in_specs,
        out_specs=pl.BlockSpec((tm, D_MODEL), lambda i, j: (i, 0)),
        out_shape=jax.ShapeDtypeStruct((N, D_MODEL), f32),
        scratch_shapes=scratch,
        compiler_params=pltpu.CompilerParams(dimension_semantics=("arbitrary", "arbitrary"),
                                             vmem_limit_bytes=VMEM_LIMIT),
        name="ffn_moe" if moe else "ffn_dense",
    )(*ins)


def _block_diag(w):
    h, di, do = w.shape
    return jnp.einsum("hij,hg->higj", w, jnp.eye(h, dtype=w.dtype)).reshape(h * di, h * do)


def _layer_weights(l, p):
    row = lambda v: v.reshape(1, -1)
    w_gates = jnp.zeros((3 * W_B, 2 * LANES), f32)
    w_gates = w_gates.at[:, 0:H_B].set(p["w_ig"][l]).at[:, LANES:LANES + H_B].set(p["w_fg"][l])
    b_gates = jnp.zeros((1, 2 * LANES), f32)
    b_gates = b_gates.at[0, 0:H_B].set(p["b_ig"][l]).at[0, LANES:LANES + H_B].set(p["b_fg"][l])
    return dict(
        norm_mix_g=row(p["norm_mix_g"][l]), w_in=p["w_in"][l].astype(bf16),
        conv_a_w=p["conv_a_w"][l], conv_a_b=row(p["conv_a_b"][l]), ln_a_g=row(p["ln_a_g"][l]), ln_a_b=row(p["ln_a_b"][l]),
        conv_b_w=p["conv_b_w"][l], conv_b_b=row(p["conv_b_b"][l]),
        w_q=p["w_q"][l].astype(bf16), w_k=p["w_k"][l].astype(bf16), w_v=p["w_v"][l].astype(bf16),
        w_gates=w_gates.astype(bf16), b_gates=b_gates, gn_b_g=row(p["gn_b_g"][l]), skip_b=row(p["skip_b"][l]),
        conv_c_w=p["conv_c_w"][l], conv_c_b=row(p["conv_c_b"][l]),
        w_rg=jnp.concatenate([_block_diag(p["w_ra"][l]), _block_diag(p["w_ix"][l])], axis=1).astype(bf16),
        b_rg=jnp.concatenate([p["b_ra"][l], p["b_ix"][l]]).reshape(1, -1), lam=row(p["lam"][l]),
        w_out=p["w_out"][l].astype(bf16),
    )


def kernel(x_prompt, x_sample, state_conv_a, state_conv_b, state_mlstm_C, state_mlstm_n, state_mlstm_m, state_conv_c, state_rglru_h, norm_mix_g, w_in, conv_a_w, conv_a_b, ln_a_g, ln_a_b, conv_b_w, conv_b_b, w_q, w_k, w_v, w_ig, b_ig, w_fg, b_fg, gn_b_g, skip_b, conv_c_w, conv_c_b, w_ra, b_ra, w_ix, b_ix, lam, w_out, norm_ffn_g, w_gate, w_up, w_down, w_router, b_router, we_gate, we_up, we_down, norm_final_g):
    p = dict(norm_mix_g=norm_mix_g, w_in=w_in, conv_a_w=conv_a_w, conv_a_b=conv_a_b, ln_a_g=ln_a_g, ln_a_b=ln_a_b,
             conv_b_w=conv_b_w, conv_b_b=conv_b_b, w_q=w_q, w_k=w_k, w_v=w_v, w_ig=w_ig, b_ig=b_ig, w_fg=w_fg,
             b_fg=b_fg, gn_b_g=gn_b_g, skip_b=skip_b, conv_c_w=conv_c_w, conv_c_b=conv_c_b, w_ra=w_ra, b_ra=b_ra,
             w_ix=w_ix, b_ix=b_ix, lam=lam, w_out=w_out)
    depth = w_in.shape[0]
    Bp, S, _ = x_prompt.shape
    Bs = x_sample.shape[0]
    sample_states = (state_conv_a, state_conv_b, state_mlstm_C, state_mlstm_n, state_mlstm_m, state_conv_c,
                     state_rglru_h)
    xp = x_prompt
    xs = x_sample.reshape(Bs, D_MODEL)
    new_p, new_s, mem_ops = [], [], []
    C_s = None
    for l in range(depth):
        lw = _layer_weights(l, p)
        xp, sa, sb, sC, sn, sm, sc, sh = _mix_prompt(xp, lw)
        new_p.append((sa, sb, sC, sn, sm[:, :, 0], sc, sh.reshape(Bp, W_C)))
        small, mem, post_in = _step_pre(xs, tuple(s[l] for i, s in enumerate(sample_states) if i != 2), lw)
        new_s.append(small)
        mem_ops.append(mem)
        if l < depth - 1:
            qc = _step_read(state_mlstm_C, l, mem[0])
        else:
            C_s, qc_all = _step_state(state_mlstm_C, *(jnp.stack([m[i] for m in mem_ops]) for i in range(4)))
            qc = qc_all[l]
        xs = _step_post(xs, post_in, qc, lw)
        g = norm_ffn_g[l].reshape(1, -1)
        final_g = norm_final_g.reshape(1, -1) if l == depth - 1 else None
        j = l // 2
        if l % 2 == 0:
            wg, wu, wd = (w[j:j + 1].astype(bf16) for w in (w_gate, w_up, w_down))
            chunks, router = 2, None
        else:
            wg, wu, wd = we_gate[j].astype(bf16), we_up[j].astype(bf16), we_down[j].astype(bf16)
            chunks = 1
            router = (jnp.pad(w_router[j], ((0, 0), (0, LANES - N_EXPERTS))),
                      jnp.pad(b_router[j], (0, LANES - N_EXPERTS)).reshape(1, -1))
        xp = _ffn(xp.reshape(Bp * S, D_MODEL), g, wg, wu, wd, chunks, router, final_g).reshape(Bp, S, D_MODEL)
        xs = _ffn(xs, g, wg, wu, wd, chunks, router, final_g)
    stack = lambda states, i: jnp.stack([st[i] for st in states])
    out_p = tuple(stack(new_p, i) for i in range(7))
    sa_s, sb_s, n_s, m_s, sc_s, h_s = (stack(new_s, i) for i in range(6))
    return (xp, xs.reshape(Bs, 1, D_MODEL)) + out_p + (sa_s, sb_s, C_s, n_s, m_s, sc_s, h_s)
```

```python
import functools

import jax
import jax.numpy as jnp
from jax import lax
from jax.experimental import pallas as pl
from jax.experimental.pallas import tpu as pltpu

f32 = jnp.float32
bf16 = jnp.bfloat16

D_MODEL = 1024
W_A, W_B, W_C = 256, 512, 256
G_A = 4
CONV_A, CONV_B, CONV_C = 31, 4, 4
H_B, DH_B = 4, 128
H_C, DH_C = 4, 64
RG_C = 8.0
D_IN = 2048
N_EXPERTS = 8
NORM_EPS = 1e-6
LN_EPS = 1e-5

LANES = 128
SUBLANES = 8
VMEM_LIMIT = 56 * 1024 * 1024

T_MIX = 512
L_CHUNK = 128
CONV_ROWS = 128
A_HIST = 32
S_HIST = 8
TM_FFN = 512
STEP_BB = 8


def _dot(a, b):
    return jnp.dot(a, b, preferred_element_type=f32)


def _dot_nt(a, b):
    return lax.dot_general(a, b, (((1,), (1,)), ((), ())), preferred_element_type=f32)


def _dot_tn(a, b):
    return lax.dot_general(a, b, (((0,), (0,)), ((), ())), preferred_element_type=f32)


def _split_terms(x, terms):
    out = []
    r = x
    for _ in range(terms - 1):
        p = r.astype(bf16)
        out.append(p)
        r = r - p.astype(f32)
    out.append(r.astype(bf16))
    return out


def _dot_split_lhs(x, m, terms):
    acc = None
    for p in _split_terms(x, terms):
        d = _dot(p, m)
        acc = d if acc is None else acc + d
    return acc


def _dot_split_rhs(m, x, terms):
    acc = None
    for p in _split_terms(x, terms):
        d = _dot(m, p)
        acc = d if acc is None else acc + d
    return acc


def _sigmoid(x):
    return 0.5 * (1.0 + jnp.tanh(0.5 * x))


def _rmsnorm(x, g):
    return x * lax.rsqrt(jnp.mean(x * x, axis=-1, keepdims=True) + NORM_EPS) * g


def _group_mask_a():
    r = lax.broadcasted_iota(jnp.int32, (W_A, W_A), 0) // (W_A // G_A)
    c = lax.broadcasted_iota(jnp.int32, (W_A, W_A), 1) // (W_A // G_A)
    return jnp.where(r == c, 1.0, 0.0).astype(bf16)


def _group_layernorm_silu(a, gm, g, b):
    inv = 1.0 / (W_A // G_A)
    mu = _dot_split_lhs(a, gm, 2) * inv
    d = a - mu
    var = _dot((d * d).astype(bf16), gm) * inv
    y = d * lax.rsqrt(var + LN_EPS) * g + b
    return y * _sigmoid(y)


def _head_layernorm(h, g):
    mu = jnp.mean(h, axis=-1, keepdims=True)
    d = h - mu
    var = jnp.mean(d * d, axis=-1, keepdims=True)
    return d * lax.rsqrt(var + LN_EPS) * g


def _rglru_gates(xcv, wrg, brg, lam):
    gates = _dot(xcv.astype(bf16), wrg) + brg
    r = _sigmoid(gates[:, :W_C])
    i = _sigmoid(gates[:, W_C:])
    log_a = RG_C * r * jax.nn.log_sigmoid(lam)
    a = jnp.exp(log_a)
    t = jnp.tanh(log_a)
    one_minus_a2 = -2.0 * t / (1.0 - t)
    u = jnp.sqrt(one_minus_a2) * i * xcv
    return a, u


def _mix_prompt_body(x_ref, g_ref, win_ref, caw_ref, cab_ref, lag_ref, lab_ref, cbw_ref, cbb_ref,
                     wq_ref, wk_ref, wv_ref, wg_ref, bg_ref, gn_ref, skip_ref, ccw_ref, ccb_ref,
                     wrg_ref, brg_ref, lam_ref, wout_ref,
                     x1_ref, sa_ref, sb_ref, sC_ref, sn_ref, sm_ref, sc_ref, sh_ref,
                     ahist, bhist, chist, conv_scr, C_scr, n_scr, m_scr, h_scr, qkv_scr, mix_scr):
    T, L = T_MIX, L_CHUNK
    s = pl.program_id(1)
    last = pl.num_programs(1) - 1

    @pl.when(s == 0)
    def _():
        ahist[0:A_HIST, :] = jnp.zeros((A_HIST, W_A), f32)
        bhist[0:S_HIST, :] = jnp.zeros((S_HIST, W_B), f32)
        chist[0:S_HIST, :] = jnp.zeros((S_HIST, W_C), f32)
        C_scr[...] = jnp.zeros(C_scr.shape, f32)
        n_scr[...] = jnp.zeros(n_scr.shape, f32)
        m_scr[...] = jnp.zeros(m_scr.shape, f32)
        h_scr[...] = jnp.zeros(h_scr.shape, f32)

    x = x_ref[...]
    hn = _rmsnorm(x, g_ref[...])
    u = _dot(hn.astype(bf16), win_ref[...])
    xa, ga = u[:, 0:256], u[:, 256:512]
    xb, zb = u[:, 512:1024], u[:, 1024:1536]
    xc, gc = u[:, 1536:1792], u[:, 1792:2048]

    ahist[pl.ds(A_HIST, T), :] = xa * _sigmoid(ga)
    caw = caw_ref[...]
    cab = cab_ref[...]
    first = A_HIST - (CONV_A - 1)

    def conv_block(r, carry):
        base = pl.multiple_of(r * CONV_ROWS, CONV_ROWS)
        for lo in range(0, W_A, LANES):
            blk = ahist[pl.ds(base, CONV_ROWS + A_HIST), lo:lo + LANES]
            acc = jnp.broadcast_to(cab[:, lo:lo + LANES], (CONV_ROWS, LANES))
            for rr in range(SUBLANES):
                n = CONV_ROWS if rr == 0 else CONV_ROWS + SUBLANES
                z = None
                for q in range(A_HIST // SUBLANES + 1):
                    j = SUBLANES * q + rr - first
                    if 0 <= j < CONV_A:
                        term = caw[j:j + 1, lo:lo + LANES] * blk[SUBLANES * q:SUBLANES * q + n, :]
                        z = term if z is None else z + term
                acc = acc + (z if rr == 0 else z[rr:rr + CONV_ROWS, :])
            conv_scr[pl.ds(base, CONV_ROWS), lo:lo + LANES] = acc
        return carry

    lax.fori_loop(0, T // CONV_ROWS, conv_block, 0)
    a_out = _group_layernorm_silu(conv_scr[...], _group_mask_a(), lag_ref[...], lab_ref[...])
    mix_scr[:, 0:W_A] = a_out.astype(bf16)

    bhist[pl.ds(S_HIST, T), :] = xb
    cbw = cbw_ref[...]
    cb = cbb_ref[...] + cbw[3:4, :] * xb
    for j in range(CONV_B - 1):
        cb = cb + cbw[j:j + 1, :] * bhist[pl.ds(S_HIST - (CONV_B - 1) + j, T), :]
    cb = cb * _sigmoid(cb)
    cb_bf = cb.astype(bf16)
    xb_bf = xb.astype(bf16)
    qs, ks, vs = [], [], []
    for h in range(H_B):
        hs = slice(h * DH_B, (h + 1) * DH_B)
        q = _dot(cb_bf[:, hs], wq_ref[h])
        k = _dot(cb_bf[:, hs], wk_ref[h]) * (DH_B ** -0.5)
        v = _dot(xb_bf[:, hs], wv_ref[h])
        qs.append(q)
        ks.append(k)
        vs.append(v)
        qkv_scr[:, (3 * h) * DH_B:(3 * h + 1) * DH_B] = q.astype(bf16)
        qkv_scr[:, (3 * h + 1) * DH_B:(3 * h + 2) * DH_B] = k.astype(bf16)
        qkv_scr[:, (3 * h + 2) * DH_B:(3 * h + 3) * DH_B] = v.astype(bf16)
    gates = _dot(qkv_scr[...], wg_ref[...]) + bg_ref[...]
    ig = gates[:, 0:LANES]
    lf = jax.nn.log_sigmoid(gates[:, LANES:2 * LANES])
    ri = lax.broadcasted_iota(jnp.int32, (L, L), 0)
    ci = lax.broadcasted_iota(jnp.int32, (L, L), 1)
    causal = ri >= ci
    tri = jnp.where(causal, 1.0, 0.0).astype(bf16)
    gn = gn_ref[...]
    skip = skip_ref[...]
    heads = range(H_B)
    stack = lambda parts: jnp.concatenate(parts, axis=0)
    gn_st = stack([jnp.broadcast_to(gn[:, h * DH_B:(h + 1) * DH_B], (L, DH_B)) for h in heads])
    for c in range(T // L):
        rows = slice(c * L, (c + 1) * L)
        b_all = _dot_split_rhs(tri, lf[rows], 3)
        c_all = ig[rows] - b_all
        c_all_t = c_all.T
        q_bf = [qs[h][rows].astype(bf16) for h in heads]
        k_bf = [ks[h][rows].astype(bf16) for h in heads]
        v_bf = [vs[h][rows].astype(bf16) for h in heads]
        C_old = [C_scr[h] for h in heads]
        n_old = [n_scr[h:h + 1, :] for h in heads]
        m_old = [m_scr[h:h + 1, 0:1] for h in heads]
        m_prev = stack([jnp.broadcast_to(m_old[h], (L, 1)) for h in heads])
        b_col = stack([b_all[:, h:h + 1] for h in heads])
        c_col = stack([c_all[:, h:h + 1] for h in heads])
        dm = stack([jnp.where(causal, c_all_t[h:h + 1, :], -jnp.inf) for h in heads])
        mx = jnp.maximum(m_prev, jnp.max(dm, axis=1, keepdims=True))
        sc = stack([_dot_nt(q_bf[h], k_bf[h]) for h in heads]) * jnp.exp(dm - mx)
        sc_bf = sc.astype(bf16)
        w_inter = jnp.exp(m_prev - mx)
        sv = stack([_dot(sc_bf[h * L:(h + 1) * L], v_bf[h]) for h in heads])
        qC = stack([_dot(q_bf[h], C_old[h].astype(bf16)) for h in heads])
        qn = jnp.sum(stack([qs[h][rows] * n_old[h] for h in heads]), axis=1, keepdims=True)
        num = sv + w_inter * qC
        den = jnp.sum(sc, axis=1, keepdims=True) + w_inter * qn
        hb = num / jnp.maximum(jnp.abs(den), jnp.exp(-(b_col + mx)))
        hb = _head_layernorm(hb, gn_st)
        mx_last = [mx[(h + 1) * L - 1:(h + 1) * L, :] for h in heads]
        g_in = jnp.exp(c_col - stack([jnp.broadcast_to(mx_last[h], (L, 1)) for h in heads]))
        for h in heads:
            hs = slice(h * DH_B, (h + 1) * DH_B)
            hr = slice(h * L, (h + 1) * L)
            b_out = _sigmoid(zb[rows, hs]) * (hb[hr] + skip[:, hs] * cb[rows, hs])
            mix_scr[rows, W_A + h * DH_B:W_A + (h + 1) * DH_B] = b_out.astype(bf16)
            gk = g_in[hr] * ks[h][rows]
            decay = jnp.exp(m_old[h] - mx_last[h])
            C_scr[h] = decay * C_old[h] + _dot_tn(gk.astype(bf16), v_bf[h])
            n_scr[h:h + 1, :] = decay * n_old[h] + jnp.sum(gk, axis=0, keepdims=True)
            m_scr[h:h + 1, :] = jnp.broadcast_to(b_all[L - 1:L, h:h + 1] + mx_last[h], (1, LANES))

    chist[pl.ds(S_HIST, T), :] = xc
    ccw = ccw_ref[...]
    xcv = ccb_ref[...] + ccw[3:4, :] * xc
    for j in range(CONV_C - 1):
        xcv = xcv + ccw[j:j + 1, :] * chist[pl.ds(S_HIST - (CONV_C - 1) + j, T), :]
    a_t, u_t = _rglru_gates(xcv, wrg_ref[...], brg_ref[...], lam_ref[...])
    row = lax.broadcasted_iota(jnp.int32, (T, W_C), 0)
    d = 1
    while d < T:
        if d < SUBLANES:
            a_sh = jnp.where(row >= d, pltpu.roll(a_t, d, axis=0), 1.0)
            u_sh = jnp.where(row >= d, pltpu.roll(u_t, d, axis=0), 0.0)
        else:
            a_sh = jnp.concatenate([jnp.ones((d, W_C), f32), a_t[:T - d]], axis=0)
            u_sh = jnp.concatenate([jnp.zeros((d, W_C), f32), u_t[:T - d]], axis=0)
        u_t = a_t * u_sh + u_t
        a_t = a_t * a_sh
        d *= 2
    hc = a_t * h_scr[...] + u_t
    h_scr[...] = hc[T - 1:T, :]
    mix_scr[:, W_A + W_B:] = (hc * jax.nn.gelu(gc)).astype(bf16)

    x1_ref[...] = x + _dot(mix_scr[...], wout_ref[...])

    ahist[0:A_HIST, :] = ahist[pl.ds(T, A_HIST), :]
    bhist[0:S_HIST, :] = bhist[pl.ds(T, S_HIST), :]
    chist[0:S_HIST, :] = chist[pl.ds(T, S_HIST), :]

    @pl.when(s == last)
    def _():
        sa_ref[...] = ahist[pl.ds(first, CONV_A - 1), :]
        sb_ref[...] = bhist[pl.ds(S_HIST - (CONV_B - 1), CONV_B - 1), :]
        sc_ref[...] = chist[pl.ds(S_HIST - (CONV_C - 1), CONV_C - 1), :]
        sC_ref[...] = C_scr[...]
        sn_ref[...] = n_scr[0:H_B, :]
        sm_ref[...] = m_scr[0:H_B, :]
        sh_ref[...] = h_scr[...]


def _full(shape):
    nd = len(shape)
    return pl.BlockSpec(shape, lambda *_: (0,) * nd)


def _mix_prompt(x, lw):
    B, S, _ = x.shape
    T = T_MIX
    weights = [lw[k] for k in ("norm_mix_g", "w_in", "conv_a_w", "conv_a_b", "ln_a_g", "ln_a_b", "conv_b_w", "conv_b_b",
                               "w_q", "w_k", "w_v", "w_gates", "b_gates", "gn_b_g", "skip_b", "conv_c_w", "conv_c_b",
                               "w_rg", "b_rg", "lam", "w_out")]
    out_shape = (
        jax.ShapeDtypeStruct((B, S, D_MODEL), f32),
        jax.ShapeDtypeStruct((B, CONV_A - 1, W_A), f32),
        jax.ShapeDtypeStruct((B, CONV_B - 1, W_B), f32),
        jax.ShapeDtypeStruct((B, H_B, DH_B, DH_B), f32),
        jax.ShapeDtypeStruct((B, H_B, DH_B), f32),
        jax.ShapeDtypeStruct((B, H_B, LANES), f32),
        jax.ShapeDtypeStruct((B, CONV_C - 1, W_C), f32),
        jax.ShapeDtypeStruct((B, 1, W_C), f32),
    )
    per_b = lambda shp: pl.BlockSpec((None,) + shp, lambda b, s: (b,) + (0,) * len(shp))
    out_specs = (
        pl.BlockSpec((None, T, D_MODEL), lambda b, s: (b, s, 0)),
        per_b((CONV_A - 1, W_A)), per_b((CONV_B - 1, W_B)), per_b((H_B, DH_B, DH_B)), per_b((H_B, DH_B)),
        per_b((H_B, LANES)), per_b((CONV_C - 1, W_C)), per_b((1, W_C)),
    )
    scratch = [
        pltpu.VMEM((A_HIST + T, W_A), f32), pltpu.VMEM((S_HIST + T, W_B), f32), pltpu.VMEM((S_HIST + T, W_C), f32),
        pltpu.VMEM((T, W_A), f32),
        pltpu.VMEM((H_B, DH_B, DH_B), f32), pltpu.VMEM((SUBLANES, DH_B), f32), pltpu.VMEM((SUBLANES, LANES), f32),
        pltpu.VMEM((1, W_C), f32),
        pltpu.VMEM((T, 3 * W_B), bf16), pltpu.VMEM((T, D_MODEL), bf16),
    ]
    return pl.pallas_call(
        _mix_prompt_body,
        grid=(B, S // T),
        in_specs=[pl.BlockSpec((None, T, D_MODEL), lambda b, s: (b, s, 0))] + [_full(w.shape) for w in weights],
        out_specs=out_specs,
        out_shape=out_shape,
        scratch_shapes=scratch,
        compiler_params=pltpu.CompilerParams(dimension_semantics=("arbitrary", "arbitrary"),
                                             vmem_limit_bytes=VMEM_LIMIT),
        name="mix_prompt",
    )(x, *weights)


def _step_pre_body(x_ref, g_ref, win_ref, sa_ref, caw_ref, cab_ref, lag_ref, lab_ref, sb_ref, cbw_ref, cbb_ref,
                   wq_ref, wk_ref, wv_ref, wg_ref, bg_ref, n_ref, m_ref, sc_ref, ccw_ref, ccb_ref,
                   wrg_ref, brg_ref, lam_ref, h_ref, skip_ref,
                   sa_o, sb_o, sc_o, n_o, m_o, h_o, q_o, gk_o, v_o, dec_o, den_o, sv_o, zsig_o, skcb_o, ac_o,
                   qkv_scr):
    x = x_ref[...]
    hn = _rmsnorm(x, g_ref[...])
    u = _dot(hn.astype(bf16), win_ref[...])
    xa, ga = u[:, 0:256], u[:, 256:512]
    xb, zb = u[:, 512:1024], u[:, 1024:1536]
    xc, gc = u[:, 1536:1792], u[:, 1792:2048]

    a_new = xa * _sigmoid(ga)
    caw = caw_ref[...]
    conv = cab_ref[...] + caw[CONV_A - 1:CONV_A, :] * a_new
    for j in range(CONV_A - 1):
        conv = conv + caw[j:j + 1, :] * sa_ref[:, j * W_A:(j + 1) * W_A]
    sa_o[:, 0:(CONV_A - 2) * W_A] = sa_ref[:, W_A:]
    sa_o[:, (CONV_A - 2) * W_A:] = a_new
    ac_o[:, 0:W_A] = _group_layernorm_silu(conv, _group_mask_a(), lag_ref[...], lab_ref[...])

    cbw = cbw_ref[...]
    cb = cbb_ref[...] + cbw[CONV_B - 1:CONV_B, :] * xb
    for j in range(CONV_B - 1):
        cb = cb + cbw[j:j + 1, :] * sb_ref[:, j * W_B:(j + 1) * W_B]
    sb_o[:, 0:(CONV_B - 2) * W_B] = sb_ref[:, W_B:]
    sb_o[:, (CONV_B - 2) * W_B:] = xb
    cb = cb * _sigmoid(cb)
    cb_bf = cb.astype(bf16)
    xb_bf = xb.astype(bf16)
    qs, ks, vs = [], [], []
    for h in range(H_B):
        hs = slice(h * DH_B, (h + 1) * DH_B)
        q = _dot(cb_bf[:, hs], wq_ref[h])
        k = _dot(cb_bf[:, hs], wk_ref[h]) * (DH_B ** -0.5)
        v = _dot(xb_bf[:, hs], wv_ref[h])
        qs.append(q)
        ks.append(k)
        vs.append(v)
        qkv_scr[:, (3 * h) * DH_B:(3 * h + 1) * DH_B] = q.astype(bf16)
        qkv_scr[:, (3 * h + 1) * DH_B:(3 * h + 2) * DH_B] = k.astype(bf16)
        qkv_scr[:, (3 * h + 2) * DH_B:(3 * h + 3) * DH_B] = v.astype(bf16)
    gates = _dot(qkv_scr[...], wg_ref[...]) + bg_ref[...]
    ig = gates[:, 0:LANES]
    lf = jax.nn.log_sigmoid(gates[:, LANES:2 * LANES])
    m0 = m_ref[...]
    m_t = jnp.maximum(lf + m0, ig)
    g_in = jnp.exp(ig - m_t)
    decay = jnp.exp(lf + m0 - m_t)
    lane = lax.broadcasted_iota(jnp.int32, ig.shape, 1)
    qk = jnp.zeros(ig.shape, f32)
    qn = jnp.zeros(ig.shape, f32)
    for h in range(H_B):
        hs = slice(h * DH_B, (h + 1) * DH_B)
        n_h = n_ref[:, hs]
        qk = jnp.where(lane == h, jnp.sum(qs[h] * ks[h], axis=-1, keepdims=True), qk)
        qn = jnp.where(lane == h, jnp.sum(qs[h] * n_h, axis=-1, keepdims=True), qn)
        gk = g_in[:, h:h + 1] * ks[h]
        n_o[:, hs] = decay[:, h:h + 1] * n_h + gk
        q_o[:, hs] = qs[h]
        gk_o[:, hs] = gk
        v_o[:, hs] = vs[h]
    s_t = qk * g_in
    den = s_t + decay * qn
    den_o[...] = jnp.maximum(jnp.abs(den), jnp.exp(-m_t))
    dec_o[...] = decay
    m_o[...] = m_t
    for h in range(H_B):
        hs = slice(h * DH_B, (h + 1) * DH_B)
        sv_o[:, hs] = s_t[:, h:h + 1] * vs[h]
    zsig_o[...] = _sigmoid(zb)
    skcb_o[...] = skip_ref[...] * cb

    ccw = ccw_ref[...]
    xcv = ccb_ref[...] + ccw[CONV_C - 1:CONV_C, :] * xc
    for j in range(CONV_C - 1):
        xcv = xcv + ccw[j:j + 1, :] * sc_ref[:, j * W_C:(j + 1) * W_C]
    sc_o[:, 0:(CONV_C - 2) * W_C] = sc_ref[:, W_C:]
    sc_o[:, (CONV_C - 2) * W_C:] = xc
    a_t, u_t = _rglru_gates(xcv, wrg_ref[...], brg_ref[...], lam_ref[...])
    hc = a_t * h_ref[...] + u_t
    h_o[...] = hc
    ac_o[:, W_A:] = hc * jax.nn.gelu(gc)


def _step_read_body(C_ref, qT_ref, qc_ref):
    for bb in range(STEP_BB):
        for h in range(H_B):
            hs = slice(h * DH_B, (h + 1) * DH_B)
            qc_ref[bb:bb + 1, hs] = jnp.sum(C_ref[bb, h] * qT_ref[h, :, bb:bb + 1], axis=0, keepdims=True)


def _step_state_body(C_ref, qT_ref, gkT_ref, v_ref, dec_ref, Cn_ref, qc_ref):
    for bb in range(STEP_BB):
        for h in range(H_B):
            hs = slice(h * DH_B, (h + 1) * DH_B)
            C = C_ref[bb, h]
            qc_ref[bb:bb + 1, hs] = jnp.sum(C * qT_ref[h, :, bb:bb + 1], axis=0, keepdims=True)
            Cn_ref[bb, h] = dec_ref[bb:bb + 1, h:h + 1] * C + gkT_ref[h, :, bb:bb + 1] * v_ref[bb:bb + 1, hs]


def _to_cols(t):
    return t.reshape(t.shape[0] // STEP_BB, STEP_BB, H_B, DH_B).transpose(0, 2, 3, 1)


def _step_read(C_all, layer, q):
    Bs = q.shape[0]
    return pl.pallas_call(
        _step_read_body,
        grid=(Bs // STEP_BB,),
        in_specs=[pl.BlockSpec((None, STEP_BB, H_B, DH_B, DH_B), lambda i: (layer, i, 0, 0, 0)),
                  pl.BlockSpec((None, H_B, DH_B, STEP_BB), lambda i: (i, 0, 0, 0))],
        out_specs=pl.BlockSpec((STEP_BB, W_B), lambda i: (i, 0)),
        out_shape=jax.ShapeDtypeStruct((Bs, W_B), f32),
        compiler_params=pltpu.CompilerParams(dimension_semantics=("arbitrary",), vmem_limit_bytes=VMEM_LIMIT),
        name="step_read",
    )(C_all, _to_cols(q))


def _step_state(C_all, q, gk, v, dec):
    depth, Bs = q.shape[0], q.shape[1]
    cols = lambda t: jnp.stack([_to_cols(t[l]) for l in range(depth)])
    blk5 = pl.BlockSpec((None, STEP_BB, H_B, DH_B, DH_B), lambda l, i: (l, i, 0, 0, 0))
    colspec = pl.BlockSpec((None, None, H_B, DH_B, STEP_BB), lambda l, i: (l, i, 0, 0, 0))
    return pl.pallas_call(
        _step_state_body,
        grid=(depth, Bs // STEP_BB),
        in_specs=[blk5, colspec, colspec,
                  pl.BlockSpec((None, STEP_BB, W_B), lambda l, i: (l, i, 0)),
                  pl.BlockSpec((None, STEP_BB, LANES), lambda l, i: (l, i, 0))],
        out_specs=(blk5, pl.BlockSpec((None, STEP_BB, W_B), lambda l, i: (l, i, 0))),
        out_shape=(jax.ShapeDtypeStruct(C_all.shape, f32), jax.ShapeDtypeStruct((depth, Bs, W_B), f32)),
        compiler_params=pltpu.CompilerParams(dimension_semantics=("arbitrary", "arbitrary"),
                                             vmem_limit_bytes=VMEM_LIMIT),
        name="step_state",
    )(C_all, cols(q), cols(gk), v, dec)


def _step_post_body(x_ref, ac_ref, zsig_ref, skcb_ref, sv_ref, qc_ref, dec_ref, den_ref, gn_ref, wout_ref,
                    x1_ref, mix_scr):
    gn = gn_ref[...]
    dec = dec_ref[...]
    den = den_ref[...]
    mix_scr[:, 0:W_A] = ac_ref[:, 0:W_A].astype(bf16)
    mix_scr[:, W_A + W_B:] = ac_ref[:, W_A:].astype(bf16)
    for h in range(H_B):
        hs = slice(h * DH_B, (h + 1) * DH_B)
        num = sv_ref[:, hs] + dec[:, h:h + 1] * qc_ref[:, hs]
        hb = _head_layernorm(num / den[:, h:h + 1], gn[:, hs])
        mix_scr[:, W_A + h * DH_B:W_A + (h + 1) * DH_B] = (zsig_ref[:, hs] * (hb + skcb_ref[:, hs])).astype(bf16)
    x1_ref[...] = x_ref[...] + _dot(mix_scr[...], wout_ref[...])


def _step_pre(x, st, lw):
    buf_a, buf_b, n0, m0, buf_c, hc0 = st
    Bs = x.shape[0]
    sa = buf_a.reshape(Bs, (CONV_A - 1) * W_A)
    sb = buf_b.reshape(Bs, (CONV_B - 1) * W_B)
    sc = buf_c.reshape(Bs, (CONV_C - 1) * W_C)
    n_in = n0.reshape(Bs, W_B)
    m_in = jnp.pad(m0, ((0, 0), (0, LANES - H_B)))
    sds = lambda *shape: jax.ShapeDtypeStruct(shape, f32)
    pre_in = [x, lw["norm_mix_g"], lw["w_in"], sa, lw["conv_a_w"], lw["conv_a_b"], lw["ln_a_g"], lw["ln_a_b"],
              sb, lw["conv_b_w"], lw["conv_b_b"], lw["w_q"], lw["w_k"], lw["w_v"], lw["w_gates"], lw["b_gates"],
              n_in, m_in, sc, lw["conv_c_w"], lw["conv_c_b"], lw["w_rg"], lw["b_rg"], lw["lam"], hc0, lw["skip_b"]]
    (sa_n, sb_n, sc_n, n_n, m_n, h_n, q, gk, v, dec, den, sv, zsig, skcb, ac) = pl.pallas_call(
        _step_pre_body,
        out_shape=(sds(*sa.shape), sds(*sb.shape), sds(*sc.shape), sds(Bs, W_B), sds(Bs, LANES), sds(Bs, W_C),
                   sds(Bs, W_B), sds(Bs, W_B), sds(Bs, W_B), sds(Bs, LANES), sds(Bs, LANES), sds(Bs, W_B),
                   sds(Bs, W_B), sds(Bs, W_B), sds(Bs, W_A + W_C)),
        scratch_shapes=[pltpu.VMEM((Bs, 3 * W_B), bf16)],
        compiler_params=pltpu.CompilerParams(vmem_limit_bytes=VMEM_LIMIT),
        name="step_pre",
    )(*pre_in)
    new = (sa_n.reshape(buf_a.shape), sb_n.reshape(buf_b.shape), n_n.reshape(n0.shape), m_n[:, :H_B],
           sc_n.reshape(buf_c.shape), h_n)
    return new, (q, gk, v, dec), (ac, zsig, skcb, sv, dec, den)


def _step_post(x, post_in, qc, lw):
    ac, zsig, skcb, sv, dec, den = post_in
    return pl.pallas_call(
        _step_post_body,
        out_shape=jax.ShapeDtypeStruct(x.shape, f32),
        scratch_shapes=[pltpu.VMEM(x.shape, bf16)],
        compiler_params=pltpu.CompilerParams(vmem_limit_bytes=VMEM_LIMIT),
        name="step_post",
    )(x, ac, zsig, skcb, sv, qc, dec, den, lw["gn_b_g"], lw["w_out"])


def _ffn_body(*refs, moe, final_norm):
    it = iter(refs)
    x_ref, g_ref, wg_ref, wu_ref, wd_ref = next(it), next(it), next(it), next(it), next(it)
    wr_ref = next(it) if moe else None
    br_ref = next(it) if moe else None
    gf_ref = next(it) if final_norm else None
    o_ref, hn_scr, acc_scr = next(it), next(it), next(it)
    comb_scr = next(it) if moe else None
    j = pl.program_id(1)

    @pl.when(j == 0)
    def _():
        hn = _rmsnorm(x_ref[...], g_ref[...])
        hn_scr[...] = hn.astype(bf16)
        acc_scr[...] = jnp.zeros(acc_scr.shape, f32)
        if moe:
            lane, i1, i2, g1, g2 = _top2(_router_logits(hn, wr_ref[...], br_ref[...]))
            comb_scr[...] = jnp.where(lane == i1, g1, 0.0) + jnp.where(lane == i2, g2, 0.0)

    hb = hn_scr[...]
    gt = _dot(hb, wg_ref[...])
    up = _dot(hb, wu_ref[...])
    act = (gt * _sigmoid(gt) * up).astype(bf16)
    y = _dot(act, wd_ref[...])
    if moe:
        lane = lax.broadcasted_iota(jnp.int32, comb_scr.shape, 1)
        y = y * jnp.sum(jnp.where(lane == j, comb_scr[...], 0.0), axis=-1, keepdims=True)
    acc_scr[...] += y

    @pl.when(j == pl.num_programs(1) - 1)
    def _():
        out = x_ref[...] + acc_scr[...]
        if final_norm:
            out = _rmsnorm(out, gf_ref[...])
        o_ref[...] = out


def _ffn(x, g, wg, wu, wd, chunks, router=None, final_g=None):
    N = x.shape[0]
    E = wg.shape[0]
    F = wg.shape[2] // chunks
    tm = min(TM_FFN, N)
    moe = router is not None
    assert not moe or chunks == 1
    final_norm = final_g is not None
    ins = [x, g, wg, wu, wd]
    in_specs = [pl.BlockSpec((tm, D_MODEL), lambda i, j: (i, 0)), _full(g.shape),
                pl.BlockSpec((None, D_MODEL, F), lambda i, j: (j // chunks, 0, j % chunks)),
                pl.BlockSpec((None, D_MODEL, F), lambda i, j: (j // chunks, 0, j % chunks)),
                pl.BlockSpec((None, F, D_MODEL), lambda i, j: (j // chunks, j % chunks, 0))]
    scratch = [pltpu.VMEM((tm, D_MODEL), bf16), pltpu.VMEM((tm, D_MODEL), f32)]
    if moe:
        ins += list(router)
        in_specs += [_full(router[0].shape), _full(router[1].shape)]
        scratch.append(pltpu.VMEM((tm, LANES), f32))
    if final_norm:
        ins.append(final_g)
        in_specs.append(_full(final_g.shape))
    return pl.pallas_call(
        functools.partial(_ffn_body, moe=moe, final_norm=final_norm),
        grid=(N // tm, E * chunks),
        in_specs=in_specs,
        out_specs=pl.BlockSpec((tm, D_MODEL), lambda i, j: (i, 0)),
        out_shape=jax.ShapeDtypeStruct((N, D_MODEL), f32),
        scratch_shapes=scratch,
        compiler_params=pltpu.CompilerParams(dimension_semantics=("arbitrary", "arbitrary"),
                                             vmem_limit_bytes=VMEM_LIMIT),
        name="ffn_moe" if moe else "ffn_dense",
    )(*ins)


TM_MOE = 512
SEG_ALIGN = 16
SEG_SIZES = (512, 256, 128, 64, 32, 16)
MOE_ROWS = 2 * TM_MOE + N_EXPERTS * SEG_ALIGN
META_COLS = 3 * N_EXPERTS


def _seg_dma(src, src_off, dst, dst_off, nrows, sem, wait):
    done = jnp.int32(0)
    for size in SEG_SIZES:
        take = (nrows & size) != 0

        @pl.when(take)
        def _(done=done, size=size):
            cp = pltpu.make_async_copy(src.at[pl.ds(pl.multiple_of(src_off + done, SEG_ALIGN), size)],
                                       dst.at[pl.ds(pl.multiple_of(dst_off + done, SEG_ALIGN), size)], sem)
            if wait:
                cp.wait()
            else:
                cp.start()

        done = done + jnp.where(take, size, 0)


def _top2(logits):
    lane = lax.broadcasted_iota(jnp.int32, logits.shape, 1).astype(f32)
    lg = jnp.where(lane < N_EXPERTS, logits, -jnp.inf)
    m1 = jnp.max(lg, axis=-1, keepdims=True)
    i1 = jnp.min(jnp.where(lg == m1, lane, float(LANES)), axis=-1, keepdims=True)
    lg2 = jnp.where(lane == i1, -jnp.inf, lg)
    m2 = jnp.max(lg2, axis=-1, keepdims=True)
    i2 = jnp.min(jnp.where(lg2 == m2, lane, float(LANES)), axis=-1, keepdims=True)
    e2 = jnp.exp(m2 - m1)
    return lane, i1, i2, 1.0 / (1.0 + e2), e2 / (1.0 + e2)


def _router_logits(hn, wr, br):
    h1, h2 = _split_terms(hn, 2)
    w1, w2 = _split_terms(wr, 2)
    return _dot(h1, w1) + (_dot(h1, w2) + _dot(h2, w1)) + br


def _expert_ranks(lane, i1, i2):
    TM = lane.shape[0]
    sel1 = lane == i1
    sel2 = lane == i2
    sel = jnp.where(sel1, 1.0, jnp.where(sel2, 1.0, 0.0))
    earlier = lax.broadcasted_iota(jnp.int32, (TM, TM), 0) > lax.broadcasted_iota(jnp.int32, (TM, TM), 1)
    rank = _dot(jnp.where(earlier, 1.0, 0.0).astype(bf16), sel.astype(bf16))
    return sel1, sel2, rank, rank[TM - 1:TM, :] + sel[TM - 1:TM, :]


def _seg_pad(c):
    return ((c + (SEG_ALIGN - 1)) // SEG_ALIGN) * SEG_ALIGN


def _moe_count_body(x_ref, g_ref, wr_ref, br_ref, route_ref, cnt_ref):
    i = pl.program_id(0)
    hn = _rmsnorm(x_ref[...], g_ref[...])
    lane, i1, i2, g1, g2 = _top2(_router_logits(hn, wr_ref[...], br_ref[...]))
    route_ref[...] = jnp.where(lane == 0, i1, jnp.where(lane == 1, i2, jnp.where(lane == 2, g1,
                                                                                 jnp.where(lane == 3, g2, 0.0))))
    _, _, _, cnt = _expert_ranks(lane, i1, i2)
    for e in range(N_EXPERTS):
        cnt_ref[i, e] = cnt[0, e].astype(jnp.int32)


def _moe_route_body(cnt_ref, x_ref, g_ref, route_ref, xs_hbm, gs_hbm, rinfo_ref, meta_ref, te_ref, nv_ref,
                    xperm, gperm, pos_smem, sem):
    TM, R = TM_MOE, MOE_ROWS
    i = pl.program_id(0)
    nt = pl.num_programs(0)
    n_tiles = te_ref.shape[0]

    @pl.when(i == 0)
    def _():
        base = jnp.int32(0)
        for e in range(N_EXPERTS):
            rows = lax.fori_loop(0, nt, lambda t, a, e=e: a + _seg_pad(cnt_ref[t, e]), jnp.int32(0))
            first_tile = base // TM
            tiles = (rows + (TM - 1)) // TM

            def mark(k, c, e=e, first_tile=first_tile):
                te_ref[first_tile + k] = e
                return c

            lax.fori_loop(0, tiles, mark, 0)
            pos_smem[e] = base
            base = base + tiles * TM
        nv = base // TM
        nv_ref[0] = nv

        def mark_rest(k, c):
            te_ref[k] = N_EXPERTS - 1
            return c

        lax.fori_loop(nv, n_tiles, mark_rest, 0)

    hn = _rmsnorm(x_ref[...], g_ref[...])
    route = route_ref[...]
    i1, i2, g1, g2 = route[:, 0:1], route[:, 1:2], route[:, 2:3], route[:, 3:4]
    lane = lax.broadcasted_iota(jnp.int32, route.shape, 1).astype(f32)
    sel1, sel2, rank, _ = _expert_ranks(lane, i1, i2)
    seg_rows, seg_off = [], []
    acc = jnp.int32(0)
    for e in range(N_EXPERTS):
        seg_rows.append(_seg_pad(cnt_ref[i, e]))
        seg_off.append(acc)
        acc = acc + seg_rows[e]
    lane1 = lax.broadcasted_iota(jnp.int32, (1, LANES), 1)
    offv = jnp.zeros((1, LANES), f32)
    for e in range(N_EXPERTS):
        offv = jnp.where(lane1 == e, seg_off[e].astype(f32), offv)
    dest = rank + offv
    d1 = jnp.sum(jnp.where(sel1, dest, 0.0), axis=-1, keepdims=True)
    d2 = jnp.sum(jnp.where(sel2, dest, 0.0), axis=-1, keepdims=True)
    rinfo = jnp.where(lane == 0, d1, jnp.where(lane == 1, d2, jnp.where(lane == 2, g1, jnp.where(lane == 3, g2, 0.0))))
    rinfo_ref[...] = rinfo
    rinfo_t = rinfo.T
    d1r, d2r, g1r, g2r = rinfo_t[0:1, :], rinfo_t[1:2, :], rinfo_t[2:3, :], rinfo_t[3:4, :]
    riota = lax.broadcasted_iota(jnp.int32, (R, TM), 0).astype(f32)
    hit1 = riota == d1r
    hit2 = riota == d2r
    perm = jnp.where(hit1, 1.0, jnp.where(hit2, 1.0, 0.0)).astype(bf16)
    xperm[...] = _dot(perm, hn.astype(bf16)).astype(bf16)
    gcol = jnp.sum(jnp.where(hit1, g1r, 0.0) + jnp.where(hit2, g2r, 0.0), axis=1, keepdims=True)
    gperm[...] = jnp.broadcast_to(gcol, (R, LANES))

    for e in range(N_EXPERTS):
        pos = pos_smem[e]
        meta_ref[i, e] = pos
        meta_ref[i, N_EXPERTS + e] = seg_off[e]
        meta_ref[i, 2 * N_EXPERTS + e] = seg_rows[e]
        _seg_dma(xperm, seg_off[e], xs_hbm, pos, seg_rows[e], sem.at[0], False)
        _seg_dma(gperm, seg_off[e], gs_hbm, pos, seg_rows[e], sem.at[1], False)
        pos_smem[e] = pos + seg_rows[e]
    for e in range(N_EXPERTS):
        _seg_dma(xperm, seg_off[e], xs_hbm, meta_ref[i, e], seg_rows[e], sem.at[0], True)
        _seg_dma(gperm, seg_off[e], gs_hbm, meta_ref[i, e], seg_rows[e], sem.at[1], True)

    @pl.when(i == nt - 1)
    def _():
        xperm[0:TM, :] = jnp.zeros((TM, D_MODEL), bf16)
        gperm[0:TM, :] = jnp.zeros((TM, LANES), f32)
        pads = []
        for e in range(N_EXPERTS):
            pos = pos_smem[e]
            pads.append((-pos) & (TM - 1))
            _seg_dma(xperm, 0, xs_hbm, pos, pads[e], sem.at[0], False)
            _seg_dma(gperm, 0, gs_hbm, pos, pads[e], sem.at[1], False)
        for e in range(N_EXPERTS):
            _seg_dma(xperm, 0, xs_hbm, pos_smem[e], pads[e], sem.at[0], True)
            _seg_dma(gperm, 0, gs_hbm, pos_smem[e], pads[e], sem.at[1], True)

        def zero_tile(k, c):
            row = pl.multiple_of(k * TM, TM)
            cx = pltpu.make_async_copy(xperm.at[pl.ds(0, TM)], xs_hbm.at[pl.ds(row, TM)], sem.at[0])
            cg = pltpu.make_async_copy(gperm.at[pl.ds(0, TM)], gs_hbm.at[pl.ds(row, TM)], sem.at[1])
            cx.start()
            cg.start()
            cx.wait()
            cg.wait()
            return c

        lax.fori_loop(nv_ref[0], n_tiles, zero_tile, 0)


def _moe_ffn_body(te_ref, nv_ref, xs_ref, gs_ref, wg_ref, wu_ref, wd_ref, ys_ref):
    i = pl.program_id(0)

    @pl.when(i < nv_ref[0])
    def _():
        xb = xs_ref[...]
        gt = _dot(xb, wg_ref[...])
        up = _dot(xb, wu_ref[...])
        act = (gt * _sigmoid(gt) * up).astype(bf16)
        ys_ref[...] = (_dot(act, wd_ref[...]) * gs_ref[:, 0:1]).astype(bf16)

    @pl.when(i >= nv_ref[0])
    def _():
        ys_ref[...] = jnp.zeros(ys_ref.shape, bf16)


def _moe_combine_body(*refs, final_norm):
    it = iter(refs)
    meta_ref, x_ref, rinfo_ref, ys_hbm = next(it), next(it), next(it), next(it)
    gf_ref = next(it) if final_norm else None
    o_ref, yperm, sem = next(it), next(it), next(it)
    TM, R = TM_MOE, MOE_ROWS
    i = pl.program_id(0)

    @pl.when(i == 0)
    def _():
        yperm[...] = jnp.zeros(yperm.shape, bf16)

    for e in range(N_EXPERTS):
        _seg_dma(ys_hbm, meta_ref[i, e], yperm, meta_ref[i, N_EXPERTS + e], meta_ref[i, 2 * N_EXPERTS + e],
                 sem.at[0], False)
    rinfo = rinfo_ref[...]
    ciota = lax.broadcasted_iota(jnp.int32, (TM, R), 1).astype(f32)
    unperm = jnp.where(ciota == rinfo[:, 0:1], 1.0, jnp.where(ciota == rinfo[:, 1:2], 1.0, 0.0)).astype(bf16)
    for e in range(N_EXPERTS):
        _seg_dma(ys_hbm, meta_ref[i, e], yperm, meta_ref[i, N_EXPERTS + e], meta_ref[i, 2 * N_EXPERTS + e],
                 sem.at[0], True)
    out = x_ref[...] + _dot(unperm, yperm[...])
    if final_norm:
        out = _rmsnorm(out, gf_ref[...])
    o_ref[...] = out


def _moe_prompt(x, g, wg, wu, wd, wr, br, final_g):
    N = x.shape[0]
    TM = TM_MOE
    nt = N // TM
    F = wg.shape[2]
    n_tiles = -(-(2 * N + nt * N_EXPERTS * (SEG_ALIGN - 1) + N_EXPERTS * (TM - 1)) // TM)
    rows = n_tiles * TM
    smem = pl.BlockSpec(memory_space=pltpu.SMEM)
    arb = pltpu.CompilerParams(dimension_semantics=("arbitrary",), vmem_limit_bytes=VMEM_LIMIT)
    route, cnt = pl.pallas_call(
        _moe_count_body,
        grid=(nt,),
        in_specs=[pl.BlockSpec((TM, D_MODEL), lambda i: (i, 0)), _full(g.shape), _full(wr.shape), _full(br.shape)],
        out_specs=[pl.BlockSpec((TM, LANES), lambda i: (i, 0)), smem],
        out_shape=[jax.ShapeDtypeStruct((N, LANES), f32), jax.ShapeDtypeStruct((nt, N_EXPERTS), jnp.int32)],
        compiler_params=arb,
        name="moe_count",
    )(x, g, wr, br)

    xs, gs, rinfo, meta, te, nv = pl.pallas_call(
        _moe_route_body,
        grid_spec=pltpu.PrefetchScalarGridSpec(
            num_scalar_prefetch=1,
            grid=(nt,),
            in_specs=[pl.BlockSpec((TM, D_MODEL), lambda i, c: (i, 0)), pl.BlockSpec(g.shape, lambda i, c: (0, 0)),
                      pl.BlockSpec((TM, LANES), lambda i, c: (i, 0))],
            out_specs=[pl.BlockSpec(memory_space=pl.ANY), pl.BlockSpec(memory_space=pl.ANY),
                       pl.BlockSpec((TM, LANES), lambda i, c: (i, 0)), smem, smem, smem],
            scratch_shapes=[pltpu.VMEM((MOE_ROWS, D_MODEL), bf16), pltpu.VMEM((MOE_ROWS, LANES), f32),
                            pltpu.SMEM((N_EXPERTS,), jnp.int32), pltpu.SemaphoreType.DMA((2,))],
        ),
        out_shape=[jax.ShapeDtypeStruct((rows, D_MODEL), bf16), jax.ShapeDtypeStruct((rows, LANES), f32),
                   jax.ShapeDtypeStruct((N, LANES), f32), jax.ShapeDtypeStruct((nt, META_COLS), jnp.int32),
                   jax.ShapeDtypeStruct((n_tiles,), jnp.int32), jax.ShapeDtypeStruct((1,), jnp.int32)],
        compiler_params=arb,
        name="moe_route",
    )(cnt, x, g, route)

    ys = pl.pallas_call(
        _moe_ffn_body,
        grid_spec=pltpu.PrefetchScalarGridSpec(
            num_scalar_prefetch=2,
            grid=(n_tiles,),
            in_specs=[pl.BlockSpec((TM, D_MODEL), lambda i, te, nv: (i, 0)),
                      pl.BlockSpec((TM, LANES), lambda i, te, nv: (i, 0)),
                      pl.BlockSpec((None, D_MODEL, F), lambda i, te, nv: (te[i], 0, 0)),
                      pl.BlockSpec((None, D_MODEL, F), lambda i, te, nv: (te[i], 0, 0)),
                      pl.BlockSpec((None, F, D_MODEL), lambda i, te, nv: (te[i], 0, 0))],
            out_specs=pl.BlockSpec((TM, D_MODEL), lambda i, te, nv: (i, 0)),
        ),
        out_shape=jax.ShapeDtypeStruct((rows, D_MODEL), bf16),
        compiler_params=arb,
        name="moe_ffn",
    )(te, nv, xs, gs, wg, wu, wd)

    final_norm = final_g is not None
    ins = [meta, x, rinfo, ys] + ([final_g] if final_norm else [])
    in_specs = [pl.BlockSpec((TM, D_MODEL), lambda i, m: (i, 0)), pl.BlockSpec((TM, LANES), lambda i, m: (i, 0)),
                pl.BlockSpec(memory_space=pl.ANY)]
    if final_norm:
        in_specs.append(pl.BlockSpec(final_g.shape, lambda i, m: (0, 0)))
    return pl.pallas_call(
        functools.partial(_moe_combine_body, final_norm=final_norm),
        grid_spec=pltpu.PrefetchScalarGridSpec(
            num_scalar_prefetch=1,
            grid=(nt,),
            in_specs=in_specs,
            out_specs=pl.BlockSpec((TM, D_MODEL), lambda i, m: (i, 0)),
            scratch_shapes=[pltpu.VMEM((MOE_ROWS, D_MODEL), bf16), pltpu.SemaphoreType.DMA((1,))],
        ),
        out_shape=jax.ShapeDtypeStruct((N, D_MODEL), f32),
        compiler_params=pltpu.CompilerParams(dimension_semantics=("arbitrary",), vmem_limit_bytes=VMEM_LIMIT),
        name="moe_combine",
    )(*ins)


def _block_diag(w):
    h, di, do = w.shape
    return jnp.einsum("hij,hg->higj", w, jnp.eye(h, dtype=w.dtype)).reshape(h * di, h * do)


def _layer_weights(l, p):
    row = lambda v: v.reshape(1, -1)
    w_gates = jnp.zeros((3 * W_B, 2 * LANES), f32)
    w_gates = w_gates.at[:, 0:H_B].set(p["w_ig"][l]).at[:, LANES:LANES + H_B].set(p["w_fg"][l])
    b_gates = jnp.zeros((1, 2 * LANES), f32)
    b_gates = b_gates.at[0, 0:H_B].set(p["b_ig"][l]).at[0, LANES:LANES + H_B].set(p["b_fg"][l])
    return dict(
        norm_mix_g=row(p["norm_mix_g"][l]), w_in=p["w_in"][l].astype(bf16),
        conv_a_w=p["conv_a_w"][l], conv_a_b=row(p["conv_a_b"][l]), ln_a_g=row(p["ln_a_g"][l]), ln_a_b=row(p["ln_a_b"][l]),
        conv_b_w=p["conv_b_w"][l], conv_b_b=row(p["conv_b_b"][l]),
        w_q=p["w_q"][l].astype(bf16), w_k=p["w_k"][l].astype(bf16), w_v=p["w_v"][l].astype(bf16),
        w_gates=w_gates.astype(bf16), b_gates=b_gates, gn_b_g=row(p["gn_b_g"][l]), skip_b=row(p["skip_b"][l]),
        conv_c_w=p["conv_c_w"][l], conv_c_b=row(p["conv_c_b"][l]),
        w_rg=jnp.concatenate([_block_diag(p["w_ra"][l]), _block_diag(p["w_ix"][l])], axis=1).astype(bf16),
        b_rg=jnp.concatenate([p["b_ra"][l], p["b_ix"][l]]).reshape(1, -1), lam=row(p["lam"][l]),
        w_out=p["w_out"][l].astype(bf16),
    )


def kernel(x_prompt, x_sample, state_conv_a, state_conv_b, state_mlstm_C, state_mlstm_n, state_mlstm_m, state_conv_c, state_rglru_h, norm_mix_g, w_in, conv_a_w, conv_a_b, ln_a_g, ln_a_b, conv_b_w, conv_b_b, w_q, w_k, w_v, w_ig, b_ig, w_fg, b_fg, gn_b_g, skip_b, conv_c_w, conv_c_b, w_ra, b_ra, w_ix, b_ix, lam, w_out, norm_ffn_g, w_gate, w_up, w_down, w_router, b_router, we_gate, we_up, we_down, norm_final_g):
    p = dict(norm_mix_g=norm_mix_g, w_in=w_in, conv_a_w=conv_a_w, conv_a_b=conv_a_b, ln_a_g=ln_a_g, ln_a_b=ln_a_b,
             conv_b_w=conv_b_w, conv_b_b=conv_b_b, w_q=w_q, w_k=w_k, w_v=w_v, w_ig=w_ig, b_ig=b_ig, w_fg=w_fg,
             b_fg=b_fg, gn_b_g=gn_b_g, skip_b=skip_b, conv_c_w=conv_c_w, conv_c_b=conv_c_b, w_ra=w_ra, b_ra=b_ra,
             w_ix=w_ix, b_ix=b_ix, lam=lam, w_out=w_out)
    depth = w_in.shape[0]
    Bp, S, _ = x_prompt.shape
    Bs = x_sample.shape[0]
    sample_states = (state_conv_a, state_conv_b, state_mlstm_C, state_mlstm_n, state_mlstm_m, state_conv_c,
                     state_rglru_h)
    xp = x_prompt
    xs = x_sample.reshape(Bs, D_MODEL)
    new_p, new_s, mem_ops = [], [], []
    C_s = None
    for l in range(depth):
        lw = _layer_weights(l, p)
        xp, sa, sb, sC, sn, sm, sc, sh = _mix_prompt(xp, lw)
        new_p.append((sa, sb, sC, sn, sm[:, :, 0], sc, sh.reshape(Bp, W_C)))
        small, mem, post_in = _step_pre(xs, tuple(s[l] for i, s in enumerate(sample_states) if i != 2), lw)
        new_s.append(small)
        mem_ops.append(mem)
        if l < depth - 1:
            qc = _step_read(state_mlstm_C, l, mem[0])
        else:
            C_s, qc_all = _step_state(state_mlstm_C, *(jnp.stack([m[i] for m in mem_ops]) for i in range(4)))
            qc = qc_all[l]
        xs = _step_post(xs, post_in, qc, lw)
        g = norm_ffn_g[l].reshape(1, -1)
        final_g = norm_final_g.reshape(1, -1) if l == depth - 1 else None
        j = l // 2
        if l % 2 == 0:
            wg, wu, wd = (w[j:j + 1].astype(bf16) for w in (w_gate, w_up, w_down))
            chunks, router = 2, None
        else:
            wg, wu, wd = we_gate[j].astype(bf16), we_up[j].astype(bf16), we_down[j].astype(bf16)
            chunks = 1
            router = (jnp.pad(w_router[j], ((0, 0), (0, LANES - N_EXPERTS))),
                      jnp.pad(b_router[j], (0, LANES - N_EXPERTS)).reshape(1, -1))
        xp2 = xp.reshape(Bp * S, D_MODEL)
        if router is None:
            xp2 = _ffn(xp2, g, wg, wu, wd, chunks, None, final_g)
        else:
            xp2 = _moe_prompt(xp2, g, wg, wu, wd, router[0], router[1], final_g)
        xp = xp2.reshape(Bp, S, D_MODEL)
        xs = _ffn(xs, g, wg, wu, wd, chunks, router, final_g)
    stack = lambda states, i: jnp.stack([st[i] for st in states])
    out_p = tuple(stack(new_p, i) for i in range(7))
    sa_s, sb_s, n_s, m_s, sc_s, h_s = (stack(new_s, i) for i in range(6))
    return (xp, xs.reshape(Bs, 1, D_MODEL)) + out_p + (sa_s, sb_s, C_s, n_s, m_s, sc_s, h_s)
```

```python
import functools

import jax
import jax.numpy as jnp
from jax import lax
from jax.experimental import pallas as pl
from jax.experimental.pallas import tpu as pltpu

f32 = jnp.float32
bf16 = jnp.bfloat16

D_MODEL = 1024
W_A, W_B, W_C = 256, 512, 256
G_A = 4
CONV_A, CONV_B, CONV_C = 31, 4, 4
H_B, DH_B = 4, 128
H_C, DH_C = 4, 64
RG_C = 8.0
D_IN = 2048
N_EXPERTS = 8
NORM_EPS = 1e-6
LN_EPS = 1e-5

LANES = 128
SUBLANES = 8
VMEM_LIMIT = 56 * 1024 * 1024

T_MIX = 512
MIX_G = 1
L_CHUNK = 128
CONV_ROWS = 128
A_HIST = 32
S_HIST = 8
TM_FFN = 512
STEP_BB = 16


def _dot(a, b):
    return jnp.dot(a, b, preferred_element_type=f32)


def _dot_nt(a, b):
    return lax.dot_general(a, b, (((1,), (1,)), ((), ())), preferred_element_type=f32)


def _dot_tn(a, b):
    return lax.dot_general(a, b, (((0,), (0,)), ((), ())), preferred_element_type=f32)


def _split_terms(x, terms):
    out = []
    r = x
    for _ in range(terms - 1):
        p = r.astype(bf16)
        out.append(p)
        r = r - p.astype(f32)
    out.append(r.astype(bf16))
    return out


def _dot_split_lhs(x, m, terms):
    acc = None
    for p in _split_terms(x, terms):
        d = _dot(p, m)
        acc = d if acc is None else acc + d
    return acc


def _dot_split_rhs(m, x, terms):
    acc = None
    for p in _split_terms(x, terms):
        d = _dot(m, p)
        acc = d if acc is None else acc + d
    return acc


def _sigmoid(x):
    return 0.5 * (1.0 + jnp.tanh(0.5 * x))


def _rmsnorm(x, g):
    return x * lax.rsqrt(jnp.mean(x * x, axis=-1, keepdims=True) + NORM_EPS) * g


def _group_mask_a():
    r = lax.broadcasted_iota(jnp.int32, (W_A, W_A), 0) // (W_A // G_A)
    c = lax.broadcasted_iota(jnp.int32, (W_A, W_A), 1) // (W_A // G_A)
    return jnp.where(r == c, 1.0, 0.0).astype(bf16)


def _group_layernorm_silu(a, gm, g, b):
    inv = 1.0 / (W_A // G_A)
    mu = _dot_split_lhs(a, gm, 2) * inv
    d = a - mu
    var = _dot((d * d).astype(bf16), gm) * inv
    y = d * lax.rsqrt(var + LN_EPS) * g + b
    return y * _sigmoid(y)


def _head_layernorm(h, g):
    mu = jnp.mean(h, axis=-1, keepdims=True)
    d = h - mu
    var = jnp.mean(d * d, axis=-1, keepdims=True)
    return d * lax.rsqrt(var + LN_EPS) * g


def _rglru_gates(xcv, wrg, brg, lam):
    gates = _dot(xcv.astype(bf16), wrg) + brg
    r = _sigmoid(gates[:, :W_C])
    i = _sigmoid(gates[:, W_C:])
    log_a = RG_C * r * jax.nn.log_sigmoid(lam)
    a = jnp.exp(log_a)
    t = jnp.tanh(log_a)
    one_minus_a2 = -2.0 * t / (1.0 - t)
    u = jnp.sqrt(one_minus_a2) * i * xcv
    return a, u


N_MIX_WEIGHTS = 21
N_MIX_OUTPUTS = 8


def _mix_prompt_body(*refs):
    x_ref, weights = refs[0], refs[1:1 + N_MIX_WEIGHTS]
    rest = refs[1 + N_MIX_WEIGHTS:]
    for g in range(MIX_G):
        _mix_tile(x_ref.at[g], *weights, *[r.at[g] for r in rest])


def _mix_tile(x_ref, g_ref, win_ref, caw_ref, cab_ref, lag_ref, lab_ref, cbw_ref, cbb_ref,
              wq_ref, wk_ref, wv_ref, wg_ref, bg_ref, gn_ref, skip_ref, ccw_ref, ccb_ref,
              wrg_ref, brg_ref, lam_ref, wout_ref,
              x1_ref, sa_ref, sb_ref, sC_ref, sn_ref, sm_ref, sc_ref, sh_ref,
              ahist, bhist, chist, conv_scr, C_scr, n_scr, m_scr, h_scr, qkv_scr, mix_scr):
    T, L = T_MIX, L_CHUNK
    s = pl.program_id(1)
    last = pl.num_programs(1) - 1

    @pl.when(s == 0)
    def _():
        ahist[0:A_HIST, :] = jnp.zeros((A_HIST, W_A), f32)
        bhist[0:S_HIST, :] = jnp.zeros((S_HIST, W_B), f32)
        chist[0:S_HIST, :] = jnp.zeros((S_HIST, W_C), f32)
        C_scr[...] = jnp.zeros(C_scr.shape, f32)
        n_scr[...] = jnp.zeros(n_scr.shape, f32)
        m_scr[...] = jnp.zeros(m_scr.shape, f32)
        h_scr[...] = jnp.zeros(h_scr.shape, f32)

    x = x_ref[...]
    hn = _rmsnorm(x, g_ref[...])
    hn_bf = hn.astype(bf16)
    ua = _dot(hn_bf, win_ref[:, 0:2 * W_A])
    ub = _dot(hn_bf, win_ref[:, 2 * W_A:2 * W_A + 2 * W_B])
    uc = _dot(hn_bf, win_ref[:, 2 * W_A + 2 * W_B:])
    xa, ga = ua[:, 0:W_A], ua[:, W_A:]
    xb, zb = ub[:, 0:W_B], ub[:, W_B:]
    xc, gc = uc[:, 0:W_C], uc[:, W_C:]

    ahist[pl.ds(A_HIST, T), :] = xa * _sigmoid(ga)
    caw = caw_ref[...]
    cab = cab_ref[...]
    first = A_HIST - (CONV_A - 1)

    for base in range(0, T, CONV_ROWS):
        for lo in range(0, W_A, LANES):
            blk = ahist[pl.ds(base, CONV_ROWS + A_HIST), lo:lo + LANES]
            acc = jnp.broadcast_to(cab[:, lo:lo + LANES], (CONV_ROWS, LANES))
            for rr in range(SUBLANES):
                n = CONV_ROWS if rr == 0 else CONV_ROWS + SUBLANES
                z = None
                for q in range(A_HIST // SUBLANES + 1):
                    j = SUBLANES * q + rr - first
                    if 0 <= j < CONV_A:
                        term = caw[j:j + 1, lo:lo + LANES] * blk[SUBLANES * q:SUBLANES * q + n, :]
                        z = term if z is None else z + term
                acc = acc + (z if rr == 0 else z[rr:rr + CONV_ROWS, :])
            conv_scr[pl.ds(base, CONV_ROWS), lo:lo + LANES] = acc
    a_out = _group_layernorm_silu(conv_scr[...], _group_mask_a(), lag_ref[...], lab_ref[...])
    y = _dot(a_out.astype(bf16), wout_ref[0:W_A, :])

    bhist[pl.ds(S_HIST, T), :] = xb
    cbw = cbw_ref[...]
    cb = cbb_ref[...] + cbw[3:4, :] * xb
    for j in range(CONV_B - 1):
        cb = cb + cbw[j:j + 1, :] * bhist[pl.ds(S_HIST - (CONV_B - 1) + j, T), :]
    cb = cb * _sigmoid(cb)
    cb_bf = cb.astype(bf16)
    xb_bf = xb.astype(bf16)
    qs, ks, vs = [], [], []
    for h in range(H_B):
        hs = slice(h * DH_B, (h + 1) * DH_B)
        q = _dot(cb_bf[:, hs], wq_ref[h])
        k = _dot(cb_bf[:, hs], wk_ref[h]) * (DH_B ** -0.5)
        v = _dot(xb_bf[:, hs], wv_ref[h])
        qs.append(q)
        ks.append(k)
        vs.append(v)
        qkv_scr[:, (3 * h) * DH_B:(3 * h + 1) * DH_B] = q.astype(bf16)
        qkv_scr[:, (3 * h + 1) * DH_B:(3 * h + 2) * DH_B] = k.astype(bf16)
        qkv_scr[:, (3 * h + 2) * DH_B:(3 * h + 3) * DH_B] = v.astype(bf16)
    gates = _dot(qkv_scr[...], wg_ref[...]) + bg_ref[...]
    ig = gates[:, 0:LANES]
    lf = jax.nn.log_sigmoid(gates[:, LANES:2 * LANES])
    ri = lax.broadcasted_iota(jnp.int32, (L, L), 0)
    ci = lax.broadcasted_iota(jnp.int32, (L, L), 1)
    causal = ri >= ci
    tri = jnp.where(causal, 1.0, 0.0).astype(bf16)
    gn = gn_ref[...]
    skip = skip_ref[...]
    heads = range(H_B)
    stack = lambda parts: jnp.concatenate(parts, axis=0)
    gn_st = stack([jnp.broadcast_to(gn[:, h * DH_B:(h + 1) * DH_B], (L, DH_B)) for h in heads])
    ones_blk = jnp.ones((L, DH_B), bf16)
    C_old = [C_scr[h] for h in heads]
    n_old = [n_scr[h:h + 1, :] for h in heads]
    m_old = [m_scr[h:h + 1, 0:1] for h in heads]
    for c in range(T // L):
        rows = slice(c * L, (c + 1) * L)
        b_all = _dot_split_rhs(tri, lf[rows], 2)
        c_all = ig[rows] - b_all
        c_all_t = c_all.T
        q_bf = [qs[h][rows].astype(bf16) for h in heads]
        k_bf = [ks[h][rows].astype(bf16) for h in heads]
        v_bf = [vs[h][rows].astype(bf16) for h in heads]
        m_prev = stack([jnp.broadcast_to(m_old[h], (L, 1)) for h in heads])
        b_col = stack([b_all[:, h:h + 1] for h in heads])
        c_col = stack([c_all[:, h:h + 1] for h in heads])
        dm = stack([jnp.where(causal, c_all_t[h:h + 1, :], -jnp.inf) for h in heads])
        mx = jnp.maximum(m_prev, jnp.max(dm, axis=1, keepdims=True))
        sc = stack([_dot_nt(q_bf[h], k_bf[h]) for h in heads]) * jnp.exp(dm - mx)
        sc_bf = sc.astype(bf16)
        w_inter = jnp.exp(m_prev - mx)
        sv_aug = [_dot(sc_bf[h * L:(h + 1) * L], jnp.concatenate([v_bf[h], ones_blk], axis=1)) for h in heads]
        sv = stack([a[:, 0:DH_B] for a in sv_aug])
        s_sum = stack([a[:, DH_B:DH_B + 1] for a in sv_aug])
        qC = stack([_dot(q_bf[h], C_old[h].astype(bf16)) for h in heads])
        qn = stack([_dot_nt(q_bf[h], jnp.broadcast_to(n_old[h], (SUBLANES, DH_B)).astype(bf16))[:, 0:1]
                    for h in heads])
        num = sv + w_inter * qC
        den = s_sum + w_inter * qn
        hb = num / jnp.maximum(jnp.abs(den), jnp.exp(-(b_col + mx)))
        hb = _head_layernorm(hb, gn_st)
        mx_last = [mx[(h + 1) * L - 1:(h + 1) * L, :] for h in heads]
        g_in = jnp.exp(c_col - stack([jnp.broadcast_to(mx_last[h], (L, 1)) for h in heads]))
        C_new, n_new, m_new = [], [], []
        for h in heads:
            hs = slice(h * DH_B, (h + 1) * DH_B)
            hr = slice(h * L, (h + 1) * L)
            b_out = _sigmoid(zb[rows, hs]) * (hb[hr] + skip[:, hs] * cb[rows, hs])
            mix_scr[rows, hs] = b_out.astype(bf16)
            gk = g_in[hr] * ks[h][rows]
            decay = jnp.exp(m_old[h] - mx_last[h])
            C_new.append(decay * C_old[h] + _dot_tn(gk.astype(bf16), v_bf[h]))
            n_new.append(decay * n_old[h] + jnp.sum(gk, axis=0, keepdims=True))
            m_new.append(b_all[L - 1:L, h:h + 1] + mx_last[h])
        C_old, n_old, m_old = C_new, n_new, m_new
    for h in heads:
        C_scr[h] = C_old[h]
        n_scr[h:h + 1, :] = n_old[h]
        m_scr[h:h + 1, :] = jnp.broadcast_to(m_old[h], (1, LANES))

    chist[pl.ds(S_HIST, T), :] = xc
    ccw = ccw_ref[...]
    xcv = ccb_ref[...] + ccw[3:4, :] * xc
    for j in range(CONV_C - 1):
        xcv = xcv + ccw[j:j + 1, :] * chist[pl.ds(S_HIST - (CONV_C - 1) + j, T), :]
    a_t, u_t = _rglru_gates(xcv, wrg_ref[...], brg_ref[...], lam_ref[...])
    row = lax.broadcasted_iota(jnp.int32, (T, W_C), 0)
    d = 1
    while d < T:
        if d < SUBLANES:
            a_sh = jnp.where(row >= d, pltpu.roll(a_t, d, axis=0), 1.0)
            u_sh = jnp.where(row >= d, pltpu.roll(u_t, d, axis=0), 0.0)
        else:
            a_sh = jnp.concatenate([jnp.ones((d, W_C), f32), a_t[:T - d]], axis=0)
            u_sh = jnp.concatenate([jnp.zeros((d, W_C), f32), u_t[:T - d]], axis=0)
        u_t = a_t * u_sh + u_t
        a_t = a_t * a_sh
        d *= 2
    hc = a_t * h_scr[...] + u_t
    h_scr[...] = hc[T - 1:T, :]
    y = y + _dot((hc * jax.nn.gelu(gc)).astype(bf16), wout_ref[W_A + W_B:, :])

    x1_ref[...] = x + (y + _dot(mix_scr[...], wout_ref[W_A:W_A + W_B, :]))

    ahist[0:A_HIST, :] = ahist[pl.ds(T, A_HIST), :]
    bhist[0:S_HIST, :] = bhist[pl.ds(T, S_HIST), :]
    chist[0:S_HIST, :] = chist[pl.ds(T, S_HIST), :]

    @pl.when(s == last)
    def _():
        sa_ref[...] = ahist[pl.ds(first, CONV_A - 1), :]
        sb_ref[...] = bhist[pl.ds(S_HIST - (CONV_B - 1), CONV_B - 1), :]
        sc_ref[...] = chist[pl.ds(S_HIST - (CONV_C - 1), CONV_C - 1), :]
        sC_ref[...] = C_scr[...]
        sn_ref[...] = n_scr[0:H_B, :]
        sm_ref[...] = m_scr[0:H_B, :]
        sh_ref[...] = h_scr[...]


def _full(shape):
    nd = len(shape)
    return pl.BlockSpec(shape, lambda *_: (0,) * nd)


def _mix_prompt(x, lw):
    B, S, _ = x.shape
    T = T_MIX
    weights = [lw[k] for k in ("norm_mix_g", "w_in", "conv_a_w", "conv_a_b", "ln_a_g", "ln_a_b", "conv_b_w", "conv_b_b",
                               "w_q", "w_k", "w_v", "w_gates", "b_gates", "gn_b_g", "skip_b", "conv_c_w", "conv_c_b",
                               "w_rg", "b_rg", "lam", "w_out")]
    out_shape = (
        jax.ShapeDtypeStruct((B, S, D_MODEL), f32),
        jax.ShapeDtypeStruct((B, CONV_A - 1, W_A), f32),
        jax.ShapeDtypeStruct((B, CONV_B - 1, W_B), f32),
        jax.ShapeDtypeStruct((B, H_B, DH_B, DH_B), f32),
        jax.ShapeDtypeStruct((B, H_B, DH_B), f32),
        jax.ShapeDtypeStruct((B, H_B, LANES), f32),
        jax.ShapeDtypeStruct((B, CONV_C - 1, W_C), f32),
        jax.ShapeDtypeStruct((B, 1, W_C), f32),
    )
    G = MIX_G
    per_b = lambda shp: pl.BlockSpec((G,) + shp, lambda b, s: (b,) + (0,) * len(shp))
    out_specs = (
        pl.BlockSpec((G, T, D_MODEL), lambda b, s: (b, s, 0)),
        per_b((CONV_A - 1, W_A)), per_b((CONV_B - 1, W_B)), per_b((H_B, DH_B, DH_B)), per_b((H_B, DH_B)),
        per_b((H_B, LANES)), per_b((CONV_C - 1, W_C)), per_b((1, W_C)),
    )
    scratch = [
        pltpu.VMEM((G, A_HIST + T, W_A), f32), pltpu.VMEM((G, S_HIST + T, W_B), f32),
        pltpu.VMEM((G, S_HIST + T, W_C), f32), pltpu.VMEM((G, T, W_A), f32),
        pltpu.VMEM((G, H_B, DH_B, DH_B), f32), pltpu.VMEM((G, SUBLANES, DH_B), f32),
        pltpu.VMEM((G, SUBLANES, LANES), f32), pltpu.VMEM((G, 1, W_C), f32),
        pltpu.VMEM((G, T, 3 * W_B), bf16), pltpu.VMEM((G, T, W_B), bf16),
    ]
    assert len(weights) == N_MIX_WEIGHTS and len(out_shape) == N_MIX_OUTPUTS and B % G == 0
    return pl.pallas_call(
        _mix_prompt_body,
        grid=(B // G, S // T),
        in_specs=[pl.BlockSpec((G, T, D_MODEL), lambda b, s: (b, s, 0))] + [_full(w.shape) for w in weights],
        out_specs=out_specs,
        out_shape=out_shape,
        scratch_shapes=scratch,
        compiler_params=pltpu.CompilerParams(dimension_semantics=("arbitrary", "arbitrary"),
                                             vmem_limit_bytes=VMEM_LIMIT),
        name="mix_prompt",
    )(x, *weights)


def _step_pre_body(x_ref, g_ref, win_ref, sa_ref, caw_ref, cab_ref, lag_ref, lab_ref, sb_ref, cbw_ref, cbb_ref,
                   wq_ref, wk_ref, wv_ref, wg_ref, bg_ref, n_ref, m_ref, sc_ref, ccw_ref, ccb_ref,
                   wrg_ref, brg_ref, lam_ref, h_ref, skip_ref,
                   sa_o, sb_o, sc_o, n_o, m_o, h_o, q_o, gk_o, v_o, dec_o, den_o, sv_o, zsig_o, skcb_o, ac_o,
                   qkv_scr):
    x = x_ref[...]
    hn = _rmsnorm(x, g_ref[...])
    u = _dot(hn.astype(bf16), win_ref[...])
    xa, ga = u[:, 0:256], u[:, 256:512]
    xb, zb = u[:, 512:1024], u[:, 1024:1536]
    xc, gc = u[:, 1536:1792], u[:, 1792:2048]

    a_new = xa * _sigmoid(ga)
    caw = caw_ref[...]
    conv = cab_ref[...] + caw[CONV_A - 1:CONV_A, :] * a_new
    for j in range(CONV_A - 1):
        conv = conv + caw[j:j + 1, :] * sa_ref[:, j * W_A:(j + 1) * W_A]
    sa_o[:, 0:(CONV_A - 2) * W_A] = sa_ref[:, W_A:]
    sa_o[:, (CONV_A - 2) * W_A:] = a_new
    ac_o[:, 0:W_A] = _group_layernorm_silu(conv, _group_mask_a(), lag_ref[...], lab_ref[...])

    cbw = cbw_ref[...]
    cb = cbb_ref[...] + cbw[CONV_B - 1:CONV_B, :] * xb
    for j in range(CONV_B - 1):
        cb = cb + cbw[j:j + 1, :] * sb_ref[:, j * W_B:(j + 1) * W_B]
    sb_o[:, 0:(CONV_B - 2) * W_B] = sb_ref[:, W_B:]
    sb_o[:, (CONV_B - 2) * W_B:] = xb
    cb = cb * _sigmoid(cb)
    cb_bf = cb.astype(bf16)
    xb_bf = xb.astype(bf16)
    qs, ks, vs = [], [], []
    for h in range(H_B):
        hs = slice(h * DH_B, (h + 1) * DH_B)
        q = _dot(cb_bf[:, hs], wq_ref[h])
        k = _dot(cb_bf[:, hs], wk_ref[h]) * (DH_B ** -0.5)
        v = _dot(xb_bf[:, hs], wv_ref[h])
        qs.append(q)
        ks.append(k)
        vs.append(v)
        qkv_scr[:, (3 * h) * DH_B:(3 * h + 1) * DH_B] = q.astype(bf16)
        qkv_scr[:, (3 * h + 1) * DH_B:(3 * h + 2) * DH_B] = k.astype(bf16)
        qkv_scr[:, (3 * h + 2) * DH_B:(3 * h + 3) * DH_B] = v.astype(bf16)
    gates = _dot(qkv_scr[...], wg_ref[...]) + bg_ref[...]
    ig = gates[:, 0:LANES]
    lf = jax.nn.log_sigmoid(gates[:, LANES:2 * LANES])
    m0 = m_ref[...]
    m_t = jnp.maximum(lf + m0, ig)
    g_in = jnp.exp(ig - m_t)
    decay = jnp.exp(lf + m0 - m_t)
    lane = lax.broadcasted_iota(jnp.int32, ig.shape, 1)
    qk = jnp.zeros(ig.shape, f32)
    qn = jnp.zeros(ig.shape, f32)
    for h in range(H_B):
        hs = slice(h * DH_B, (h + 1) * DH_B)
        n_h = n_ref[:, hs]
        qk = jnp.where(lane == h, jnp.sum(qs[h] * ks[h], axis=-1, keepdims=True), qk)
        qn = jnp.where(lane == h, jnp.sum(qs[h] * n_h, axis=-1, keepdims=True), qn)
        gk = g_in[:, h:h + 1] * ks[h]
        n_o[:, hs] = decay[:, h:h + 1] * n_h + gk
        q_o[:, hs] = qs[h]
        gk_o[:, hs] = gk
        v_o[:, hs] = vs[h]
    s_t = qk * g_in
    den = s_t + decay * qn
    den_o[...] = jnp.maximum(jnp.abs(den), jnp.exp(-m_t))
    dec_o[...] = decay
    m_o[...] = m_t
    for h in range(H_B):
        hs = slice(h * DH_B, (h + 1) * DH_B)
        sv_o[:, hs] = s_t[:, h:h + 1] * vs[h]
    zsig_o[...] = _sigmoid(zb)
    skcb_o[...] = skip_ref[...] * cb

    ccw = ccw_ref[...]
    xcv = ccb_ref[...] + ccw[CONV_C - 1:CONV_C, :] * xc
    for j in range(CONV_C - 1):
        xcv = xcv + ccw[j:j + 1, :] * sc_ref[:, j * W_C:(j + 1) * W_C]
    sc_o[:, 0:(CONV_C - 2) * W_C] = sc_ref[:, W_C:]
    sc_o[:, (CONV_C - 2) * W_C:] = xc
    a_t, u_t = _rglru_gates(xcv, wrg_ref[...], brg_ref[...], lam_ref[...])
    hc = a_t * h_ref[...] + u_t
    h_o[...] = hc
    ac_o[:, W_A:] = hc * jax.nn.gelu(gc)


def _step_read_body(C_ref, qT_ref, qc_ref):
    for bb in range(STEP_BB):
        for h in range(H_B):
            hs = slice(h * DH_B, (h + 1) * DH_B)
            qc_ref[bb:bb + 1, hs] = jnp.sum(C_ref[bb, h] * qT_ref[h, :, bb:bb + 1], axis=0, keepdims=True)


def _step_state_body(C_ref, qT_ref, gkT_ref, v_ref, dec_ref, Cn_ref, qc_ref):
    for bb in range(STEP_BB):
        for h in range(H_B):
            hs = slice(h * DH_B, (h + 1) * DH_B)
            C = C_ref[bb, h]
            qc_ref[bb:bb + 1, hs] = jnp.sum(C * qT_ref[h, :, bb:bb + 1], axis=0, keepdims=True)
            Cn_ref[bb, h] = dec_ref[bb:bb + 1, h:h + 1] * C + gkT_ref[h, :, bb:bb + 1] * v_ref[bb:bb + 1, hs]


def _to_cols(t):
    return t.reshape(t.shape[0] // STEP_BB, STEP_BB, H_B, DH_B).transpose(0, 2, 3, 1)


def _step_read(C_all, layer, q):
    Bs = q.shape[0]
    return pl.pallas_call(
        _step_read_body,
        grid=(Bs // STEP_BB,),
        in_specs=[pl.BlockSpec((None, STEP_BB, H_B, DH_B, DH_B), lambda i: (layer, i, 0, 0, 0)),
                  pl.BlockSpec((None, H_B, DH_B, STEP_BB), lambda i: (i, 0, 0, 0))],
        out_specs=pl.BlockSpec((STEP_BB, W_B), lambda i: (i, 0)),
        out_shape=jax.ShapeDtypeStruct((Bs, W_B), f32),
        compiler_params=pltpu.CompilerParams(dimension_semantics=("arbitrary",), vmem_limit_bytes=VMEM_LIMIT),
        name="step_read",
    )(C_all, _to_cols(q))


def _step_state(C_all, q, gk, v, dec):
    depth, Bs = q.shape[0], q.shape[1]
    cols = lambda t: jnp.stack([_to_cols(t[l]) for l in range(depth)])
    blk5 = pl.BlockSpec((None, STEP_BB, H_B, DH_B, DH_B), lambda l, i: (l, i, 0, 0, 0))
    colspec = pl.BlockSpec((None, None, H_B, DH_B, STEP_BB), lambda l, i: (l, i, 0, 0, 0))
    return pl.pallas_call(
        _step_state_body,
        grid=(depth, Bs // STEP_BB),
        in_specs=[blk5, colspec, colspec,
                  pl.BlockSpec((None, STEP_BB, W_B), lambda l, i: (l, i, 0)),
                  pl.BlockSpec((None, STEP_BB, LANES), lambda l, i: (l, i, 0))],
        out_specs=(blk5, pl.BlockSpec((None, STEP_BB, W_B), lambda l, i: (l, i, 0))),
        out_shape=(jax.ShapeDtypeStruct(C_all.shape, f32), jax.ShapeDtypeStruct((depth, Bs, W_B), f32)),
        compiler_params=pltpu.CompilerParams(dimension_semantics=("arbitrary", "arbitrary"),
                                             vmem_limit_bytes=VMEM_LIMIT),
        name="step_state",
    )(C_all, cols(q), cols(gk), v, dec)


def _step_post_body(x_ref, ac_ref, zsig_ref, skcb_ref, sv_ref, qc_ref, dec_ref, den_ref, gn_ref, wout_ref,
                    x1_ref, mix_scr):
    gn = gn_ref[...]
    dec = dec_ref[...]
    den = den_ref[...]
    mix_scr[:, 0:W_A] = ac_ref[:, 0:W_A].astype(bf16)
    mix_scr[:, W_A + W_B:] = ac_ref[:, W_A:].astype(bf16)
    for h in range(H_B):
        hs = slice(h * DH_B, (h + 1) * DH_B)
        num = sv_ref[:, hs] + dec[:, h:h + 1] * qc_ref[:, hs]
        hb = _head_layernorm(num / den[:, h:h + 1], gn[:, hs])
        mix_scr[:, W_A + h * DH_B:W_A + (h + 1) * DH_B] = (zsig_ref[:, hs] * (hb + skcb_ref[:, hs])).astype(bf16)
    x1_ref[...] = x_ref[...] + _dot(mix_scr[...], wout_ref[...])


def _step_pre(x, st, lw):
    buf_a, buf_b, n0, m0, buf_c, hc0 = st
    Bs = x.shape[0]
    sa = buf_a.reshape(Bs, (CONV_A - 1) * W_A)
    sb = buf_b.reshape(Bs, (CONV_B - 1) * W_B)
    sc = buf_c.reshape(Bs, (CONV_C - 1) * W_C)
    n_in = n0.reshape(Bs, W_B)
    m_in = jnp.pad(m0, ((0, 0), (0, LANES - H_B)))
    sds = lambda *shape: jax.ShapeDtypeStruct(shape, f32)
    pre_in = [x, lw["norm_mix_g"], lw["w_in"], sa, lw["conv_a_w"], lw["conv_a_b"], lw["ln_a_g"], lw["ln_a_b"],
              sb, lw["conv_b_w"], lw["conv_b_b"], lw["w_q"], lw["w_k"], lw["w_v"], lw["w_gates"], lw["b_gates"],
              n_in, m_in, sc, lw["conv_c_w"], lw["conv_c_b"], lw["w_rg"], lw["b_rg"], lw["lam"], hc0, lw["skip_b"]]
    (sa_n, sb_n, sc_n, n_n, m_n, h_n, q, gk, v, dec, den, sv, zsig, skcb, ac) = pl.pallas_call(
        _step_pre_body,
        out_shape=(sds(*sa.shape), sds(*sb.shape), sds(*sc.shape), sds(Bs, W_B), sds(Bs, LANES), sds(Bs, W_C),
                   sds(Bs, W_B), sds(Bs, W_B), sds(Bs, W_B), sds(Bs, LANES), sds(Bs, LANES), sds(Bs, W_B),
                   sds(Bs, W_B), sds(Bs, W_B), sds(Bs, W_A + W_C)),
        scratch_shapes=[pltpu.VMEM((Bs, 3 * W_B), bf16)],
        compiler_params=pltpu.CompilerParams(vmem_limit_bytes=VMEM_LIMIT),
        name="step_pre",
    )(*pre_in)
    new = (sa_n.reshape(buf_a.shape), sb_n.reshape(buf_b.shape), n_n.reshape(n0.shape), m_n[:, :H_B],
           sc_n.reshape(buf_c.shape), h_n)
    return new, (q, gk, v, dec), (ac, zsig, skcb, sv, dec, den)


def _step_post(x, post_in, qc, lw):
    ac, zsig, skcb, sv, dec, den = post_in
    return pl.pallas_call(
        _step_post_body,
        out_shape=jax.ShapeDtypeStruct(x.shape, f32),
        scratch_shapes=[pltpu.VMEM(x.shape, bf16)],
        compiler_params=pltpu.CompilerParams(vmem_limit_bytes=VMEM_LIMIT),
        name="step_post",
    )(x, ac, zsig, skcb, sv, qc, dec, den, lw["gn_b_g"], lw["w_out"])


def _ffn_body(*refs, moe, final_norm):
    it = iter(refs)
    x_ref, g_ref, wg_ref, wu_ref, wd_ref = next(it), next(it), next(it), next(it), next(it)
    wr_ref = next(it) if moe else None
    br_ref = next(it) if moe else None
    gf_ref = next(it) if final_norm else None
    o_ref, hn_scr, acc_scr = next(it), next(it), next(it)
    comb_scr = next(it) if moe else None
    j = pl.program_id(1)

    @pl.when(j == 0)
    def _():
        hn = _rmsnorm(x_ref[...], g_ref[...])
        hn_scr[...] = hn.astype(bf16)
        acc_scr[...] = jnp.zeros(acc_scr.shape, f32)
        if moe:
            lane, i1, i2, g1, g2 = _top2(_router_logits(hn, wr_ref[...], br_ref[...]))
            comb_scr[...] = jnp.where(lane == i1, g1, 0.0) + jnp.where(lane == i2, g2, 0.0)

    hb = hn_scr[...]
    gt = _dot(hb, wg_ref[...])
    up = _dot(hb, wu_ref[...])
    act = (gt * _sigmoid(gt) * up).astype(bf16)
    y = _dot(act, wd_ref[...])
    if moe:
        lane = lax.broadcasted_iota(jnp.int32, comb_scr.shape, 1)
        y = y * jnp.sum(jnp.where(lane == j, comb_scr[...], 0.0), axis=-1, keepdims=True)
    acc_scr[...] += y

    @pl.when(j == pl.num_programs(1) - 1)
    def _():
        out = x_ref[...] + acc_scr[...]
        if final_norm:
            out = _rmsnorm(out, gf_ref[...])
        o_ref[...] = out


def _ffn(x, g, wg, wu, wd, chunks, router=None, final_g=None):
    N = x.shape[0]
    E = wg.shape[0]
    F = wg.shape[2] // chunks
    tm = min(TM_FFN, N)
    moe = router is not None
    assert not moe or chunks == 1
    final_norm = final_g is not None
    ins = [x, g, wg, wu, wd]
    in_specs = [pl.BlockSpec((tm, D_MODEL), lambda i, j: (i, 0)), _full(g.shape),
                pl.BlockSpec((None, D_MODEL, F), lambda i, j: (j // chunks, 0, j % chunks)),
                pl.BlockSpec((None, D_MODEL, F), lambda i, j: (j // chunks, 0, j % chunks)),
                pl.BlockSpec((None, F, D_MODEL), lambda i, j: (j // chunks, j % chunks, 0))]
    scratch = [pltpu.VMEM((tm, D_MODEL), bf16), pltpu.VMEM((tm, D_MODEL), f32)]
    if moe:
        ins += list(router)
        in_specs += [_full(router[0].shape), _full(router[1].shape)]
        scratch.append(pltpu.VMEM((tm, LANES), f32))
    if final_norm:
        ins.append(final_g)
        in_specs.append(_full(final_g.shape))
    return pl.pallas_call(
        functools.partial(_ffn_body, moe=moe, final_norm=final_norm),
        grid=(N // tm, E * chunks),
        in_specs=in_specs,
        out_specs=pl.BlockSpec((tm, D_MODEL), lambda i, j: (i, 0)),
        out_shape=jax.ShapeDtypeStruct((N, D_MODEL), f32),
        scratch_shapes=scratch,
        compiler_params=pltpu.CompilerParams(dimension_semantics=("arbitrary", "arbitrary"),
                                             vmem_limit_bytes=VMEM_LIMIT),
        name="ffn_moe" if moe else "ffn_dense",
    )(*ins)


TM_MOE = 512
SEG_ALIGN = 16
SEG_SIZES = (512, 256, 128, 64, 32, 16)
MOE_ROWS = 2 * TM_MOE + N_EXPERTS * SEG_ALIGN
META_COLS = 3 * N_EXPERTS


def _seg_dma(src, src_off, dst, dst_off, nrows, sem, wait):
    done = jnp.int32(0)
    for size in SEG_SIZES:
        take = (nrows & size) != 0

        @pl.when(take)
        def _(done=done, size=size):
            cp = pltpu.make_async_copy(src.at[pl.ds(pl.multiple_of(src_off + done, SEG_ALIGN), size)],
                                       dst.at[pl.ds(pl.multiple_of(dst_off + done, SEG_ALIGN), size)], sem)
            if wait:
                cp.wait()
            else:
                cp.start()

        done = done + jnp.where(take, size, 0)


def _top2(logits):
    lane = lax.broadcasted_iota(jnp.int32, logits.shape, 1).astype(f32)
    lg = jnp.where(lane < N_EXPERTS, logits, -jnp.inf)
    m1 = jnp.max(lg, axis=-1, keepdims=True)
    i1 = jnp.min(jnp.where(lg == m1, lane, float(LANES)), axis=-1, keepdims=True)
    lg2 = jnp.where(lane == i1, -jnp.inf, lg)
    m2 = jnp.max(lg2, axis=-1, keepdims=True)
    i2 = jnp.min(jnp.where(lg2 == m2, lane, float(LANES)), axis=-1, keepdims=True)
    e2 = jnp.exp(m2 - m1)
    return lane, i1, i2, 1.0 / (1.0 + e2), e2 / (1.0 + e2)


def _router_logits(hn, wr, br):
    h1, h2 = _split_terms(hn, 2)
    w1, w2 = _split_terms(wr, 2)
    return _dot(h1, w1) + (_dot(h1, w2) + _dot(h2, w1)) + br


def _expert_ranks(lane, i1, i2):
    TM = lane.shape[0]
    sel1 = lane == i1
    sel2 = lane == i2
    sel = jnp.where(sel1, 1.0, jnp.where(sel2, 1.0, 0.0))
    earlier = lax.broadcasted_iota(jnp.int32, (TM, TM), 0) > lax.broadcasted_iota(jnp.int32, (TM, TM), 1)
    rank = _dot(jnp.where(earlier, 1.0, 0.0).astype(bf16), sel.astype(bf16))
    return sel1, sel2, rank, rank[TM - 1:TM, :] + sel[TM - 1:TM, :]


def _seg_pad(c):
    return ((c + (SEG_ALIGN - 1)) // SEG_ALIGN) * SEG_ALIGN


def _moe_count_body(x_ref, g_ref, wr_ref, br_ref, route_ref, cnt_ref):
    i = pl.program_id(0)
    hn = _rmsnorm(x_ref[...], g_ref[...])
    lane, i1, i2, g1, g2 = _top2(_router_logits(hn, wr_ref[...], br_ref[...]))
    route_ref[...] = jnp.where(lane == 0, i1, jnp.where(lane == 1, i2, jnp.where(lane == 2, g1,
                                                                                 jnp.where(lane == 3, g2, 0.0))))
    _, _, _, cnt = _expert_ranks(lane, i1, i2)
    for e in range(N_EXPERTS):
        cnt_ref[i, e] = cnt[0, e].astype(jnp.int32)


def _moe_route_body(cnt_ref, x_ref, g_ref, route_ref, xs_hbm, gs_hbm, rinfo_ref, meta_ref, te_ref, nv_ref,
                    xperm, gperm, pos_smem, sem):
    TM, R = TM_MOE, MOE_ROWS
    i = pl.program_id(0)
    nt = pl.num_programs(0)
    n_tiles = te_ref.shape[0]

    @pl.when(i == 0)
    def _():
        base = jnp.int32(0)
        for e in range(N_EXPERTS):
            rows = lax.fori_loop(0, nt, lambda t, a, e=e: a + _seg_pad(cnt_ref[t, e]), jnp.int32(0))
            first_tile = base // TM
            tiles = (rows + (TM - 1)) // TM

            def mark(k, c, e=e, first_tile=first_tile):
                te_ref[first_tile + k] = e
                return c

            lax.fori_loop(0, tiles, mark, 0)
            pos_smem[e] = base
            base = base + tiles * TM
        nv = base // TM
        nv_ref[0] = nv

        def mark_rest(k, c):
            te_ref[k] = N_EXPERTS - 1
            return c

        lax.fori_loop(nv, n_tiles, mark_rest, 0)

    hn = _rmsnorm(x_ref[...], g_ref[...])
    route = route_ref[...]
    i1, i2, g1, g2 = route[:, 0:1], route[:, 1:2], route[:, 2:3], route[:, 3:4]
    lane = lax.broadcasted_iota(jnp.int32, route.shape, 1).astype(f32)
    sel1, sel2, rank, _ = _expert_ranks(lane, i1, i2)
    seg_rows, seg_off = [], []
    acc = jnp.int32(0)
    for e in range(N_EXPERTS):
        seg_rows.append(_seg_pad(cnt_ref[i, e]))
        seg_off.append(acc)
        acc = acc + seg_rows[e]
    lane1 = lax.broadcasted_iota(jnp.int32, (1, LANES), 1)
    offv = jnp.zeros((1, LANES), f32)
    for e in range(N_EXPERTS):
        offv = jnp.where(lane1 == e, seg_off[e].astype(f32), offv)
    dest = rank + offv
    d1 = jnp.sum(jnp.where(sel1, dest, 0.0), axis=-1, keepdims=True)
    d2 = jnp.sum(jnp.where(sel2, dest, 0.0), axis=-1, keepdims=True)
    rinfo = jnp.where(lane == 0, d1, jnp.where(lane == 1, d2, jnp.where(lane == 2, g1, jnp.where(lane == 3, g2, 0.0))))
    rinfo_ref[...] = rinfo
    rinfo_t = rinfo.T
    d1r, d2r, g1r, g2r = rinfo_t[0:1, :], rinfo_t[1:2, :], rinfo_t[2:3, :], rinfo_t[3:4, :]
    riota = lax.broadcasted_iota(jnp.int32, (R, TM), 0).astype(f32)
    hit1 = riota == d1r
    hit2 = riota == d2r
    perm = jnp.where(hit1, 1.0, jnp.where(hit2, 1.0, 0.0)).astype(bf16)
    xperm[...] = _dot(perm, hn.astype(bf16)).astype(bf16)
    gcol = jnp.sum(jnp.where(hit1, g1r, 0.0) + jnp.where(hit2, g2r, 0.0), axis=1, keepdims=True)
    gperm[...] = jnp.broadcast_to(gcol, (R, LANES))

    for e in range(N_EXPERTS):
        pos = pos_smem[e]
        meta_ref[i, e] = pos
        meta_ref[i, N_EXPERTS + e] = seg_off[e]
        meta_ref[i, 2 * N_EXPERTS + e] = seg_rows[e]
        _seg_dma(xperm, seg_off[e], xs_hbm, pos, seg_rows[e], sem.at[0], False)
        _seg_dma(gperm, seg_off[e], gs_hbm, pos, seg_rows[e], sem.at[1], False)
        pos_smem[e] = pos + seg_rows[e]
    for e in range(N_EXPERTS):
        _seg_dma(xperm, seg_off[e], xs_hbm, meta_ref[i, e], seg_rows[e], sem.at[0], True)
        _seg_dma(gperm, seg_off[e], gs_hbm, meta_ref[i, e], seg_rows[e], sem.at[1], True)

    @pl.when(i == nt - 1)
    def _():
        xperm[0:TM, :] = jnp.zeros((TM, D_MODEL), bf16)
        gperm[0:TM, :] = jnp.zeros((TM, LANES), f32)
        pads = []
        for e in range(N_EXPERTS):
            pos = pos_smem[e]
            pads.append((-pos) & (TM - 1))
            _seg_dma(xperm, 0, xs_hbm, pos, pads[e], sem.at[0], False)
            _seg_dma(gperm, 0, gs_hbm, pos, pads[e], sem.at[1], False)
        for e in range(N_EXPERTS):
            _seg_dma(xperm, 0, xs_hbm, pos_smem[e], pads[e], sem.at[0], True)
            _seg_dma(gperm, 0, gs_hbm, pos_smem[e], pads[e], sem.at[1], True)

        def zero_tile(k, c):
            row = pl.multiple_of(k * TM, TM)
            cx = pltpu.make_async_copy(xperm.at[pl.ds(0, TM)], xs_hbm.at[pl.ds(row, TM)], sem.at[0])
            cg = pltpu.make_async_copy(gperm.at[pl.ds(0, TM)], gs_hbm.at[pl.ds(row, TM)], sem.at[1])
            cx.start()
            cg.start()
            cx.wait()
            cg.wait()
            return c

        lax.fori_loop(nv_ref[0], n_tiles, zero_tile, 0)


def _moe_ffn_body(te_ref, nv_ref, xs_ref, gs_ref, wg_ref, wu_ref, wd_ref, ys_ref):
    i = pl.program_id(0)

    @pl.when(i < nv_ref[0])
    def _():
        xb = xs_ref[...]
        gt = _dot(xb, wg_ref[...])
        up = _dot(xb, wu_ref[...])
        act = (gt * _sigmoid(gt) * up).astype(bf16)
        ys_ref[...] = (_dot(act, wd_ref[...]) * gs_ref[:, 0:1]).astype(bf16)

    @pl.when(i >= nv_ref[0])
    def _():
        ys_ref[...] = jnp.zeros(ys_ref.shape, bf16)


def _moe_combine_body(*refs, final_norm):
    it = iter(refs)
    meta_ref, x_ref, rinfo_ref, ys_hbm = next(it), next(it), next(it), next(it)
    gf_ref = next(it) if final_norm else None
    o_ref, yperm, sem = next(it), next(it), next(it)
    TM, R = TM_MOE, MOE_ROWS
    i = pl.program_id(0)

    @pl.when(i == 0)
    def _():
        yperm[...] = jnp.zeros(yperm.shape, bf16)

    for e in range(N_EXPERTS):
        _seg_dma(ys_hbm, meta_ref[i, e], yperm, meta_ref[i, N_EXPERTS + e], meta_ref[i, 2 * N_EXPERTS + e],
                 sem.at[0], False)
    rinfo = rinfo_ref[...]
    ciota = lax.broadcasted_iota(jnp.int32, (TM, R), 1).astype(f32)
    unperm = jnp.where(ciota == rinfo[:, 0:1], 1.0, jnp.where(ciota == rinfo[:, 1:2], 1.0, 0.0)).astype(bf16)
    for e in range(N_EXPERTS):
        _seg_dma(ys_hbm, meta_ref[i, e], yperm, meta_ref[i, N_EXPERTS + e], meta_ref[i, 2 * N_EXPERTS + e],
                 sem.at[0], True)
    out = x_ref[...] + _dot(unperm, yperm[...])
    if final_norm:
        out = _rmsnorm(out, gf_ref[...])
    o_ref[...] = out


def _moe_prompt(x, g, wg, wu, wd, wr, br, final_g):
    N = x.shape[0]
    TM = TM_MOE
    nt = N // TM
    F = wg.shape[2]
    n_tiles = -(-(2 * N + nt * N_EXPERTS * (SEG_ALIGN - 1) + N_EXPERTS * (TM - 1)) // TM)
    rows = n_tiles * TM
    smem = pl.BlockSpec(memory_space=pltpu.SMEM)
    arb = pltpu.CompilerParams(dimension_semantics=("arbitrary",), vmem_limit_bytes=VMEM_LIMIT)
    route, cnt = pl.pallas_call(
        _moe_count_body,
        grid=(nt,),
        in_specs=[pl.BlockSpec((TM, D_MODEL), lambda i: (i, 0)), _full(g.shape), _full(wr.shape), _full(br.shape)],
        out_specs=[pl.BlockSpec((TM, LANES), lambda i: (i, 0)), smem],
        out_shape=[jax.ShapeDtypeStruct((N, LANES), f32), jax.ShapeDtypeStruct((nt, N_EXPERTS), jnp.int32)],
        compiler_params=arb,
        name="moe_count",
    )(x, g, wr, br)

    xs, gs, rinfo, meta, te, nv = pl.pallas_call(
        _moe_route_body,
        grid_spec=pltpu.PrefetchScalarGridSpec(
            num_scalar_prefetch=1,
            grid=(nt,),
            in_specs=[pl.BlockSpec((TM, D_MODEL), lambda i, c: (i, 0)), pl.BlockSpec(g.shape, lambda i, c: (0, 0)),
                      pl.BlockSpec((TM, LANES), lambda i, c: (i, 0))],
            out_specs=[pl.BlockSpec(memory_space=pl.ANY), pl.BlockSpec(memory_space=pl.ANY),
                       pl.BlockSpec((TM, LANES), lambda i, c: (i, 0)), smem, smem, smem],
            scratch_shapes=[pltpu.VMEM((MOE_ROWS, D_MODEL), bf16), pltpu.VMEM((MOE_ROWS, LANES), f32),
                            pltpu.SMEM((N_EXPERTS,), jnp.int32), pltpu.SemaphoreType.DMA((2,))],
        ),
        out_shape=[jax.ShapeDtypeStruct((rows, D_MODEL), bf16), jax.ShapeDtypeStruct((rows, LANES), f32),
                   jax.ShapeDtypeStruct((N, LANES), f32), jax.ShapeDtypeStruct((nt, META_COLS), jnp.int32),
                   jax.ShapeDtypeStruct((n_tiles,), jnp.int32), jax.ShapeDtypeStruct((1,), jnp.int32)],
        compiler_params=arb,
        name="moe_route",
    )(cnt, x, g, route)

    ys = pl.pallas_call(
        _moe_ffn_body,
        grid_spec=pltpu.PrefetchScalarGridSpec(
            num_scalar_prefetch=2,
            grid=(n_tiles,),
            in_specs=[pl.BlockSpec((TM, D_MODEL), lambda i, te, nv: (i, 0)),
                      pl.BlockSpec((TM, LANES), lambda i, te, nv: (i, 0)),
                      pl.BlockSpec((None, D_MODEL, F), lambda i, te, nv: (te[i], 0, 0)),
                      pl.BlockSpec((None, D_MODEL, F), lambda i, te, nv: (te[i], 0, 0)),
                      pl.BlockSpec((None, F, D_MODEL), lambda i, te, nv: (te[i], 0, 0))],
            out_specs=pl.BlockSpec((TM, D_MODEL), lambda i, te, nv: (i, 0)),
        ),
        out_shape=jax.ShapeDtypeStruct((rows, D_MODEL), bf16),
        compiler_params=arb,
        name="moe_ffn",
    )(te, nv, xs, gs, wg, wu, wd)

    final_norm = final_g is not None
    ins = [meta, x, rinfo, ys] + ([final_g] if final_norm else [])
    in_specs = [pl.BlockSpec((TM, D_MODEL), lambda i, m: (i, 0)), pl.BlockSpec((TM, LANES), lambda i, m: (i, 0)),
                pl.BlockSpec(memory_space=pl.ANY)]
    if final_norm:
        in_specs.append(pl.BlockSpec(final_g.shape, lambda i, m: (0, 0)))
    return pl.pallas_call(
        functools.partial(_moe_combine_body, final_norm=final_norm),
        grid_spec=pltpu.PrefetchScalarGridSpec(
            num_scalar_prefetch=1,
            grid=(nt,),
            in_specs=in_specs,
            out_specs=pl.BlockSpec((TM, D_MODEL), lambda i, m: (i, 0)),
            scratch_shapes=[pltpu.VMEM((MOE_ROWS, D_MODEL), bf16), pltpu.SemaphoreType.DMA((1,))],
        ),
        out_shape=jax.ShapeDtypeStruct((N, D_MODEL), f32),
        compiler_params=pltpu.CompilerParams(dimension_semantics=("arbitrary",), vmem_limit_bytes=VMEM_LIMIT),
        name="moe_combine",
    )(*ins)


def _block_diag(w):
    h, di, do = w.shape
    return jnp.einsum("hij,hg->higj", w, jnp.eye(h, dtype=w.dtype)).reshape(h * di, h * do)


def _layer_weights(l, p):
    row = lambda v: v.reshape(1, -1)
    w_gates = jnp.zeros((3 * W_B, 2 * LANES), f32)
    w_gates = w_gates.at[:, 0:H_B].set(p["w_ig"][l]).at[:, LANES:LANES + H_B].set(p["w_fg"][l])
    b_gates = jnp.zeros((1, 2 * LANES), f32)
    b_gates = b_gates.at[0, 0:H_B].set(p["b_ig"][l]).at[0, LANES:LANES + H_B].set(p["b_fg"][l])
    return dict(
        norm_mix_g=row(p["norm_mix_g"][l]), w_in=p["w_in"][l].astype(bf16),
        conv_a_w=p["conv_a_w"][l], conv_a_b=row(p["conv_a_b"][l]), ln_a_g=row(p["ln_a_g"][l]), ln_a_b=row(p["ln_a_b"][l]),
        conv_b_w=p["conv_b_w"][l], conv_b_b=row(p["conv_b_b"][l]),
        w_q=p["w_q"][l].astype(bf16), w_k=p["w_k"][l].astype(bf16), w_v=p["w_v"][l].astype(bf16),
        w_gates=w_gates.astype(bf16), b_gates=b_gates, gn_b_g=row(p["gn_b_g"][l]), skip_b=row(p["skip_b"][l]),
        conv_c_w=p["conv_c_w"][l], conv_c_b=row(p["conv_c_b"][l]),
        w_rg=jnp.concatenate([_block_diag(p["w_ra"][l]), _block_diag(p["w_ix"][l])], axis=1).astype(bf16),
        b_rg=jnp.concatenate([p["b_ra"][l], p["b_ix"][l]]).reshape(1, -1), lam=row(p["lam"][l]),
        w_out=p["w_out"][l].astype(bf16),
    )


def kernel(x_prompt, x_sample, state_conv_a, state_conv_b, state_mlstm_C, state_mlstm_n, state_mlstm_m, state_conv_c, state_rglru_h, norm_mix_g, w_in, conv_a_w, conv_a_b, ln_a_g, ln_a_b, conv_b_w, conv_b_b, w_q, w_k, w_v, w_ig, b_ig, w_fg, b_fg, gn_b_g, skip_b, conv_c_w, conv_c_b, w_ra, b_ra, w_ix, b_ix, lam, w_out, norm_ffn_g, w_gate, w_up, w_down, w_router, b_router, we_gate, we_up, we_down, norm_final_g):
    p = dict(norm_mix_g=norm_mix_g, w_in=w_in, conv_a_w=conv_a_w, conv_a_b=conv_a_b, ln_a_g=ln_a_g, ln_a_b=ln_a_b,
             conv_b_w=conv_b_w, conv_b_b=conv_b_b, w_q=w_q, w_k=w_k, w_v=w_v, w_ig=w_ig, b_ig=b_ig, w_fg=w_fg,
             b_fg=b_fg, gn_b_g=gn_b_g, skip_b=skip_b, conv_c_w=conv_c_w, conv_c_b=conv_c_b, w_ra=w_ra, b_ra=b_ra,
             w_ix=w_ix, b_ix=b_ix, lam=lam, w_out=w_out)
    depth = w_in.shape[0]
    Bp, S, _ = x_prompt.shape
    Bs = x_sample.shape[0]
    sample_states = (state_conv_a, state_conv_b, state_mlstm_C, state_mlstm_n, state_mlstm_m, state_conv_c,
                     state_rglru_h)
    xp = x_prompt
    xs = x_sample.reshape(Bs, D_MODEL)
    new_p, new_s, mem_ops = [], [], []
    C_s = None
    for l in range(depth):
        lw = _layer_weights(l, p)
        xp, sa, sb, sC, sn, sm, sc, sh = _mix_prompt(xp, lw)
        new_p.append((sa, sb, sC, sn, sm[:, :, 0], sc, sh.reshape(Bp, W_C)))
        small, mem, post_in = _step_pre(xs, tuple(s[l] for i, s in enumerate(sample_states) if i != 2), lw)
        new_s.append(small)
        mem_ops.append(mem)
        if l < depth - 1:
            qc = _step_read(state_mlstm_C, l, mem[0])
        else:
            C_s, qc_all = _step_state(state_mlstm_C, *(jnp.stack([m[i] for m in mem_ops]) for i in range(4)))
            qc = qc_all[l]
        xs = _step_post(xs, post_in, qc, lw)
        g = norm_ffn_g[l].reshape(1, -1)
        final_g = norm_final_g.reshape(1, -1) if l == depth - 1 else None
        j = l // 2
        if l % 2 == 0:
            wg, wu, wd = (w[j:j + 1].astype(bf16) for w in (w_gate, w_up, w_down))
            chunks, router = 2, None
        else:
            wg, wu, wd = we_gate[j].astype(bf16), we_up[j].astype(bf16), we_down[j].astype(bf16)
            chunks = 1
            router = (jnp.pad(w_router[j], ((0, 0), (0, LANES - N_EXPERTS))),
                      jnp.pad(b_router[j], (0, LANES - N_EXPERTS)).reshape(1, -1))
        xp2 = xp.reshape(Bp * S, D_MODEL)
        if router is None:
            xp2 = _ffn(xp2, g, wg, wu, wd, chunks, None, final_g)
        else:
            xp2 = _moe_prompt(xp2, g, wg, wu, wd, router[0], router[1], final_g)
        xp = xp2.reshape(Bp, S, D_MODEL)
        xs = _ffn(xs, g, wg, wu, wd, chunks, router, final_g)
    stack = lambda states, i: jnp.stack([st[i] for st in states])
    out_p = tuple(stack(new_p, i) for i in range(7))
    sa_s, sb_s, n_s, m_s, sc_s, h_s = (stack(new_s, i) for i in range(6))
    return (xp, xs.reshape(Bs, 1, D_MODEL)) + out_p + (sa_s, sb_s, C_s, n_s, m_s, sc_s, h_s)
```

```python
import functools

import jax
import jax.numpy as jnp
from jax import lax
from jax.experimental import pallas as pl
from jax.experimental.pallas import tpu as pltpu

f32 = jnp.float32
bf16 = jnp.bfloat16

D_MODEL = 1024
W_A, W_B, W_C = 256, 512, 256
G_A = 4
CONV_A, CONV_B, CONV_C = 31, 4, 4
H_B, DH_B = 4, 128
H_C, DH_C = 4, 64
RG_C = 8.0
D_IN = 2048
N_EXPERTS = 8
NORM_EPS = 1e-6
LN_EPS = 1e-5

LANES = 128
SUBLANES = 8
MXU_WIDTH = 256
VMEM_LIMIT = 56 * 1024 * 1024

T_MIX = 512
MIX_G = 1
L_CHUNK = 128
CONV_ROWS = 128
A_HIST = 32
S_HIST = 8
TM_FFN = 512
STEP_BB = 16


def _dot(a, b):
    return jnp.dot(a, b, preferred_element_type=f32)


def _dot_nt(a, b):
    return lax.dot_general(a, b, (((1,), (1,)), ((), ())), preferred_element_type=f32)


def _dot_tn(a, b):
    return lax.dot_general(a, b, (((0,), (0,)), ((), ())), preferred_element_type=f32)


def _split_terms(x, terms):
    out = []
    r = x
    for _ in range(terms - 1):
        p = r.astype(bf16)
        out.append(p)
        r = r - p.astype(f32)
    out.append(r.astype(bf16))
    return out


def _dot_split_lhs(x, m, terms):
    acc = None
    for p in _split_terms(x, terms):
        d = _dot(p, m)
        acc = d if acc is None else acc + d
    return acc


def _dot_split_rhs(m, x, terms):
    acc = None
    for p in _split_terms(x, terms):
        d = _dot(m, p)
        acc = d if acc is None else acc + d
    return acc


def _sigmoid(x):
    return 0.5 * (1.0 + jnp.tanh(0.5 * x))


def _rmsnorm(x, g):
    return x * lax.rsqrt(jnp.mean(x * x, axis=-1, keepdims=True) + NORM_EPS) * g


def _group_mask_a():
    r = lax.broadcasted_iota(jnp.int32, (W_A, W_A), 0) // (W_A // G_A)
    c = lax.broadcasted_iota(jnp.int32, (W_A, W_A), 1) // (W_A // G_A)
    return jnp.where(r == c, 1.0, 0.0).astype(bf16)


def _group_layernorm_silu(a, gm, g, b):
    inv = 1.0 / (W_A // G_A)
    mu = _dot_split_lhs(a, gm, 2) * inv
    d = a - mu
    var = _dot((d * d).astype(bf16), gm) * inv
    y = d * lax.rsqrt(var + LN_EPS) * g + b
    return y * _sigmoid(y)


def _head_layernorm(h, g):
    mu = jnp.mean(h, axis=-1, keepdims=True)
    d = h - mu
    var = jnp.mean(d * d, axis=-1, keepdims=True)
    return d * lax.rsqrt(var + LN_EPS) * g


def _rglru_gates(xcv, wrg, brg, lam):
    gates = _dot(xcv.astype(bf16), wrg) + brg
    r = _sigmoid(gates[:, :W_C])
    i = _sigmoid(gates[:, W_C:])
    log_a = RG_C * r * jax.nn.log_sigmoid(lam)
    a = jnp.exp(log_a)
    t = jnp.tanh(log_a)
    one_minus_a2 = -2.0 * t / (1.0 - t)
    u = jnp.sqrt(one_minus_a2) * i * xcv
    return a, u


N_MIX_WEIGHTS = 21
N_MIX_OUTPUTS = 8


def _mix_prompt_body(*refs):
    x_ref, weights = refs[0], refs[1:1 + N_MIX_WEIGHTS]
    rest = refs[1 + N_MIX_WEIGHTS:]
    for g in range(MIX_G):
        _mix_tile(x_ref.at[g], *weights, *[r.at[g] for r in rest])


def _mix_tile(x_ref, g_ref, win_ref, caw_ref, cab_ref, lag_ref, lab_ref, cbw_ref, cbb_ref,
              wq_ref, wk_ref, wv_ref, wg_ref, bg_ref, gn_ref, skip_ref, ccw_ref, ccb_ref,
              wrg_ref, brg_ref, lam_ref, wout_ref,
              x1_ref, sa_ref, sb_ref, sC_ref, sn_ref, sm_ref, sc_ref, sh_ref,
              ahist, bhist, chist, conv_scr, C_scr, n_scr, m_scr, h_scr, qkv_scr, mix_scr):
    T, L = T_MIX, L_CHUNK
    s = pl.program_id(1)
    last = pl.num_programs(1) - 1

    @pl.when(s == 0)
    def _():
        ahist[0:A_HIST, :] = jnp.zeros((A_HIST, W_A), f32)
        bhist[0:S_HIST, :] = jnp.zeros((S_HIST, W_B), f32)
        chist[0:S_HIST, :] = jnp.zeros((S_HIST, W_C), f32)
        C_scr[...] = jnp.zeros(C_scr.shape, f32)
        n_scr[...] = jnp.zeros(n_scr.shape, f32)
        m_scr[...] = jnp.zeros(m_scr.shape, f32)
        h_scr[...] = jnp.zeros(h_scr.shape, f32)

    x = x_ref[...]
    hn = _rmsnorm(x, g_ref[...])
    hn_bf = hn.astype(bf16)
    ua = _dot(hn_bf, win_ref[:, 0:2 * W_A])
    ub = _dot(hn_bf, win_ref[:, 2 * W_A:2 * W_A + 2 * W_B])
    uc = _dot(hn_bf, win_ref[:, 2 * W_A + 2 * W_B:])
    xa, ga = ua[:, 0:W_A], ua[:, W_A:]
    xb, zb = ub[:, 0:W_B], ub[:, W_B:]
    xc, gc = uc[:, 0:W_C], uc[:, W_C:]

    ahist[pl.ds(A_HIST, T), :] = xa * _sigmoid(ga)
    caw = caw_ref[...]
    cab = cab_ref[...]
    first = A_HIST - (CONV_A - 1)

    for base in range(0, T, CONV_ROWS):
        for lo in range(0, W_A, LANES):
            blk = ahist[pl.ds(base, CONV_ROWS + A_HIST), lo:lo + LANES]
            acc = jnp.broadcast_to(cab[:, lo:lo + LANES], (CONV_ROWS, LANES))
            for rr in range(SUBLANES):
                n = CONV_ROWS if rr == 0 else CONV_ROWS + SUBLANES
                z = None
                for q in range(A_HIST // SUBLANES + 1):
                    j = SUBLANES * q + rr - first
                    if 0 <= j < CONV_A:
                        term = caw[j:j + 1, lo:lo + LANES] * blk[SUBLANES * q:SUBLANES * q + n, :]
                        z = term if z is None else z + term
                acc = acc + (z if rr == 0 else z[rr:rr + CONV_ROWS, :])
            conv_scr[pl.ds(base, CONV_ROWS), lo:lo + LANES] = acc
    a_out = _group_layernorm_silu(conv_scr[...], _group_mask_a(), lag_ref[...], lab_ref[...])
    y = _dot(a_out.astype(bf16), wout_ref[0:W_A, :])

    bhist[pl.ds(S_HIST, T), :] = xb
    cbw = cbw_ref[...]
    cb = cbb_ref[...] + cbw[3:4, :] * xb
    for j in range(CONV_B - 1):
        cb = cb + cbw[j:j + 1, :] * bhist[pl.ds(S_HIST - (CONV_B - 1) + j, T), :]
    cb = cb * _sigmoid(cb)
    cb_bf = cb.astype(bf16)
    xb_bf = xb.astype(bf16)
    qs, ks, vs = [], [], []
    for h in range(H_B):
        hs = slice(h * DH_B, (h + 1) * DH_B)
        q = _dot(cb_bf[:, hs], wq_ref[h])
        k = _dot(cb_bf[:, hs], wk_ref[h]) * (DH_B ** -0.5)
        v = _dot(xb_bf[:, hs], wv_ref[h])
        qs.append(q)
        ks.append(k)
        vs.append(v)
        qkv_scr[:, (3 * h) * DH_B:(3 * h + 1) * DH_B] = q.astype(bf16)
        qkv_scr[:, (3 * h + 1) * DH_B:(3 * h + 2) * DH_B] = k.astype(bf16)
        qkv_scr[:, (3 * h + 2) * DH_B:(3 * h + 3) * DH_B] = v.astype(bf16)
    gates = _dot(qkv_scr[...], wg_ref[...]) + bg_ref[...]
    ig = gates[:, 0:LANES]
    lf = jax.nn.log_sigmoid(gates[:, LANES:2 * LANES])
    ri = lax.broadcasted_iota(jnp.int32, (L, L), 0)
    ci = lax.broadcasted_iota(jnp.int32, (L, L), 1)
    causal = ri >= ci
    tri = jnp.where(causal, 1.0, 0.0).astype(bf16)
    gn = gn_ref[...]
    skip = skip_ref[...]
    heads = range(H_B)
    stack = lambda parts: jnp.concatenate(parts, axis=0)
    gn_st = stack([jnp.broadcast_to(gn[:, h * DH_B:(h + 1) * DH_B], (L, DH_B)) for h in heads])
    ones_blk = jnp.ones((L, DH_B), bf16)
    C_old = [C_scr[h] for h in heads]
    n_old = [n_scr[h:h + 1, :] for h in heads]
    m_old = [m_scr[h:h + 1, 0:1] for h in heads]
    for c in range(T // L):
        rows = slice(c * L, (c + 1) * L)
        b_all = _dot_split_rhs(tri, lf[rows], 2)
        c_all = ig[rows] - b_all
        c_all_t = c_all.T
        q_bf = [qs[h][rows].astype(bf16) for h in heads]
        k_bf = [ks[h][rows].astype(bf16) for h in heads]
        v_bf = [vs[h][rows].astype(bf16) for h in heads]
        m_prev = stack([jnp.broadcast_to(m_old[h], (L, 1)) for h in heads])
        b_col = stack([b_all[:, h:h + 1] for h in heads])
        c_col = stack([c_all[:, h:h + 1] for h in heads])
        dm = stack([jnp.where(causal, c_all_t[h:h + 1, :], -jnp.inf) for h in heads])
        mx = jnp.maximum(m_prev, jnp.max(dm, axis=1, keepdims=True))
        sc = stack([_dot_nt(q_bf[h], k_bf[h]) for h in heads]) * jnp.exp(dm - mx)
        sc_bf = sc.astype(bf16)
        w_inter = jnp.exp(m_prev - mx)
        sv_aug = [_dot(sc_bf[h * L:(h + 1) * L], jnp.concatenate([v_bf[h], ones_blk], axis=1)) for h in heads]
        sv = stack([a[:, 0:DH_B] for a in sv_aug])
        s_sum = stack([a[:, DH_B:DH_B + 1] for a in sv_aug])
        qC = stack([_dot(q_bf[h], C_old[h].astype(bf16)) for h in heads])
        qn = stack([_dot_nt(q_bf[h], jnp.broadcast_to(n_old[h], (SUBLANES, DH_B)).astype(bf16))[:, 0:1]
                    for h in heads])
        num = sv + w_inter * qC
        den = s_sum + w_inter * qn
        hb = num / jnp.maximum(jnp.abs(den), jnp.exp(-(b_col + mx)))
        hb = _head_layernorm(hb, gn_st)
        mx_last = [mx[(h + 1) * L - 1:(h + 1) * L, :] for h in heads]
        g_in = jnp.exp(c_col - stack([jnp.broadcast_to(mx_last[h], (L, 1)) for h in heads]))
        C_new, n_new, m_new = [], [], []
        for h in heads:
            hs = slice(h * DH_B, (h + 1) * DH_B)
            hr = slice(h * L, (h + 1) * L)
            b_out = _sigmoid(zb[rows, hs]) * (hb[hr] + skip[:, hs] * cb[rows, hs])
            mix_scr[rows, hs] = b_out.astype(bf16)
            gk = g_in[hr] * ks[h][rows]
            decay = jnp.exp(m_old[h] - mx_last[h])
            C_new.append(decay * C_old[h] + _dot_tn(gk.astype(bf16), v_bf[h]))
            n_new.append(decay * n_old[h] + jnp.sum(gk, axis=0, keepdims=True))
            m_new.append(b_all[L - 1:L, h:h + 1] + mx_last[h])
        C_old, n_old, m_old = C_new, n_new, m_new
    for h in heads:
        C_scr[h] = C_old[h]
        n_scr[h:h + 1, :] = n_old[h]
        m_scr[h:h + 1, :] = jnp.broadcast_to(m_old[h], (1, LANES))

    chist[pl.ds(S_HIST, T), :] = xc
    ccw = ccw_ref[...]
    xcv = ccb_ref[...] + ccw[3:4, :] * xc
    for j in range(CONV_C - 1):
        xcv = xcv + ccw[j:j + 1, :] * chist[pl.ds(S_HIST - (CONV_C - 1) + j, T), :]
    a_t, u_t = _rglru_gates(xcv, wrg_ref[...], brg_ref[...], lam_ref[...])
    row = lax.broadcasted_iota(jnp.int32, (T, W_C), 0)
    d = 1
    while d < T:
        if d < SUBLANES:
            a_sh = jnp.where(row >= d, pltpu.roll(a_t, d, axis=0), 1.0)
            u_sh = jnp.where(row >= d, pltpu.roll(u_t, d, axis=0), 0.0)
        else:
            a_sh = jnp.concatenate([jnp.ones((d, W_C), f32), a_t[:T - d]], axis=0)
            u_sh = jnp.concatenate([jnp.zeros((d, W_C), f32), u_t[:T - d]], axis=0)
        u_t = a_t * u_sh + u_t
        a_t = a_t * a_sh
        d *= 2
    hc = a_t * h_scr[...] + u_t
    h_scr[...] = hc[T - 1:T, :]
    y = y + _dot((hc * jax.nn.gelu(gc)).astype(bf16), wout_ref[W_A + W_B:, :])

    x1_ref[...] = x + (y + _dot(mix_scr[...], wout_ref[W_A:W_A + W_B, :]))

    ahist[0:A_HIST, :] = ahist[pl.ds(T, A_HIST), :]
    bhist[0:S_HIST, :] = bhist[pl.ds(T, S_HIST), :]
    chist[0:S_HIST, :] = chist[pl.ds(T, S_HIST), :]

    @pl.when(s == last)
    def _():
        sa_ref[...] = ahist[pl.ds(first, CONV_A - 1), :]
        sb_ref[...] = bhist[pl.ds(S_HIST - (CONV_B - 1), CONV_B - 1), :]
        sc_ref[...] = chist[pl.ds(S_HIST - (CONV_C - 1), CONV_C - 1), :]
        sC_ref[...] = C_scr[...]
        sn_ref[...] = n_scr[0:H_B, :]
        sm_ref[...] = m_scr[0:H_B, :]
        sh_ref[...] = h_scr[...]


def _full(shape):
    nd = len(shape)
    return pl.BlockSpec(shape, lambda *_: (0,) * nd)


def _mix_prompt(x, lw):
    B, S, _ = x.shape
    T = T_MIX
    weights = [lw[k] for k in ("norm_mix_g", "w_in", "conv_a_w", "conv_a_b", "ln_a_g", "ln_a_b", "conv_b_w", "conv_b_b",
                               "w_q", "w_k", "w_v", "w_gates", "b_gates", "gn_b_g", "skip_b", "conv_c_w", "conv_c_b",
                               "w_rg", "b_rg", "lam", "w_out")]
    out_shape = (
        jax.ShapeDtypeStruct((B, S, D_MODEL), f32),
        jax.ShapeDtypeStruct((B, CONV_A - 1, W_A), f32),
        jax.ShapeDtypeStruct((B, CONV_B - 1, W_B), f32),
        jax.ShapeDtypeStruct((B, H_B, DH_B, DH_B), f32),
        jax.ShapeDtypeStruct((B, H_B, DH_B), f32),
        jax.ShapeDtypeStruct((B, H_B, LANES), f32),
        jax.ShapeDtypeStruct((B, CONV_C - 1, W_C), f32),
        jax.ShapeDtypeStruct((B, 1, W_C), f32),
    )
    G = MIX_G
    per_b = lambda shp: pl.BlockSpec((G,) + shp, lambda b, s: (b,) + (0,) * len(shp))
    out_specs = (
        pl.BlockSpec((G, T, D_MODEL), lambda b, s: (b, s, 0)),
        per_b((CONV_A - 1, W_A)), per_b((CONV_B - 1, W_B)), per_b((H_B, DH_B, DH_B)), per_b((H_B, DH_B)),
        per_b((H_B, LANES)), per_b((CONV_C - 1, W_C)), per_b((1, W_C)),
    )
    scratch = [
        pltpu.VMEM((G, A_HIST + T, W_A), f32), pltpu.VMEM((G, S_HIST + T, W_B), f32),
        pltpu.VMEM((G, S_HIST + T, W_C), f32), pltpu.VMEM((G, T, W_A), f32),
        pltpu.VMEM((G, H_B, DH_B, DH_B), f32), pltpu.VMEM((G, SUBLANES, DH_B), f32),
        pltpu.VMEM((G, SUBLANES, LANES), f32), pltpu.VMEM((G, 1, W_C), f32),
        pltpu.VMEM((G, T, 3 * W_B), bf16), pltpu.VMEM((G, T, W_B), bf16),
    ]
    assert len(weights) == N_MIX_WEIGHTS and len(out_shape) == N_MIX_OUTPUTS and B % G == 0
    return pl.pallas_call(
        _mix_prompt_body,
        grid=(B // G, S // T),
        in_specs=[pl.BlockSpec((G, T, D_MODEL), lambda b, s: (b, s, 0))] + [_full(w.shape) for w in weights],
        out_specs=out_specs,
        out_shape=out_shape,
        scratch_shapes=scratch,
        compiler_params=pltpu.CompilerParams(dimension_semantics=("arbitrary", "arbitrary"),
                                             vmem_limit_bytes=VMEM_LIMIT),
        name="mix_prompt",
    )(x, *weights)


def _step_pre_body(x_ref, g_ref, win_ref, sa_ref, caw_ref, cab_ref, lag_ref, lab_ref, sb_ref, cbw_ref, cbb_ref,
                   wq_ref, wk_ref, wv_ref, wg_ref, bg_ref, n_ref, m_ref, sc_ref, ccw_ref, ccb_ref,
                   wrg_ref, brg_ref, lam_ref, h_ref, skip_ref,
                   sa_o, sb_o, sc_o, n_o, m_o, h_o, q_o, gk_o, v_o, dec_o, den_o, sv_o, zsig_o, skcb_o, ac_o,
                   qkv_scr):
    x = x_ref[...]
    hn = _rmsnorm(x, g_ref[...])
    u = _dot(hn.astype(bf16), win_ref[...])
    xa, ga = u[:, 0:256], u[:, 256:512]
    xb, zb = u[:, 512:1024], u[:, 1024:1536]
    xc, gc = u[:, 1536:1792], u[:, 1792:2048]

    a_new = xa * _sigmoid(ga)
    caw = caw_ref[...]
    conv = cab_ref[...] + caw[CONV_A - 1:CONV_A, :] * a_new
    for j in range(CONV_A - 1):
        conv = conv + caw[j:j + 1, :] * sa_ref[:, j * W_A:(j + 1) * W_A]
    sa_o[:, 0:(CONV_A - 2) * W_A] = sa_ref[:, W_A:]
    sa_o[:, (CONV_A - 2) * W_A:] = a_new
    ac_o[:, 0:W_A] = _group_layernorm_silu(conv, _group_mask_a(), lag_ref[...], lab_ref[...])

    cbw = cbw_ref[...]
    cb = cbb_ref[...] + cbw[CONV_B - 1:CONV_B, :] * xb
    for j in range(CONV_B - 1):
        cb = cb + cbw[j:j + 1, :] * sb_ref[:, j * W_B:(j + 1) * W_B]
    sb_o[:, 0:(CONV_B - 2) * W_B] = sb_ref[:, W_B:]
    sb_o[:, (CONV_B - 2) * W_B:] = xb
    cb = cb * _sigmoid(cb)
    cb_bf = cb.astype(bf16)
    xb_bf = xb.astype(bf16)
    qs, ks, vs = [], [], []
    for h in range(H_B):
        hs = slice(h * DH_B, (h + 1) * DH_B)
        q = _dot(cb_bf[:, hs], wq_ref[h])
        k = _dot(cb_bf[:, hs], wk_ref[h]) * (DH_B ** -0.5)
        v = _dot(xb_bf[:, hs], wv_ref[h])
        qs.append(q)
        ks.append(k)
        vs.append(v)
        qkv_scr[:, (3 * h) * DH_B:(3 * h + 1) * DH_B] = q.astype(bf16)
        qkv_scr[:, (3 * h + 1) * DH_B:(3 * h + 2) * DH_B] = k.astype(bf16)
        qkv_scr[:, (3 * h + 2) * DH_B:(3 * h + 3) * DH_B] = v.astype(bf16)
    gates = _dot(qkv_scr[...], wg_ref[...]) + bg_ref[...]
    ig = gates[:, 0:LANES]
    lf = jax.nn.log_sigmoid(gates[:, LANES:2 * LANES])
    m0 = m_ref[...]
    m_t = jnp.maximum(lf + m0, ig)
    g_in = jnp.exp(ig - m_t)
    decay = jnp.exp(lf + m0 - m_t)
    lane = lax.broadcasted_iota(jnp.int32, ig.shape, 1)
    qk = jnp.zeros(ig.shape, f32)
    qn = jnp.zeros(ig.shape, f32)
    for h in range(H_B):
        hs = slice(h * DH_B, (h + 1) * DH_B)
        n_h = n_ref[:, hs]
        qk = jnp.where(lane == h, jnp.sum(qs[h] * ks[h], axis=-1, keepdims=True), qk)
        qn = jnp.where(lane == h, jnp.sum(qs[h] * n_h, axis=-1, keepdims=True), qn)
        gk = g_in[:, h:h + 1] * ks[h]
        n_o[:, hs] = decay[:, h:h + 1] * n_h + gk
        q_o[:, hs] = qs[h]
        gk_o[:, hs] = gk
        v_o[:, hs] = vs[h]
    s_t = qk * g_in
    den = s_t + decay * qn
    den_o[...] = jnp.maximum(jnp.abs(den), jnp.exp(-m_t))
    dec_o[...] = decay
    m_o[...] = m_t
    for h in range(H_B):
        hs = slice(h * DH_B, (h + 1) * DH_B)
        sv_o[:, hs] = s_t[:, h:h + 1] * vs[h]
    zsig_o[...] = _sigmoid(zb)
    skcb_o[...] = skip_ref[...] * cb

    ccw = ccw_ref[...]
    xcv = ccb_ref[...] + ccw[CONV_C - 1:CONV_C, :] * xc
    for j in range(CONV_C - 1):
        xcv = xcv + ccw[j:j + 1, :] * sc_ref[:, j * W_C:(j + 1) * W_C]
    sc_o[:, 0:(CONV_C - 2) * W_C] = sc_ref[:, W_C:]
    sc_o[:, (CONV_C - 2) * W_C:] = xc
    a_t, u_t = _rglru_gates(xcv, wrg_ref[...], brg_ref[...], lam_ref[...])
    hc = a_t * h_ref[...] + u_t
    h_o[...] = hc
    ac_o[:, W_A:] = hc * jax.nn.gelu(gc)


def _step_read_body(C_ref, qT_ref, qc_ref):
    for bb in range(STEP_BB):
        for h in range(H_B):
            hs = slice(h * DH_B, (h + 1) * DH_B)
            qc_ref[bb:bb + 1, hs] = jnp.sum(C_ref[bb, h] * qT_ref[h, :, bb:bb + 1], axis=0, keepdims=True)


def _step_state_body(C_ref, qT_ref, gkT_ref, v_ref, dec_ref, Cn_ref, qc_ref):
    for bb in range(STEP_BB):
        for h in range(H_B):
            hs = slice(h * DH_B, (h + 1) * DH_B)
            C = C_ref[bb, h]
            qc_ref[bb:bb + 1, hs] = jnp.sum(C * qT_ref[h, :, bb:bb + 1], axis=0, keepdims=True)
            Cn_ref[bb, h] = dec_ref[bb:bb + 1, h:h + 1] * C + gkT_ref[h, :, bb:bb + 1] * v_ref[bb:bb + 1, hs]


def _to_cols(t):
    return t.reshape(t.shape[0] // STEP_BB, STEP_BB, H_B, DH_B).transpose(0, 2, 3, 1)


def _step_read(C_all, layer, q):
    Bs = q.shape[0]
    return pl.pallas_call(
        _step_read_body,
        grid=(Bs // STEP_BB,),
        in_specs=[pl.BlockSpec((None, STEP_BB, H_B, DH_B, DH_B), lambda i: (layer, i, 0, 0, 0)),
                  pl.BlockSpec((None, H_B, DH_B, STEP_BB), lambda i: (i, 0, 0, 0))],
        out_specs=pl.BlockSpec((STEP_BB, W_B), lambda i: (i, 0)),
        out_shape=jax.ShapeDtypeStruct((Bs, W_B), f32),
        compiler_params=pltpu.CompilerParams(dimension_semantics=("arbitrary",), vmem_limit_bytes=VMEM_LIMIT),
        name="step_read",
    )(C_all, _to_cols(q))


def _step_state(C_all, q, gk, v, dec):
    depth, Bs = q.shape[0], q.shape[1]
    cols = lambda t: jnp.stack([_to_cols(t[l]) for l in range(depth)])
    blk5 = pl.BlockSpec((None, STEP_BB, H_B, DH_B, DH_B), lambda l, i: (l, i, 0, 0, 0))
    colspec = pl.BlockSpec((None, None, H_B, DH_B, STEP_BB), lambda l, i: (l, i, 0, 0, 0))
    return pl.pallas_call(
        _step_state_body,
        grid=(depth, Bs // STEP_BB),
        in_specs=[blk5, colspec, colspec,
                  pl.BlockSpec((None, STEP_BB, W_B), lambda l, i: (l, i, 0)),
                  pl.BlockSpec((None, STEP_BB, LANES), lambda l, i: (l, i, 0))],
        out_specs=(blk5, pl.BlockSpec((None, STEP_BB, W_B), lambda l, i: (l, i, 0))),
        out_shape=(jax.ShapeDtypeStruct(C_all.shape, f32), jax.ShapeDtypeStruct((depth, Bs, W_B), f32)),
        compiler_params=pltpu.CompilerParams(dimension_semantics=("arbitrary", "arbitrary"),
                                             vmem_limit_bytes=VMEM_LIMIT),
        name="step_state",
    )(C_all, cols(q), cols(gk), v, dec)


def _step_post_body(x_ref, ac_ref, zsig_ref, skcb_ref, sv_ref, qc_ref, dec_ref, den_ref, gn_ref, wout_ref,
                    x1_ref, mix_scr):
    gn = gn_ref[...]
    dec = dec_ref[...]
    den = den_ref[...]
    mix_scr[:, 0:W_A] = ac_ref[:, 0:W_A].astype(bf16)
    mix_scr[:, W_A + W_B:] = ac_ref[:, W_A:].astype(bf16)
    for h in range(H_B):
        hs = slice(h * DH_B, (h + 1) * DH_B)
        num = sv_ref[:, hs] + dec[:, h:h + 1] * qc_ref[:, hs]
        hb = _head_layernorm(num / den[:, h:h + 1], gn[:, hs])
        mix_scr[:, W_A + h * DH_B:W_A + (h + 1) * DH_B] = (zsig_ref[:, hs] * (hb + skcb_ref[:, hs])).astype(bf16)
    x1_ref[...] = x_ref[...] + _dot(mix_scr[...], wout_ref[...])


def _step_pre(x, st, lw):
    buf_a, buf_b, n0, m0, buf_c, hc0 = st
    Bs = x.shape[0]
    sa = buf_a.reshape(Bs, (CONV_A - 1) * W_A)
    sb = buf_b.reshape(Bs, (CONV_B - 1) * W_B)
    sc = buf_c.reshape(Bs, (CONV_C - 1) * W_C)
    n_in = n0.reshape(Bs, W_B)
    m_in = jnp.pad(m0, ((0, 0), (0, LANES - H_B)))
    sds = lambda *shape: jax.ShapeDtypeStruct(shape, f32)
    pre_in = [x, lw["norm_mix_g"], lw["w_in"], sa, lw["conv_a_w"], lw["conv_a_b"], lw["ln_a_g"], lw["ln_a_b"],
              sb, lw["conv_b_w"], lw["conv_b_b"], lw["w_q"], lw["w_k"], lw["w_v"], lw["w_gates"], lw["b_gates"],
              n_in, m_in, sc, lw["conv_c_w"], lw["conv_c_b"], lw["w_rg"], lw["b_rg"], lw["lam"], hc0, lw["skip_b"]]
    (sa_n, sb_n, sc_n, n_n, m_n, h_n, q, gk, v, dec, den, sv, zsig, skcb, ac) = pl.pallas_call(
        _step_pre_body,
        out_shape=(sds(*sa.shape), sds(*sb.shape), sds(*sc.shape), sds(Bs, W_B), sds(Bs, LANES), sds(Bs, W_C),
                   sds(Bs, W_B), sds(Bs, W_B), sds(Bs, W_B), sds(Bs, LANES), sds(Bs, LANES), sds(Bs, W_B),
                   sds(Bs, W_B), sds(Bs, W_B), sds(Bs, W_A + W_C)),
        scratch_shapes=[pltpu.VMEM((Bs, 3 * W_B), bf16)],
        compiler_params=pltpu.CompilerParams(vmem_limit_bytes=VMEM_LIMIT),
        name="step_pre",
    )(*pre_in)
    new = (sa_n.reshape(buf_a.shape), sb_n.reshape(buf_b.shape), n_n.reshape(n0.shape), m_n[:, :H_B],
           sc_n.reshape(buf_c.shape), h_n)
    return new, (q, gk, v, dec), (ac, zsig, skcb, sv, dec, den)


def _step_post(x, post_in, qc, lw):
    ac, zsig, skcb, sv, dec, den = post_in
    return pl.pallas_call(
        _step_post_body,
        out_shape=jax.ShapeDtypeStruct(x.shape, f32),
        scratch_shapes=[pltpu.VMEM(x.shape, bf16)],
        compiler_params=pltpu.CompilerParams(vmem_limit_bytes=VMEM_LIMIT),
        name="step_post",
    )(x, ac, zsig, skcb, sv, qc, dec, den, lw["gn_b_g"], lw["w_out"])


def _swiglu(hb, wg_ref, wu_ref, wd_ref):
    f = wd_ref.shape[0]
    main = f // MXU_WIDTH * MXU_WIDTH
    glu = lambda gt, up: (gt * _sigmoid(gt) * up).astype(bf16)
    if main == f:
        act = glu(_dot(hb, wg_ref[...]), _dot(hb, wu_ref[...]))
    else:
        tail = _dot(hb, jnp.concatenate([wg_ref[:, main:], wu_ref[:, main:]], axis=1))
        act = jnp.concatenate([glu(_dot(hb, wg_ref[:, :main]), _dot(hb, wu_ref[:, :main])),
                               glu(tail[:, :f - main], tail[:, f - main:])], axis=1)
    return _dot(act, wd_ref[...])


def _ffn_body(*refs, moe, final_norm):
    it = iter(refs)
    x_ref, g_ref, wg_ref, wu_ref, wd_ref = next(it), next(it), next(it), next(it), next(it)
    wr_ref = next(it) if moe else None
    br_ref = next(it) if moe else None
    gf_ref = next(it) if final_norm else None
    o_ref, hn_scr, acc_scr = next(it), next(it), next(it)
    comb_scr = next(it) if moe else None
    j = pl.program_id(1)

    @pl.when(j == 0)
    def _():
        hn = _rmsnorm(x_ref[...], g_ref[...])
        hn_scr[...] = hn.astype(bf16)
        acc_scr[...] = jnp.zeros(acc_scr.shape, f32)
        if moe:
            lane, i1, i2, g1, g2 = _top2(_router_logits(hn, wr_ref[...], br_ref[...]))
            comb_scr[...] = jnp.where(lane == i1, g1, 0.0) + jnp.where(lane == i2, g2, 0.0)

    y = _swiglu(hn_scr[...], wg_ref, wu_ref, wd_ref)
    if moe:
        lane = lax.broadcasted_iota(jnp.int32, comb_scr.shape, 1)
        y = y * jnp.sum(jnp.where(lane == j, comb_scr[...], 0.0), axis=-1, keepdims=True)
    acc_scr[...] += y

    @pl.when(j == pl.num_programs(1) - 1)
    def _():
        out = x_ref[...] + acc_scr[...]
        if final_norm:
            out = _rmsnorm(out, gf_ref[...])
        o_ref[...] = out


def _ffn(x, g, wg, wu, wd, router=None, final_g=None):
    N = x.shape[0]
    E, F = wd.shape[0], wd.shape[1]
    tm = min(TM_FFN, N)
    moe = router is not None
    final_norm = final_g is not None
    ins = [x, g, wg, wu, wd]
    once = dict(pipeline_mode=pl.Buffered(1)) if E == 1 else {}
    in_specs = [pl.BlockSpec((tm, D_MODEL), lambda i, j: (i, 0)), _full(g.shape),
                pl.BlockSpec((None, D_MODEL, F), lambda i, j: (j, 0, 0), **once),
                pl.BlockSpec((None, D_MODEL, F), lambda i, j: (j, 0, 0), **once),
                pl.BlockSpec((None, F, D_MODEL), lambda i, j: (j, 0, 0), **once)]
    scratch = [pltpu.VMEM((tm, D_MODEL), bf16), pltpu.VMEM((tm, D_MODEL), f32)]
    if moe:
        ins += list(router)
        in_specs += [_full(router[0].shape), _full(router[1].shape)]
        scratch.append(pltpu.VMEM((tm, LANES), f32))
    if final_norm:
        ins.append(final_g)
        in_specs.append(_full(final_g.shape))
    return pl.pallas_call(
        functools.partial(_ffn_body, moe=moe, final_norm=final_norm),
        grid=(N // tm, E),
        in_specs=in_specs,
        out_specs=pl.BlockSpec((tm, D_MODEL), lambda i, j: (i, 0)),
        out_shape=jax.ShapeDtypeStruct((N, D_MODEL), f32),
        scratch_shapes=scratch,
        compiler_params=pltpu.CompilerParams(dimension_semantics=("arbitrary", "arbitrary"),
                                             vmem_limit_bytes=VMEM_LIMIT),
        name="ffn_moe" if moe else "ffn_dense",
    )(*ins)


TM_MOE = 512
SEG_ALIGN = 16
SEG_SIZES = (512, 256, 128, 64, 32, 16)
MOE_ROWS = 2 * TM_MOE + N_EXPERTS * SEG_ALIGN
META_COLS = 3 * N_EXPERTS


def _seg_dma(src, src_off, dst, dst_off, nrows, sem, wait):
    done = jnp.int32(0)
    for size in SEG_SIZES:
        take = (nrows & size) != 0

        @pl.when(take)
        def _(done=done, size=size):
            cp = pltpu.make_async_copy(src.at[pl.ds(pl.multiple_of(src_off + done, SEG_ALIGN), size)],
                                       dst.at[pl.ds(pl.multiple_of(dst_off + done, SEG_ALIGN), size)], sem)
            if wait:
                cp.wait()
            else:
                cp.start()

        done = done + jnp.where(take, size, 0)


def _top2(logits):
    lane = lax.broadcasted_iota(jnp.int32, logits.shape, 1).astype(f32)
    lg = jnp.where(lane < N_EXPERTS, logits, -jnp.inf)
    m1 = jnp.max(lg, axis=-1, keepdims=True)
    i1 = jnp.min(jnp.where(lg == m1, lane, float(LANES)), axis=-1, keepdims=True)
    lg2 = jnp.where(lane == i1, -jnp.inf, lg)
    m2 = jnp.max(lg2, axis=-1, keepdims=True)
    i2 = jnp.min(jnp.where(lg2 == m2, lane, float(LANES)), axis=-1, keepdims=True)
    e2 = jnp.exp(m2 - m1)
    return lane, i1, i2, 1.0 / (1.0 + e2), e2 / (1.0 + e2)


def _router_logits(hn, wr, br):
    h1, h2 = _split_terms(hn, 2)
    w1, w2 = _split_terms(wr, 2)
    return _dot(h1, w1) + (_dot(h1, w2) + _dot(h2, w1)) + br


def _expert_ranks(lane, i1, i2):
    TM = lane.shape[0]
    sel1 = lane == i1
    sel2 = lane == i2
    sel = jnp.where(sel1, 1.0, jnp.where(sel2, 1.0, 0.0))
    earlier = lax.broadcasted_iota(jnp.int32, (TM, TM), 0) > lax.broadcasted_iota(jnp.int32, (TM, TM), 1)
    rank = _dot(jnp.where(earlier, 1.0, 0.0).astype(bf16), sel.astype(bf16))
    return sel1, sel2, rank, rank[TM - 1:TM, :] + sel[TM - 1:TM, :]


def _seg_pad(c):
    return ((c + (SEG_ALIGN - 1)) // SEG_ALIGN) * SEG_ALIGN


def _moe_count_body(x_ref, g_ref, wr_ref, br_ref, route_ref, cnt_ref):
    i = pl.program_id(0)
    hn = _rmsnorm(x_ref[...], g_ref[...])
    lane, i1, i2, g1, g2 = _top2(_router_logits(hn, wr_ref[...], br_ref[...]))
    route_ref[...] = jnp.where(lane == 0, i1, jnp.where(lane == 1, i2, jnp.where(lane == 2, g1,
                                                                                 jnp.where(lane == 3, g2, 0.0))))
    _, _, _, cnt = _expert_ranks(lane, i1, i2)
    for e in range(N_EXPERTS):
        cnt_ref[i, e] = cnt[0, e].astype(jnp.int32)


def _moe_route_body(cnt_ref, x_ref, g_ref, route_ref, xs_hbm, gs_hbm, rinfo_ref, meta_ref, te_ref, nv_ref,
                    xperm, gperm, pos_smem, sem):
    TM, R = TM_MOE, MOE_ROWS
    i = pl.program_id(0)
    nt = pl.num_programs(0)
    n_tiles = te_ref.shape[0]

    @pl.when(i == 0)
    def _():
        base = jnp.int32(0)
        for e in range(N_EXPERTS):
            rows = lax.fori_loop(0, nt, lambda t, a, e=e: a + _seg_pad(cnt_ref[t, e]), jnp.int32(0))
            first_tile = base // TM
            tiles = (rows + (TM - 1)) // TM

            def mark(k, c, e=e, first_tile=first_tile):
                te_ref[first_tile + k] = e
                return c

            lax.fori_loop(0, tiles, mark, 0)
            pos_smem[e] = base
            base = base + tiles * TM
        nv = base // TM
        nv_ref[0] = nv

        def mark_rest(k, c):
            te_ref[k] = N_EXPERTS - 1
            return c

        lax.fori_loop(nv, n_tiles, mark_rest, 0)

    hn = _rmsnorm(x_ref[...], g_ref[...])
    route = route_ref[...]
    i1, i2, g1, g2 = route[:, 0:1], route[:, 1:2], route[:, 2:3], route[:, 3:4]
    lane = lax.broadcasted_iota(jnp.int32, route.shape, 1).astype(f32)
    sel1, sel2, rank, _ = _expert_ranks(lane, i1, i2)
    seg_rows, seg_off = [], []
    acc = jnp.int32(0)
    for e in range(N_EXPERTS):
        seg_rows.append(_seg_pad(cnt_ref[i, e]))
        seg_off.append(acc)
        acc = acc + seg_rows[e]
    lane1 = lax.broadcasted_iota(jnp.int32, (1, LANES), 1)
    offv = jnp.zeros((1, LANES), f32)
    for e in range(N_EXPERTS):
        offv = jnp.where(lane1 == e, seg_off[e].astype(f32), offv)
    dest = rank + offv
    d1 = jnp.sum(jnp.where(sel1, dest, 0.0), axis=-1, keepdims=True)
    d2 = jnp.sum(jnp.where(sel2, dest, 0.0), axis=-1, keepdims=True)
    rinfo = jnp.where(lane == 0, d1, jnp.where(lane == 1, d2, jnp.where(lane == 2, g1, jnp.where(lane == 3, g2, 0.0))))
    rinfo_ref[...] = rinfo
    rinfo_t = rinfo.T
    d1r, d2r, g1r, g2r = rinfo_t[0:1, :], rinfo_t[1:2, :], rinfo_t[2:3, :], rinfo_t[3:4, :]
    riota = lax.broadcasted_iota(jnp.int32, (R, TM), 0).astype(f32)
    hit1 = riota == d1r
    hit2 = riota == d2r
    perm = jnp.where(hit1, 1.0, jnp.where(hit2, 1.0, 0.0)).astype(bf16)
    xperm[...] = _dot(perm, hn.astype(bf16)).astype(bf16)
    gcol = jnp.sum(jnp.where(hit1, g1r, 0.0) + jnp.where(hit2, g2r, 0.0), axis=1, keepdims=True)
    gperm[...] = jnp.broadcast_to(gcol, (R, LANES))

    for e in range(N_EXPERTS):
        pos = pos_smem[e]
        meta_ref[i, e] = pos
        meta_ref[i, N_EXPERTS + e] = seg_off[e]
        meta_ref[i, 2 * N_EXPERTS + e] = seg_rows[e]
        _seg_dma(xperm, seg_off[e], xs_hbm, pos, seg_rows[e], sem.at[0], False)
        _seg_dma(gperm, seg_off[e], gs_hbm, pos, seg_rows[e], sem.at[1], False)
        pos_smem[e] = pos + seg_rows[e]
    for e in range(N_EXPERTS):
        _seg_dma(xperm, seg_off[e], xs_hbm, meta_ref[i, e], seg_rows[e], sem.at[0], True)
        _seg_dma(gperm, seg_off[e], gs_hbm, meta_ref[i, e], seg_rows[e], sem.at[1], True)

    @pl.when(i == nt - 1)
    def _():
        xperm[0:TM, :] = jnp.zeros((TM, D_MODEL), bf16)
        gperm[0:TM, :] = jnp.zeros((TM, LANES), f32)
        pads = []
        for e in range(N_EXPERTS):
            pos = pos_smem[e]
            pads.append((-pos) & (TM - 1))
            _seg_dma(xperm, 0, xs_hbm, pos, pads[e], sem.at[0], False)
            _seg_dma(gperm, 0, gs_hbm, pos, pads[e], sem.at[1], False)
        for e in range(N_EXPERTS):
            _seg_dma(xperm, 0, xs_hbm, pos_smem[e], pads[e], sem.at[0], True)
            _seg_dma(gperm, 0, gs_hbm, pos_smem[e], pads[e], sem.at[1], True)

        def zero_tile(k, c):
            row = pl.multiple_of(k * TM, TM)
            cx = pltpu.make_async_copy(xperm.at[pl.ds(0, TM)], xs_hbm.at[pl.ds(row, TM)], sem.at[0])
            cg = pltpu.make_async_copy(gperm.at[pl.ds(0, TM)], gs_hbm.at[pl.ds(row, TM)], sem.at[1])
            cx.start()
            cg.start()
            cx.wait()
            cg.wait()
            return c

        lax.fori_loop(nv_ref[0], n_tiles, zero_tile, 0)


def _moe_ffn_body(te_ref, nv_ref, xs_ref, gs_ref, wg_ref, wu_ref, wd_ref, ys_ref):
    i = pl.program_id(0)

    @pl.when(i < nv_ref[0])
    def _():
        ys_ref[...] = (_swiglu(xs_ref[...], wg_ref, wu_ref, wd_ref) * gs_ref[:, 0:1]).astype(bf16)

    @pl.when(i >= nv_ref[0])
    def _():
        ys_ref[...] = jnp.zeros(ys_ref.shape, bf16)


def _moe_combine_body(*refs, final_norm):
    it = iter(refs)
    meta_ref, x_ref, rinfo_ref, ys_hbm = next(it), next(it), next(it), next(it)
    gf_ref = next(it) if final_norm else None
    o_ref, yperm, sem = next(it), next(it), next(it)
    TM, R = TM_MOE, MOE_ROWS
    i = pl.program_id(0)

    @pl.when(i == 0)
    def _():
        yperm[...] = jnp.zeros(yperm.shape, bf16)

    for e in range(N_EXPERTS):
        _seg_dma(ys_hbm, meta_ref[i, e], yperm, meta_ref[i, N_EXPERTS + e], meta_ref[i, 2 * N_EXPERTS + e],
                 sem.at[0], False)
    rinfo = rinfo_ref[...]
    ciota = lax.broadcasted_iota(jnp.int32, (TM, R), 1).astype(f32)
    unperm = jnp.where(ciota == rinfo[:, 0:1], 1.0, jnp.where(ciota == rinfo[:, 1:2], 1.0, 0.0)).astype(bf16)
    for e in range(N_EXPERTS):
        _seg_dma(ys_hbm, meta_ref[i, e], yperm, meta_ref[i, N_EXPERTS + e], meta_ref[i, 2 * N_EXPERTS + e],
                 sem.at[0], True)
    out = x_ref[...] + _dot(unperm, yperm[...])
    if final_norm:
        out = _rmsnorm(out, gf_ref[...])
    o_ref[...] = out


def _moe_prompt(x, g, wg, wu, wd, wr, br, final_g):
    N = x.shape[0]
    TM = TM_MOE
    nt = N // TM
    F = wd.shape[1]
    n_tiles = -(-(2 * N + nt * N_EXPERTS * (SEG_ALIGN - 1) + N_EXPERTS * (TM - 1)) // TM)
    rows = n_tiles * TM
    smem = pl.BlockSpec(memory_space=pltpu.SMEM)
    arb = pltpu.CompilerParams(dimension_semantics=("arbitrary",), vmem_limit_bytes=VMEM_LIMIT)
    route, cnt = pl.pallas_call(
        _moe_count_body,
        grid=(nt,),
        in_specs=[pl.BlockSpec((TM, D_MODEL), lambda i: (i, 0)), _full(g.shape), _full(wr.shape), _full(br.shape)],
        out_specs=[pl.BlockSpec((TM, LANES), lambda i: (i, 0)), smem],
        out_shape=[jax.ShapeDtypeStruct((N, LANES), f32), jax.ShapeDtypeStruct((nt, N_EXPERTS), jnp.int32)],
        compiler_params=arb,
        name="moe_count",
    )(x, g, wr, br)

    xs, gs, rinfo, meta, te, nv = pl.pallas_call(
        _moe_route_body,
        grid_spec=pltpu.PrefetchScalarGridSpec(
            num_scalar_prefetch=1,
            grid=(nt,),
            in_specs=[pl.BlockSpec((TM, D_MODEL), lambda i, c: (i, 0)), pl.BlockSpec(g.shape, lambda i, c: (0, 0)),
                      pl.BlockSpec((TM, LANES), lambda i, c: (i, 0))],
            out_specs=[pl.BlockSpec(memory_space=pl.ANY), pl.BlockSpec(memory_space=pl.ANY),
                       pl.BlockSpec((TM, LANES), lambda i, c: (i, 0)), smem, smem, smem],
            scratch_shapes=[pltpu.VMEM((MOE_ROWS, D_MODEL), bf16), pltpu.VMEM((MOE_ROWS, LANES), f32),
                            pltpu.SMEM((N_EXPERTS,), jnp.int32), pltpu.SemaphoreType.DMA((2,))],
        ),
        out_shape=[jax.ShapeDtypeStruct((rows, D_MODEL), bf16), jax.ShapeDtypeStruct((rows, LANES), f32),
                   jax.ShapeDtypeStruct((N, LANES), f32), jax.ShapeDtypeStruct((nt, META_COLS), jnp.int32),
                   jax.ShapeDtypeStruct((n_tiles,), jnp.int32), jax.ShapeDtypeStruct((1,), jnp.int32)],
        compiler_params=arb,
        name="moe_route",
    )(cnt, x, g, route)

    ys = pl.pallas_call(
        _moe_ffn_body,
        grid_spec=pltpu.PrefetchScalarGridSpec(
            num_scalar_prefetch=2,
            grid=(n_tiles,),
            in_specs=[pl.BlockSpec((TM, D_MODEL), lambda i, te, nv: (i, 0)),
                      pl.BlockSpec((TM, LANES), lambda i, te, nv: (i, 0)),
                      pl.BlockSpec((None, D_MODEL, F), lambda i, te, nv: (te[i], 0, 0)),
                      pl.BlockSpec((None, D_MODEL, F), lambda i, te, nv: (te[i], 0, 0)),
                      pl.BlockSpec((None, F, D_MODEL), lambda i, te, nv: (te[i], 0, 0))],
            out_specs=pl.BlockSpec((TM, D_MODEL), lambda i, te, nv: (i, 0)),
        ),
        out_shape=jax.ShapeDtypeStruct((rows, D_MODEL), bf16),
        compiler_params=arb,
        name="moe_ffn",
    )(te, nv, xs, gs, wg, wu, wd)

    final_norm = final_g is not None
    ins = [meta, x, rinfo, ys] + ([final_g] if final_norm else [])
    in_specs = [pl.BlockSpec((TM, D_MODEL), lambda i, m: (i, 0)), pl.BlockSpec((TM, LANES), lambda i, m: (i, 0)),
                pl.BlockSpec(memory_space=pl.ANY)]
    if final_norm:
        in_specs.append(pl.BlockSpec(final_g.shape, lambda i, m: (0, 0)))
    return pl.pallas_call(
        functools.partial(_moe_combine_body, final_norm=final_norm),
        grid_spec=pltpu.PrefetchScalarGridSpec(
            num_scalar_prefetch=1,
            grid=(nt,),
            in_specs=in_specs,
            out_specs=pl.BlockSpec((TM, D_MODEL), lambda i, m: (i, 0)),
            scratch_shapes=[pltpu.VMEM((MOE_ROWS, D_MODEL), bf16), pltpu.SemaphoreType.DMA((1,))],
        ),
        out_shape=jax.ShapeDtypeStruct((N, D_MODEL), f32),
        compiler_params=pltpu.CompilerParams(dimension_semantics=("arbitrary",), vmem_limit_bytes=VMEM_LIMIT),
        name="moe_combine",
    )(*ins)


def _block_diag(w):
    h, di, do = w.shape
    return jnp.einsum("hij,hg->higj", w, jnp.eye(h, dtype=w.dtype)).reshape(h * di, h * do)


def _layer_weights(l, p):
    row = lambda v: v.reshape(1, -1)
    w_gates = jnp.zeros((3 * W_B, 2 * LANES), f32)
    w_gates = w_gates.at[:, 0:H_B].set(p["w_ig"][l]).at[:, LANES:LANES + H_B].set(p["w_fg"][l])
    b_gates = jnp.zeros((1, 2 * LANES), f32)
    b_gates = b_gates.at[0, 0:H_B].set(p["b_ig"][l]).at[0, LANES:LANES + H_B].set(p["b_fg"][l])
    return dict(
        norm_mix_g=row(p["norm_mix_g"][l]), w_in=p["w_in"][l].astype(bf16),
        conv_a_w=p["conv_a_w"][l], conv_a_b=row(p["conv_a_b"][l]), ln_a_g=row(p["ln_a_g"][l]), ln_a_b=row(p["ln_a_b"][l]),
        conv_b_w=p["conv_b_w"][l], conv_b_b=row(p["conv_b_b"][l]),
        w_q=p["w_q"][l].astype(bf16), w_k=p["w_k"][l].astype(bf16), w_v=p["w_v"][l].astype(bf16),
        w_gates=w_gates.astype(bf16), b_gates=b_gates, gn_b_g=row(p["gn_b_g"][l]), skip_b=row(p["skip_b"][l]),
        conv_c_w=p["conv_c_w"][l], conv_c_b=row(p["conv_c_b"][l]),
        w_rg=jnp.concatenate([_block_diag(p["w_ra"][l]), _block_diag(p["w_ix"][l])], axis=1).astype(bf16),
        b_rg=jnp.concatenate([p["b_ra"][l], p["b_ix"][l]]).reshape(1, -1), lam=row(p["lam"][l]),
        w_out=p["w_out"][l].astype(bf16),
    )


def kernel(x_prompt, x_sample, state_conv_a, state_conv_b, state_mlstm_C, state_mlstm_n, state_mlstm_m, state_conv_c, state_rglru_h, norm_mix_g, w_in, conv_a_w, conv_a_b, ln_a_g, ln_a_b, conv_b_w, conv_b_b, w_q, w_k, w_v, w_ig, b_ig, w_fg, b_fg, gn_b_g, skip_b, conv_c_w, conv_c_b, w_ra, b_ra, w_ix, b_ix, lam, w_out, norm_ffn_g, w_gate, w_up, w_down, w_router, b_router, we_gate, we_up, we_down, norm_final_g):
    p = dict(norm_mix_g=norm_mix_g, w_in=w_in, conv_a_w=conv_a_w, conv_a_b=conv_a_b, ln_a_g=ln_a_g, ln_a_b=ln_a_b,
             conv_b_w=conv_b_w, conv_b_b=conv_b_b, w_q=w_q, w_k=w_k, w_v=w_v, w_ig=w_ig, b_ig=b_ig, w_fg=w_fg,
             b_fg=b_fg, gn_b_g=gn_b_g, skip_b=skip_b, conv_c_w=conv_c_w, conv_c_b=conv_c_b, w_ra=w_ra, b_ra=b_ra,
             w_ix=w_ix, b_ix=b_ix, lam=lam, w_out=w_out)
    depth = w_in.shape[0]
    Bp, S, _ = x_prompt.shape
    Bs = x_sample.shape[0]
    sample_states = (state_conv_a, state_conv_b, state_mlstm_C, state_mlstm_n, state_mlstm_m, state_conv_c,
                     state_rglru_h)
    xp = x_prompt
    xs = x_sample.reshape(Bs, D_MODEL)
    new_p, new_s, mem_ops = [], [], []
    C_s = None
    for l in range(depth):
        lw = _layer_weights(l, p)
        xp, sa, sb, sC, sn, sm, sc, sh = _mix_prompt(xp, lw)
        new_p.append((sa, sb, sC, sn, sm[:, :, 0], sc, sh.reshape(Bp, W_C)))
        small, mem, post_in = _step_pre(xs, tuple(s[l] for i, s in enumerate(sample_states) if i != 2), lw)
        new_s.append(small)
        mem_ops.append(mem)
        if l < depth - 1:
            qc = _step_read(state_mlstm_C, l, mem[0])
        else:
            C_s, qc_all = _step_state(state_mlstm_C, *(jnp.stack([m[i] for m in mem_ops]) for i in range(4)))
            qc = qc_all[l]
        xs = _step_post(xs, post_in, qc, lw)
        g = norm_ffn_g[l].reshape(1, -1)
        final_g = norm_final_g.reshape(1, -1) if l == depth - 1 else None
        j = l // 2
        if l % 2 == 0:
            wg, wu, wd = (w[j:j + 1].astype(bf16) for w in (w_gate, w_up, w_down))
            router = None
        else:
            wg, wu, wd = we_gate[j].astype(bf16), we_up[j].astype(bf16), we_down[j].astype(bf16)
            router = (jnp.pad(w_router[j], ((0, 0), (0, LANES - N_EXPERTS))),
                      jnp.pad(b_router[j], (0, LANES - N_EXPERTS)).reshape(1, -1))
        xp2 = xp.reshape(Bp * S, D_MODEL)
        if router is None:
            xp2 = _ffn(xp2, g, wg, wu, wd, None, final_g)
        else:
            xp2 = _moe_prompt(xp2, g, wg, wu, wd, router[0], router[1], final_g)
        xp = xp2.reshape(Bp, S, D_MODEL)
        xs = _ffn(xs, g, wg, wu, wd, router, final_g)
    stack = lambda states, i: jnp.stack([st[i] for st in states])
    out_p = tuple(stack(new_p, i) for i in range(7))
    sa_s, sb_s, n_s, m_s, sc_s, h_s = (stack(new_s, i) for i in range(6))
    return (xp, xs.reshape(Bs, 1, D_MODEL)) + out_p + (sa_s, sb_s, C_s, n_s, m_s, sc_s, h_s)
```

```python
import functools

import jax
import jax.numpy as jnp
from jax import lax
from jax.experimental import pallas as pl
from jax.experimental.pallas import tpu as pltpu

f32 = jnp.float32
bf16 = jnp.bfloat16

D_MODEL = 1024
W_A, W_B, W_C = 256, 512, 256
G_A = 4
CONV_A, CONV_B, CONV_C = 31, 4, 4
H_B, DH_B = 4, 128
H_C, DH_C = 4, 64
RG_C = 8.0
D_IN = 2048
N_EXPERTS = 8
NORM_EPS = 1e-6
LN_EPS = 1e-5

LANES = 128
SUBLANES = 8
MXU_WIDTH = 256
VMEM_LIMIT = 56 * 1024 * 1024

T_MIX = 512
MIX_G = 1
L_CHUNK = 128
CONV_ROWS = 128
A_HIST = 32
S_HIST = 8
TM_FFN = 512
STEP_BB = 16


def _dot(a, b):
    return jnp.dot(a, b, preferred_element_type=f32)


def _dot_nt(a, b):
    return lax.dot_general(a, b, (((1,), (1,)), ((), ())), preferred_element_type=f32)


def _dot_tn(a, b):
    return lax.dot_general(a, b, (((0,), (0,)), ((), ())), preferred_element_type=f32)


def _split_terms(x, terms):
    out = []
    r = x
    for _ in range(terms - 1):
        p = r.astype(bf16)
        out.append(p)
        r = r - p.astype(f32)
    out.append(r.astype(bf16))
    return out


def _dot_split_lhs(x, m, terms):
    acc = None
    for p in _split_terms(x, terms):
        d = _dot(p, m)
        acc = d if acc is None else acc + d
    return acc


def _dot_split_rhs(m, x, terms):
    acc = None
    for p in _split_terms(x, terms):
        d = _dot(m, p)
        acc = d if acc is None else acc + d
    return acc


def _sigmoid(x):
    return 0.5 * (1.0 + jnp.tanh(0.5 * x))


def _rmsnorm(x, g):
    return x * lax.rsqrt(jnp.mean(x * x, axis=-1, keepdims=True) + NORM_EPS) * g


def _group_mask_a():
    r = lax.broadcasted_iota(jnp.int32, (W_A, W_A), 0) // (W_A // G_A)
    c = lax.broadcasted_iota(jnp.int32, (W_A, W_A), 1) // (W_A // G_A)
    return jnp.where(r == c, 1.0, 0.0).astype(bf16)


def _group_layernorm_silu(a, gm, g, b):
    inv = 1.0 / (W_A // G_A)
    mu = _dot_split_lhs(a, gm, 2) * inv
    d = a - mu
    var = _dot((d * d).astype(bf16), gm) * inv
    y = d * lax.rsqrt(var + LN_EPS) * g + b
    return y * _sigmoid(y)


def _head_layernorm(h, g):
    mu = jnp.mean(h, axis=-1, keepdims=True)
    d = h - mu
    var = jnp.mean(d * d, axis=-1, keepdims=True)
    return d * lax.rsqrt(var + LN_EPS) * g


def _rglru_gates(xcv, wrg, brg, lam):
    gates = _dot(xcv.astype(bf16), wrg) + brg
    r = _sigmoid(gates[:, :W_C])
    i = _sigmoid(gates[:, W_C:])
    log_a = RG_C * r * jax.nn.log_sigmoid(lam)
    a = jnp.exp(log_a)
    t = jnp.tanh(log_a)
    one_minus_a2 = -2.0 * t / (1.0 - t)
    u = jnp.sqrt(one_minus_a2) * i * xcv
    return a, u


N_MIX_WEIGHTS = 21
N_MIX_OUTPUTS = 8


def _mix_prompt_body(*refs):
    x_ref, weights = refs[0], refs[1:1 + N_MIX_WEIGHTS]
    rest = refs[1 + N_MIX_WEIGHTS:]
    for g in range(MIX_G):
        _mix_tile(x_ref.at[g], *weights, *[r.at[g] for r in rest])


def _mix_tile(x_ref, g_ref, win_ref, caw_ref, cab_ref, lag_ref, lab_ref, cbw_ref, cbb_ref,
              wq_ref, wk_ref, wv_ref, wg_ref, bg_ref, gn_ref, skip_ref, ccw_ref, ccb_ref,
              wrg_ref, brg_ref, lam_ref, wout_ref,
              x1_ref, sa_ref, sb_ref, sC_ref, sn_ref, sm_ref, sc_ref, sh_ref,
              ahist, bhist, chist, conv_scr, C_scr, n_scr, m_scr, h_scr, qkv_scr, mix_scr):
    T, L = T_MIX, L_CHUNK
    s = pl.program_id(1)
    last = pl.num_programs(1) - 1

    @pl.when(s == 0)
    def _():
        ahist[0:A_HIST, :] = jnp.zeros((A_HIST, W_A), f32)
        bhist[0:S_HIST, :] = jnp.zeros((S_HIST, W_B), f32)
        chist[0:S_HIST, :] = jnp.zeros((S_HIST, W_C), f32)
        C_scr[...] = jnp.zeros(C_scr.shape, f32)
        n_scr[...] = jnp.zeros(n_scr.shape, f32)
        m_scr[...] = jnp.zeros(m_scr.shape, f32)
        h_scr[...] = jnp.zeros(h_scr.shape, f32)

    x = x_ref[...]
    hn = _rmsnorm(x, g_ref[...])
    hn_bf = hn.astype(bf16)
    ua = _dot(hn_bf, win_ref[:, 0:2 * W_A])
    ub = _dot(hn_bf, win_ref[:, 2 * W_A:2 * W_A + 2 * W_B])
    uc = _dot(hn_bf, win_ref[:, 2 * W_A + 2 * W_B:])
    xa, ga = ua[:, 0:W_A], ua[:, W_A:]
    xb, zb = ub[:, 0:W_B], ub[:, W_B:]
    xc, gc = uc[:, 0:W_C], uc[:, W_C:]

    ahist[pl.ds(A_HIST, T), :] = xa * _sigmoid(ga)
    caw = caw_ref[...]
    cab = cab_ref[...]
    first = A_HIST - (CONV_A - 1)

    for base in range(0, T, CONV_ROWS):
        for lo in range(0, W_A, LANES):
            blk = ahist[pl.ds(base, CONV_ROWS + A_HIST), lo:lo + LANES]
            acc = jnp.broadcast_to(cab[:, lo:lo + LANES], (CONV_ROWS, LANES))
            for rr in range(SUBLANES):
                n = CONV_ROWS if rr == 0 else CONV_ROWS + SUBLANES
                z = None
                for q in range(A_HIST // SUBLANES + 1):
                    j = SUBLANES * q + rr - first
                    if 0 <= j < CONV_A:
                        term = caw[j:j + 1, lo:lo + LANES] * blk[SUBLANES * q:SUBLANES * q + n, :]
                        z = term if z is None else z + term
                acc = acc + (z if rr == 0 else z[rr:rr + CONV_ROWS, :])
            conv_scr[pl.ds(base, CONV_ROWS), lo:lo + LANES] = acc
    a_out = _group_layernorm_silu(conv_scr[...], _group_mask_a(), lag_ref[...], lab_ref[...])
    y = _dot(a_out.astype(bf16), wout_ref[0:W_A, :])

    bhist[pl.ds(S_HIST, T), :] = xb
    cbw = cbw_ref[...]
    cb = cbb_ref[...] + cbw[3:4, :] * xb
    for j in range(CONV_B - 1):
        cb = cb + cbw[j:j + 1, :] * bhist[pl.ds(S_HIST - (CONV_B - 1) + j, T), :]
    cb = cb * _sigmoid(cb)
    cb_bf = cb.astype(bf16)
    xb_bf = xb.astype(bf16)
    qs, ks, vs = [], [], []
    for h in range(H_B):
        hs = slice(h * DH_B, (h + 1) * DH_B)
        q = _dot(cb_bf[:, hs], wq_ref[h])
        k = _dot(cb_bf[:, hs], wk_ref[h]) * (DH_B ** -0.5)
        v = _dot(xb_bf[:, hs], wv_ref[h])
        qs.append(q)
        ks.append(k)
        vs.append(v)
        qkv_scr[:, (3 * h) * DH_B:(3 * h + 1) * DH_B] = q.astype(bf16)
        qkv_scr[:, (3 * h + 1) * DH_B:(3 * h + 2) * DH_B] = k.astype(bf16)
        qkv_scr[:, (3 * h + 2) * DH_B:(3 * h + 3) * DH_B] = v.astype(bf16)
    gates = _dot(qkv_scr[...], wg_ref[...]) + bg_ref[...]
    ig = gates[:, 0:LANES]
    lf = jax.nn.log_sigmoid(gates[:, LANES:2 * LANES])
    ri = lax.broadcasted_iota(jnp.int32, (L, L), 0)
    ci = lax.broadcasted_iota(jnp.int32, (L, L), 1)
    causal = ri >= ci
    tri = jnp.where(causal, 1.0, 0.0).astype(bf16)
    gn = gn_ref[...]
    skip = skip_ref[...]
    heads = range(H_B)
    stack = lambda parts: jnp.concatenate(parts, axis=0)
    gn_st = stack([jnp.broadcast_to(gn[:, h * DH_B:(h + 1) * DH_B], (L, DH_B)) for h in heads])
    ones_blk = jnp.ones((L, DH_B), bf16)
    C_old = [C_scr[h] for h in heads]
    n_old = [n_scr[h:h + 1, :] for h in heads]
    m_old = [m_scr[h:h + 1, 0:1] for h in heads]
    for c in range(T // L):
        rows = slice(c * L, (c + 1) * L)
        b_all = _dot_split_rhs(tri, lf[rows], 2)
        c_all = ig[rows] - b_all
        c_all_t = c_all.T
        q_bf = [qs[h][rows].astype(bf16) for h in heads]
        k_bf = [ks[h][rows].astype(bf16) for h in heads]
        v_bf = [vs[h][rows].astype(bf16) for h in heads]
        m_prev = stack([jnp.broadcast_to(m_old[h], (L, 1)) for h in heads])
        b_col = stack([b_all[:, h:h + 1] for h in heads])
        c_col = stack([c_all[:, h:h + 1] for h in heads])
        dm = stack([jnp.where(causal, c_all_t[h:h + 1, :], -jnp.inf) for h in heads])
        mx = jnp.maximum(m_prev, jnp.max(dm, axis=1, keepdims=True))
        sc = stack([_dot_nt(q_bf[h], k_bf[h]) for h in heads]) * jnp.exp(dm - mx)
        sc_bf = sc.astype(bf16)
        w_inter = jnp.exp(m_prev - mx)
        sv_aug = [_dot(sc_bf[h * L:(h + 1) * L], jnp.concatenate([v_bf[h], ones_blk], axis=1)) for h in heads]
        sv = stack([a[:, 0:DH_B] for a in sv_aug])
        s_sum = stack([a[:, DH_B:DH_B + 1] for a in sv_aug])
        qC = stack([_dot(q_bf[h], C_old[h].astype(bf16)) for h in heads])
        qn = stack([_dot_nt(q_bf[h], jnp.broadcast_to(n_old[h], (SUBLANES, DH_B)).astype(bf16))[:, 0:1]
                    for h in heads])
        num = sv + w_inter * qC
        den = s_sum + w_inter * qn
        hb = num / jnp.maximum(jnp.abs(den), jnp.exp(-(b_col + mx)))
        hb = _head_layernorm(hb, gn_st)
        mx_last = [mx[(h + 1) * L - 1:(h + 1) * L, :] for h in heads]
        g_in = jnp.exp(c_col - stack([jnp.broadcast_to(mx_last[h], (L, 1)) for h in heads]))
        C_new, n_new, m_new = [], [], []
        for h in heads:
            hs = slice(h * DH_B, (h + 1) * DH_B)
            hr = slice(h * L, (h + 1) * L)
            b_out = _sigmoid(zb[rows, hs]) * (hb[hr] + skip[:, hs] * cb[rows, hs])
            mix_scr[rows, hs] = b_out.astype(bf16)
            gk = g_in[hr] * ks[h][rows]
            decay = jnp.exp(m_old[h] - mx_last[h])
            C_new.append(decay * C_old[h] + _dot_tn(gk.astype(bf16), v_bf[h]))
            n_new.append(decay * n_old[h] + jnp.sum(gk, axis=0, keepdims=True))
            m_new.append(b_all[L - 1:L, h:h + 1] + mx_last[h])
        C_old, n_old, m_old = C_new, n_new, m_new
    for h in heads:
        C_scr[h] = C_old[h]
        n_scr[h:h + 1, :] = n_old[h]
        m_scr[h:h + 1, :] = jnp.broadcast_to(m_old[h], (1, LANES))

    chist[pl.ds(S_HIST, T), :] = xc
    ccw = ccw_ref[...]
    xcv = ccb_ref[...] + ccw[3:4, :] * xc
    for j in range(CONV_C - 1):
        xcv = xcv + ccw[j:j + 1, :] * chist[pl.ds(S_HIST - (CONV_C - 1) + j, T), :]
    a_t, u_t = _rglru_gates(xcv, wrg_ref[...], brg_ref[...], lam_ref[...])
    row = lax.broadcasted_iota(jnp.int32, (T, W_C), 0)
    d = 1
    while d < T:
        if d < SUBLANES:
            a_sh = jnp.where(row >= d, pltpu.roll(a_t, d, axis=0), 1.0)
            u_sh = jnp.where(row >= d, pltpu.roll(u_t, d, axis=0), 0.0)
        else:
            a_sh = jnp.concatenate([jnp.ones((d, W_C), f32), a_t[:T - d]], axis=0)
            u_sh = jnp.concatenate([jnp.zeros((d, W_C), f32), u_t[:T - d]], axis=0)
        u_t = a_t * u_sh + u_t
        a_t = a_t * a_sh
        d *= 2
    hc = a_t * h_scr[...] + u_t
    h_scr[...] = hc[T - 1:T, :]
    y = y + _dot((hc * jax.nn.gelu(gc)).astype(bf16), wout_ref[W_A + W_B:, :])

    x1_ref[...] = x + (y + _dot(mix_scr[...], wout_ref[W_A:W_A + W_B, :]))

    ahist[0:A_HIST, :] = ahist[pl.ds(T, A_HIST), :]
    bhist[0:S_HIST, :] = bhist[pl.ds(T, S_HIST), :]
    chist[0:S_HIST, :] = chist[pl.ds(T, S_HIST), :]

    @pl.when(s == last)
    def _():
        sa_ref[...] = ahist[pl.ds(first, CONV_A - 1), :]
        sb_ref[...] = bhist[pl.ds(S_HIST - (CONV_B - 1), CONV_B - 1), :]
        sc_ref[...] = chist[pl.ds(S_HIST - (CONV_C - 1), CONV_C - 1), :]
        sC_ref[...] = C_scr[...]
        sn_ref[...] = n_scr[0:H_B, :]
        sm_ref[...] = m_scr[0:H_B, :]
        sh_ref[...] = h_scr[...]


def _full(shape):
    nd = len(shape)
    return pl.BlockSpec(shape, lambda *_: (0,) * nd)


def _layer(w, l):
    nd = w.ndim - 1
    return pl.BlockSpec((None,) + w.shape[1:], lambda *_: (l,) + (0,) * nd)


def _mix_prompt(x, lw, l):
    B, S, _ = x.shape
    T = T_MIX
    weights = [lw[k] for k in ("norm_mix_g", "w_in", "conv_a_w", "conv_a_b", "ln_a_g", "ln_a_b", "conv_b_w", "conv_b_b",
                               "w_q", "w_k", "w_v", "w_gates", "b_gates", "gn_b_g", "skip_b", "conv_c_w", "conv_c_b",
                               "w_rg", "b_rg", "lam", "w_out")]
    out_shape = (
        jax.ShapeDtypeStruct((B, S, D_MODEL), f32),
        jax.ShapeDtypeStruct((B, CONV_A - 1, W_A), f32),
        jax.ShapeDtypeStruct((B, CONV_B - 1, W_B), f32),
        jax.ShapeDtypeStruct((B, H_B, DH_B, DH_B), f32),
        jax.ShapeDtypeStruct((B, H_B, DH_B), f32),
        jax.ShapeDtypeStruct((B, H_B, LANES), f32),
        jax.ShapeDtypeStruct((B, CONV_C - 1, W_C), f32),
        jax.ShapeDtypeStruct((B, 1, W_C), f32),
    )
    G = MIX_G
    per_b = lambda shp: pl.BlockSpec((G,) + shp, lambda b, s: (b,) + (0,) * len(shp))
    out_specs = (
        pl.BlockSpec((G, T, D_MODEL), lambda b, s: (b, s, 0)),
        per_b((CONV_A - 1, W_A)), per_b((CONV_B - 1, W_B)), per_b((H_B, DH_B, DH_B)), per_b((H_B, DH_B)),
        per_b((H_B, LANES)), per_b((CONV_C - 1, W_C)), per_b((1, W_C)),
    )
    scratch = [
        pltpu.VMEM((G, A_HIST + T, W_A), f32), pltpu.VMEM((G, S_HIST + T, W_B), f32),
        pltpu.VMEM((G, S_HIST + T, W_C), f32), pltpu.VMEM((G, T, W_A), f32),
        pltpu.VMEM((G, H_B, DH_B, DH_B), f32), pltpu.VMEM((G, SUBLANES, DH_B), f32),
        pltpu.VMEM((G, SUBLANES, LANES), f32), pltpu.VMEM((G, 1, W_C), f32),
        pltpu.VMEM((G, T, 3 * W_B), bf16), pltpu.VMEM((G, T, W_B), bf16),
    ]
    assert len(weights) == N_MIX_WEIGHTS and len(out_shape) == N_MIX_OUTPUTS and B % G == 0
    return pl.pallas_call(
        _mix_prompt_body,
        grid=(B // G, S // T),
        in_specs=[pl.BlockSpec((G, T, D_MODEL), lambda b, s: (b, s, 0))] + [_layer(w, l) for w in weights],
        out_specs=out_specs,
        out_shape=out_shape,
        scratch_shapes=scratch,
        compiler_params=pltpu.CompilerParams(dimension_semantics=("arbitrary", "arbitrary"),
                                             vmem_limit_bytes=VMEM_LIMIT),
        name="mix_prompt",
    )(x, *weights)


def _step_conv(st_ref, w, bias, x_new, st_out):
    k1 = st_ref.shape[0]
    y = bias + w[k1:k1 + 1, :] * x_new
    for j in range(k1):
        y = y + w[j:j + 1, :] * st_ref[j]
        if j > 0:
            st_out[j - 1] = st_ref[j]
    st_out[k1 - 1] = x_new
    return y


def _step_pre_body(x_ref, g_ref, win_ref, sa_ref, caw_ref, cab_ref, lag_ref, lab_ref, sb_ref, cbw_ref, cbb_ref,
                   wq_ref, wk_ref, wv_ref, wg_ref, bg_ref, n_ref, m_ref, sc_ref, ccw_ref, ccb_ref,
                   wrg_ref, brg_ref, lam_ref, h_ref, skip_ref,
                   sa_o, sb_o, sc_o, n_o, m_o, h_o, q_o, gk_o, v_o, dec_o, den_o, sv_o, zsig_o, skcb_o, ac_o,
                   qkv_scr):
    x = x_ref[...]
    hn = _rmsnorm(x, g_ref[...])
    u = _dot(hn.astype(bf16), win_ref[...])
    xa, ga = u[:, 0:256], u[:, 256:512]
    xb, zb = u[:, 512:1024], u[:, 1024:1536]
    xc, gc = u[:, 1536:1792], u[:, 1792:2048]

    a_new = xa * _sigmoid(ga)
    conv = _step_conv(sa_ref, caw_ref[...], cab_ref[...], a_new, sa_o)
    ac_o[:, 0:W_A] = _group_layernorm_silu(conv, _group_mask_a(), lag_ref[...], lab_ref[...])

    cb = _step_conv(sb_ref, cbw_ref[...], cbb_ref[...], xb, sb_o)
    cb = cb * _sigmoid(cb)
    cb_bf = cb.astype(bf16)
    xb_bf = xb.astype(bf16)
    qs, ks, vs = [], [], []
    for h in range(H_B):
        hs = slice(h * DH_B, (h + 1) * DH_B)
        q = _dot(cb_bf[:, hs], wq_ref[h])
        k = _dot(cb_bf[:, hs], wk_ref[h]) * (DH_B ** -0.5)
        v = _dot(xb_bf[:, hs], wv_ref[h])
        qs.append(q)
        ks.append(k)
        vs.append(v)
        qkv_scr[:, (3 * h) * DH_B:(3 * h + 1) * DH_B] = q.astype(bf16)
        qkv_scr[:, (3 * h + 1) * DH_B:(3 * h + 2) * DH_B] = k.astype(bf16)
        qkv_scr[:, (3 * h + 2) * DH_B:(3 * h + 3) * DH_B] = v.astype(bf16)
    gates = _dot(qkv_scr[...], wg_ref[...]) + bg_ref[...]
    ig = gates[:, 0:LANES]
    lf = jax.nn.log_sigmoid(gates[:, LANES:2 * LANES])
    m0 = m_ref[...]
    m_t = jnp.maximum(lf + m0, ig)
    g_in = jnp.exp(ig - m_t)
    decay = jnp.exp(lf + m0 - m_t)
    lane = lax.broadcasted_iota(jnp.int32, ig.shape, 1)
    qk = jnp.zeros(ig.shape, f32)
    qn = jnp.zeros(ig.shape, f32)
    for h in range(H_B):
        hs = slice(h * DH_B, (h + 1) * DH_B)
        n_h = n_ref[:, h, :]
        qk = jnp.where(lane == h, jnp.sum(qs[h] * ks[h], axis=-1, keepdims=True), qk)
        qn = jnp.where(lane == h, jnp.sum(qs[h] * n_h, axis=-1, keepdims=True), qn)
        gk = g_in[:, h:h + 1] * ks[h]
        n_o[:, h, :] = decay[:, h:h + 1] * n_h + gk
        q_o[:, hs] = qs[h]
        gk_o[:, hs] = gk
        v_o[:, hs] = vs[h]
    s_t = qk * g_in
    den = s_t + decay * qn
    den_o[...] = jnp.maximum(jnp.abs(den), jnp.exp(-m_t))
    dec_o[...] = decay
    m_o[...] = m_t
    for h in range(H_B):
        hs = slice(h * DH_B, (h + 1) * DH_B)
        sv_o[:, hs] = s_t[:, h:h + 1] * vs[h]
    zsig_o[...] = _sigmoid(zb)
    skcb_o[...] = skip_ref[...] * cb

    xcv = _step_conv(sc_ref, ccw_ref[...], ccb_ref[...], xc, sc_o)
    a_t, u_t = _rglru_gates(xcv, wrg_ref[...], brg_ref[...], lam_ref[...])
    hc = a_t * h_ref[...] + u_t
    h_o[...] = hc
    ac_o[:, W_A:] = hc * jax.nn.gelu(gc)


def _step_read_body(C_ref, qT_ref, qc_ref):
    for bb in range(STEP_BB):
        for h in range(H_B):
            hs = slice(h * DH_B, (h + 1) * DH_B)
            qc_ref[bb:bb + 1, hs] = jnp.sum(C_ref[bb, h] * qT_ref[h, :, bb:bb + 1], axis=0, keepdims=True)


def _step_state_body(C_ref, qT_ref, gkT_ref, v_ref, dec_ref, Cn_ref, qc_ref):
    for bb in range(STEP_BB):
        for h in range(H_B):
            hs = slice(h * DH_B, (h + 1) * DH_B)
            C = C_ref[bb, h]
            qc_ref[bb:bb + 1, hs] = jnp.sum(C * qT_ref[h, :, bb:bb + 1], axis=0, keepdims=True)
            Cn_ref[bb, h] = dec_ref[bb:bb + 1, h:h + 1] * C + gkT_ref[h, :, bb:bb + 1] * v_ref[bb:bb + 1, hs]


def _to_cols(t):
    return t.reshape(t.shape[0] // STEP_BB, STEP_BB, H_B, DH_B).transpose(0, 2, 3, 1)


def _step_read(C_all, layer, q):
    Bs = q.shape[0]
    return pl.pallas_call(
        _step_read_body,
        grid=(Bs // STEP_BB,),
        in_specs=[pl.BlockSpec((None, STEP_BB, H_B, DH_B, DH_B), lambda i: (layer, i, 0, 0, 0)),
                  pl.BlockSpec((None, H_B, DH_B, STEP_BB), lambda i: (i, 0, 0, 0))],
        out_specs=pl.BlockSpec((STEP_BB, W_B), lambda i: (i, 0)),
        out_shape=jax.ShapeDtypeStruct((Bs, W_B), f32),
        compiler_params=pltpu.CompilerParams(dimension_semantics=("arbitrary",), vmem_limit_bytes=VMEM_LIMIT),
        name="step_read",
    )(C_all, _to_cols(q))


def _step_state(C_all, q, gk, v, dec):
    depth, Bs = q.shape[0], q.shape[1]
    cols = lambda t: jnp.stack([_to_cols(t[l]) for l in range(depth)])
    blk5 = pl.BlockSpec((None, STEP_BB, H_B, DH_B, DH_B), lambda l, i: (l, i, 0, 0, 0))
    colspec = pl.BlockSpec((None, None, H_B, DH_B, STEP_BB), lambda l, i: (l, i, 0, 0, 0))
    return pl.pallas_call(
        _step_state_body,
        grid=(depth, Bs // STEP_BB),
        in_specs=[blk5, colspec, colspec,
                  pl.BlockSpec((None, STEP_BB, W_B), lambda l, i: (l, i, 0)),
                  pl.BlockSpec((None, STEP_BB, LANES), lambda l, i: (l, i, 0))],
        out_specs=(blk5, pl.BlockSpec((None, STEP_BB, W_B), lambda l, i: (l, i, 0))),
        out_shape=(jax.ShapeDtypeStruct(C_all.shape, f32), jax.ShapeDtypeStruct((depth, Bs, W_B), f32)),
        compiler_params=pltpu.CompilerParams(dimension_semantics=("arbitrary", "arbitrary"),
                                             vmem_limit_bytes=VMEM_LIMIT),
        name="step_state",
    )(C_all, cols(q), cols(gk), v, dec)


def _step_post_body(x_ref, ac_ref, zsig_ref, skcb_ref, sv_ref, qc_ref, dec_ref, den_ref, gn_ref, wout_ref,
                    x1_ref, mix_scr):
    gn = gn_ref[...]
    dec = dec_ref[...]
    den = den_ref[...]
    mix_scr[:, 0:W_A] = ac_ref[:, 0:W_A].astype(bf16)
    mix_scr[:, W_A + W_B:] = ac_ref[:, W_A:].astype(bf16)
    for h in range(H_B):
        hs = slice(h * DH_B, (h + 1) * DH_B)
        num = sv_ref[:, hs] + dec[:, h:h + 1] * qc_ref[:, hs]
        hb = _head_layernorm(num / den[:, h:h + 1], gn[:, hs])
        mix_scr[:, W_A + h * DH_B:W_A + (h + 1) * DH_B] = (zsig_ref[:, hs] * (hb + skcb_ref[:, hs])).astype(bf16)
    x1_ref[...] = x_ref[...] + _dot(mix_scr[...], wout_ref[...])


def _step_pre(x, st, lw, l):
    buf_a, buf_b, n0, m0, buf_c, hc0 = st
    buf_a, buf_b, buf_c = (b.transpose(0, 2, 1, 3) for b in (buf_a, buf_b, buf_c))
    Bs = x.shape[0]
    m_in = jnp.pad(m0, ((0, 0), (0, 0), (0, LANES - H_B)))
    sds = lambda *shape: jax.ShapeDtypeStruct(shape, f32)
    W = lambda k: (lw[k], True)
    L = lambda a: (a, True)
    A = lambda a: (a, False)
    pre_in = [A(x), W("norm_mix_g"), W("w_in"), L(buf_a), W("conv_a_w"), W("conv_a_b"), W("ln_a_g"), W("ln_a_b"),
              L(buf_b), W("conv_b_w"), W("conv_b_b"), W("w_q"), W("w_k"), W("w_v"), W("w_gates"), W("b_gates"),
              L(n0), L(m_in), L(buf_c), W("conv_c_w"), W("conv_c_b"), W("w_rg"), W("b_rg"), W("lam"), L(hc0),
              W("skip_b")]
    out_shape = (sds(*buf_a.shape[1:]), sds(*buf_b.shape[1:]), sds(*buf_c.shape[1:]), sds(*n0.shape[1:]),
                 sds(Bs, LANES), sds(Bs, W_C),
                 sds(Bs, W_B), sds(Bs, W_B), sds(Bs, W_B), sds(Bs, LANES), sds(Bs, LANES), sds(Bs, W_B),
                 sds(Bs, W_B), sds(Bs, W_B), sds(Bs, W_A + W_C))
    (sa_n, sb_n, sc_n, n_n, m_n, h_n, q, gk, v, dec, den, sv, zsig, skcb, ac) = pl.pallas_call(
        _step_pre_body,
        grid=(1,),
        in_specs=[_layer(a, l) if is_w else _full(a.shape) for a, is_w in pre_in],
        out_specs=[_full(o.shape) for o in out_shape],
        out_shape=out_shape,
        scratch_shapes=[pltpu.VMEM((Bs, 3 * W_B), bf16)],
        compiler_params=pltpu.CompilerParams(dimension_semantics=("arbitrary",), vmem_limit_bytes=VMEM_LIMIT),
        name="step_pre",
    )(*[a for a, _ in pre_in])
    new = (sa_n, sb_n, n_n, m_n[:, :H_B], sc_n, h_n)
    return new, (q, gk, v, dec), (ac, zsig, skcb, sv, dec, den)


def _step_post(x, post_in, qc, lw, l):
    ac, zsig, skcb, sv, dec, den = post_in
    acts = [x, ac, zsig, skcb, sv, qc, dec, den]
    return pl.pallas_call(
        _step_post_body,
        grid=(1,),
        in_specs=[_full(a.shape) for a in acts] + [_layer(lw["gn_b_g"], l), _layer(lw["w_out"], l)],
        out_specs=_full(x.shape),
        out_shape=jax.ShapeDtypeStruct(x.shape, f32),
        scratch_shapes=[pltpu.VMEM(x.shape, bf16)],
        compiler_params=pltpu.CompilerParams(dimension_semantics=("arbitrary",), vmem_limit_bytes=VMEM_LIMIT),
        name="step_post",
    )(*acts, lw["gn_b_g"], lw["w_out"])


def _swiglu(hb, wg_ref, wu_ref, wd_ref):
    f = wd_ref.shape[0]
    main = f // MXU_WIDTH * MXU_WIDTH
    glu = lambda gt, up: (gt * _sigmoid(gt) * up).astype(bf16)
    if main == f:
        act = glu(_dot(hb, wg_ref[...]), _dot(hb, wu_ref[...]))
    else:
        tail = _dot(hb, jnp.concatenate([wg_ref[:, main:], wu_ref[:, main:]], axis=1))
        act = jnp.concatenate([glu(_dot(hb, wg_ref[:, :main]), _dot(hb, wu_ref[:, :main])),
                               glu(tail[:, :f - main], tail[:, f - main:])], axis=1)
    return _dot(act, wd_ref[...])


def _ffn_body(*refs, moe, final_norm):
    it = iter(refs)
    x_ref, g_ref, wg_ref, wu_ref, wd_ref = next(it), next(it), next(it), next(it), next(it)
    wr_ref = next(it) if moe else None
    br_ref = next(it) if moe else None
    gf_ref = next(it) if final_norm else None
    o_ref, hn_scr, acc_scr = next(it), next(it), next(it)
    comb_scr = next(it) if moe else None
    j = pl.program_id(1)

    @pl.when(j == 0)
    def _():
        hn = _rmsnorm(x_ref[...], g_ref[...])
        hn_scr[...] = hn.astype(bf16)
        acc_scr[...] = jnp.zeros(acc_scr.shape, f32)
        if moe:
            lane, i1, i2, g1, g2 = _top2(_router_logits(hn, wr_ref[...], br_ref[...]))
            comb_scr[...] = jnp.where(lane == i1, g1, 0.0) + jnp.where(lane == i2, g2, 0.0)

    y = _swiglu(hn_scr[...], wg_ref, wu_ref, wd_ref)
    if moe:
        lane = lax.broadcasted_iota(jnp.int32, comb_scr.shape, 1)
        y = y * jnp.sum(jnp.where(lane == j, comb_scr[...], 0.0), axis=-1, keepdims=True)
    acc_scr[...] += y

    @pl.when(j == pl.num_programs(1) - 1)
    def _():
        out = x_ref[...] + acc_scr[...]
        if final_norm:
            out = _rmsnorm(out, gf_ref[...])
        o_ref[...] = out


def _ffn(x, g, wg, wu, wd, router=None, final_g=None):
    N = x.shape[0]
    E, F = wd.shape[0], wd.shape[1]
    tm = min(TM_FFN, N)
    moe = router is not None
    final_norm = final_g is not None
    ins = [x, g, wg, wu, wd]
    once = dict(pipeline_mode=pl.Buffered(1)) if E == 1 else {}
    in_specs = [pl.BlockSpec((tm, D_MODEL), lambda i, j: (i, 0)), _full(g.shape),
                pl.BlockSpec((None, D_MODEL, F), lambda i, j: (j, 0, 0), **once),
                pl.BlockSpec((None, D_MODEL, F), lambda i, j: (j, 0, 0), **once),
                pl.BlockSpec((None, F, D_MODEL), lambda i, j: (j, 0, 0), **once)]
    scratch = [pltpu.VMEM((tm, D_MODEL), bf16), pltpu.VMEM((tm, D_MODEL), f32)]
    if moe:
        ins += list(router)
        in_specs += [_full(router[0].shape), _full(router[1].shape)]
        scratch.append(pltpu.VMEM((tm, LANES), f32))
    if final_norm:
        ins.append(final_g)
        in_specs.append(_full(final_g.shape))
    return pl.pallas_call(
        functools.partial(_ffn_body, moe=moe, final_norm=final_norm),
        grid=(N // tm, E),
        in_specs=in_specs,
        out_specs=pl.BlockSpec((tm, D_MODEL), lambda i, j: (i, 0)),
        out_shape=jax.ShapeDtypeStruct((N, D_MODEL), f32),
        scratch_shapes=scratch,
        compiler_params=pltpu.CompilerParams(dimension_semantics=("arbitrary", "arbitrary"),
                                             vmem_limit_bytes=VMEM_LIMIT),
        name="ffn_moe" if moe else "ffn_dense",
    )(*ins)


TM_MOE = 512
SEG_ALIGN = 16
SEG_SIZES = (512, 256, 128, 64, 32, 16)
MOE_ROWS = 2 * TM_MOE + N_EXPERTS * SEG_ALIGN
META_COLS = 3 * N_EXPERTS


def _seg_dma(src, src_off, dst, dst_off, nrows, sem, wait):
    done = jnp.int32(0)
    for size in SEG_SIZES:
        take = (nrows & size) != 0

        @pl.when(take)
        def _(done=done, size=size):
            cp = pltpu.make_async_copy(src.at[pl.ds(pl.multiple_of(src_off + done, SEG_ALIGN), size)],
                                       dst.at[pl.ds(pl.multiple_of(dst_off + done, SEG_ALIGN), size)], sem)
            if wait:
                cp.wait()
            else:
                cp.start()

        done = done + jnp.where(take, size, 0)


def _top2(logits):
    lane = lax.broadcasted_iota(jnp.int32, logits.shape, 1).astype(f32)
    lg = jnp.where(lane < N_EXPERTS, logits, -jnp.inf)
    m1 = jnp.max(lg, axis=-1, keepdims=True)
    i1 = jnp.min(jnp.where(lg == m1, lane, float(LANES)), axis=-1, keepdims=True)
    lg2 = jnp.where(lane == i1, -jnp.inf, lg)
    m2 = jnp.max(lg2, axis=-1, keepdims=True)
    i2 = jnp.min(jnp.where(lg2 == m2, lane, float(LANES)), axis=-1, keepdims=True)
    e2 = jnp.exp(m2 - m1)
    return lane, i1, i2, 1.0 / (1.0 + e2), e2 / (1.0 + e2)


def _router_logits(hn, wr, br):
    h1, h2 = _split_terms(hn, 2)
    w1, w2 = _split_terms(wr, 2)
    return _dot(h1, w1) + (_dot(h1, w2) + _dot(h2, w1)) + br


def _expert_ranks(lane, i1, i2):
    TM = lane.shape[0]
    sel1 = lane == i1
    sel2 = lane == i2
    sel = jnp.where(sel1, 1.0, jnp.where(sel2, 1.0, 0.0))
    earlier = lax.broadcasted_iota(jnp.int32, (TM, TM), 0) > lax.broadcasted_iota(jnp.int32, (TM, TM), 1)
    rank = _dot(jnp.where(earlier, 1.0, 0.0).astype(bf16), sel.astype(bf16))
    return sel1, sel2, rank, rank[TM - 1:TM, :] + sel[TM - 1:TM, :]


def _seg_pad(c):
    return ((c + (SEG_ALIGN - 1)) // SEG_ALIGN) * SEG_ALIGN


def _moe_count_body(x_ref, g_ref, wr_ref, br_ref, route_ref, cnt_ref):
    i = pl.program_id(0)
    hn = _rmsnorm(x_ref[...], g_ref[...])
    lane, i1, i2, g1, g2 = _top2(_router_logits(hn, wr_ref[...], br_ref[...]))
    route_ref[...] = jnp.where(lane == 0, i1, jnp.where(lane == 1, i2, jnp.where(lane == 2, g1,
                                                                                 jnp.where(lane == 3, g2, 0.0))))
    _, _, _, cnt = _expert_ranks(lane, i1, i2)
    for e in range(N_EXPERTS):
        cnt_ref[i, e] = cnt[0, e].astype(jnp.int32)


def _moe_route_body(cnt_ref, x_ref, g_ref, route_ref, xs_hbm, gs_hbm, rinfo_ref, meta_ref, te_ref, nv_ref,
                    xperm, gperm, pos_smem, sem):
    TM, R = TM_MOE, MOE_ROWS
    i = pl.program_id(0)
    nt = pl.num_programs(0)
    n_tiles = te_ref.shape[0]

    @pl.when(i == 0)
    def _():
        base = jnp.int32(0)
        for e in range(N_EXPERTS):
            rows = lax.fori_loop(0, nt, lambda t, a, e=e: a + _seg_pad(cnt_ref[t, e]), jnp.int32(0))
            first_tile = base // TM
            tiles = (rows + (TM - 1)) // TM

            def mark(k, c, e=e, first_tile=first_tile):
                te_ref[first_tile + k] = e
                return c

            lax.fori_loop(0, tiles, mark, 0)
            pos_smem[e] = base
            base = base + tiles * TM
        nv = base // TM
        nv_ref[0] = nv

        def mark_rest(k, c):
            te_ref[k] = N_EXPERTS - 1
            return c

        lax.fori_loop(nv, n_tiles, mark_rest, 0)

    hn = _rmsnorm(x_ref[...], g_ref[...])
    route = route_ref[...]
    i1, i2, g1, g2 = route[:, 0:1], route[:, 1:2], route[:, 2:3], route[:, 3:4]
    lane = lax.broadcasted_iota(jnp.int32, route.shape, 1).astype(f32)
    sel1, sel2, rank, _ = _expert_ranks(lane, i1, i2)
    seg_rows, seg_off = [], []
    acc = jnp.int32(0)
    for e in range(N_EXPERTS):
        seg_rows.append(_seg_pad(cnt_ref[i, e]))
        seg_off.append(acc)
        acc = acc + seg_rows[e]
    lane1 = lax.broadcasted_iota(jnp.int32, (1, LANES), 1)
    offv = jnp.zeros((1, LANES), f32)
    for e in range(N_EXPERTS):
        offv = jnp.where(lane1 == e, seg_off[e].astype(f32), offv)
    dest = rank + offv
    d1 = jnp.sum(jnp.where(sel1, dest, 0.0), axis=-1, keepdims=True)
    d2 = jnp.sum(jnp.where(sel2, dest, 0.0), axis=-1, keepdims=True)
    rinfo = jnp.where(lane == 0, d1, jnp.where(lane == 1, d2, jnp.where(lane == 2, g1, jnp.where(lane == 3, g2, 0.0))))
    rinfo_ref[...] = rinfo
    rinfo_t = rinfo.T
    d1r, d2r, g1r, g2r = rinfo_t[0:1, :], rinfo_t[1:2, :], rinfo_t[2:3, :], rinfo_t[3:4, :]
    riota = lax.broadcasted_iota(jnp.int32, (R, TM), 0).astype(f32)
    hit1 = riota == d1r
    hit2 = riota == d2r
    perm = jnp.where(hit1, 1.0, jnp.where(hit2, 1.0, 0.0)).astype(bf16)
    xperm[i % 2] = _dot(perm, hn.astype(bf16)).astype(bf16)
    gcol = jnp.sum(jnp.where(hit1, g1r, 0.0) + jnp.where(hit2, g2r, 0.0), axis=1, keepdims=True)
    gperm[i % 2] = jnp.broadcast_to(gcol, (R, LANES))

    for e in range(N_EXPERTS):
        pos = pos_smem[e]
        meta_ref[i, e] = pos
        meta_ref[i, N_EXPERTS + e] = seg_off[e]
        meta_ref[i, 2 * N_EXPERTS + e] = seg_rows[e]
        pos_smem[e] = pos + seg_rows[e]

    def copies(tile, wait):
        for slot in range(2):
            @pl.when(tile % 2 == slot)
            def _(slot=slot):
                for e in range(N_EXPERTS):
                    src_off, dst_off, n = meta_ref[tile, N_EXPERTS + e], meta_ref[tile, e], meta_ref[tile, 2 * N_EXPERTS + e]
                    _seg_dma(xperm.at[slot], src_off, xs_hbm, dst_off, n, sem.at[2 * slot], wait)
                    _seg_dma(gperm.at[slot], src_off, gs_hbm, dst_off, n, sem.at[2 * slot + 1], wait)

    copies(i, False)
    pl.when(i > 0)(lambda: copies(i - 1, True))

    @pl.when(i == nt - 1)
    def _():
        copies(i, True)
        xz, gz = xperm.at[0], gperm.at[0]
        xz[0:TM, :] = jnp.zeros((TM, D_MODEL), bf16)
        gz[0:TM, :] = jnp.zeros((TM, LANES), f32)
        pads = []
        for e in range(N_EXPERTS):
            pos = pos_smem[e]
            pads.append((-pos) & (TM - 1))
            _seg_dma(xz, 0, xs_hbm, pos, pads[e], sem.at[0], False)
            _seg_dma(gz, 0, gs_hbm, pos, pads[e], sem.at[1], False)
        for e in range(N_EXPERTS):
            _seg_dma(xz, 0, xs_hbm, pos_smem[e], pads[e], sem.at[0], True)
            _seg_dma(gz, 0, gs_hbm, pos_smem[e], pads[e], sem.at[1], True)

        def zero_tile(k, c):
            row = pl.multiple_of(k * TM, TM)
            cx = pltpu.make_async_copy(xz.at[pl.ds(0, TM)], xs_hbm.at[pl.ds(row, TM)], sem.at[0])
            cg = pltpu.make_async_copy(gz.at[pl.ds(0, TM)], gs_hbm.at[pl.ds(row, TM)], sem.at[1])
            cx.start()
            cg.start()
            cx.wait()
            cg.wait()
            return c

        lax.fori_loop(nv_ref[0], n_tiles, zero_tile, 0)


def _moe_ffn_body(te_ref, nv_ref, xs_ref, gs_ref, wg_ref, wu_ref, wd_ref, ys_ref):
    i = pl.program_id(0)

    @pl.when(i < nv_ref[0])
    def _():
        ys_ref[...] = (_swiglu(xs_ref[...], wg_ref, wu_ref, wd_ref) * gs_ref[:, 0:1]).astype(bf16)

    @pl.when(i >= nv_ref[0])
    def _():
        ys_ref[...] = jnp.zeros(ys_ref.shape, bf16)


def _moe_combine_body(*refs, final_norm):
    it = iter(refs)
    meta_ref, x_ref, rinfo_ref, ys_hbm = next(it), next(it), next(it), next(it)
    gf_ref = next(it) if final_norm else None
    o_ref, yperm, sem = next(it), next(it), next(it)
    TM, R = TM_MOE, MOE_ROWS
    i = pl.program_id(0)
    nt = pl.num_programs(0)

    def fetch(tile, slot, wait):
        for e in range(N_EXPERTS):
            _seg_dma(ys_hbm, meta_ref[tile, e], yperm.at[slot], meta_ref[tile, N_EXPERTS + e],
                     meta_ref[tile, 2 * N_EXPERTS + e], sem.at[slot], wait)

    def both_slots(tile, wait):
        for slot in range(2):
            pl.when(tile % 2 == slot)(lambda slot=slot: fetch(tile, slot, wait))

    @pl.when(i == 0)
    def _():
        yperm[...] = jnp.zeros(yperm.shape, bf16)
        fetch(0, 0, False)

    @pl.when(i + 1 < nt)
    def _():
        both_slots(i + 1, False)

    rinfo = rinfo_ref[...]
    ciota = lax.broadcasted_iota(jnp.int32, (TM, R), 1).astype(f32)
    unperm = jnp.where(ciota == rinfo[:, 0:1], 1.0, jnp.where(ciota == rinfo[:, 1:2], 1.0, 0.0)).astype(bf16)
    both_slots(i, True)
    out = x_ref[...] + _dot(unperm, yperm[i % 2])
    if final_norm:
        out = _rmsnorm(out, gf_ref[...])
    o_ref[...] = out


def _moe_prompt(x, g, wg, wu, wd, wr, br, final_g):
    N = x.shape[0]
    TM = TM_MOE
    nt = N // TM
    F = wd.shape[1]
    n_tiles = -(-(2 * N + nt * N_EXPERTS * (SEG_ALIGN - 1) + N_EXPERTS * (TM - 1)) // TM)
    rows = n_tiles * TM
    smem = pl.BlockSpec(memory_space=pltpu.SMEM)
    arb = pltpu.CompilerParams(dimension_semantics=("arbitrary",), vmem_limit_bytes=VMEM_LIMIT)
    route, cnt = pl.pallas_call(
        _moe_count_body,
        grid=(nt,),
        in_specs=[pl.BlockSpec((TM, D_MODEL), lambda i: (i, 0)), _full(g.shape), _full(wr.shape), _full(br.shape)],
        out_specs=[pl.BlockSpec((TM, LANES), lambda i: (i, 0)), smem],
        out_shape=[jax.ShapeDtypeStruct((N, LANES), f32), jax.ShapeDtypeStruct((nt, N_EXPERTS), jnp.int32)],
        compiler_params=arb,
        name="moe_count",
    )(x, g, wr, br)

    xs, gs, rinfo, meta, te, nv = pl.pallas_call(
        _moe_route_body,
        grid_spec=pltpu.PrefetchScalarGridSpec(
            num_scalar_prefetch=1,
            grid=(nt,),
            in_specs=[pl.BlockSpec((TM, D_MODEL), lambda i, c: (i, 0)), pl.BlockSpec(g.shape, lambda i, c: (0, 0)),
                      pl.BlockSpec((TM, LANES), lambda i, c: (i, 0))],
            out_specs=[pl.BlockSpec(memory_space=pl.ANY), pl.BlockSpec(memory_space=pl.ANY),
                       pl.BlockSpec((TM, LANES), lambda i, c: (i, 0)), smem, smem, smem],
            scratch_shapes=[pltpu.VMEM((2, MOE_ROWS, D_MODEL), bf16), pltpu.VMEM((2, MOE_ROWS, LANES), f32),
                            pltpu.SMEM((N_EXPERTS,), jnp.int32), pltpu.SemaphoreType.DMA((4,))],
        ),
        out_shape=[jax.ShapeDtypeStruct((rows, D_MODEL), bf16), jax.ShapeDtypeStruct((rows, LANES), f32),
                   jax.ShapeDtypeStruct((N, LANES), f32), jax.ShapeDtypeStruct((nt, META_COLS), jnp.int32),
                   jax.ShapeDtypeStruct((n_tiles,), jnp.int32), jax.ShapeDtypeStruct((1,), jnp.int32)],
        compiler_params=arb,
        name="moe_route",
    )(cnt, x, g, route)

    ys = pl.pallas_call(
        _moe_ffn_body,
        grid_spec=pltpu.PrefetchScalarGridSpec(
            num_scalar_prefetch=2,
            grid=(n_tiles,),
            in_specs=[pl.BlockSpec((TM, D_MODEL), lambda i, te, nv: (i, 0)),
                      pl.BlockSpec((TM, LANES), lambda i, te, nv: (i, 0)),
                      pl.BlockSpec((None, D_MODEL, F), lambda i, te, nv: (te[i], 0, 0)),
                      pl.BlockSpec((None, D_MODEL, F), lambda i, te, nv: (te[i], 0, 0)),
                      pl.BlockSpec((None, F, D_MODEL), lambda i, te, nv: (te[i], 0, 0))],
            out_specs=pl.BlockSpec((TM, D_MODEL), lambda i, te, nv: (i, 0)),
        ),
        out_shape=jax.ShapeDtypeStruct((rows, D_MODEL), bf16),
        compiler_params=arb,
        name="moe_ffn",
    )(te, nv, xs, gs, wg, wu, wd)

    final_norm = final_g is not None
    ins = [meta, x, rinfo, ys] + ([final_g] if final_norm else [])
    in_specs = [pl.BlockSpec((TM, D_MODEL), lambda i, m: (i, 0)), pl.BlockSpec((TM, LANES), lambda i, m: (i, 0)),
                pl.BlockSpec(memory_space=pl.ANY)]
    if final_norm:
        in_specs.append(pl.BlockSpec(final_g.shape, lambda i, m: (0, 0)))
    return pl.pallas_call(
        functools.partial(_moe_combine_body, final_norm=final_norm),
        grid_spec=pltpu.PrefetchScalarGridSpec(
            num_scalar_prefetch=1,
            grid=(nt,),
            in_specs=in_specs,
            out_specs=pl.BlockSpec((TM, D_MODEL), lambda i, m: (i, 0)),
            scratch_shapes=[pltpu.VMEM((2, MOE_ROWS, D_MODEL), bf16), pltpu.SemaphoreType.DMA((2,))],
        ),
        out_shape=jax.ShapeDtypeStruct((N, D_MODEL), f32),
        compiler_params=pltpu.CompilerParams(dimension_semantics=("arbitrary",), vmem_limit_bytes=VMEM_LIMIT),
        name="moe_combine",
    )(*ins)


def _block_diag(w):
    d, h, di, do = w.shape
    return jnp.einsum("lhij,hg->lhigj", w, jnp.eye(h, dtype=w.dtype)).reshape(d, h * di, h * do)


def _mixer_weights(p):
    row = lambda v: v[:, None, :]
    lanes = lambda w: jnp.pad(w, [(0, 0)] * (w.ndim - 1) + [(0, LANES - w.shape[-1])])
    return dict(
        norm_mix_g=row(p["norm_mix_g"]), w_in=p["w_in"].astype(bf16),
        conv_a_w=p["conv_a_w"], conv_a_b=row(p["conv_a_b"]), ln_a_g=row(p["ln_a_g"]), ln_a_b=row(p["ln_a_b"]),
        conv_b_w=p["conv_b_w"], conv_b_b=row(p["conv_b_b"]),
        w_q=p["w_q"].astype(bf16), w_k=p["w_k"].astype(bf16), w_v=p["w_v"].astype(bf16),
        w_gates=jnp.concatenate([lanes(p["w_ig"]), lanes(p["w_fg"])], axis=-1).astype(bf16),
        b_gates=row(jnp.concatenate([lanes(p["b_ig"]), lanes(p["b_fg"])], axis=-1)),
        gn_b_g=row(p["gn_b_g"]), skip_b=row(p["skip_b"]),
        conv_c_w=p["conv_c_w"], conv_c_b=row(p["conv_c_b"]),
        w_rg=jnp.concatenate([_block_diag(p["w_ra"]), _block_diag(p["w_ix"])], axis=-1).astype(bf16),
        b_rg=row(jnp.concatenate([p["b_ra"], p["b_ix"]], axis=-1)), lam=row(p["lam"]),
        w_out=p["w_out"].astype(bf16),
    )


def kernel(x_prompt, x_sample, state_conv_a, state_conv_b, state_mlstm_C, state_mlstm_n, state_mlstm_m, state_conv_c, state_rglru_h, norm_mix_g, w_in, conv_a_w, conv_a_b, ln_a_g, ln_a_b, conv_b_w, conv_b_b, w_q, w_k, w_v, w_ig, b_ig, w_fg, b_fg, gn_b_g, skip_b, conv_c_w, conv_c_b, w_ra, b_ra, w_ix, b_ix, lam, w_out, norm_ffn_g, w_gate, w_up, w_down, w_router, b_router, we_gate, we_up, we_down, norm_final_g):
    p = dict(norm_mix_g=norm_mix_g, w_in=w_in, conv_a_w=conv_a_w, conv_a_b=conv_a_b, ln_a_g=ln_a_g, ln_a_b=ln_a_b,
             conv_b_w=conv_b_w, conv_b_b=conv_b_b, w_q=w_q, w_k=w_k, w_v=w_v, w_ig=w_ig, b_ig=b_ig, w_fg=w_fg,
             b_fg=b_fg, gn_b_g=gn_b_g, skip_b=skip_b, conv_c_w=conv_c_w, conv_c_b=conv_c_b, w_ra=w_ra, b_ra=b_ra,
             w_ix=w_ix, b_ix=b_ix, lam=lam, w_out=w_out)
    depth = w_in.shape[0]
    Bp, S, _ = x_prompt.shape
    Bs = x_sample.shape[0]
    sample_states = (state_conv_a, state_conv_b, state_mlstm_C, state_mlstm_n, state_mlstm_m, state_conv_c,
                     state_rglru_h)
    xp = x_prompt
    xs = x_sample.reshape(Bs, D_MODEL)
    new_p, new_s, mem_ops = [], [], []
    C_s = None
    lw = _mixer_weights(p)
    for l in range(depth):
        xp, sa, sb, sC, sn, sm, sc, sh = _mix_prompt(xp, lw, l)
        new_p.append((sa, sb, sC, sn, sm[:, :, 0], sc, sh.reshape(Bp, W_C)))
        small, mem, post_in = _step_pre(xs, tuple(s for i, s in enumerate(sample_states) if i != 2), lw, l)
        new_s.append(small)
        mem_ops.append(mem)
        if l < depth - 1:
            qc = _step_read(state_mlstm_C, l, mem[0])
        else:
            C_s, qc_all = _step_state(state_mlstm_C, *(jnp.stack([m[i] for m in mem_ops]) for i in range(4)))
            qc = qc_all[l]
        xs = _step_post(xs, post_in, qc, lw, l)
        g = norm_ffn_g[l].reshape(1, -1)
        final_g = norm_final_g.reshape(1, -1) if l == depth - 1 else None
        j = l // 2
        if l % 2 == 0:
            wg, wu, wd = (w[j:j + 1].astype(bf16) for w in (w_gate, w_up, w_down))
            router = None
        else:
            wg, wu, wd = we_gate[j].astype(bf16), we_up[j].astype(bf16), we_down[j].astype(bf16)
            router = (jnp.pad(w_router[j], ((0, 0), (0, LANES - N_EXPERTS))),
                      jnp.pad(b_router[j], (0, LANES - N_EXPERTS)).reshape(1, -1))
        xp2 = xp.reshape(Bp * S, D_MODEL)
        if router is None:
            xp2 = _ffn(xp2, g, wg, wu, wd, None, final_g)
        else:
            xp2 = _moe_prompt(xp2, g, wg, wu, wd, router[0], router[1], final_g)
        xp = xp2.reshape(Bp, S, D_MODEL)
        xs = _ffn(xs, g, wg, wu, wd, router, final_g)
    stack = lambda states, i: jnp.stack([st[i] for st in states])
    out_p = tuple(stack(new_p, i) for i in range(7))
    sa_s, sb_s, n_s, m_s, sc_s, h_s = (stack(new_s, i) for i in range(6))
    sa_s, sb_s, sc_s = (s.transpose(0, 2, 1, 3) for s in (sa_s, sb_s, sc_s))
    return (xp, xs.reshape(Bs, 1, D_MODEL)) + out_p + (sa_s, sb_s, C_s, n_s, m_s, sc_s, h_s)
```

```python
import functools

import jax
import jax.numpy as jnp
from jax import lax
from jax.experimental import pallas as pl
from jax.experimental.pallas import tpu as pltpu

f32 = jnp.float32
bf16 = jnp.bfloat16

D_MODEL = 1024
W_A, W_B, W_C = 256, 512, 256
G_A = 4
CONV_A, CONV_B, CONV_C = 31, 4, 4
H_B, DH_B = 4, 128
H_C, DH_C = 4, 64
RG_C = 8.0
D_IN = 2048
N_EXPERTS = 8
NORM_EPS = 1e-6
LN_EPS = 1e-5

LANES = 128
SUBLANES = 8
MXU_WIDTH = 256
VMEM_LIMIT = 56 * 1024 * 1024

T_MIX = 512
L_CHUNK = 128
CONV_ROWS = 128
A_HIST = 32
S_HIST = 8
TM_FFN = 512
STEP_BB = 32


def _dot(a, b):
    return jnp.dot(a, b, preferred_element_type=f32)


def _dot_nt(a, b):
    return lax.dot_general(a, b, (((1,), (1,)), ((), ())), preferred_element_type=f32)


def _dot_tn(a, b):
    return lax.dot_general(a, b, (((0,), (0,)), ((), ())), preferred_element_type=f32)


def _split_terms(x, terms):
    out = []
    r = x
    for _ in range(terms - 1):
        p = r.astype(bf16)
        out.append(p)
        r = r - p.astype(f32)
    out.append(r.astype(bf16))
    return out


def _dot_split_lhs(x, m, terms):
    acc = None
    for p in _split_terms(x, terms):
        d = _dot(p, m)
        acc = d if acc is None else acc + d
    return acc


def _dot_split_rhs(m, x, terms):
    acc = None
    for p in _split_terms(x, terms):
        d = _dot(m, p)
        acc = d if acc is None else acc + d
    return acc


def _sigmoid(x):
    return 0.5 * (1.0 + jnp.tanh(0.5 * x))


def _rmsnorm(x, g):
    return x * lax.rsqrt(jnp.mean(x * x, axis=-1, keepdims=True) + NORM_EPS) * g


def _group_mask_a():
    r = lax.broadcasted_iota(jnp.int32, (W_A, W_A), 0) // (W_A // G_A)
    c = lax.broadcasted_iota(jnp.int32, (W_A, W_A), 1) // (W_A // G_A)
    return jnp.where(r == c, 1.0, 0.0).astype(bf16)


def _group_layernorm_silu(a, gm, g, b):
    inv = 1.0 / (W_A // G_A)
    mu = _dot_split_lhs(a, gm, 2) * inv
    d = a - mu
    var = _dot((d * d).astype(bf16), gm) * inv
    y = d * lax.rsqrt(var + LN_EPS) * g + b
    return y * _sigmoid(y)


def _head_layernorm(h, g):
    mu = jnp.mean(h, axis=-1, keepdims=True)
    d = h - mu
    var = jnp.mean(d * d, axis=-1, keepdims=True)
    return d * lax.rsqrt(var + LN_EPS) * g


def _rglru_gates(xcv, wrg, brg, lam):
    gates = _dot(xcv.astype(bf16), wrg) + brg
    r = _sigmoid(gates[:, :W_C])
    i = _sigmoid(gates[:, W_C:])
    log_a = RG_C * r * jax.nn.log_sigmoid(lam)
    a = jnp.exp(log_a)
    t = jnp.tanh(log_a)
    one_minus_a2 = -2.0 * t / (1.0 - t)
    u = jnp.sqrt(one_minus_a2) * i * xcv
    return a, u


def _mix_prompt_body(x_ref, g_ref, win_ref, caw_ref, cab_ref, lag_ref, lab_ref, cbw_ref, cbb_ref,
                     wq_ref, wk_ref, wv_ref, wg_ref, bg_ref, gn_ref, skip_ref, ccw_ref, ccb_ref,
                     wrg_ref, brg_ref, lam_ref, wout_ref,
                     x1_ref, sa_ref, sb_ref, sC_ref, sn_ref, sm_ref, sc_ref, sh_ref,
                     ahist, bhist, chist, conv_scr, C_scr, n_scr, m_scr, h_scr, qkv_scr, mix_scr):
    T, L = T_MIX, L_CHUNK
    s = pl.program_id(1)
    last = pl.num_programs(1) - 1

    @pl.when(s == 0)
    def _():
        ahist[0:A_HIST, :] = jnp.zeros((A_HIST, W_A), f32)
        bhist[0:S_HIST, :] = jnp.zeros((S_HIST, W_B), f32)
        chist[0:S_HIST, :] = jnp.zeros((S_HIST, W_C), f32)
        C_scr[...] = jnp.zeros(C_scr.shape, f32)
        n_scr[...] = jnp.zeros(n_scr.shape, f32)
        m_scr[...] = jnp.zeros(m_scr.shape, f32)
        h_scr[...] = jnp.zeros(h_scr.shape, f32)

    x = x_ref[...]
    hn = _rmsnorm(x, g_ref[...])
    hn_bf = hn.astype(bf16)
    ua = _dot(hn_bf, win_ref[:, 0:2 * W_A])
    ub = _dot(hn_bf, win_ref[:, 2 * W_A:2 * W_A + 2 * W_B])
    uc = _dot(hn_bf, win_ref[:, 2 * W_A + 2 * W_B:])
    xa, ga = ua[:, 0:W_A], ua[:, W_A:]
    xb, zb = ub[:, 0:W_B], ub[:, W_B:]
    xc, gc = uc[:, 0:W_C], uc[:, W_C:]

    bhist[pl.ds(S_HIST, T), :] = xb
    cbw = cbw_ref[...]
    cb = cbb_ref[...] + cbw[3:4, :] * xb
    for j in range(CONV_B - 1):
        cb = cb + cbw[j:j + 1, :] * bhist[pl.ds(S_HIST - (CONV_B - 1) + j, T), :]
    cb = cb * _sigmoid(cb)
    cb_bf = cb.astype(bf16)
    xb_bf = xb.astype(bf16)
    qs, ks, vs = [], [], []
    for h in range(H_B):
        hs = slice(h * DH_B, (h + 1) * DH_B)
        q = _dot(cb_bf[:, hs], wq_ref[h])
        k = _dot(cb_bf[:, hs], wk_ref[h]) * (DH_B ** -0.5)
        v = _dot(xb_bf[:, hs], wv_ref[h])
        qs.append(q)
        ks.append(k)
        vs.append(v)
        qkv_scr[:, (3 * h) * DH_B:(3 * h + 1) * DH_B] = q.astype(bf16)
        qkv_scr[:, (3 * h + 1) * DH_B:(3 * h + 2) * DH_B] = k.astype(bf16)
        qkv_scr[:, (3 * h + 2) * DH_B:(3 * h + 3) * DH_B] = v.astype(bf16)
    gates = _dot(qkv_scr[...], wg_ref[...]) + bg_ref[...]
    ig = gates[:, 0:LANES]
    lf = jax.nn.log_sigmoid(gates[:, LANES:2 * LANES])
    ri = lax.broadcasted_iota(jnp.int32, (L, L), 0)
    ci = lax.broadcasted_iota(jnp.int32, (L, L), 1)
    causal = ri >= ci
    tri = jnp.where(causal, 1.0, 0.0).astype(bf16)
    gn = gn_ref[...]
    skip = skip_ref[...]
    heads = range(H_B)
    stack = lambda parts: jnp.concatenate(parts, axis=0)
    gn_st = stack([jnp.broadcast_to(gn[:, h * DH_B:(h + 1) * DH_B], (L, DH_B)) for h in heads])
    ones_blk = jnp.ones((L, DH_B), bf16)
    C_old = [C_scr[h] for h in heads]
    n_old = [n_scr[h:h + 1, :] for h in heads]
    m_old = [m_scr[h:h + 1, 0:1] for h in heads]
    for c in range(T // L):
        rows = slice(c * L, (c + 1) * L)
        b_all = _dot_split_rhs(tri, lf[rows], 2)
        c_all = ig[rows] - b_all
        c_all_t = c_all.T
        q_bf = [qs[h][rows].astype(bf16) for h in heads]
        k_bf = [ks[h][rows].astype(bf16) for h in heads]
        v_bf = [vs[h][rows].astype(bf16) for h in heads]
        m_prev = stack([jnp.broadcast_to(m_old[h], (L, 1)) for h in heads])
        b_col = stack([b_all[:, h:h + 1] for h in heads])
        c_col = stack([c_all[:, h:h + 1] for h in heads])
        dm = stack([jnp.where(causal, c_all_t[h:h + 1, :], -jnp.inf) for h in heads])
        mx = jnp.maximum(m_prev, jnp.max(dm, axis=1, keepdims=True))
        sc = stack([_dot_nt(q_bf[h], k_bf[h]) for h in heads]) * jnp.exp(dm - mx)
        sc_bf = sc.astype(bf16)
        w_inter = jnp.exp(m_prev - mx)
        sv_aug = [_dot(sc_bf[h * L:(h + 1) * L], jnp.concatenate([v_bf[h], ones_blk], axis=1)) for h in heads]
        sv = stack([a[:, 0:DH_B] for a in sv_aug])
        s_sum = stack([a[:, DH_B:DH_B + 1] for a in sv_aug])
        qC = stack([_dot(q_bf[h], C_old[h].astype(bf16)) for h in heads])
        qn = stack([_dot_nt(q_bf[h], jnp.broadcast_to(n_old[h], (SUBLANES, DH_B)).astype(bf16))[:, 0:1]
                    for h in heads])
        num = sv + w_inter * qC
        den = s_sum + w_inter * qn
        hb = num / jnp.maximum(jnp.abs(den), jnp.exp(-(b_col + mx)))
        hb = _head_layernorm(hb, gn_st)
        mx_last = [mx[(h + 1) * L - 1:(h + 1) * L, :] for h in heads]
        g_in = jnp.exp(c_col - stack([jnp.broadcast_to(mx_last[h], (L, 1)) for h in heads]))
        C_new, n_new, m_new = [], [], []
        for h in heads:
            hs = slice(h * DH_B, (h + 1) * DH_B)
            hr = slice(h * L, (h + 1) * L)
            b_out = _sigmoid(zb[rows, hs]) * (hb[hr] + skip[:, hs] * cb[rows, hs])
            mix_scr[rows, hs] = b_out.astype(bf16)
            gk = g_in[hr] * ks[h][rows]
            decay = jnp.exp(m_old[h] - mx_last[h])
            C_new.append(decay * C_old[h] + _dot_tn(gk.astype(bf16), v_bf[h]))
            n_new.append(decay * n_old[h] + jnp.sum(gk, axis=0, keepdims=True))
            m_new.append(b_all[L - 1:L, h:h + 1] + mx_last[h])
        C_old, n_old, m_old = C_new, n_new, m_new
    for h in heads:
        C_scr[h] = C_old[h]
        n_scr[h:h + 1, :] = n_old[h]
        m_scr[h:h + 1, :] = jnp.broadcast_to(m_old[h], (1, LANES))

    ahist[pl.ds(A_HIST, T), :] = xa * _sigmoid(ga)
    caw = caw_ref[...]
    cab = cab_ref[...]
    first = A_HIST - (CONV_A - 1)

    for base in range(0, T, CONV_ROWS):
        for lo in range(0, W_A, LANES):
            blk = ahist[pl.ds(base, CONV_ROWS + A_HIST), lo:lo + LANES]
            acc = jnp.broadcast_to(cab[:, lo:lo + LANES], (CONV_ROWS, LANES))
            for rr in range(SUBLANES):
                n = CONV_ROWS if rr == 0 else CONV_ROWS + SUBLANES
                z = None
                for q in range(A_HIST // SUBLANES + 1):
                    j = SUBLANES * q + rr - first
                    if 0 <= j < CONV_A:
                        term = caw[j:j + 1, lo:lo + LANES] * blk[SUBLANES * q:SUBLANES * q + n, :]
                        z = term if z is None else z + term
                acc = acc + (z if rr == 0 else z[rr:rr + CONV_ROWS, :])
            conv_scr[pl.ds(base, CONV_ROWS), lo:lo + LANES] = acc
    a_out = _group_layernorm_silu(conv_scr[...], _group_mask_a(), lag_ref[...], lab_ref[...])
    y = _dot(a_out.astype(bf16), wout_ref[0:W_A, :])

    chist[pl.ds(S_HIST, T), :] = xc
    ccw = ccw_ref[...]
    xcv = ccb_ref[...] + ccw[3:4, :] * xc
    for j in range(CONV_C - 1):
        xcv = xcv + ccw[j:j + 1, :] * chist[pl.ds(S_HIST - (CONV_C - 1) + j, T), :]
    a_t, u_t = _rglru_gates(xcv, wrg_ref[...], brg_ref[...], lam_ref[...])
    row = lax.broadcasted_iota(jnp.int32, (T, W_C), 0)
    d = 1
    while d < T:
        if d < SUBLANES:
            a_sh = jnp.where(row >= d, pltpu.roll(a_t, d, axis=0), 1.0)
            u_sh = jnp.where(row >= d, pltpu.roll(u_t, d, axis=0), 0.0)
        else:
            a_sh = jnp.concatenate([jnp.ones((d, W_C), f32), a_t[:T - d]], axis=0)
            u_sh = jnp.concatenate([jnp.zeros((d, W_C), f32), u_t[:T - d]], axis=0)
        u_t = a_t * u_sh + u_t
        a_t = a_t * a_sh
        d *= 2
    hc = a_t * h_scr[...] + u_t
    h_scr[...] = hc[T - 1:T, :]
    y = y + _dot((hc * jax.nn.gelu(gc)).astype(bf16), wout_ref[W_A + W_B:, :])

    x1_ref[...] = x + (y + _dot(mix_scr[...], wout_ref[W_A:W_A + W_B, :]))

    ahist[0:A_HIST, :] = ahist[pl.ds(T, A_HIST), :]
    bhist[0:S_HIST, :] = bhist[pl.ds(T, S_HIST), :]
    chist[0:S_HIST, :] = chist[pl.ds(T, S_HIST), :]

    @pl.when(s == last)
    def _():
        sa_ref[...] = ahist[pl.ds(first, CONV_A - 1), :]
        sb_ref[...] = bhist[pl.ds(S_HIST - (CONV_B - 1), CONV_B - 1), :]
        sc_ref[...] = chist[pl.ds(S_HIST - (CONV_C - 1), CONV_C - 1), :]
        sC_ref[...] = C_scr[...]
        sn_ref[...] = n_scr[0:H_B, :]
        sm_ref[...] = m_scr[0:H_B, :]
        sh_ref[...] = h_scr[...]


def _full(shape):
    nd = len(shape)
    return pl.BlockSpec(shape, lambda *_: (0,) * nd)


def _layer(w, l):
    nd = w.ndim - 1
    return pl.BlockSpec((None,) + w.shape[1:], lambda *_: (l,) + (0,) * nd)


def _mix_prompt(x, lw, l):
    B, S, _ = x.shape
    T = T_MIX
    weights = [lw[k] for k in ("norm_mix_g", "w_in", "conv_a_w", "conv_a_b", "ln_a_g", "ln_a_b", "conv_b_w", "conv_b_b",
                               "w_q", "w_k", "w_v", "w_gates", "b_gates", "gn_b_g", "skip_b", "conv_c_w", "conv_c_b",
                               "w_rg", "b_rg", "lam", "w_out")]
    out_shape = (
        jax.ShapeDtypeStruct((B, S, D_MODEL), f32),
        jax.ShapeDtypeStruct((B, CONV_A - 1, W_A), f32),
        jax.ShapeDtypeStruct((B, CONV_B - 1, W_B), f32),
        jax.ShapeDtypeStruct((B, H_B, DH_B, DH_B), f32),
        jax.ShapeDtypeStruct((B, H_B, DH_B), f32),
        jax.ShapeDtypeStruct((B, H_B, LANES), f32),
        jax.ShapeDtypeStruct((B, CONV_C - 1, W_C), f32),
        jax.ShapeDtypeStruct((B, 1, W_C), f32),
    )
    per_b = lambda shp: pl.BlockSpec((None,) + shp, lambda b, s: (b,) + (0,) * len(shp))
    out_specs = (
        pl.BlockSpec((None, T, D_MODEL), lambda b, s: (b, s, 0)),
        per_b((CONV_A - 1, W_A)), per_b((CONV_B - 1, W_B)), per_b((H_B, DH_B, DH_B)), per_b((H_B, DH_B)),
        per_b((H_B, LANES)), per_b((CONV_C - 1, W_C)), per_b((1, W_C)),
    )
    scratch = [
        pltpu.VMEM((A_HIST + T, W_A), f32), pltpu.VMEM((S_HIST + T, W_B), f32), pltpu.VMEM((S_HIST + T, W_C), f32),
        pltpu.VMEM((T, W_A), f32),
        pltpu.VMEM((H_B, DH_B, DH_B), f32), pltpu.VMEM((SUBLANES, DH_B), f32), pltpu.VMEM((SUBLANES, LANES), f32),
        pltpu.VMEM((1, W_C), f32),
        pltpu.VMEM((T, 3 * W_B), bf16), pltpu.VMEM((T, W_B), bf16),
    ]
    return pl.pallas_call(
        _mix_prompt_body,
        grid=(B, S // T),
        in_specs=[pl.BlockSpec((None, T, D_MODEL), lambda b, s: (b, s, 0))] + [_layer(w, l) for w in weights],
        out_specs=out_specs,
        out_shape=out_shape,
        scratch_shapes=scratch,
        compiler_params=pltpu.CompilerParams(dimension_semantics=("arbitrary", "arbitrary"),
                                             vmem_limit_bytes=VMEM_LIMIT),
        name="mix_prompt",
    )(x, *weights)


def _step_conv(st_ref, w, bias, x_new, st_out):
    k1 = st_ref.shape[0]
    y = bias + w[k1:k1 + 1, :] * x_new
    for j in range(k1):
        y = y + w[j:j + 1, :] * st_ref[j]
        if j > 0:
            st_out[j - 1] = st_ref[j]
    st_out[k1 - 1] = x_new
    return y


def _step_pre_body(x_ref, g_ref, win_ref, sa_ref, caw_ref, cab_ref, lag_ref, lab_ref, sb_ref, cbw_ref, cbb_ref,
                   wq_ref, wk_ref, wv_ref, wg_ref, bg_ref, n_ref, m_ref, sc_ref, ccw_ref, ccb_ref,
                   wrg_ref, brg_ref, lam_ref, h_ref, skip_ref,
                   sa_o, sb_o, sc_o, n_o, m_o, h_o, q_o, gk_o, v_o, dec_o, den_o, sv_o, zsig_o, skcb_o, ac_o,
                   qkv_scr):
    x = x_ref[...]
    hn = _rmsnorm(x, g_ref[...])
    u = _dot(hn.astype(bf16), win_ref[...])
    xa, ga = u[:, 0:256], u[:, 256:512]
    xb, zb = u[:, 512:1024], u[:, 1024:1536]
    xc, gc = u[:, 1536:1792], u[:, 1792:2048]

    a_new = xa * _sigmoid(ga)
    conv = _step_conv(sa_ref, caw_ref[...], cab_ref[...], a_new, sa_o)
    ac_o[:, 0:W_A] = _group_layernorm_silu(conv, _group_mask_a(), lag_ref[...], lab_ref[...])

    cb = _step_conv(sb_ref, cbw_ref[...], cbb_ref[...], xb, sb_o)
    cb = cb * _sigmoid(cb)
    cb_bf = cb.astype(bf16)
    xb_bf = xb.astype(bf16)
    qs, ks, vs = [], [], []
    for h in range(H_B):
        hs = slice(h * DH_B, (h + 1) * DH_B)
        q = _dot(cb_bf[:, hs], wq_ref[h])
        k = _dot(cb_bf[:, hs], wk_ref[h]) * (DH_B ** -0.5)
        v = _dot(xb_bf[:, hs], wv_ref[h])
        qs.append(q)
        ks.append(k)
        vs.append(v)
        qkv_scr[:, (3 * h) * DH_B:(3 * h + 1) * DH_B] = q.astype(bf16)
        qkv_scr[:, (3 * h + 1) * DH_B:(3 * h + 2) * DH_B] = k.astype(bf16)
        qkv_scr[:, (3 * h + 2) * DH_B:(3 * h + 3) * DH_B] = v.astype(bf16)
    gates = _dot(qkv_scr[...], wg_ref[...]) + bg_ref[...]
    ig = gates[:, 0:LANES]
    lf = jax.nn.log_sigmoid(gates[:, LANES:2 * LANES])
    m0 = m_ref[...]
    m_t = jnp.maximum(lf + m0, ig)
    g_in = jnp.exp(ig - m_t)
    decay = jnp.exp(lf + m0 - m_t)
    lane = lax.broadcasted_iota(jnp.int32, ig.shape, 1)
    qk = jnp.zeros(ig.shape, f32)
    qn = jnp.zeros(ig.shape, f32)
    for h in range(H_B):
        hs = slice(h * DH_B, (h + 1) * DH_B)
        n_h = n_ref[:, h, :]
        qk = jnp.where(lane == h, jnp.sum(qs[h] * ks[h], axis=-1, keepdims=True), qk)
        qn = jnp.where(lane == h, jnp.sum(qs[h] * n_h, axis=-1, keepdims=True), qn)
        gk = g_in[:, h:h + 1] * ks[h]
        n_o[:, h, :] = decay[:, h:h + 1] * n_h + gk
        q_o[:, hs] = qs[h]
        gk_o[:, hs] = gk
        v_o[:, hs] = vs[h]
    s_t = qk * g_in
    den = s_t + decay * qn
    den_o[...] = jnp.maximum(jnp.abs(den), jnp.exp(-m_t))
    dec_o[...] = decay
    m_o[...] = m_t
    for h in range(H_B):
        hs = slice(h * DH_B, (h + 1) * DH_B)
        sv_o[:, hs] = s_t[:, h:h + 1] * vs[h]
    zsig_o[...] = _sigmoid(zb)
    skcb_o[...] = skip_ref[...] * cb

    xcv = _step_conv(sc_ref, ccw_ref[...], ccb_ref[...], xc, sc_o)
    a_t, u_t = _rglru_gates(xcv, wrg_ref[...], brg_ref[...], lam_ref[...])
    hc = a_t * h_ref[...] + u_t
    h_o[...] = hc
    ac_o[:, W_A:] = hc * jax.nn.gelu(gc)


def _step_read_body(C_ref, qT_ref, qc_ref):
    for bb in range(STEP_BB):
        for h in range(H_B):
            hs = slice(h * DH_B, (h + 1) * DH_B)
            qc_ref[bb:bb + 1, hs] = jnp.sum(C_ref[bb, h] * qT_ref[h, :, bb:bb + 1], axis=0, keepdims=True)


def _step_state_body(C_ref, qT_ref, gkT_ref, v_ref, dec_ref, Cn_ref, qc_ref):
    for bb in range(STEP_BB):
        for h in range(H_B):
            hs = slice(h * DH_B, (h + 1) * DH_B)
            C = C_ref[bb, h]
            qc_ref[bb:bb + 1, hs] = jnp.sum(C * qT_ref[h, :, bb:bb + 1], axis=0, keepdims=True)
            Cn_ref[bb, h] = dec_ref[bb:bb + 1, h:h + 1] * C + gkT_ref[h, :, bb:bb + 1] * v_ref[bb:bb + 1, hs]


def _to_cols(t):
    return t.reshape(t.shape[0] // STEP_BB, STEP_BB, H_B, DH_B).transpose(0, 2, 3, 1)


def _step_read(C_all, layer, q):
    Bs = q.shape[0]
    return pl.pallas_call(
        _step_read_body,
        grid=(Bs // STEP_BB,),
        in_specs=[pl.BlockSpec((None, STEP_BB, H_B, DH_B, DH_B), lambda i: (layer, i, 0, 0, 0)),
                  pl.BlockSpec((None, H_B, DH_B, STEP_BB), lambda i: (i, 0, 0, 0))],
        out_specs=pl.BlockSpec((STEP_BB, W_B), lambda i: (i, 0)),
        out_shape=jax.ShapeDtypeStruct((Bs, W_B), f32),
        compiler_params=pltpu.CompilerParams(dimension_semantics=("arbitrary",), vmem_limit_bytes=VMEM_LIMIT),
        name="step_read",
    )(C_all, _to_cols(q))


def _step_state(C_all, q, gk, v, dec):
    depth, Bs = q.shape[0], q.shape[1]
    cols = lambda t: jnp.stack([_to_cols(t[l]) for l in range(depth)])
    blk5 = pl.BlockSpec((None, STEP_BB, H_B, DH_B, DH_B), lambda l, i: (l, i, 0, 0, 0))
    colspec = pl.BlockSpec((None, None, H_B, DH_B, STEP_BB), lambda l, i: (l, i, 0, 0, 0))
    return pl.pallas_call(
        _step_state_body,
        grid=(depth, Bs // STEP_BB),
        in_specs=[blk5, colspec, colspec,
                  pl.BlockSpec((None, STEP_BB, W_B), lambda l, i: (l, i, 0)),
                  pl.BlockSpec((None, STEP_BB, LANES), lambda l, i: (l, i, 0))],
        out_specs=(blk5, pl.BlockSpec((None, STEP_BB, W_B), lambda l, i: (l, i, 0))),
        out_shape=(jax.ShapeDtypeStruct(C_all.shape, f32), jax.ShapeDtypeStruct((depth, Bs, W_B), f32)),
        compiler_params=pltpu.CompilerParams(dimension_semantics=("arbitrary", "arbitrary"),
                                             vmem_limit_bytes=VMEM_LIMIT),
        name="step_state",
    )(C_all, cols(q), cols(gk), v, dec)


def _step_post_body(x_ref, ac_ref, zsig_ref, skcb_ref, sv_ref, qc_ref, dec_ref, den_ref, gn_ref, wout_ref,
                    x1_ref, mix_scr):
    gn = gn_ref[...]
    dec = dec_ref[...]
    den = den_ref[...]
    mix_scr[:, 0:W_A] = ac_ref[:, 0:W_A].astype(bf16)
    mix_scr[:, W_A + W_B:] = ac_ref[:, W_A:].astype(bf16)
    for h in range(H_B):
        hs = slice(h * DH_B, (h + 1) * DH_B)
        num = sv_ref[:, hs] + dec[:, h:h + 1] * qc_ref[:, hs]
        hb = _head_layernorm(num / den[:, h:h + 1], gn[:, hs])
        mix_scr[:, W_A + h * DH_B:W_A + (h + 1) * DH_B] = (zsig_ref[:, hs] * (hb + skcb_ref[:, hs])).astype(bf16)
    x1_ref[...] = x_ref[...] + _dot(mix_scr[...], wout_ref[...])


def _step_pre(x, st, lw, l):
    buf_a, buf_b, n0, m0, buf_c, hc0 = st
    buf_a, buf_b, buf_c = (b.transpose(0, 2, 1, 3) for b in (buf_a, buf_b, buf_c))
    Bs = x.shape[0]
    m_in = jnp.pad(m0, ((0, 0), (0, 0), (0, LANES - H_B)))
    sds = lambda *shape: jax.ShapeDtypeStruct(shape, f32)
    W = lambda k: (lw[k], True)
    L = lambda a: (a, True)
    A = lambda a: (a, False)
    pre_in = [A(x), W("norm_mix_g"), W("w_in"), L(buf_a), W("conv_a_w"), W("conv_a_b"), W("ln_a_g"), W("ln_a_b"),
              L(buf_b), W("conv_b_w"), W("conv_b_b"), W("w_q"), W("w_k"), W("w_v"), W("w_gates"), W("b_gates"),
              L(n0), L(m_in), L(buf_c), W("conv_c_w"), W("conv_c_b"), W("w_rg"), W("b_rg"), W("lam"), L(hc0),
              W("skip_b")]
    out_shape = (sds(*buf_a.shape[1:]), sds(*buf_b.shape[1:]), sds(*buf_c.shape[1:]), sds(*n0.shape[1:]),
                 sds(Bs, LANES), sds(Bs, W_C),
                 sds(Bs, W_B), sds(Bs, W_B), sds(Bs, W_B), sds(Bs, LANES), sds(Bs, LANES), sds(Bs, W_B),
                 sds(Bs, W_B), sds(Bs, W_B), sds(Bs, W_A + W_C))
    (sa_n, sb_n, sc_n, n_n, m_n, h_n, q, gk, v, dec, den, sv, zsig, skcb, ac) = pl.pallas_call(
        _step_pre_body,
        grid=(1,),
        in_specs=[_layer(a, l) if is_w else _full(a.shape) for a, is_w in pre_in],
        out_specs=[_full(o.shape) for o in out_shape],
        out_shape=out_shape,
        scratch_shapes=[pltpu.VMEM((Bs, 3 * W_B), bf16)],
        compiler_params=pltpu.CompilerParams(dimension_semantics=("arbitrary",), vmem_limit_bytes=VMEM_LIMIT),
        name="step_pre",
    )(*[a for a, _ in pre_in])
    new = (sa_n, sb_n, n_n, m_n[:, :H_B], sc_n, h_n)
    return new, (q, gk, v, dec), (ac, zsig, skcb, sv, dec, den)


def _step_post(x, post_in, qc, lw, l):
    ac, zsig, skcb, sv, dec, den = post_in
    acts = [x, ac, zsig, skcb, sv, qc, dec, den]
    return pl.pallas_call(
        _step_post_body,
        grid=(1,),
        in_specs=[_full(a.shape) for a in acts] + [_layer(lw["gn_b_g"], l), _layer(lw["w_out"], l)],
        out_specs=_full(x.shape),
        out_shape=jax.ShapeDtypeStruct(x.shape, f32),
        scratch_shapes=[pltpu.VMEM(x.shape, bf16)],
        compiler_params=pltpu.CompilerParams(dimension_semantics=("arbitrary",), vmem_limit_bytes=VMEM_LIMIT),
        name="step_post",
    )(*acts, lw["gn_b_g"], lw["w_out"])


def _swiglu(hb, wg_ref, wu_ref, wd_ref):
    f = wd_ref.shape[0]
    main = f // MXU_WIDTH * MXU_WIDTH
    glu = lambda gt, up: (gt * _sigmoid(gt) * up).astype(bf16)
    if main == f:
        act = glu(_dot(hb, wg_ref[...]), _dot(hb, wu_ref[...]))
    else:
        tail = _dot(hb, jnp.concatenate([wg_ref[:, main:], wu_ref[:, main:]], axis=1))
        act = jnp.concatenate([glu(_dot(hb, wg_ref[:, :main]), _dot(hb, wu_ref[:, :main])),
                               glu(tail[:, :f - main], tail[:, f - main:])], axis=1)
    return _dot(act, wd_ref[...])


def _ffn_body(*refs, moe, final_norm):
    it = iter(refs)
    x_ref, g_ref, wg_ref, wu_ref, wd_ref = next(it), next(it), next(it), next(it), next(it)
    wr_ref = next(it) if moe else None
    br_ref = next(it) if moe else None
    gf_ref = next(it) if final_norm else None
    o_ref, hn_scr, acc_scr = next(it), next(it), next(it)
    comb_scr = next(it) if moe else None
    j = pl.program_id(1)

    @pl.when(j == 0)
    def _():
        hn = _rmsnorm(x_ref[...], g_ref[...])
        hn_scr[...] = hn.astype(bf16)
        acc_scr[...] = jnp.zeros(acc_scr.shape, f32)
        if moe:
            lane, i1, i2, g1, g2 = _top2(_router_logits(hn, wr_ref[...], br_ref[...]))
            comb_scr[...] = jnp.where(lane == i1, g1, 0.0) + jnp.where(lane == i2, g2, 0.0)

    y = _swiglu(hn_scr[...], wg_ref, wu_ref, wd_ref)
    if moe:
        lane = lax.broadcasted_iota(jnp.int32, comb_scr.shape, 1)
        y = y * jnp.sum(jnp.where(lane == j, comb_scr[...], 0.0), axis=-1, keepdims=True)
    acc_scr[...] += y

    @pl.when(j == pl.num_programs(1) - 1)
    def _():
        out = x_ref[...] + acc_scr[...]
        if final_norm:
            out = _rmsnorm(out, gf_ref[...])
        o_ref[...] = out


def _ffn(x, g, wg, wu, wd, router=None, final_g=None):
    N = x.shape[0]
    E, F = wd.shape[0], wd.shape[1]
    tm = min(TM_FFN, N)
    moe = router is not None
    final_norm = final_g is not None
    ins = [x, g, wg, wu, wd]
    once = dict(pipeline_mode=pl.Buffered(1)) if E == 1 else {}
    in_specs = [pl.BlockSpec((tm, D_MODEL), lambda i, j: (i, 0)), _full(g.shape),
                pl.BlockSpec((None, D_MODEL, F), lambda i, j: (j, 0, 0), **once),
                pl.BlockSpec((None, D_MODEL, F), lambda i, j: (j, 0, 0), **once),
                pl.BlockSpec((None, F, D_MODEL), lambda i, j: (j, 0, 0), **once)]
    scratch = [pltpu.VMEM((tm, D_MODEL), bf16), pltpu.VMEM((tm, D_MODEL), f32)]
    if moe:
        ins += list(router)
        in_specs += [_full(router[0].shape), _full(router[1].shape)]
        scratch.append(pltpu.VMEM((tm, LANES), f32))
    if final_norm:
        ins.append(final_g)
        in_specs.append(_full(final_g.shape))
    return pl.pallas_call(
        functools.partial(_ffn_body, moe=moe, final_norm=final_norm),
        grid=(N // tm, E),
        in_specs=in_specs,
        out_specs=pl.BlockSpec((tm, D_MODEL), lambda i, j: (i, 0)),
        out_shape=jax.ShapeDtypeStruct((N, D_MODEL), f32),
        scratch_shapes=scratch,
        compiler_params=pltpu.CompilerParams(dimension_semantics=("arbitrary", "arbitrary"),
                                             vmem_limit_bytes=VMEM_LIMIT),
        name="ffn_moe" if moe else "ffn_dense",
    )(*ins)


TM_MOE = 512
SEG_ALIGN = 16
SEG_SIZES = (512, 256, 128, 64, 32, 16)
MOE_ROWS = 2 * TM_MOE + N_EXPERTS * SEG_ALIGN
META_COLS = 3 * N_EXPERTS


def _seg_dma(src, src_off, dst, dst_off, nrows, sem, wait):
    done = jnp.int32(0)
    for size in SEG_SIZES:
        take = (nrows & size) != 0

        @pl.when(take)
        def _(done=done, size=size):
            cp = pltpu.make_async_copy(src.at[pl.ds(pl.multiple_of(src_off + done, SEG_ALIGN), size)],
                                       dst.at[pl.ds(pl.multiple_of(dst_off + done, SEG_ALIGN), size)], sem)
            if wait:
                cp.wait()
            else:
                cp.start()

        done = done + jnp.where(take, size, 0)


def _top2(logits):
    lane = lax.broadcasted_iota(jnp.int32, logits.shape, 1).astype(f32)
    lg = jnp.where(lane < N_EXPERTS, logits, -jnp.inf)
    m1 = jnp.max(lg, axis=-1, keepdims=True)
    i1 = jnp.min(jnp.where(lg == m1, lane, float(LANES)), axis=-1, keepdims=True)
    lg2 = jnp.where(lane == i1, -jnp.inf, lg)
    m2 = jnp.max(lg2, axis=-1, keepdims=True)
    i2 = jnp.min(jnp.where(lg2 == m2, lane, float(LANES)), axis=-1, keepdims=True)
    e2 = jnp.exp(m2 - m1)
    return lane, i1, i2, 1.0 / (1.0 + e2), e2 / (1.0 + e2)


def _router_logits(hn, wr, br):
    h1, h2 = _split_terms(hn, 2)
    w1, w2 = _split_terms(wr, 2)
    return _dot(h1, w1) + (_dot(h1, w2) + _dot(h2, w1)) + br


def _expert_ranks(lane, i1, i2):
    TM = lane.shape[0]
    sel1 = lane == i1
    sel2 = lane == i2
    sel = jnp.where(sel1, 1.0, jnp.where(sel2, 1.0, 0.0))
    earlier = lax.broadcasted_iota(jnp.int32, (TM, TM), 0) > lax.broadcasted_iota(jnp.int32, (TM, TM), 1)
    rank = _dot(jnp.where(earlier, 1.0, 0.0).astype(bf16), sel.astype(bf16))
    return sel1, sel2, rank, rank[TM - 1:TM, :] + sel[TM - 1:TM, :]


def _seg_pad(c):
    return ((c + (SEG_ALIGN - 1)) // SEG_ALIGN) * SEG_ALIGN


def _moe_count_body(x_ref, g_ref, wr_ref, br_ref, route_ref, cnt_ref):
    i = pl.program_id(0)
    hn = _rmsnorm(x_ref[...], g_ref[...])
    lane, i1, i2, g1, g2 = _top2(_router_logits(hn, wr_ref[...], br_ref[...]))
    route_ref[...] = jnp.where(lane == 0, i1, jnp.where(lane == 1, i2, jnp.where(lane == 2, g1,
                                                                                 jnp.where(lane == 3, g2, 0.0))))
    cnt = jnp.sum(jnp.where(lane == i1, 1.0, jnp.where(lane == i2, 1.0, 0.0)), axis=0, keepdims=True)
    for e in range(N_EXPERTS):
        cnt_ref[i, e] = cnt[0, e].astype(jnp.int32)


def _moe_route_body(cnt_ref, x_ref, g_ref, route_ref, xs_hbm, gs_hbm, rinfo_ref, meta_ref, te_ref, nv_ref,
                    xperm, gperm, pos_smem, sem):
    TM, R = TM_MOE, MOE_ROWS
    i = pl.program_id(0)
    nt = pl.num_programs(0)
    n_tiles = te_ref.shape[0]

    @pl.when(i == 0)
    def _():
        base = jnp.int32(0)
        for e in range(N_EXPERTS):
            rows = lax.fori_loop(0, nt, lambda t, a, e=e: a + _seg_pad(cnt_ref[t, e]), jnp.int32(0))
            first_tile = base // TM
            tiles = (rows + (TM - 1)) // TM

            def mark(k, c, e=e, first_tile=first_tile):
                te_ref[first_tile + k] = e
                return c

            lax.fori_loop(0, tiles, mark, 0)
            pos_smem[e] = base
            base = base + tiles * TM
        nv = base // TM
        nv_ref[0] = nv

        def mark_rest(k, c):
            te_ref[k] = N_EXPERTS - 1
            return c

        lax.fori_loop(nv, n_tiles, mark_rest, 0)

    hn = _rmsnorm(x_ref[...], g_ref[...])
    route = route_ref[...]
    i1, i2, g1, g2 = route[:, 0:1], route[:, 1:2], route[:, 2:3], route[:, 3:4]
    lane = lax.broadcasted_iota(jnp.int32, route.shape, 1).astype(f32)
    sel1, sel2, rank, _ = _expert_ranks(lane, i1, i2)
    seg_rows, seg_off = [], []
    acc = jnp.int32(0)
    for e in range(N_EXPERTS):
        seg_rows.append(_seg_pad(cnt_ref[i, e]))
        seg_off.append(acc)
        acc = acc + seg_rows[e]
    lane1 = lax.broadcasted_iota(jnp.int32, (1, LANES), 1)
    offv = jnp.zeros((1, LANES), f32)
    for e in range(N_EXPERTS):
        offv = jnp.where(lane1 == e, seg_off[e].astype(f32), offv)
    dest = rank + offv
    d1 = jnp.sum(jnp.where(sel1, dest, 0.0), axis=-1, keepdims=True)
    d2 = jnp.sum(jnp.where(sel2, dest, 0.0), axis=-1, keepdims=True)
    rinfo = jnp.where(lane == 0, d1, jnp.where(lane == 1, d2, jnp.where(lane == 2, g1, jnp.where(lane == 3, g2, 0.0))))
    rinfo_ref[...] = rinfo
    rinfo_t = rinfo.T
    d1r, d2r, g1r, g2r = rinfo_t[0:1, :], rinfo_t[1:2, :], rinfo_t[2:3, :], rinfo_t[3:4, :]
    riota = lax.broadcasted_iota(jnp.int32, (R, TM), 0).astype(f32)
    hit1 = riota == d1r
    hit2 = riota == d2r
    perm = jnp.where(hit1, 1.0, jnp.where(hit2, 1.0, 0.0)).astype(bf16)
    xperm[i % 2] = _dot(perm, hn.astype(bf16)).astype(bf16)
    gcol = jnp.sum(jnp.where(hit1, g1r, 0.0) + jnp.where(hit2, g2r, 0.0), axis=1, keepdims=True)
    gperm[i % 2] = jnp.broadcast_to(gcol, (R, LANES))

    for e in range(N_EXPERTS):
        pos = pos_smem[e]
        meta_ref[i, e] = pos
        meta_ref[i, N_EXPERTS + e] = seg_off[e]
        meta_ref[i, 2 * N_EXPERTS + e] = seg_rows[e]
        pos_smem[e] = pos + seg_rows[e]

    def copies(tile, wait):
        for slot in range(2):
            @pl.when(tile % 2 == slot)
            def _(slot=slot):
                for e in range(N_EXPERTS):
                    src_off, dst_off, n = meta_ref[tile, N_EXPERTS + e], meta_ref[tile, e], meta_ref[tile, 2 * N_EXPERTS + e]
                    _seg_dma(xperm.at[slot], src_off, xs_hbm, dst_off, n, sem.at[2 * slot], wait)
                    _seg_dma(gperm.at[slot], src_off, gs_hbm, dst_off, n, sem.at[2 * slot + 1], wait)

    copies(i, False)
    pl.when(i > 0)(lambda: copies(i - 1, True))

    @pl.when(i == nt - 1)
    def _():
        copies(i, True)
        xz, gz = xperm.at[0], gperm.at[0]
        xz[0:TM, :] = jnp.zeros((TM, D_MODEL), bf16)
        gz[0:TM, :] = jnp.zeros((TM, LANES), f32)
        pads = []
        for e in range(N_EXPERTS):
            pos = pos_smem[e]
            pads.append((-pos) & (TM - 1))
            _seg_dma(xz, 0, xs_hbm, pos, pads[e], sem.at[0], False)
            _seg_dma(gz, 0, gs_hbm, pos, pads[e], sem.at[1], False)
        for e in range(N_EXPERTS):
            _seg_dma(xz, 0, xs_hbm, pos_smem[e], pads[e], sem.at[0], True)
            _seg_dma(gz, 0, gs_hbm, pos_smem[e], pads[e], sem.at[1], True)

        def zero_tile(k, c):
            row = pl.multiple_of(k * TM, TM)
            cx = pltpu.make_async_copy(xz.at[pl.ds(0, TM)], xs_hbm.at[pl.ds(row, TM)], sem.at[0])
            cg = pltpu.make_async_copy(gz.at[pl.ds(0, TM)], gs_hbm.at[pl.ds(row, TM)], sem.at[1])
            cx.start()
            cg.start()
            cx.wait()
            cg.wait()
            return c

        lax.fori_loop(nv_ref[0], n_tiles, zero_tile, 0)


def _moe_ffn_body(te_ref, nv_ref, xs_ref, gs_ref, wg_ref, wu_ref, wd_ref, ys_ref):
    i = pl.program_id(0)

    @pl.when(i < nv_ref[0])
    def _():
        ys_ref[...] = (_swiglu(xs_ref[...], wg_ref, wu_ref, wd_ref) * gs_ref[:, 0:1]).astype(bf16)

    @pl.when(i >= nv_ref[0])
    def _():
        ys_ref[...] = jnp.zeros(ys_ref.shape, bf16)


def _moe_combine_body(*refs, final_norm):
    it = iter(refs)
    meta_ref, x_ref, rinfo_ref, ys_hbm = next(it), next(it), next(it), next(it)
    gf_ref = next(it) if final_norm else None
    o_ref, yperm, sem = next(it), next(it), next(it)
    TM, R = TM_MOE, MOE_ROWS
    i = pl.program_id(0)
    nt = pl.num_programs(0)

    def fetch(tile, slot, wait):
        for e in range(N_EXPERTS):
            _seg_dma(ys_hbm, meta_ref[tile, e], yperm.at[slot], meta_ref[tile, N_EXPERTS + e],
                     meta_ref[tile, 2 * N_EXPERTS + e], sem.at[slot], wait)

    def both_slots(tile, wait):
        for slot in range(2):
            pl.when(tile % 2 == slot)(lambda slot=slot: fetch(tile, slot, wait))

    @pl.when(i == 0)
    def _():
        yperm[...] = jnp.zeros(yperm.shape, bf16)
        fetch(0, 0, False)

    @pl.when(i + 1 < nt)
    def _():
        both_slots(i + 1, False)

    rinfo = rinfo_ref[...]
    ciota = lax.broadcasted_iota(jnp.int32, (TM, R), 1).astype(f32)
    unperm = jnp.where(ciota == rinfo[:, 0:1], 1.0, jnp.where(ciota == rinfo[:, 1:2], 1.0, 0.0)).astype(bf16)
    both_slots(i, True)
    out = x_ref[...] + _dot(unperm, yperm[i % 2])
    if final_norm:
        out = _rmsnorm(out, gf_ref[...])
    o_ref[...] = out


def _moe_prompt(x, g, wg, wu, wd, wr, br, final_g):
    N = x.shape[0]
    TM = TM_MOE
    nt = N // TM
    F = wd.shape[1]
    n_tiles = -(-(2 * N + nt * N_EXPERTS * (SEG_ALIGN - 1) + N_EXPERTS * (TM - 1)) // TM)
    rows = n_tiles * TM
    smem = pl.BlockSpec(memory_space=pltpu.SMEM)
    arb = pltpu.CompilerParams(dimension_semantics=("arbitrary",), vmem_limit_bytes=VMEM_LIMIT)
    route, cnt = pl.pallas_call(
        _moe_count_body,
        grid=(nt,),
        in_specs=[pl.BlockSpec((TM, D_MODEL), lambda i: (i, 0)), _full(g.shape), _full(wr.shape), _full(br.shape)],
        out_specs=[pl.BlockSpec((TM, LANES), lambda i: (i, 0)), smem],
        out_shape=[jax.ShapeDtypeStruct((N, LANES), f32), jax.ShapeDtypeStruct((nt, N_EXPERTS), jnp.int32)],
        compiler_params=arb,
        name="moe_count",
    )(x, g, wr, br)

    xs, gs, rinfo, meta, te, nv = pl.pallas_call(
        _moe_route_body,
        grid_spec=pltpu.PrefetchScalarGridSpec(
            num_scalar_prefetch=1,
            grid=(nt,),
            in_specs=[pl.BlockSpec((TM, D_MODEL), lambda i, c: (i, 0)), pl.BlockSpec(g.shape, lambda i, c: (0, 0)),
                      pl.BlockSpec((TM, LANES), lambda i, c: (i, 0))],
            out_specs=[pl.BlockSpec(memory_space=pl.ANY), pl.BlockSpec(memory_space=pl.ANY),
                       pl.BlockSpec((TM, LANES), lambda i, c: (i, 0)), smem, smem, smem],
            scratch_shapes=[pltpu.VMEM((2, MOE_ROWS, D_MODEL), bf16), pltpu.VMEM((2, MOE_ROWS, LANES), f32),
                            pltpu.SMEM((N_EXPERTS,), jnp.int32), pltpu.SemaphoreType.DMA((4,))],
        ),
        out_shape=[jax.ShapeDtypeStruct((rows, D_MODEL), bf16), jax.ShapeDtypeStruct((rows, LANES), f32),
                   jax.ShapeDtypeStruct((N, LANES), f32), jax.ShapeDtypeStruct((nt, META_COLS), jnp.int32),
                   jax.ShapeDtypeStruct((n_tiles,), jnp.int32), jax.ShapeDtypeStruct((1,), jnp.int32)],
        compiler_params=arb,
        name="moe_route",
    )(cnt, x, g, route)

    ys = pl.pallas_call(
        _moe_ffn_body,
        grid_spec=pltpu.PrefetchScalarGridSpec(
            num_scalar_prefetch=2,
            grid=(n_tiles,),
            in_specs=[pl.BlockSpec((TM, D_MODEL), lambda i, te, nv: (i, 0)),
                      pl.BlockSpec((TM, LANES), lambda i, te, nv: (i, 0)),
                      pl.BlockSpec((None, D_MODEL, F), lambda i, te, nv: (te[i], 0, 0)),
                      pl.BlockSpec((None, D_MODEL, F), lambda i, te, nv: (te[i], 0, 0)),
                      pl.BlockSpec((None, F, D_MODEL), lambda i, te, nv: (te[i], 0, 0))],
            out_specs=pl.BlockSpec((TM, D_MODEL), lambda i, te, nv: (i, 0)),
        ),
        out_shape=jax.ShapeDtypeStruct((rows, D_MODEL), bf16),
        compiler_params=arb,
        name="moe_ffn",
    )(te, nv, xs, gs, wg, wu, wd)

    final_norm = final_g is not None
    ins = [meta, x, rinfo, ys] + ([final_g] if final_norm else [])
    in_specs = [pl.BlockSpec((TM, D_MODEL), lambda i, m: (i, 0)), pl.BlockSpec((TM, LANES), lambda i, m: (i, 0)),
                pl.BlockSpec(memory_space=pl.ANY)]
    if final_norm:
        in_specs.append(pl.BlockSpec(final_g.shape, lambda i, m: (0, 0)))
    return pl.pallas_call(
        functools.partial(_moe_combine_body, final_norm=final_norm),
        grid_spec=pltpu.PrefetchScalarGridSpec(
            num_scalar_prefetch=1,
            grid=(nt,),
            in_specs=in_specs,
            out_specs=pl.BlockSpec((TM, D_MODEL), lambda i, m: (i, 0)),
            scratch_shapes=[pltpu.VMEM((2, MOE_ROWS, D_MODEL), bf16), pltpu.SemaphoreType.DMA((2,))],
        ),
        out_shape=jax.ShapeDtypeStruct((N, D_MODEL), f32),
        compiler_params=pltpu.CompilerParams(dimension_semantics=("arbitrary",), vmem_limit_bytes=VMEM_LIMIT),
        name="moe_combine",
    )(*ins)


def _block_diag(w):
    d, h, di, do = w.shape
    return jnp.einsum("lhij,hg->lhigj", w, jnp.eye(h, dtype=w.dtype)).reshape(d, h * di, h * do)


def _mixer_weights(p):
    row = lambda v: v[:, None, :]
    lanes = lambda w: jnp.pad(w, [(0, 0)] * (w.ndim - 1) + [(0, LANES - w.shape[-1])])
    return dict(
        norm_mix_g=row(p["norm_mix_g"]), w_in=p["w_in"].astype(bf16),
        conv_a_w=p["conv_a_w"], conv_a_b=row(p["conv_a_b"]), ln_a_g=row(p["ln_a_g"]), ln_a_b=row(p["ln_a_b"]),
        conv_b_w=p["conv_b_w"], conv_b_b=row(p["conv_b_b"]),
        w_q=p["w_q"].astype(bf16), w_k=p["w_k"].astype(bf16), w_v=p["w_v"].astype(bf16),
        w_gates=jnp.concatenate([lanes(p["w_ig"]), lanes(p["w_fg"])], axis=-1).astype(bf16),
        b_gates=row(jnp.concatenate([lanes(p["b_ig"]), lanes(p["b_fg"])], axis=-1)),
        gn_b_g=row(p["gn_b_g"]), skip_b=row(p["skip_b"]),
        conv_c_w=p["conv_c_w"], conv_c_b=row(p["conv_c_b"]),
        w_rg=jnp.concatenate([_block_diag(p["w_ra"]), _block_diag(p["w_ix"])], axis=-1).astype(bf16),
        b_rg=row(jnp.concatenate([p["b_ra"], p["b_ix"]], axis=-1)), lam=row(p["lam"]),
        w_out=p["w_out"].astype(bf16),
    )


def kernel(x_prompt, x_sample, state_conv_a, state_conv_b, state_mlstm_C, state_mlstm_n, state_mlstm_m, state_conv_c, state_rglru_h, norm_mix_g, w_in, conv_a_w, conv_a_b, ln_a_g, ln_a_b, conv_b_w, conv_b_b, w_q, w_k, w_v, w_ig, b_ig, w_fg, b_fg, gn_b_g, skip_b, conv_c_w, conv_c_b, w_ra, b_ra, w_ix, b_ix, lam, w_out, norm_ffn_g, w_gate, w_up, w_down, w_router, b_router, we_gate, we_up, we_down, norm_final_g):
    p = dict(norm_mix_g=norm_mix_g, w_in=w_in, conv_a_w=conv_a_w, conv_a_b=conv_a_b, ln_a_g=ln_a_g, ln_a_b=ln_a_b,
             conv_b_w=conv_b_w, conv_b_b=conv_b_b, w_q=w_q, w_k=w_k, w_v=w_v, w_ig=w_ig, b_ig=b_ig, w_fg=w_fg,
             b_fg=b_fg, gn_b_g=gn_b_g, skip_b=skip_b, conv_c_w=conv_c_w, conv_c_b=conv_c_b, w_ra=w_ra, b_ra=b_ra,
             w_ix=w_ix, b_ix=b_ix, lam=lam, w_out=w_out)
    depth = w_in.shape[0]
    Bp, S, _ = x_prompt.shape
    Bs = x_sample.shape[0]
    sample_states = (state_conv_a, state_conv_b, state_mlstm_C, state_mlstm_n, state_mlstm_m, state_conv_c,
                     state_rglru_h)
    xp = x_prompt
    xs = x_sample.reshape(Bs, D_MODEL)
    new_p, new_s, mem_ops = [], [], []
    C_s = None
    lw = _mixer_weights(p)
    for l in range(depth):
        xp, sa, sb, sC, sn, sm, sc, sh = _mix_prompt(xp, lw, l)
        new_p.append((sa, sb, sC, sn, sm[:, :, 0], sc, sh.reshape(Bp, W_C)))
        small, mem, post_in = _step_pre(xs, tuple(s for i, s in enumerate(sample_states) if i != 2), lw, l)
        new_s.append(small)
        mem_ops.append(mem)
        if l < depth - 1:
            qc = _step_read(state_mlstm_C, l, mem[0])
        else:
            C_s, qc_all = _step_state(state_mlstm_C, *(jnp.stack([m[i] for m in mem_ops]) for i in range(4)))
            qc = qc_all[l]
        xs = _step_post(xs, post_in, qc, lw, l)
        g = norm_ffn_g[l].reshape(1, -1)
        final_g = norm_final_g.reshape(1, -1) if l == depth - 1 else None
        j = l // 2
        if l % 2 == 0:
            wg, wu, wd = (w[j:j + 1].astype(bf16) for w in (w_gate, w_up, w_down))
            router = None
        else:
            wg, wu, wd = we_gate[j].astype(bf16), we_up[j].astype(bf16), we_down[j].astype(bf16)
            router = (jnp.pad(w_router[j], ((0, 0), (0, LANES - N_EXPERTS))),
                      jnp.pad(b_router[j], (0, LANES - N_EXPERTS)).reshape(1, -1))
        xp2 = xp.reshape(Bp * S, D_MODEL)
        if router is None:
            xp2 = _ffn(xp2, g, wg, wu, wd, None, final_g)
        else:
            xp2 = _moe_prompt(xp2, g, wg, wu, wd, router[0], router[1], final_g)
        xp = xp2.reshape(Bp, S, D_MODEL)
        xs = _ffn(xs, g, wg, wu, wd, router, final_g)
    stack = lambda states, i: jnp.stack([st[i] for st in states])
    out_p = tuple(stack(new_p, i) for i in range(7))
    sa_s, sb_s, n_s, m_s, sc_s, h_s = (stack(new_s, i) for i in range(6))
    sa_s, sb_s, sc_s = (s.transpose(0, 2, 1, 3) for s in (sa_s, sb_s, sc_s))
    return (xp, xs.reshape(Bs, 1, D_MODEL)) + out_p + (sa_s, sb_s, C_s, n_s, m_s, sc_s, h_s)
```

```python
import functools

import jax
import jax.numpy as jnp
from jax import lax
from jax.experimental import pallas as pl
from jax.experimental.pallas import tpu as pltpu

f32 = jnp.float32
bf16 = jnp.bfloat16

D_MODEL = 1024
W_A, W_B, W_C = 256, 512, 256
G_A = 4
CONV_A, CONV_B, CONV_C = 31, 4, 4
H_B, DH_B = 4, 128
H_C, DH_C = 4, 64
RG_C = 8.0
D_IN = 2048
N_EXPERTS = 8
NORM_EPS = 1e-6
LN_EPS = 1e-5

LANES = 128
SUBLANES = 8
MXU_WIDTH = 256
VMEM_LIMIT = 56 * 1024 * 1024

T_MIX = 256
MIX_G = 4
L_CHUNK = 128
CONV_ROWS = 128
A_HIST = 32
S_HIST = 8
TM_FFN = 512
STEP_BB = 32


def _dot(a, b):
    return jnp.dot(a, b, preferred_element_type=f32)


def _dot_nt(a, b):
    return lax.dot_general(a, b, (((1,), (1,)), ((), ())), preferred_element_type=f32)


def _dot_tn(a, b):
    return lax.dot_general(a, b, (((0,), (0,)), ((), ())), preferred_element_type=f32)


def _split_terms(x, terms):
    out = []
    r = x
    for _ in range(terms - 1):
        p = r.astype(bf16)
        out.append(p)
        r = r - p.astype(f32)
    out.append(r.astype(bf16))
    return out


def _dot_split_lhs(x, m, terms):
    acc = None
    for p in _split_terms(x, terms):
        d = _dot(p, m)
        acc = d if acc is None else acc + d
    return acc


def _dot_split_rhs(m, x, terms):
    acc = None
    for p in _split_terms(x, terms):
        d = _dot(m, p)
        acc = d if acc is None else acc + d
    return acc


def _sigmoid(x):
    return 0.5 * (1.0 + jnp.tanh(0.5 * x))


def _rmsnorm(x, g):
    return x * lax.rsqrt(jnp.mean(x * x, axis=-1, keepdims=True) + NORM_EPS) * g


def _group_mask_a():
    r = lax.broadcasted_iota(jnp.int32, (W_A, W_A), 0) // (W_A // G_A)
    c = lax.broadcasted_iota(jnp.int32, (W_A, W_A), 1) // (W_A // G_A)
    return jnp.where(r == c, 1.0, 0.0).astype(bf16)


def _group_layernorm_silu(a, gm, g, b):
    inv = 1.0 / (W_A // G_A)
    mu = _dot_split_lhs(a, gm, 2) * inv
    d = a - mu
    var = _dot((d * d).astype(bf16), gm) * inv
    y = d * lax.rsqrt(var + LN_EPS) * g + b
    return y * _sigmoid(y)


def _head_layernorm(h, g):
    mu = jnp.mean(h, axis=-1, keepdims=True)
    d = h - mu
    var = jnp.mean(d * d, axis=-1, keepdims=True)
    return d * lax.rsqrt(var + LN_EPS) * g


def _rglru_gates(xcv, wrg, brg, lam):
    gates = _dot(xcv.astype(bf16), wrg) + brg
    r = _sigmoid(gates[:, :W_C])
    i = _sigmoid(gates[:, W_C:])
    log_a = RG_C * r * jax.nn.log_sigmoid(lam)
    a = jnp.exp(log_a)
    t = jnp.tanh(log_a)
    one_minus_a2 = -2.0 * t / (1.0 - t)
    u = jnp.sqrt(one_minus_a2) * i * xcv
    return a, u


N_MIX_WEIGHTS = 21


def _mix_prompt_body(*refs):
    x_ref, weights = refs[0], refs[1:1 + N_MIX_WEIGHTS]
    rest = refs[1 + N_MIX_WEIGHTS:]
    tiles = [_mix_tile(x_ref.at[g], *weights, *[r.at[g] for r in rest]) for g in range(MIX_G)]
    done = object()
    while tiles:
        tiles = [t for t in tiles if next(t, done) is not done]


def _mix_tile(x_ref, g_ref, win_ref, caw_ref, cab_ref, lag_ref, lab_ref, cbw_ref, cbb_ref,
              wq_ref, wk_ref, wv_ref, wg_ref, bg_ref, gn_ref, skip_ref, ccw_ref, ccb_ref,
              wrg_ref, brg_ref, lam_ref, wout_ref,
              x1_ref, sa_ref, sb_ref, sC_ref, sn_ref, sm_ref, sc_ref, sh_ref,
              ahist, bhist, chist, conv_scr, C_scr, n_scr, m_scr, h_scr, qkv_scr, mix_scr):
    T, L = T_MIX, L_CHUNK
    s = pl.program_id(1)
    last = pl.num_programs(1) - 1

    @pl.when(s == 0)
    def _():
        ahist[0:A_HIST, :] = jnp.zeros((A_HIST, W_A), f32)
        bhist[0:S_HIST, :] = jnp.zeros((S_HIST, W_B), f32)
        chist[0:S_HIST, :] = jnp.zeros((S_HIST, W_C), f32)
        C_scr[...] = jnp.zeros(C_scr.shape, f32)
        n_scr[...] = jnp.zeros(n_scr.shape, f32)
        m_scr[...] = jnp.zeros(m_scr.shape, f32)
        h_scr[...] = jnp.zeros(h_scr.shape, f32)

    yield
    x = x_ref[...]
    hn = _rmsnorm(x, g_ref[...])
    hn_bf = hn.astype(bf16)
    yield
    ua = _dot(hn_bf, win_ref[:, 0:2 * W_A])
    ub = _dot(hn_bf, win_ref[:, 2 * W_A:2 * W_A + 2 * W_B])
    uc = _dot(hn_bf, win_ref[:, 2 * W_A + 2 * W_B:])
    xa, ga = ua[:, 0:W_A], ua[:, W_A:]
    xb, zb = ub[:, 0:W_B], ub[:, W_B:]
    xc, gc = uc[:, 0:W_C], uc[:, W_C:]
    yield

    bhist[pl.ds(S_HIST, T), :] = xb
    cbw = cbw_ref[...]
    cb = cbb_ref[...] + cbw[3:4, :] * xb
    for j in range(CONV_B - 1):
        cb = cb + cbw[j:j + 1, :] * bhist[pl.ds(S_HIST - (CONV_B - 1) + j, T), :]
    cb = cb * _sigmoid(cb)
    cb_bf = cb.astype(bf16)
    xb_bf = xb.astype(bf16)
    yield
    qs, ks, vs = [], [], []
    for h in range(H_B):
        hs = slice(h * DH_B, (h + 1) * DH_B)
        q = _dot(cb_bf[:, hs], wq_ref[h])
        k = _dot(cb_bf[:, hs], wk_ref[h]) * (DH_B ** -0.5)
        v = _dot(xb_bf[:, hs], wv_ref[h])
        qs.append(q)
        ks.append(k)
        vs.append(v)
        qkv_scr[:, (3 * h) * DH_B:(3 * h + 1) * DH_B] = q.astype(bf16)
        qkv_scr[:, (3 * h + 1) * DH_B:(3 * h + 2) * DH_B] = k.astype(bf16)
        qkv_scr[:, (3 * h + 2) * DH_B:(3 * h + 3) * DH_B] = v.astype(bf16)
    yield
    gates = _dot(qkv_scr[...], wg_ref[...]) + bg_ref[...]
    ig = gates[:, 0:LANES]
    lf = jax.nn.log_sigmoid(gates[:, LANES:2 * LANES])
    yield
    ri = lax.broadcasted_iota(jnp.int32, (L, L), 0)
    ci = lax.broadcasted_iota(jnp.int32, (L, L), 1)
    causal = ri >= ci
    tri = jnp.where(causal, 1.0, 0.0).astype(bf16)
    gn = gn_ref[...]
    skip = skip_ref[...]
    heads = range(H_B)
    stack = lambda parts: jnp.concatenate(parts, axis=0)
    gn_st = stack([jnp.broadcast_to(gn[:, h * DH_B:(h + 1) * DH_B], (L, DH_B)) for h in heads])
    ones_blk = jnp.ones((L, DH_B), bf16)
    C_old = [C_scr[h] for h in heads]
    n_old = [n_scr[h:h + 1, :] for h in heads]
    m_old = [m_scr[h:h + 1, 0:1] for h in heads]
    for c in range(T // L):
        rows = slice(c * L, (c + 1) * L)
        b_all = _dot_split_rhs(tri, lf[rows], 2)
        c_all = ig[rows] - b_all
        c_all_t = c_all.T
        yield
        q_bf = [qs[h][rows].astype(bf16) for h in heads]
        k_bf = [ks[h][rows].astype(bf16) for h in heads]
        v_bf = [vs[h][rows].astype(bf16) for h in heads]
        m_prev = stack([jnp.broadcast_to(m_old[h], (L, 1)) for h in heads])
        b_col = stack([b_all[:, h:h + 1] for h in heads])
        c_col = stack([c_all[:, h:h + 1] for h in heads])
        dm = stack([jnp.where(causal, c_all_t[h:h + 1, :], -jnp.inf) for h in heads])
        mx = jnp.maximum(m_prev, jnp.max(dm, axis=1, keepdims=True))
        yield
        sc = stack([_dot_nt(q_bf[h], k_bf[h]) for h in heads]) * jnp.exp(dm - mx)
        sc_bf = sc.astype(bf16)
        yield
        w_inter = jnp.exp(m_prev - mx)
        sv_aug = [_dot(sc_bf[h * L:(h + 1) * L], jnp.concatenate([v_bf[h], ones_blk], axis=1)) for h in heads]
        sv = stack([a[:, 0:DH_B] for a in sv_aug])
        s_sum = stack([a[:, DH_B:DH_B + 1] for a in sv_aug])
        qC = stack([_dot(q_bf[h], C_old[h].astype(bf16)) for h in heads])
        qn = stack([_dot_nt(q_bf[h], jnp.broadcast_to(n_old[h], (SUBLANES, DH_B)).astype(bf16))[:, 0:1]
                    for h in heads])
        yield
        num = sv + w_inter * qC
        den = s_sum + w_inter * qn
        hb = num / jnp.maximum(jnp.abs(den), jnp.exp(-(b_col + mx)))
        hb = _head_layernorm(hb, gn_st)
        yield
        mx_last = [mx[(h + 1) * L - 1:(h + 1) * L, :] for h in heads]
        g_in = jnp.exp(c_col - stack([jnp.broadcast_to(mx_last[h], (L, 1)) for h in heads]))
        C_new, n_new, m_new = [], [], []
        for h in heads:
            hs = slice(h * DH_B, (h + 1) * DH_B)
            hr = slice(h * L, (h + 1) * L)
            b_out = _sigmoid(zb[rows, hs]) * (hb[hr] + skip[:, hs] * cb[rows, hs])
            mix_scr[rows, hs] = b_out.astype(bf16)
            gk = g_in[hr] * ks[h][rows]
            decay = jnp.exp(m_old[h] - mx_last[h])
            C_new.append(decay * C_old[h] + _dot_tn(gk.astype(bf16), v_bf[h]))
            n_new.append(decay * n_old[h] + jnp.sum(gk, axis=0, keepdims=True))
            m_new.append(b_all[L - 1:L, h:h + 1] + mx_last[h])
        C_old, n_old, m_old = C_new, n_new, m_new
        yield
    for h in heads:
        C_scr[h] = C_old[h]
        n_scr[h:h + 1, :] = n_old[h]
        m_scr[h:h + 1, :] = jnp.broadcast_to(m_old[h], (1, LANES))

    ahist[pl.ds(A_HIST, T), :] = xa * _sigmoid(ga)
    yield
    caw = caw_ref[...]
    cab = cab_ref[...]
    first = A_HIST - (CONV_A - 1)

    for base in range(0, T, CONV_ROWS):
        for lo in range(0, W_A, LANES):
            blk = ahist[pl.ds(base, CONV_ROWS + A_HIST), lo:lo + LANES]
            acc = jnp.broadcast_to(cab[:, lo:lo + LANES], (CONV_ROWS, LANES))
            for rr in range(SUBLANES):
                n = CONV_ROWS if rr == 0 else CONV_ROWS + SUBLANES
                z = None
                for q in range(A_HIST // SUBLANES + 1):
                    j = SUBLANES * q + rr - first
                    if 0 <= j < CONV_A:
                        term = caw[j:j + 1, lo:lo + LANES] * blk[SUBLANES * q:SUBLANES * q + n, :]
                        z = term if z is None else z + term
                acc = acc + (z if rr == 0 else z[rr:rr + CONV_ROWS, :])
            conv_scr[pl.ds(base, CONV_ROWS), lo:lo + LANES] = acc
            yield
    a_out = _group_layernorm_silu(conv_scr[...], _group_mask_a(), lag_ref[...], lab_ref[...])
    y = _dot(a_out.astype(bf16), wout_ref[0:W_A, :])
    yield

    chist[pl.ds(S_HIST, T), :] = xc
    ccw = ccw_ref[...]
    xcv = ccb_ref[...] + ccw[3:4, :] * xc
    for j in range(CONV_C - 1):
        xcv = xcv + ccw[j:j + 1, :] * chist[pl.ds(S_HIST - (CONV_C - 1) + j, T), :]
    yield
    a_t, u_t = _rglru_gates(xcv, wrg_ref[...], brg_ref[...], lam_ref[...])
    yield
    row = lax.broadcasted_iota(jnp.int32, (T, W_C), 0)
    d = 1
    while d < T:
        if d < SUBLANES:
            a_sh = jnp.where(row >= d, pltpu.roll(a_t, d, axis=0), 1.0)
            u_sh = jnp.where(row >= d, pltpu.roll(u_t, d, axis=0), 0.0)
        else:
            a_sh = jnp.concatenate([jnp.ones((d, W_C), f32), a_t[:T - d]], axis=0)
            u_sh = jnp.concatenate([jnp.zeros((d, W_C), f32), u_t[:T - d]], axis=0)
        u_t = a_t * u_sh + u_t
        a_t = a_t * a_sh
        d *= 2
    hc = a_t * h_scr[...] + u_t
    h_scr[...] = hc[T - 1:T, :]
    yield
    y = y + _dot((hc * jax.nn.gelu(gc)).astype(bf16), wout_ref[W_A + W_B:, :])

    yield
    x1_ref[...] = x + (y + _dot(mix_scr[...], wout_ref[W_A:W_A + W_B, :]))
    yield

    ahist[0:A_HIST, :] = ahist[pl.ds(T, A_HIST), :]
    bhist[0:S_HIST, :] = bhist[pl.ds(T, S_HIST), :]
    chist[0:S_HIST, :] = chist[pl.ds(T, S_HIST), :]

    @pl.when(s == last)
    def _():
        sa_ref[...] = ahist[pl.ds(first, CONV_A - 1), :]
        sb_ref[...] = bhist[pl.ds(S_HIST - (CONV_B - 1), CONV_B - 1), :]
        sc_ref[...] = chist[pl.ds(S_HIST - (CONV_C - 1), CONV_C - 1), :]
        sC_ref[...] = C_scr[...]
        sn_ref[...] = n_scr[0:H_B, :]
        sm_ref[...] = m_scr[0:H_B, :]
        sh_ref[...] = h_scr[...]


def _full(shape):
    nd = len(shape)
    return pl.BlockSpec(shape, lambda *_: (0,) * nd)


def _layer(w, l):
    nd = w.ndim - 1
    return pl.BlockSpec((None,) + w.shape[1:], lambda *_: (l,) + (0,) * nd)


def _mix_prompt(x, lw, l):
    B, S, _ = x.shape
    T = T_MIX
    weights = [lw[k] for k in ("norm_mix_g", "w_in", "conv_a_w", "conv_a_b", "ln_a_g", "ln_a_b", "conv_b_w", "conv_b_b",
                               "w_q", "w_k", "w_v", "w_gates", "b_gates", "gn_b_g", "skip_b", "conv_c_w", "conv_c_b",
                               "w_rg", "b_rg", "lam", "w_out")]
    out_shape = (
        jax.ShapeDtypeStruct((B, S, D_MODEL), f32),
        jax.ShapeDtypeStruct((B, CONV_A - 1, W_A), f32),
        jax.ShapeDtypeStruct((B, CONV_B - 1, W_B), f32),
        jax.ShapeDtypeStruct((B, H_B, DH_B, DH_B), f32),
        jax.ShapeDtypeStruct((B, H_B, DH_B), f32),
        jax.ShapeDtypeStruct((B, H_B, LANES), f32),
        jax.ShapeDtypeStruct((B, CONV_C - 1, W_C), f32),
        jax.ShapeDtypeStruct((B, 1, W_C), f32),
    )
    G = MIX_G
    per_b = lambda shp: pl.BlockSpec((G,) + shp, lambda b, s: (b,) + (0,) * len(shp))
    out_specs = (
        pl.BlockSpec((G, T, D_MODEL), lambda b, s: (b, s, 0)),
        per_b((CONV_A - 1, W_A)), per_b((CONV_B - 1, W_B)), per_b((H_B, DH_B, DH_B)), per_b((H_B, DH_B)),
        per_b((H_B, LANES)), per_b((CONV_C - 1, W_C)), per_b((1, W_C)),
    )
    scratch = [
        pltpu.VMEM((G, A_HIST + T, W_A), f32), pltpu.VMEM((G, S_HIST + T, W_B), f32),
        pltpu.VMEM((G, S_HIST + T, W_C), f32), pltpu.VMEM((G, T, W_A), f32),
        pltpu.VMEM((G, H_B, DH_B, DH_B), f32), pltpu.VMEM((G, SUBLANES, DH_B), f32),
        pltpu.VMEM((G, SUBLANES, LANES), f32), pltpu.VMEM((G, 1, W_C), f32),
        pltpu.VMEM((G, T, 3 * W_B), bf16), pltpu.VMEM((G, T, W_B), bf16),
    ]
    assert len(weights) == N_MIX_WEIGHTS and B % G == 0
    return pl.pallas_call(
        _mix_prompt_body,
        grid=(B // G, S // T),
        in_specs=[pl.BlockSpec((G, T, D_MODEL), lambda b, s: (b, s, 0))] + [_layer(w, l) for w in weights],
        out_specs=out_specs,
        out_shape=out_shape,
        scratch_shapes=scratch,
        compiler_params=pltpu.CompilerParams(dimension_semantics=("arbitrary", "arbitrary"),
                                             vmem_limit_bytes=VMEM_LIMIT),
        name="mix_prompt",
    )(x, *weights)


def _step_conv(st_ref, w, bias, x_new, st_out):
    k1 = st_ref.shape[0]
    y = bias + w[k1:k1 + 1, :] * x_new
    for j in range(k1):
        y = y + w[j:j + 1, :] * st_ref[j]
        if j > 0:
            st_out[j - 1] = st_ref[j]
    st_out[k1 - 1] = x_new
    return y


def _step_pre_body(x_ref, g_ref, win_ref, sa_ref, caw_ref, cab_ref, lag_ref, lab_ref, sb_ref, cbw_ref, cbb_ref,
                   wq_ref, wk_ref, wv_ref, wg_ref, bg_ref, n_ref, m_ref, sc_ref, ccw_ref, ccb_ref,
                   wrg_ref, brg_ref, lam_ref, h_ref, skip_ref,
                   sa_o, sb_o, sc_o, n_o, m_o, h_o, q_o, gk_o, v_o, dec_o, den_o, sv_o, zsig_o, skcb_o, ac_o,
                   qkv_scr):
    x = x_ref[...]
    hn = _rmsnorm(x, g_ref[...])
    u = _dot(hn.astype(bf16), win_ref[...])
    xa, ga = u[:, 0:256], u[:, 256:512]
    xb, zb = u[:, 512:1024], u[:, 1024:1536]
    xc, gc = u[:, 1536:1792], u[:, 1792:2048]

    a_new = xa * _sigmoid(ga)
    conv = _step_conv(sa_ref, caw_ref[...], cab_ref[...], a_new, sa_o)
    ac_o[:, 0:W_A] = _group_layernorm_silu(conv, _group_mask_a(), lag_ref[...], lab_ref[...])

    cb = _step_conv(sb_ref, cbw_ref[...], cbb_ref[...], xb, sb_o)
    cb = cb * _sigmoid(cb)
    cb_bf = cb.astype(bf16)
    xb_bf = xb.astype(bf16)
    qs, ks, vs = [], [], []
    for h in range(H_B):
        hs = slice(h * DH_B, (h + 1) * DH_B)
        q = _dot(cb_bf[:, hs], wq_ref[h])
        k = _dot(cb_bf[:, hs], wk_ref[h]) * (DH_B ** -0.5)
        v = _dot(xb_bf[:, hs], wv_ref[h])
        qs.append(q)
        ks.append(k)
        vs.append(v)
        qkv_scr[:, (3 * h) * DH_B:(3 * h + 1) * DH_B] = q.astype(bf16)
        qkv_scr[:, (3 * h + 1) * DH_B:(3 * h + 2) * DH_B] = k.astype(bf16)
        qkv_scr[:, (3 * h + 2) * DH_B:(3 * h + 3) * DH_B] = v.astype(bf16)
    gates = _dot(qkv_scr[...], wg_ref[...]) + bg_ref[...]
    ig = gates[:, 0:LANES]
    lf = jax.nn.log_sigmoid(gates[:, LANES:2 * LANES])
    m0 = m_ref[...]
    m_t = jnp.maximum(lf + m0, ig)
    g_in = jnp.exp(ig - m_t)
    decay = jnp.exp(lf + m0 - m_t)
    lane = lax.broadcasted_iota(jnp.int32, ig.shape, 1)
    qk = jnp.zeros(ig.shape, f32)
    qn = jnp.zeros(ig.shape, f32)
    for h in range(H_B):
        hs = slice(h * DH_B, (h + 1) * DH_B)
        n_h = n_ref[:, h, :]
        qk = jnp.where(lane == h, jnp.sum(qs[h] * ks[h], axis=-1, keepdims=True), qk)
        qn = jnp.where(lane == h, jnp.sum(qs[h] * n_h, axis=-1, keepdims=True), qn)
        gk = g_in[:, h:h + 1] * ks[h]
        n_o[:, h, :] = decay[:, h:h + 1] * n_h + gk
        q_o[:, hs] = qs[h]
        gk_o[:, hs] = gk
        v_o[:, hs] = vs[h]
    s_t = qk * g_in
    den = s_t + decay * qn
    den_o[...] = jnp.maximum(jnp.abs(den), jnp.exp(-m_t))
    dec_o[...] = decay
    m_o[...] = m_t
    for h in range(H_B):
        hs = slice(h * DH_B, (h + 1) * DH_B)
        sv_o[:, hs] = s_t[:, h:h + 1] * vs[h]
    zsig_o[...] = _sigmoid(zb)
    skcb_o[...] = skip_ref[...] * cb

    xcv = _step_conv(sc_ref, ccw_ref[...], ccb_ref[...], xc, sc_o)
    a_t, u_t = _rglru_gates(xcv, wrg_ref[...], brg_ref[...], lam_ref[...])
    hc = a_t * h_ref[...] + u_t
    h_o[...] = hc
    ac_o[:, W_A:] = hc * jax.nn.gelu(gc)


def _step_read_body(C_ref, qT_ref, qc_ref):
    for bb in range(STEP_BB):
        for h in range(H_B):
            hs = slice(h * DH_B, (h + 1) * DH_B)
            qc_ref[bb:bb + 1, hs] = jnp.sum(C_ref[bb, h] * qT_ref[h, :, bb:bb + 1], axis=0, keepdims=True)


def _step_state_body(C_ref, qT_ref, gkT_ref, v_ref, dec_ref, Cn_ref, qc_ref):
    for bb in range(STEP_BB):
        for h in range(H_B):
            hs = slice(h * DH_B, (h + 1) * DH_B)
            C = C_ref[bb, h]
            qc_ref[bb:bb + 1, hs] = jnp.sum(C * qT_ref[h, :, bb:bb + 1], axis=0, keepdims=True)
            Cn_ref[bb, h] = dec_ref[bb:bb + 1, h:h + 1] * C + gkT_ref[h, :, bb:bb + 1] * v_ref[bb:bb + 1, hs]


def _to_cols(t):
    return t.reshape(t.shape[0] // STEP_BB, STEP_BB, H_B, DH_B).transpose(0, 2, 3, 1)


def _step_read(C_all, layer, q):
    Bs = q.shape[0]
    return pl.pallas_call(
        _step_read_body,
        grid=(Bs // STEP_BB,),
        in_specs=[pl.BlockSpec((None, STEP_BB, H_B, DH_B, DH_B), lambda i: (layer, i, 0, 0, 0)),
                  pl.BlockSpec((None, H_B, DH_B, STEP_BB), lambda i: (i, 0, 0, 0))],
        out_specs=pl.BlockSpec((STEP_BB, W_B), lambda i: (i, 0)),
        out_shape=jax.ShapeDtypeStruct((Bs, W_B), f32),
        compiler_params=pltpu.CompilerParams(dimension_semantics=("arbitrary",), vmem_limit_bytes=VMEM_LIMIT),
        name="step_read",
    )(C_all, _to_cols(q))


def _step_state(C_all, q, gk, v, dec):
    depth, Bs = q.shape[0], q.shape[1]
    cols = lambda t: jnp.stack([_to_cols(t[l]) for l in range(depth)])
    blk5 = pl.BlockSpec((None, STEP_BB, H_B, DH_B, DH_B), lambda l, i: (l, i, 0, 0, 0))
    colspec = pl.BlockSpec((None, None, H_B, DH_B, STEP_BB), lambda l, i: (l, i, 0, 0, 0))
    return pl.pallas_call(
        _step_state_body,
        grid=(depth, Bs // STEP_BB),
        in_specs=[blk5, colspec, colspec,
                  pl.BlockSpec((None, STEP_BB, W_B), lambda l, i: (l, i, 0)),
                  pl.BlockSpec((None, STEP_BB, LANES), lambda l, i: (l, i, 0))],
        out_specs=(blk5, pl.BlockSpec((None, STEP_BB, W_B), lambda l, i: (l, i, 0))),
        out_shape=(jax.ShapeDtypeStruct(C_all.shape, f32), jax.ShapeDtypeStruct((depth, Bs, W_B), f32)),
        compiler_params=pltpu.CompilerParams(dimension_semantics=("arbitrary", "arbitrary"),
                                             vmem_limit_bytes=VMEM_LIMIT),
        name="step_state",
    )(C_all, cols(q), cols(gk), v, dec)


def _step_post_body(x_ref, ac_ref, zsig_ref, skcb_ref, sv_ref, qc_ref, dec_ref, den_ref, gn_ref, wout_ref,
                    x1_ref, mix_scr):
    gn = gn_ref[...]
    dec = dec_ref[...]
    den = den_ref[...]
    mix_scr[:, 0:W_A] = ac_ref[:, 0:W_A].astype(bf16)
    mix_scr[:, W_A + W_B:] = ac_ref[:, W_A:].astype(bf16)
    for h in range(H_B):
        hs = slice(h * DH_B, (h + 1) * DH_B)
        num = sv_ref[:, hs] + dec[:, h:h + 1] * qc_ref[:, hs]
        hb = _head_layernorm(num / den[:, h:h + 1], gn[:, hs])
        mix_scr[:, W_A + h * DH_B:W_A + (h + 1) * DH_B] = (zsig_ref[:, hs] * (hb + skcb_ref[:, hs])).astype(bf16)
    x1_ref[...] = x_ref[...] + _dot(mix_scr[...], wout_ref[...])


def _step_pre(x, st, lw, l):
    buf_a, buf_b, n0, m0, buf_c, hc0 = st
    buf_a, buf_b, buf_c = (b.transpose(0, 2, 1, 3) for b in (buf_a, buf_b, buf_c))
    Bs = x.shape[0]
    m_in = jnp.pad(m0, ((0, 0), (0, 0), (0, LANES - H_B)))
    sds = lambda *shape: jax.ShapeDtypeStruct(shape, f32)
    W = lambda k: (lw[k], True)
    L = lambda a: (a, True)
    A = lambda a: (a, False)
    pre_in = [A(x), W("norm_mix_g"), W("w_in"), L(buf_a), W("conv_a_w"), W("conv_a_b"), W("ln_a_g"), W("ln_a_b"),
              L(buf_b), W("conv_b_w"), W("conv_b_b"), W("w_q"), W("w_k"), W("w_v"), W("w_gates"), W("b_gates"),
              L(n0), L(m_in), L(buf_c), W("conv_c_w"), W("conv_c_b"), W("w_rg"), W("b_rg"), W("lam"), L(hc0),
              W("skip_b")]
    out_shape = (sds(*buf_a.shape[1:]), sds(*buf_b.shape[1:]), sds(*buf_c.shape[1:]), sds(*n0.shape[1:]),
                 sds(Bs, LANES), sds(Bs, W_C),
                 sds(Bs, W_B), sds(Bs, W_B), sds(Bs, W_B), sds(Bs, LANES), sds(Bs, LANES), sds(Bs, W_B),
                 sds(Bs, W_B), sds(Bs, W_B), sds(Bs, W_A + W_C))
    (sa_n, sb_n, sc_n, n_n, m_n, h_n, q, gk, v, dec, den, sv, zsig, skcb, ac) = pl.pallas_call(
        _step_pre_body,
        grid=(1,),
        in_specs=[_layer(a, l) if is_w else _full(a.shape) for a, is_w in pre_in],
        out_specs=[_full(o.shape) for o in out_shape],
        out_shape=out_shape,
        scratch_shapes=[pltpu.VMEM((Bs, 3 * W_B), bf16)],
        compiler_params=pltpu.CompilerParams(dimension_semantics=("arbitrary",), vmem_limit_bytes=VMEM_LIMIT),
        name="step_pre",
    )(*[a for a, _ in pre_in])
    new = (sa_n, sb_n, n_n, m_n[:, :H_B], sc_n, h_n)
    return new, (q, gk, v, dec), (ac, zsig, skcb, sv, dec, den)


def _step_post(x, post_in, qc, lw, l):
    ac, zsig, skcb, sv, dec, den = post_in
    acts = [x, ac, zsig, skcb, sv, qc, dec, den]
    return pl.pallas_call(
        _step_post_body,
        grid=(1,),
        in_specs=[_full(a.shape) for a in acts] + [_layer(lw["gn_b_g"], l), _layer(lw["w_out"], l)],
        out_specs=_full(x.shape),
        out_shape=jax.ShapeDtypeStruct(x.shape, f32),
        scratch_shapes=[pltpu.VMEM(x.shape, bf16)],
        compiler_params=pltpu.CompilerParams(dimension_semantics=("arbitrary",), vmem_limit_bytes=VMEM_LIMIT),
        name="step_post",
    )(*acts, lw["gn_b_g"], lw["w_out"])


def _swiglu(hb, wg_ref, wu_ref, wd_ref):
    f = wd_ref.shape[0]
    main = f // MXU_WIDTH * MXU_WIDTH
    glu = lambda gt, up: (gt * _sigmoid(gt) * up).astype(bf16)
    if main == f:
        act = glu(_dot(hb, wg_ref[...]), _dot(hb, wu_ref[...]))
    else:
        tail = _dot(hb, jnp.concatenate([wg_ref[:, main:], wu_ref[:, main:]], axis=1))
        act = jnp.concatenate([glu(_dot(hb, wg_ref[:, :main]), _dot(hb, wu_ref[:, :main])),
                               glu(tail[:, :f - main], tail[:, f - main:])], axis=1)
    return _dot(act, wd_ref[...])


def _ffn_body(*refs, moe, final_norm):
    it = iter(refs)
    x_ref, g_ref, wg_ref, wu_ref, wd_ref = next(it), next(it), next(it), next(it), next(it)
    wr_ref = next(it) if moe else None
    br_ref = next(it) if moe else None
    gf_ref = next(it) if final_norm else None
    o_ref, hn_scr, acc_scr = next(it), next(it), next(it)
    comb_scr = next(it) if moe else None
    j = pl.program_id(1)

    @pl.when(j == 0)
    def _():
        hn = _rmsnorm(x_ref[...], g_ref[...])
        hn_scr[...] = hn.astype(bf16)
        acc_scr[...] = jnp.zeros(acc_scr.shape, f32)
        if moe:
            lane, i1, i2, g1, g2 = _top2(_router_logits(hn, wr_ref[...], br_ref[...]))
            comb_scr[...] = jnp.where(lane == i1, g1, 0.0) + jnp.where(lane == i2, g2, 0.0)

    y = _swiglu(hn_scr[...], wg_ref, wu_ref, wd_ref)
    if moe:
        lane = lax.broadcasted_iota(jnp.int32, comb_scr.shape, 1)
        y = y * jnp.sum(jnp.where(lane == j, comb_scr[...], 0.0), axis=-1, keepdims=True)
    acc_scr[...] += y

    @pl.when(j == pl.num_programs(1) - 1)
    def _():
        out = x_ref[...] + acc_scr[...]
        if final_norm:
            out = _rmsnorm(out, gf_ref[...])
        o_ref[...] = out


def _ffn(x, g, wg, wu, wd, router=None, final_g=None):
    N = x.shape[0]
    E, F = wd.shape[0], wd.shape[1]
    tm = min(TM_FFN, N)
    moe = router is not None
    final_norm = final_g is not None
    ins = [x, g, wg, wu, wd]
    once = dict(pipeline_mode=pl.Buffered(1)) if E == 1 else {}
    in_specs = [pl.BlockSpec((tm, D_MODEL), lambda i, j: (i, 0)), _full(g.shape),
                pl.BlockSpec((None, D_MODEL, F), lambda i, j: (j, 0, 0), **once),
                pl.BlockSpec((None, D_MODEL, F), lambda i, j: (j, 0, 0), **once),
                pl.BlockSpec((None, F, D_MODEL), lambda i, j: (j, 0, 0), **once)]
    scratch = [pltpu.VMEM((tm, D_MODEL), bf16), pltpu.VMEM((tm, D_MODEL), f32)]
    if moe:
        ins += list(router)
        in_specs += [_full(router[0].shape), _full(router[1].shape)]
        scratch.append(pltpu.VMEM((tm, LANES), f32))
    if final_norm:
        ins.append(final_g)
        in_specs.append(_full(final_g.shape))
    return pl.pallas_call(
        functools.partial(_ffn_body, moe=moe, final_norm=final_norm),
        grid=(N // tm, E),
        in_specs=in_specs,
        out_specs=pl.BlockSpec((tm, D_MODEL), lambda i, j: (i, 0)),
        out_shape=jax.ShapeDtypeStruct((N, D_MODEL), f32),
        scratch_shapes=scratch,
        compiler_params=pltpu.CompilerParams(dimension_semantics=("arbitrary", "arbitrary"),
                                             vmem_limit_bytes=VMEM_LIMIT),
        name="ffn_moe" if moe else "ffn_dense",
    )(*ins)


TM_MOE = 512
SEG_ALIGN = 16
SEG_SIZES = (512, 256, 128, 64, 32, 16)
MOE_ROWS = 2 * TM_MOE + N_EXPERTS * SEG_ALIGN
META_COLS = 3 * N_EXPERTS


def _seg_dma(src, src_off, dst, dst_off, nrows, sem, wait):
    done = jnp.int32(0)
    for size in SEG_SIZES:
        take = (nrows & size) != 0

        @pl.when(take)
        def _(done=done, size=size):
            cp = pltpu.make_async_copy(src.at[pl.ds(pl.multiple_of(src_off + done, SEG_ALIGN), size)],
                                       dst.at[pl.ds(pl.multiple_of(dst_off + done, SEG_ALIGN), size)], sem)
            if wait:
                cp.wait()
            else:
                cp.start()

        done = done + jnp.where(take, size, 0)


def _top2(logits):
    lane = lax.broadcasted_iota(jnp.int32, logits.shape, 1).astype(f32)
    lg = jnp.where(lane < N_EXPERTS, logits, -jnp.inf)
    m1 = jnp.max(lg, axis=-1, keepdims=True)
    i1 = jnp.min(jnp.where(lg == m1, lane, float(LANES)), axis=-1, keepdims=True)
    lg2 = jnp.where(lane == i1, -jnp.inf, lg)
    m2 = jnp.max(lg2, axis=-1, keepdims=True)
    i2 = jnp.min(jnp.where(lg2 == m2, lane, float(LANES)), axis=-1, keepdims=True)
    e2 = jnp.exp(m2 - m1)
    return lane, i1, i2, 1.0 / (1.0 + e2), e2 / (1.0 + e2)


def _router_logits(hn, wr, br):
    h1, h2 = _split_terms(hn, 2)
    w1, w2 = _split_terms(wr, 2)
    return _dot(h1, w1) + (_dot(h1, w2) + _dot(h2, w1)) + br


def _expert_ranks(lane, i1, i2):
    TM = lane.shape[0]
    sel1 = lane == i1
    sel2 = lane == i2
    sel = jnp.where(sel1, 1.0, jnp.where(sel2, 1.0, 0.0))
    earlier = lax.broadcasted_iota(jnp.int32, (TM, TM), 0) > lax.broadcasted_iota(jnp.int32, (TM, TM), 1)
    rank = _dot(jnp.where(earlier, 1.0, 0.0).astype(bf16), sel.astype(bf16))
    return sel1, sel2, rank, rank[TM - 1:TM, :] + sel[TM - 1:TM, :]


def _seg_pad(c):
    return ((c + (SEG_ALIGN - 1)) // SEG_ALIGN) * SEG_ALIGN


def _moe_count_body(x_ref, g_ref, wr_ref, br_ref, route_ref, cnt_ref):
    i = pl.program_id(0)
    hn = _rmsnorm(x_ref[...], g_ref[...])
    lane, i1, i2, g1, g2 = _top2(_router_logits(hn, wr_ref[...], br_ref[...]))
    route_ref[...] = jnp.where(lane == 0, i1, jnp.where(lane == 1, i2, jnp.where(lane == 2, g1,
                                                                                 jnp.where(lane == 3, g2, 0.0))))
    cnt = jnp.sum(jnp.where(lane == i1, 1.0, jnp.where(lane == i2, 1.0, 0.0)), axis=0, keepdims=True)
    for e in range(N_EXPERTS):
        cnt_ref[i, e] = cnt[0, e].astype(jnp.int32)


def _moe_route_body(cnt_ref, x_ref, g_ref, route_ref, xs_hbm, gs_hbm, rinfo_ref, meta_ref, te_ref, nv_ref,
                    xperm, gperm, pos_smem, sem):
    TM, R = TM_MOE, MOE_ROWS
    i = pl.program_id(0)
    nt = pl.num_programs(0)
    n_tiles = te_ref.shape[0]

    @pl.when(i == 0)
    def _():
        base = jnp.int32(0)
        for e in range(N_EXPERTS):
            rows = lax.fori_loop(0, nt, lambda t, a, e=e: a + _seg_pad(cnt_ref[t, e]), jnp.int32(0))
            first_tile = base // TM
            tiles = (rows + (TM - 1)) // TM

            def mark(k, c, e=e, first_tile=first_tile):
                te_ref[first_tile + k] = e
                return c

            lax.fori_loop(0, tiles, mark, 0)
            pos_smem[e] = base
            base = base + tiles * TM
        nv = base // TM
        nv_ref[0] = nv

        def mark_rest(k, c):
            te_ref[k] = N_EXPERTS - 1
            return c

        lax.fori_loop(nv, n_tiles, mark_rest, 0)

    hn = _rmsnorm(x_ref[...], g_ref[...])
    route = route_ref[...]
    i1, i2, g1, g2 = route[:, 0:1], route[:, 1:2], route[:, 2:3], route[:, 3:4]
    lane = lax.broadcasted_iota(jnp.int32, route.shape, 1).astype(f32)
    sel1, sel2, rank, _ = _expert_ranks(lane, i1, i2)
    seg_rows, seg_off = [], []
    acc = jnp.int32(0)
    for e in range(N_EXPERTS):
        seg_rows.append(_seg_pad(cnt_ref[i, e]))
        seg_off.append(acc)
        acc = acc + seg_rows[e]
    lane1 = lax.broadcasted_iota(jnp.int32, (1, LANES), 1)
    offv = jnp.zeros((1, LANES), f32)
    for e in range(N_EXPERTS):
        offv = jnp.where(lane1 == e, seg_off[e].astype(f32), offv)
    dest = rank + offv
    d1 = jnp.sum(jnp.where(sel1, dest, 0.0), axis=-1, keepdims=True)
    d2 = jnp.sum(jnp.where(sel2, dest, 0.0), axis=-1, keepdims=True)
    rinfo = jnp.where(lane == 0, d1, jnp.where(lane == 1, d2, jnp.where(lane == 2, g1, jnp.where(lane == 3, g2, 0.0))))
    rinfo_ref[...] = rinfo
    rinfo_t = rinfo.T
    d1r, d2r, g1r, g2r = rinfo_t[0:1, :], rinfo_t[1:2, :], rinfo_t[2:3, :], rinfo_t[3:4, :]
    riota = lax.broadcasted_iota(jnp.int32, (R, TM), 0).astype(f32)
    hit1 = riota == d1r
    hit2 = riota == d2r
    perm = jnp.where(hit1, 1.0, jnp.where(hit2, 1.0, 0.0)).astype(bf16)
    xperm[i % 2] = _dot(perm, hn.astype(bf16)).astype(bf16)
    gcol = jnp.sum(jnp.where(hit1, g1r, 0.0) + jnp.where(hit2, g2r, 0.0), axis=1, keepdims=True)
    gperm[i % 2] = jnp.broadcast_to(gcol, (R, LANES))

    for e in range(N_EXPERTS):
        pos = pos_smem[e]
        meta_ref[i, e] = pos
        meta_ref[i, N_EXPERTS + e] = seg_off[e]
        meta_ref[i, 2 * N_EXPERTS + e] = seg_rows[e]
        pos_smem[e] = pos + seg_rows[e]

    def copies(tile, wait):
        for slot in range(2):
            @pl.when(tile % 2 == slot)
            def _(slot=slot):
                for e in range(N_EXPERTS):
                    src_off, dst_off, n = meta_ref[tile, N_EXPERTS + e], meta_ref[tile, e], meta_ref[tile, 2 * N_EXPERTS + e]
                    _seg_dma(xperm.at[slot], src_off, xs_hbm, dst_off, n, sem.at[2 * slot], wait)
                    _seg_dma(gperm.at[slot], src_off, gs_hbm, dst_off, n, sem.at[2 * slot + 1], wait)

    copies(i, False)
    pl.when(i > 0)(lambda: copies(i - 1, True))

    @pl.when(i == nt - 1)
    def _():
        copies(i, True)
        xz, gz = xperm.at[0], gperm.at[0]
        xz[0:TM, :] = jnp.zeros((TM, D_MODEL), bf16)
        gz[0:TM, :] = jnp.zeros((TM, LANES), f32)
        pads = []
        for e in range(N_EXPERTS):
            pos = pos_smem[e]
            pads.append((-pos) & (TM - 1))
            _seg_dma(xz, 0, xs_hbm, pos, pads[e], sem.at[0], False)
            _seg_dma(gz, 0, gs_hbm, pos, pads[e], sem.at[1], False)
        for e in range(N_EXPERTS):
            _seg_dma(xz, 0, xs_hbm, pos_smem[e], pads[e], sem.at[0], True)
            _seg_dma(gz, 0, gs_hbm, pos_smem[e], pads[e], sem.at[1], True)

        def zero_tile(k, c):
            row = pl.multiple_of(k * TM, TM)
            cx = pltpu.make_async_copy(xz.at[pl.ds(0, TM)], xs_hbm.at[pl.ds(row, TM)], sem.at[0])
            cg = pltpu.make_async_copy(gz.at[pl.ds(0, TM)], gs_hbm.at[pl.ds(row, TM)], sem.at[1])
            cx.start()
            cg.start()
            cx.wait()
            cg.wait()
            return c

        lax.fori_loop(nv_ref[0], n_tiles, zero_tile, 0)


def _moe_ffn_body(te_ref, nv_ref, xs_ref, gs_ref, wg_ref, wu_ref, wd_ref, ys_ref):
    i = pl.program_id(0)

    @pl.when(i < nv_ref[0])
    def _():
        ys_ref[...] = (_swiglu(xs_ref[...], wg_ref, wu_ref, wd_ref) * gs_ref[:, 0:1]).astype(bf16)

    @pl.when(i >= nv_ref[0])
    def _():
        ys_ref[...] = jnp.zeros(ys_ref.shape, bf16)


def _moe_combine_body(*refs, final_norm):
    it = iter(refs)
    meta_ref, x_ref, rinfo_ref, ys_hbm = next(it), next(it), next(it), next(it)
    gf_ref = next(it) if final_norm else None
    o_ref, yperm, sem = next(it), next(it), next(it)
    TM, R = TM_MOE, MOE_ROWS
    i = pl.program_id(0)
    nt = pl.num_programs(0)

    def fetch(tile, slot, wait):
        for e in range(N_EXPERTS):
            _seg_dma(ys_hbm, meta_ref[tile, e], yperm.at[slot], meta_ref[tile, N_EXPERTS + e],
                     meta_ref[tile, 2 * N_EXPERTS + e], sem.at[slot], wait)

    def both_slots(tile, wait):
        for slot in range(2):
            pl.when(tile % 2 == slot)(lambda slot=slot: fetch(tile, slot, wait))

    @pl.when(i == 0)
    def _():
        yperm[...] = jnp.zeros(yperm.shape, bf16)
        fetch(0, 0, False)

    @pl.when(i + 1 < nt)
    def _():
        both_slots(i + 1, False)

    rinfo = rinfo_ref[...]
    ciota = lax.broadcasted_iota(jnp.int32, (TM, R), 1).astype(f32)
    unperm = jnp.where(ciota == rinfo[:, 0:1], 1.0, jnp.where(ciota == rinfo[:, 1:2], 1.0, 0.0)).astype(bf16)
    both_slots(i, True)
    out = x_ref[...] + _dot(unperm, yperm[i % 2])
    if final_norm:
        out = _rmsnorm(out, gf_ref[...])
    o_ref[...] = out


def _moe_prompt(x, g, wg, wu, wd, wr, br, final_g):
    N = x.shape[0]
    TM = TM_MOE
    nt = N // TM
    F = wd.shape[1]
    n_tiles = -(-(2 * N + nt * N_EXPERTS * (SEG_ALIGN - 1) + N_EXPERTS * (TM - 1)) // TM)
    rows = n_tiles * TM
    smem = pl.BlockSpec(memory_space=pltpu.SMEM)
    arb = pltpu.CompilerParams(dimension_semantics=("arbitrary",), vmem_limit_bytes=VMEM_LIMIT)
    route, cnt = pl.pallas_call(
        _moe_count_body,
        grid=(nt,),
        in_specs=[pl.BlockSpec((TM, D_MODEL), lambda i: (i, 0)), _full(g.shape), _full(wr.shape), _full(br.shape)],
        out_specs=[pl.BlockSpec((TM, LANES), lambda i: (i, 0)), smem],
        out_shape=[jax.ShapeDtypeStruct((N, LANES), f32), jax.ShapeDtypeStruct((nt, N_EXPERTS), jnp.int32)],
        compiler_params=arb,
        name="moe_count",
    )(x, g, wr, br)

    xs, gs, rinfo, meta, te, nv = pl.pallas_call(
        _moe_route_body,
        grid_spec=pltpu.PrefetchScalarGridSpec(
            num_scalar_prefetch=1,
            grid=(nt,),
            in_specs=[pl.BlockSpec((TM, D_MODEL), lambda i, c: (i, 0)), pl.BlockSpec(g.shape, lambda i, c: (0, 0)),
                      pl.BlockSpec((TM, LANES), lambda i, c: (i, 0))],
            out_specs=[pl.BlockSpec(memory_space=pl.ANY), pl.BlockSpec(memory_space=pl.ANY),
                       pl.BlockSpec((TM, LANES), lambda i, c: (i, 0)), smem, smem, smem],
            scratch_shapes=[pltpu.VMEM((2, MOE_ROWS, D_MODEL), bf16), pltpu.VMEM((2, MOE_ROWS, LANES), f32),
                            pltpu.SMEM((N_EXPERTS,), jnp.int32), pltpu.SemaphoreType.DMA((4,))],
        ),
        out_shape=[jax.ShapeDtypeStruct((rows, D_MODEL), bf16), jax.ShapeDtypeStruct((rows, LANES), f32),
                   jax.ShapeDtypeStruct((N, LANES), f32), jax.ShapeDtypeStruct((nt, META_COLS), jnp.int32),
                   jax.ShapeDtypeStruct((n_tiles,), jnp.int32), jax.ShapeDtypeStruct((1,), jnp.int32)],
        compiler_params=arb,
        name="moe_route",
    )(cnt, x, g, route)

    ys = pl.pallas_call(
        _moe_ffn_body,
        grid_spec=pltpu.PrefetchScalarGridSpec(
            num_scalar_prefetch=2,
            grid=(n_tiles,),
            in_specs=[pl.BlockSpec((TM, D_MODEL), lambda i, te, nv: (i, 0)),
                      pl.BlockSpec((TM, LANES), lambda i, te, nv: (i, 0)),
                      pl.BlockSpec((None, D_MODEL, F), lambda i, te, nv: (te[i], 0, 0)),
                      pl.BlockSpec((None, D_MODEL, F), lambda i, te, nv: (te[i], 0, 0)),
                      pl.BlockSpec((None, F, D_MODEL), lambda i, te, nv: (te[i], 0, 0))],
            out_specs=pl.BlockSpec((TM, D_MODEL), lambda i, te, nv: (i, 0)),
        ),
        out_shape=jax.ShapeDtypeStruct((rows, D_MODEL), bf16),
        compiler_params=arb,
        name="moe_ffn",
    )(te, nv, xs, gs, wg, wu, wd)

    final_norm = final_g is not None
    ins = [meta, x, rinfo, ys] + ([final_g] if final_norm else [])
    in_specs = [pl.BlockSpec((TM, D_MODEL), lambda i, m: (i, 0)), pl.BlockSpec((TM, LANES), lambda i, m: (i, 0)),
                pl.BlockSpec(memory_space=pl.ANY)]
    if final_norm:
        in_specs.append(pl.BlockSpec(final_g.shape, lambda i, m: (0, 0)))
    return pl.pallas_call(
        functools.partial(_moe_combine_body, final_norm=final_norm),
        grid_spec=pltpu.PrefetchScalarGridSpec(
            num_scalar_prefetch=1,
            grid=(nt,),
            in_specs=in_specs,
            out_specs=pl.BlockSpec((TM, D_MODEL), lambda i, m: (i, 0)),
            scratch_shapes=[pltpu.VMEM((2, MOE_ROWS, D_MODEL), bf16), pltpu.SemaphoreType.DMA((2,))],
        ),
        out_shape=jax.ShapeDtypeStruct((N, D_MODEL), f32),
        compiler_params=pltpu.CompilerParams(dimension_semantics=("arbitrary",), vmem_limit_bytes=VMEM_LIMIT),
        name="moe_combine",
    )(*ins)


def _block_diag(w):
    d, h, di, do = w.shape
    return jnp.einsum("lhij,hg->lhigj", w, jnp.eye(h, dtype=w.dtype)).reshape(d, h * di, h * do)


def _mixer_weights(p):
    row = lambda v: v[:, None, :]
    lanes = lambda w: jnp.pad(w, [(0, 0)] * (w.ndim - 1) + [(0, LANES - w.shape[-1])])
    return dict(
        norm_mix_g=row(p["norm_mix_g"]), w_in=p["w_in"].astype(bf16),
        conv_a_w=p["conv_a_w"], conv_a_b=row(p["conv_a_b"]), ln_a_g=row(p["ln_a_g"]), ln_a_b=row(p["ln_a_b"]),
        conv_b_w=p["conv_b_w"], conv_b_b=row(p["conv_b_b"]),
        w_q=p["w_q"].astype(bf16), w_k=p["w_k"].astype(bf16), w_v=p["w_v"].astype(bf16),
        w_gates=jnp.concatenate([lanes(p["w_ig"]), lanes(p["w_fg"])], axis=-1).astype(bf16),
        b_gates=row(jnp.concatenate([lanes(p["b_ig"]), lanes(p["b_fg"])], axis=-1)),
        gn_b_g=row(p["gn_b_g"]), skip_b=row(p["skip_b"]),
        conv_c_w=p["conv_c_w"], conv_c_b=row(p["conv_c_b"]),
        w_rg=jnp.concatenate([_block_diag(p["w_ra"]), _block_diag(p["w_ix"])], axis=-1).astype(bf16),
        b_rg=row(jnp.concatenate([p["b_ra"], p["b_ix"]], axis=-1)), lam=row(p["lam"]),
        w_out=p["w_out"].astype(bf16),
    )


def kernel(x_prompt, x_sample, state_conv_a, state_conv_b, state_mlstm_C, state_mlstm_n, state_mlstm_m, state_conv_c, state_rglru_h, norm_mix_g, w_in, conv_a_w, conv_a_b, ln_a_g, ln_a_b, conv_b_w, conv_b_b, w_q, w_k, w_v, w_ig, b_ig, w_fg, b_fg, gn_b_g, skip_b, conv_c_w, conv_c_b, w_ra, b_ra, w_ix, b_ix, lam, w_out, norm_ffn_g, w_gate, w_up, w_down, w_router, b_router, we_gate, we_up, we_down, norm_final_g):
    p = dict(norm_mix_g=norm_mix_g, w_in=w_in, conv_a_w=conv_a_w, conv_a_b=conv_a_b, ln_a_g=ln_a_g, ln_a_b=ln_a_b,
             conv_b_w=conv_b_w, conv_b_b=conv_b_b, w_q=w_q, w_k=w_k, w_v=w_v, w_ig=w_ig, b_ig=b_ig, w_fg=w_fg,
             b_fg=b_fg, gn_b_g=gn_b_g, skip_b=skip_b, conv_c_w=conv_c_w, conv_c_b=conv_c_b, w_ra=w_ra, b_ra=b_ra,
             w_ix=w_ix, b_ix=b_ix, lam=lam, w_out=w_out)
    depth = w_in.shape[0]
    Bp, S, _ = x_prompt.shape
    Bs = x_sample.shape[0]
    sample_states = (state_conv_a, state_conv_b, state_mlstm_C, state_mlstm_n, state_mlstm_m, state_conv_c,
                     state_rglru_h)
    xp = x_prompt
    xs = x_sample.reshape(Bs, D_MODEL)
    new_p, new_s, mem_ops = [], [], []
    C_s = None
    lw = _mixer_weights(p)
    for l in range(depth):
        xp, sa, sb, sC, sn, sm, sc, sh = _mix_prompt(xp, lw, l)
        new_p.append((sa, sb, sC, sn, sm[:, :, 0], sc, sh.reshape(Bp, W_C)))
        small, mem, post_in = _step_pre(xs, tuple(s for i, s in enumerate(sample_states) if i != 2), lw, l)
        new_s.append(small)
        mem_ops.append(mem)
        if l < depth - 1:
            qc = _step_read(state_mlstm_C, l, mem[0])
        else:
            C_s, qc_all = _step_state(state_mlstm_C, *(jnp.stack([m[i] for m in mem_ops]) for i in range(4)))
            qc = qc_all[l]
        xs = _step_post(xs, post_in, qc, lw, l)
        g = norm_ffn_g[l].reshape(1, -1)
        final_g = norm_final_g.reshape(1, -1) if l == depth - 1 else None
        j = l // 2
        if l % 2 == 0:
            wg, wu, wd = (w[j:j + 1].astype(bf16) for w in (w_gate, w_up, w_down))
            router = None
        else:
            wg, wu, wd = we_gate[j].astype(bf16), we_up[j].astype(bf16), we_down[j].astype(bf16)
            router = (jnp.pad(w_router[j], ((0, 0), (0, LANES - N_EXPERTS))),
                      jnp.pad(b_router[j], (0, LANES - N_EXPERTS)).reshape(1, -1))
        xp2 = xp.reshape(Bp * S, D_MODEL)
        if router is None:
            xp2 = _ffn(xp2, g, wg, wu, wd, None, final_g)
        else:
            xp2 = _moe_prompt(xp2, g, wg, wu, wd, router[0], router[1], final_g)
        xp = xp2.reshape(Bp, S, D_MODEL)
        xs = _ffn(xs, g, wg, wu, wd, router, final_g)
    stack = lambda states, i: jnp.stack([st[i] for st in states])
    out_p = tuple(stack(new_p, i) for i in range(7))
    sa_s, sb_s, n_s, m_s, sc_s, h_s = (stack(new_s, i) for i in range(6))
    sa_s, sb_s, sc_s = (s.transpose(0, 2, 1, 3) for s in (sa_s, sb_s, sc_s))
    return (xp, xs.reshape(Bs, 1, D_MODEL)) + out_p + (sa_s, sb_s, C_s, n_s, m_s, sc_s, h_s)
```

```python
import functools

import jax
import jax.numpy as jnp
from jax import lax
from jax.experimental import pallas as pl
from jax.experimental.pallas import tpu as pltpu

f32 = jnp.float32
bf16 = jnp.bfloat16

D_MODEL = 1024
W_A, W_B, W_C = 256, 512, 256
G_A = 4
CONV_A, CONV_B, CONV_C = 31, 4, 4
H_B, DH_B = 4, 128
RG_C = 8.0
N_EXPERTS = 8
NORM_EPS = 1e-6
LN_EPS = 1e-5

LANES = 128
SUBLANES = 8
MXU_WIDTH = 256
VMEM_LIMIT = 56 * 1024 * 1024

T_MIX = 256
MIX_G = 4
L_CHUNK = 128
CONV_ROWS = 128
A_HIST = 32
S_HIST = 8
TM_FFN = 512
STEP_BB = 32


def _dot(a, b):
    return jnp.dot(a, b, preferred_element_type=f32)


def _dot_nt(a, b):
    return lax.dot_general(a, b, (((1,), (1,)), ((), ())), preferred_element_type=f32)


def _dot_tn(a, b):
    return lax.dot_general(a, b, (((0,), (0,)), ((), ())), preferred_element_type=f32)


def _split_terms(x, terms):
    out = []
    r = x
    for _ in range(terms - 1):
        p = r.astype(bf16)
        out.append(p)
        r = r - p.astype(f32)
    out.append(r.astype(bf16))
    return out


def _dot_split_lhs(x, m, terms):
    acc = None
    for p in _split_terms(x, terms):
        d = _dot(p, m)
        acc = d if acc is None else acc + d
    return acc


def _dot_split_rhs(m, x, terms):
    acc = None
    for p in _split_terms(x, terms):
        d = _dot(m, p)
        acc = d if acc is None else acc + d
    return acc


def _sigmoid(x):
    return 0.5 * (1.0 + jnp.tanh(0.5 * x))


def _rmsnorm(x, g):
    return x * lax.rsqrt(jnp.mean(x * x, axis=-1, keepdims=True) + NORM_EPS) * g


def _group_mask_a():
    r = lax.broadcasted_iota(jnp.int32, (W_A, W_A), 0) // (W_A // G_A)
    c = lax.broadcasted_iota(jnp.int32, (W_A, W_A), 1) // (W_A // G_A)
    return jnp.where(r == c, 1.0, 0.0).astype(bf16)


def _group_layernorm_silu(a, gm, g, b):
    inv = 1.0 / (W_A // G_A)
    mu = _dot_split_lhs(a, gm, 2) * inv
    d = a - mu
    var = _dot((d * d).astype(bf16), gm) * inv
    y = d * lax.rsqrt(var + LN_EPS) * g + b
    return y * _sigmoid(y)


def _head_layernorm(h, g):
    mu = jnp.mean(h, axis=-1, keepdims=True)
    d = h - mu
    var = jnp.mean(d * d, axis=-1, keepdims=True)
    return d * lax.rsqrt(var + LN_EPS) * g


def _rglru_gates(xcv, wrg, brg, lam):
    gates = _dot(xcv.astype(bf16), wrg) + brg
    r = _sigmoid(gates[:, :W_C])
    i = _sigmoid(gates[:, W_C:])
    log_a = RG_C * r * jax.nn.log_sigmoid(lam)
    a = jnp.exp(log_a)
    t = jnp.tanh(log_a)
    one_minus_a2 = -2.0 * t / (1.0 - t)
    u = jnp.sqrt(one_minus_a2) * i * xcv
    return a, u


N_MIX_WEIGHTS = 21


def _mix_prompt_body(*refs):
    x_ref, weights = refs[0], refs[1:1 + N_MIX_WEIGHTS]
    rest = refs[1 + N_MIX_WEIGHTS:]
    tiles = [_mix_tile(x_ref.at[g], *weights, *[r.at[g] for r in rest]) for g in range(MIX_G)]
    done = object()
    while tiles:
        tiles = [t for t in tiles if next(t, done) is not done]


def _mix_tile(x_ref, g_ref, win_ref, caw_ref, cab_ref, lag_ref, lab_ref, cbw_ref, cbb_ref,
              wq_ref, wk_ref, wv_ref, wg_ref, bg_ref, gn_ref, skip_ref, ccw_ref, ccb_ref,
              wrg_ref, brg_ref, lam_ref, wout_ref,
              x1_ref, sa_ref, sb_ref, sC_ref, sn_ref, sm_ref, sc_ref, sh_ref,
              ahist, bhist, chist, conv_scr, C_scr, n_scr, m_scr, h_scr, qkv_scr, mix_scr):
    T, L = T_MIX, L_CHUNK
    s = pl.program_id(1)
    last = pl.num_programs(1) - 1

    @pl.when(s == 0)
    def _():
        ahist[0:A_HIST, :] = jnp.zeros((A_HIST, W_A), f32)
        bhist[0:S_HIST, :] = jnp.zeros((S_HIST, W_B), f32)
        chist[0:S_HIST, :] = jnp.zeros((S_HIST, W_C), f32)
        C_scr[...] = jnp.zeros(C_scr.shape, f32)
        n_scr[...] = jnp.zeros(n_scr.shape, f32)
        m_scr[...] = jnp.zeros(m_scr.shape, f32)
        h_scr[...] = jnp.zeros(h_scr.shape, f32)

    yield
    x = x_ref[...]
    hn = _rmsnorm(x, g_ref[...])
    hn_bf = hn.astype(bf16)
    yield
    ua = _dot(hn_bf, win_ref[:, 0:2 * W_A])
    ub = _dot(hn_bf, win_ref[:, 2 * W_A:2 * W_A + 2 * W_B])
    uc = _dot(hn_bf, win_ref[:, 2 * W_A + 2 * W_B:])
    xa, ga = ua[:, 0:W_A], ua[:, W_A:]
    xb, zb = ub[:, 0:W_B], ub[:, W_B:]
    xc, gc = uc[:, 0:W_C], uc[:, W_C:]
    yield

    bhist[pl.ds(S_HIST, T), :] = xb
    cbw = cbw_ref[...]
    cb = cbb_ref[...] + cbw[3:4, :] * xb
    for j in range(CONV_B - 1):
        cb = cb + cbw[j:j + 1, :] * bhist[pl.ds(S_HIST - (CONV_B - 1) + j, T), :]
    cb = cb * _sigmoid(cb)
    cb_bf = cb.astype(bf16)
    xb_bf = xb.astype(bf16)
    yield
    qs, ks, vs = [], [], []
    for h in range(H_B):
        hs = slice(h * DH_B, (h + 1) * DH_B)
        q = _dot(cb_bf[:, hs], wq_ref[h])
        k = _dot(cb_bf[:, hs], wk_ref[h]) * (DH_B ** -0.5)
        v = _dot(xb_bf[:, hs], wv_ref[h])
        qs.append(q)
        ks.append(k)
        vs.append(v)
        qkv_scr[:, (3 * h) * DH_B:(3 * h + 1) * DH_B] = q.astype(bf16)
        qkv_scr[:, (3 * h + 1) * DH_B:(3 * h + 2) * DH_B] = k.astype(bf16)
        qkv_scr[:, (3 * h + 2) * DH_B:(3 * h + 3) * DH_B] = v.astype(bf16)
    yield
    gates = _dot(qkv_scr[...], wg_ref[...]) + bg_ref[...]
    ig = gates[:, 0:LANES]
    lf = jax.nn.log_sigmoid(gates[:, LANES:2 * LANES])
    yield
    ri = lax.broadcasted_iota(jnp.int32, (L, L), 0)
    ci = lax.broadcasted_iota(jnp.int32, (L, L), 1)
    causal = ri >= ci
    tri = jnp.where(causal, 1.0, 0.0).astype(bf16)
    gn = gn_ref[...]
    skip = skip_ref[...]
    heads = range(H_B)
    stack = lambda parts: jnp.concatenate(parts, axis=0)
    gn_st = stack([jnp.broadcast_to(gn[:, h * DH_B:(h + 1) * DH_B], (L, DH_B)) for h in heads])
    ones_blk = jnp.ones((L, DH_B), bf16)
    C_old = [C_scr[h] for h in heads]
    n_old = [n_scr[h:h + 1, :] for h in heads]
    m_old = [m_scr[h:h + 1, 0:1] for h in heads]
    for c in range(T // L):
        rows = slice(c * L, (c + 1) * L)
        b_all = _dot_split_rhs(tri, lf[rows], 2)
        c_all = ig[rows] - b_all
        c_all_t = c_all.T
        yield
        q_bf = [qs[h][rows].astype(bf16) for h in heads]
        k_bf = [ks[h][rows].astype(bf16) for h in heads]
        v_bf = [vs[h][rows].astype(bf16) for h in heads]
        m_prev = stack([jnp.broadcast_to(m_old[h], (L, 1)) for h in heads])
        b_col = stack([b_all[:, h:h + 1] for h in heads])
        c_col = stack([c_all[:, h:h + 1] for h in heads])
        dm = stack([jnp.where(causal, c_all_t[h:h + 1, :], -jnp.inf) for h in heads])
        mx = jnp.maximum(m_prev, jnp.max(dm, axis=1, keepdims=True))
        yield
        sc = stack([_dot_nt(q_bf[h], k_bf[h]) for h in heads]) * jnp.exp(dm - mx)
        sc_bf = sc.astype(bf16)
        yield
        w_inter = jnp.exp(m_prev - mx)
        sv_aug = [_dot(sc_bf[h * L:(h + 1) * L], jnp.concatenate([v_bf[h], ones_blk], axis=1)) for h in heads]
        sv = stack([a[:, 0:DH_B] for a in sv_aug])
        s_sum = stack([a[:, DH_B:DH_B + 1] for a in sv_aug])
        qC = stack([_dot(q_bf[h], C_old[h].astype(bf16)) for h in heads])
        qn = stack([_dot_nt(q_bf[h], jnp.broadcast_to(n_old[h], (SUBLANES, DH_B)).astype(bf16))[:, 0:1]
                    for h in heads])
        yield
        num = sv + w_inter * qC
        den = s_sum + w_inter * qn
        hb = num / jnp.maximum(jnp.abs(den), jnp.exp(-(b_col + mx)))
        hb = _head_layernorm(hb, gn_st)
        yield
        mx_last = [mx[(h + 1) * L - 1:(h + 1) * L, :] for h in heads]
        g_in = jnp.exp(c_col - stack([jnp.broadcast_to(mx_last[h], (L, 1)) for h in heads]))
        C_new, n_new, m_new = [], [], []
        for h in heads:
            hs = slice(h * DH_B, (h + 1) * DH_B)
            hr = slice(h * L, (h + 1) * L)
            b_out = _sigmoid(zb[rows, hs]) * (hb[hr] + skip[:, hs] * cb[rows, hs])
            mix_scr[rows, hs] = b_out.astype(bf16)
            gk = g_in[hr] * ks[h][rows]
            decay = jnp.exp(m_old[h] - mx_last[h])
            C_new.append(decay * C_old[h] + _dot_tn(gk.astype(bf16), v_bf[h]))
            n_new.append(decay * n_old[h] + jnp.sum(gk, axis=0, keepdims=True))
            m_new.append(b_all[L - 1:L, h:h + 1] + mx_last[h])
        C_old, n_old, m_old = C_new, n_new, m_new
        yield
    for h in heads:
        C_scr[h] = C_old[h]
        n_scr[h:h + 1, :] = n_old[h]
        m_scr[h:h + 1, :] = jnp.broadcast_to(m_old[h], (1, LANES))

    ahist[pl.ds(A_HIST, T), :] = xa * _sigmoid(ga)
    yield
    caw = caw_ref[...]
    cab = cab_ref[...]
    first = A_HIST - (CONV_A - 1)

    for base in range(0, T, CONV_ROWS):
        for lo in range(0, W_A, LANES):
            blk = ahist[pl.ds(base, CONV_ROWS + A_HIST), lo:lo + LANES]
            acc = jnp.broadcast_to(cab[:, lo:lo + LANES], (CONV_ROWS, LANES))
            for rr in range(SUBLANES):
                n = CONV_ROWS if rr == 0 else CONV_ROWS + SUBLANES
                z = None
                for q in range(A_HIST // SUBLANES + 1):
                    j = SUBLANES * q + rr - first
                    if 0 <= j < CONV_A:
                        term = caw[j:j + 1, lo:lo + LANES] * blk[SUBLANES * q:SUBLANES * q + n, :]
                        z = term if z is None else z + term
                acc = acc + (z if rr == 0 else z[rr:rr + CONV_ROWS, :])
            conv_scr[pl.ds(base, CONV_ROWS), lo:lo + LANES] = acc
            yield
    a_out = _group_layernorm_silu(conv_scr[...], _group_mask_a(), lag_ref[...], lab_ref[...])
    y = _dot(a_out.astype(bf16), wout_ref[0:W_A, :])
    yield

    chist[pl.ds(S_HIST, T), :] = xc
    ccw = ccw_ref[...]
    xcv = ccb_ref[...] + ccw[3:4, :] * xc
    for j in range(CONV_C - 1):
        xcv = xcv + ccw[j:j + 1, :] * chist[pl.ds(S_HIST - (CONV_C - 1) + j, T), :]
    yield
    a_t, u_t = _rglru_gates(xcv, wrg_ref[...], brg_ref[...], lam_ref[...])
    yield
    row = lax.broadcasted_iota(jnp.int32, (T, W_C), 0)
    d = 1
    while d < T:
        if d < SUBLANES:
            a_sh = jnp.where(row >= d, pltpu.roll(a_t, d, axis=0), 1.0)
            u_sh = jnp.where(row >= d, pltpu.roll(u_t, d, axis=0), 0.0)
        else:
            a_sh = jnp.concatenate([jnp.ones((d, W_C), f32), a_t[:T - d]], axis=0)
            u_sh = jnp.concatenate([jnp.zeros((d, W_C), f32), u_t[:T - d]], axis=0)
        u_t = a_t * u_sh + u_t
        a_t = a_t * a_sh
        d *= 2
    hc = a_t * h_scr[...] + u_t
    h_scr[...] = hc[T - 1:T, :]
    yield
    y = y + _dot((hc * jax.nn.gelu(gc)).astype(bf16), wout_ref[W_A + W_B:, :])

    yield
    x1_ref[...] = x + (y + _dot(mix_scr[...], wout_ref[W_A:W_A + W_B, :]))
    yield

    ahist[0:A_HIST, :] = ahist[pl.ds(T, A_HIST), :]
    bhist[0:S_HIST, :] = bhist[pl.ds(T, S_HIST), :]
    chist[0:S_HIST, :] = chist[pl.ds(T, S_HIST), :]

    @pl.when(s == last)
    def _():
        sa_ref[...] = ahist[pl.ds(first, CONV_A - 1), :]
        sb_ref[...] = bhist[pl.ds(S_HIST - (CONV_B - 1), CONV_B - 1), :]
        sc_ref[...] = chist[pl.ds(S_HIST - (CONV_C - 1), CONV_C - 1), :]
        sC_ref[...] = C_scr[...]
        sn_ref[...] = n_scr[0:H_B, :]
        sm_ref[...] = m_scr[0:H_B, :]
        sh_ref[...] = h_scr[...]


def _full(shape):
    nd = len(shape)
    return pl.BlockSpec(shape, lambda *_: (0,) * nd)


def _layer(w, l):
    nd = w.ndim - 1
    return pl.BlockSpec((None,) + w.shape[1:], lambda *_: (l,) + (0,) * nd)


def _mix_prompt(x, lw, l):
    B, S, _ = x.shape
    T = T_MIX
    weights = [lw[k] for k in ("norm_mix_g", "w_in", "conv_a_w", "conv_a_b", "ln_a_g", "ln_a_b", "conv_b_w", "conv_b_b",
                               "w_q", "w_k", "w_v", "w_gates", "b_gates", "gn_b_g", "skip_b", "conv_c_w", "conv_c_b",
                               "w_rg", "b_rg", "lam", "w_out")]
    out_shape = (
        jax.ShapeDtypeStruct((B, S, D_MODEL), f32),
        jax.ShapeDtypeStruct((B, CONV_A - 1, W_A), f32),
        jax.ShapeDtypeStruct((B, CONV_B - 1, W_B), f32),
        jax.ShapeDtypeStruct((B, H_B, DH_B, DH_B), f32),
        jax.ShapeDtypeStruct((B, H_B, DH_B), f32),
        jax.ShapeDtypeStruct((B, H_B, LANES), f32),
        jax.ShapeDtypeStruct((B, CONV_C - 1, W_C), f32),
        jax.ShapeDtypeStruct((B, 1, W_C), f32),
    )
    G = MIX_G
    per_b = lambda shp: pl.BlockSpec((G,) + shp, lambda b, s: (b,) + (0,) * len(shp))
    out_specs = (
        pl.BlockSpec((G, T, D_MODEL), lambda b, s: (b, s, 0)),
        per_b((CONV_A - 1, W_A)), per_b((CONV_B - 1, W_B)), per_b((H_B, DH_B, DH_B)), per_b((H_B, DH_B)),
        per_b((H_B, LANES)), per_b((CONV_C - 1, W_C)), per_b((1, W_C)),
    )
    scratch = [
        pltpu.VMEM((G, A_HIST + T, W_A), f32), pltpu.VMEM((G, S_HIST + T, W_B), f32),
        pltpu.VMEM((G, S_HIST + T, W_C), f32), pltpu.VMEM((G, T, W_A), f32),
        pltpu.VMEM((G, H_B, DH_B, DH_B), f32), pltpu.VMEM((G, SUBLANES, DH_B), f32),
        pltpu.VMEM((G, SUBLANES, LANES), f32), pltpu.VMEM((G, 1, W_C), f32),
        pltpu.VMEM((G, T, 3 * W_B), bf16), pltpu.VMEM((G, T, W_B), bf16),
    ]
    assert len(weights) == N_MIX_WEIGHTS and B % G == 0
    return pl.pallas_call(
        _mix_prompt_body,
        grid=(B // G, S // T),
        in_specs=[pl.BlockSpec((G, T, D_MODEL), lambda b, s: (b, s, 0))] + [_layer(w, l) for w in weights],
        out_specs=out_specs,
        out_shape=out_shape,
        scratch_shapes=scratch,
        compiler_params=pltpu.CompilerParams(dimension_semantics=("arbitrary", "arbitrary"),
                                             vmem_limit_bytes=VMEM_LIMIT),
        name="mix_prompt",
    )(x, *weights)


def _step_conv(st_ref, w, bias, x_new, st_out):
    k1 = st_ref.shape[0]
    y = bias + w[k1:k1 + 1, :] * x_new
    for j in range(k1):
        y = y + w[j:j + 1, :] * st_ref[j]
        if j > 0:
            st_out[j - 1] = st_ref[j]
    st_out[k1 - 1] = x_new
    return y


def _step_pre_body(x_ref, g_ref, win_ref, sa_ref, caw_ref, cab_ref, lag_ref, lab_ref, sb_ref, cbw_ref, cbb_ref,
                   wq_ref, wk_ref, wv_ref, wg_ref, bg_ref, n_ref, m_ref, sc_ref, ccw_ref, ccb_ref,
                   wrg_ref, brg_ref, lam_ref, h_ref, skip_ref,
                   sa_o, sb_o, sc_o, n_o, m_o, h_o, q_o, gk_o, v_o, dec_o, den_o, sv_o, zsig_o, skcb_o, ac_o,
                   qkv_scr):
    x = x_ref[...]
    hn = _rmsnorm(x, g_ref[...])
    u = _dot(hn.astype(bf16), win_ref[...])
    b0, c0 = 2 * W_A, 2 * W_A + 2 * W_B
    xa, ga = u[:, 0:W_A], u[:, W_A:b0]
    xb, zb = u[:, b0:b0 + W_B], u[:, b0 + W_B:c0]
    xc, gc = u[:, c0:c0 + W_C], u[:, c0 + W_C:]

    a_new = xa * _sigmoid(ga)
    conv = _step_conv(sa_ref, caw_ref[...], cab_ref[...], a_new, sa_o)
    ac_o[:, 0:W_A] = _group_layernorm_silu(conv, _group_mask_a(), lag_ref[...], lab_ref[...])

    cb = _step_conv(sb_ref, cbw_ref[...], cbb_ref[...], xb, sb_o)
    cb = cb * _sigmoid(cb)
    cb_bf = cb.astype(bf16)
    xb_bf = xb.astype(bf16)
    qs, ks, vs = [], [], []
    for h in range(H_B):
        hs = slice(h * DH_B, (h + 1) * DH_B)
        q = _dot(cb_bf[:, hs], wq_ref[h])
        k = _dot(cb_bf[:, hs], wk_ref[h]) * (DH_B ** -0.5)
        v = _dot(xb_bf[:, hs], wv_ref[h])
        qs.append(q)
        ks.append(k)
        vs.append(v)
        qkv_scr[:, (3 * h) * DH_B:(3 * h + 1) * DH_B] = q.astype(bf16)
        qkv_scr[:, (3 * h + 1) * DH_B:(3 * h + 2) * DH_B] = k.astype(bf16)
        qkv_scr[:, (3 * h + 2) * DH_B:(3 * h + 3) * DH_B] = v.astype(bf16)
    gates = _dot(qkv_scr[...], wg_ref[...]) + bg_ref[...]
    ig = gates[:, 0:LANES]
    lf = jax.nn.log_sigmoid(gates[:, LANES:2 * LANES])
    m0 = m_ref[...]
    m_t = jnp.maximum(lf + m0, ig)
    g_in = jnp.exp(ig - m_t)
    decay = jnp.exp(lf + m0 - m_t)
    lane = lax.broadcasted_iota(jnp.int32, ig.shape, 1)
    qk = jnp.zeros(ig.shape, f32)
    qn = jnp.zeros(ig.shape, f32)
    for h in range(H_B):
        hs = slice(h * DH_B, (h + 1) * DH_B)
        n_h = n_ref[:, h, :]
        qk = jnp.where(lane == h, jnp.sum(qs[h] * ks[h], axis=-1, keepdims=True), qk)
        qn = jnp.where(lane == h, jnp.sum(qs[h] * n_h, axis=-1, keepdims=True), qn)
        gk = g_in[:, h:h + 1] * ks[h]
        n_o[:, h, :] = decay[:, h:h + 1] * n_h + gk
        q_o[:, hs] = qs[h]
        gk_o[:, hs] = gk
        v_o[:, hs] = vs[h]
    s_t = qk * g_in
    den = s_t + decay * qn
    den_o[...] = jnp.maximum(jnp.abs(den), jnp.exp(-m_t))
    dec_o[...] = decay
    m_o[...] = m_t
    for h in range(H_B):
        hs = slice(h * DH_B, (h + 1) * DH_B)
        sv_o[:, hs] = s_t[:, h:h + 1] * vs[h]
    zsig_o[...] = _sigmoid(zb)
    skcb_o[...] = skip_ref[...] * cb

    xcv = _step_conv(sc_ref, ccw_ref[...], ccb_ref[...], xc, sc_o)
    a_t, u_t = _rglru_gates(xcv, wrg_ref[...], brg_ref[...], lam_ref[...])
    hc = a_t * h_ref[...] + u_t
    h_o[...] = hc
    ac_o[:, W_A:] = hc * jax.nn.gelu(gc)


def _step_read_body(C_ref, qT_ref, qc_ref):
    for bb in range(STEP_BB):
        for h in range(H_B):
            hs = slice(h * DH_B, (h + 1) * DH_B)
            qc_ref[bb:bb + 1, hs] = jnp.sum(C_ref[bb, h] * qT_ref[h, :, bb:bb + 1], axis=0, keepdims=True)


def _step_state_body(C_ref, qT_ref, gkT_ref, v_ref, dec_ref, Cn_ref, qc_ref):
    for bb in range(STEP_BB):
        for h in range(H_B):
            hs = slice(h * DH_B, (h + 1) * DH_B)
            C = C_ref[bb, h]
            qc_ref[bb:bb + 1, hs] = jnp.sum(C * qT_ref[h, :, bb:bb + 1], axis=0, keepdims=True)
            Cn_ref[bb, h] = dec_ref[bb:bb + 1, h:h + 1] * C + gkT_ref[h, :, bb:bb + 1] * v_ref[bb:bb + 1, hs]


def _to_cols(t):
    return t.reshape(t.shape[0] // STEP_BB, STEP_BB, H_B, DH_B).transpose(0, 2, 3, 1)


def _step_read(C_all, layer, q):
    Bs = q.shape[0]
    return pl.pallas_call(
        _step_read_body,
        grid=(Bs // STEP_BB,),
        in_specs=[pl.BlockSpec((None, STEP_BB, H_B, DH_B, DH_B), lambda i: (layer, i, 0, 0, 0)),
                  pl.BlockSpec((None, H_B, DH_B, STEP_BB), lambda i: (i, 0, 0, 0))],
        out_specs=pl.BlockSpec((STEP_BB, W_B), lambda i: (i, 0)),
        out_shape=jax.ShapeDtypeStruct((Bs, W_B), f32),
        compiler_params=pltpu.CompilerParams(dimension_semantics=("arbitrary",), vmem_limit_bytes=VMEM_LIMIT),
        name="step_read",
    )(C_all, _to_cols(q))


def _step_state(C_all, q, gk, v, dec):
    depth, Bs = q.shape[0], q.shape[1]
    cols = lambda t: jnp.stack([_to_cols(t[l]) for l in range(depth)])
    blk5 = pl.BlockSpec((None, STEP_BB, H_B, DH_B, DH_B), lambda l, i: (l, i, 0, 0, 0))
    colspec = pl.BlockSpec((None, None, H_B, DH_B, STEP_BB), lambda l, i: (l, i, 0, 0, 0))
    return pl.pallas_call(
        _step_state_body,
        grid=(depth, Bs // STEP_BB),
        in_specs=[blk5, colspec, colspec,
                  pl.BlockSpec((None, STEP_BB, W_B), lambda l, i: (l, i, 0)),
                  pl.BlockSpec((None, STEP_BB, LANES), lambda l, i: (l, i, 0))],
        out_specs=(blk5, pl.BlockSpec((None, STEP_BB, W_B), lambda l, i: (l, i, 0))),
        out_shape=(jax.ShapeDtypeStruct(C_all.shape, f32), jax.ShapeDtypeStruct((depth, Bs, W_B), f32)),
        compiler_params=pltpu.CompilerParams(dimension_semantics=("arbitrary", "arbitrary"),
                                             vmem_limit_bytes=VMEM_LIMIT),
        name="step_state",
    )(C_all, cols(q), cols(gk), v, dec)


def _step_post_body(x_ref, ac_ref, zsig_ref, skcb_ref, sv_ref, qc_ref, dec_ref, den_ref, gn_ref, wout_ref,
                    x1_ref, mix_scr):
    gn = gn_ref[...]
    dec = dec_ref[...]
    den = den_ref[...]
    mix_scr[:, 0:W_A] = ac_ref[:, 0:W_A].astype(bf16)
    mix_scr[:, W_A + W_B:] = ac_ref[:, W_A:].astype(bf16)
    for h in range(H_B):
        hs = slice(h * DH_B, (h + 1) * DH_B)
        num = sv_ref[:, hs] + dec[:, h:h + 1] * qc_ref[:, hs]
        hb = _head_layernorm(num / den[:, h:h + 1], gn[:, hs])
        mix_scr[:, W_A + h * DH_B:W_A + (h + 1) * DH_B] = (zsig_ref[:, hs] * (hb + skcb_ref[:, hs])).astype(bf16)
    x1_ref[...] = x_ref[...] + _dot(mix_scr[...], wout_ref[...])


def _step_pre(x, st, lw, l):
    buf_a, buf_b, n0, m0, buf_c, hc0 = st
    buf_a, buf_b, buf_c = (b.transpose(0, 2, 1, 3) for b in (buf_a, buf_b, buf_c))
    Bs = x.shape[0]
    m_in = jnp.pad(m0, ((0, 0), (0, 0), (0, LANES - H_B)))
    sds = lambda *shape: jax.ShapeDtypeStruct(shape, f32)
    W = lambda k: (lw[k], True)
    L = lambda a: (a, True)
    A = lambda a: (a, False)
    pre_in = [A(x), W("norm_mix_g"), W("w_in"), L(buf_a), W("conv_a_w"), W("conv_a_b"), W("ln_a_g"), W("ln_a_b"),
              L(buf_b), W("conv_b_w"), W("conv_b_b"), W("w_q"), W("w_k"), W("w_v"), W("w_gates"), W("b_gates"),
              L(n0), L(m_in), L(buf_c), W("conv_c_w"), W("conv_c_b"), W("w_rg"), W("b_rg"), W("lam"), L(hc0),
              W("skip_b")]
    out_shape = (sds(*buf_a.shape[1:]), sds(*buf_b.shape[1:]), sds(*buf_c.shape[1:]), sds(*n0.shape[1:]),
                 sds(Bs, LANES), sds(Bs, W_C),
                 sds(Bs, W_B), sds(Bs, W_B), sds(Bs, W_B), sds(Bs, LANES), sds(Bs, LANES), sds(Bs, W_B),
                 sds(Bs, W_B), sds(Bs, W_B), sds(Bs, W_A + W_C))
    (sa_n, sb_n, sc_n, n_n, m_n, h_n, q, gk, v, dec, den, sv, zsig, skcb, ac) = pl.pallas_call(
        _step_pre_body,
        grid=(1,),
        in_specs=[_layer(a, l) if is_w else _full(a.shape) for a, is_w in pre_in],
        out_specs=[_full(o.shape) for o in out_shape],
        out_shape=out_shape,
        scratch_shapes=[pltpu.VMEM((Bs, 3 * W_B), bf16)],
        compiler_params=pltpu.CompilerParams(dimension_semantics=("arbitrary",), vmem_limit_bytes=VMEM_LIMIT),
        name="step_pre",
    )(*[a for a, _ in pre_in])
    new = (sa_n, sb_n, n_n, m_n[:, :H_B], sc_n, h_n)
    return new, (q, gk, v, dec), (ac, zsig, skcb, sv, dec, den)


def _step_post(x, post_in, qc, lw, l):
    ac, zsig, skcb, sv, dec, den = post_in
    acts = [x, ac, zsig, skcb, sv, qc, dec, den]
    return pl.pallas_call(
        _step_post_body,
        grid=(1,),
        in_specs=[_full(a.shape) for a in acts] + [_layer(lw["gn_b_g"], l), _layer(lw["w_out"], l)],
        out_specs=_full(x.shape),
        out_shape=jax.ShapeDtypeStruct(x.shape, f32),
        scratch_shapes=[pltpu.VMEM(x.shape, bf16)],
        compiler_params=pltpu.CompilerParams(dimension_semantics=("arbitrary",), vmem_limit_bytes=VMEM_LIMIT),
        name="step_post",
    )(*acts, lw["gn_b_g"], lw["w_out"])


def _swiglu(hb, wg_ref, wu_ref, wd_ref):
    f = wd_ref.shape[0]
    main = f // MXU_WIDTH * MXU_WIDTH
    glu = lambda gt, up: (gt * _sigmoid(gt) * up).astype(bf16)
    if main == f:
        act = glu(_dot(hb, wg_ref[...]), _dot(hb, wu_ref[...]))
    else:
        tail = _dot(hb, jnp.concatenate([wg_ref[:, main:], wu_ref[:, main:]], axis=1))
        act = jnp.concatenate([glu(_dot(hb, wg_ref[:, :main]), _dot(hb, wu_ref[:, :main])),
                               glu(tail[:, :f - main], tail[:, f - main:])], axis=1)
    return _dot(act, wd_ref[...])


def _ffn_body(*refs, moe, final_norm):
    it = iter(refs)
    x_ref, g_ref, wg_ref, wu_ref, wd_ref = next(it), next(it), next(it), next(it), next(it)
    wr_ref = next(it) if moe else None
    br_ref = next(it) if moe else None
    gf_ref = next(it) if final_norm else None
    o_ref, hn_scr, acc_scr = next(it), next(it), next(it)
    comb_scr = next(it) if moe else None
    j = pl.program_id(1)

    @pl.when(j == 0)
    def _():
        hn = _rmsnorm(x_ref[...], g_ref[...])
        hn_scr[...] = hn.astype(bf16)
        acc_scr[...] = jnp.zeros(acc_scr.shape, f32)
        if moe:
            lane, i1, i2, g1, g2 = _top2(_router_logits(hn, wr_ref[...], br_ref[...]))
            comb_scr[...] = jnp.where(lane == i1, g1, 0.0) + jnp.where(lane == i2, g2, 0.0)

    y = _swiglu(hn_scr[...], wg_ref, wu_ref, wd_ref)
    if moe:
        lane = lax.broadcasted_iota(jnp.int32, comb_scr.shape, 1)
        y = y * jnp.sum(jnp.where(lane == j, comb_scr[...], 0.0), axis=-1, keepdims=True)
    acc_scr[...] += y

    @pl.when(j == pl.num_programs(1) - 1)
    def _():
        out = x_ref[...] + acc_scr[...]
        if final_norm:
            out = _rmsnorm(out, gf_ref[...])
        o_ref[...] = out


def _ffn(x, g, wg, wu, wd, router=None, final_g=None):
    N = x.shape[0]
    E, F = wd.shape[0], wd.shape[1]
    tm = min(TM_FFN, N)
    moe = router is not None
    final_norm = final_g is not None
    ins = [x, g, wg, wu, wd]
    once = dict(pipeline_mode=pl.Buffered(1)) if E == 1 else {}
    in_specs = [pl.BlockSpec((tm, D_MODEL), lambda i, j: (i, 0)), _full(g.shape),
                pl.BlockSpec((None, D_MODEL, F), lambda i, j: (j, 0, 0), **once),
                pl.BlockSpec((None, D_MODEL, F), lambda i, j: (j, 0, 0), **once),
                pl.BlockSpec((None, F, D_MODEL), lambda i, j: (j, 0, 0), **once)]
    scratch = [pltpu.VMEM((tm, D_MODEL), bf16), pltpu.VMEM((tm, D_MODEL), f32)]
    if moe:
        ins += list(router)
        in_specs += [_full(router[0].shape), _full(router[1].shape)]
        scratch.append(pltpu.VMEM((tm, LANES), f32))
    if final_norm:
        ins.append(final_g)
        in_specs.append(_full(final_g.shape))
    return pl.pallas_call(
        functools.partial(_ffn_body, moe=moe, final_norm=final_norm),
        grid=(N // tm, E),
        in_specs=in_specs,
        out_specs=pl.BlockSpec((tm, D_MODEL), lambda i, j: (i, 0)),
        out_shape=jax.ShapeDtypeStruct((N, D_MODEL), f32),
        scratch_shapes=scratch,
        compiler_params=pltpu.CompilerParams(dimension_semantics=("arbitrary", "arbitrary"),
                                             vmem_limit_bytes=VMEM_LIMIT),
        name="ffn_moe" if moe else "ffn_dense",
    )(*ins)


TM_MOE = 512
SEG_ALIGN = 16
SEG_SIZES = (512, 256, 128, 64, 32, 16)
MOE_ROWS = 2 * TM_MOE + N_EXPERTS * SEG_ALIGN
META_COLS = 3 * N_EXPERTS


def _seg_dma(src, src_off, dst, dst_off, nrows, sem, wait):
    done = jnp.int32(0)
    for size in SEG_SIZES:
        take = (nrows & size) != 0

        @pl.when(take)
        def _(done=done, size=size):
            cp = pltpu.make_async_copy(src.at[pl.ds(pl.multiple_of(src_off + done, SEG_ALIGN), size)],
                                       dst.at[pl.ds(pl.multiple_of(dst_off + done, SEG_ALIGN), size)], sem)
            if wait:
                cp.wait()
            else:
                cp.start()

        done = done + jnp.where(take, size, 0)


def _top2(logits):
    lane = lax.broadcasted_iota(jnp.int32, logits.shape, 1).astype(f32)
    lg = jnp.where(lane < N_EXPERTS, logits, -jnp.inf)
    m1 = jnp.max(lg, axis=-1, keepdims=True)
    i1 = jnp.min(jnp.where(lg == m1, lane, float(LANES)), axis=-1, keepdims=True)
    lg2 = jnp.where(lane == i1, -jnp.inf, lg)
    m2 = jnp.max(lg2, axis=-1, keepdims=True)
    i2 = jnp.min(jnp.where(lg2 == m2, lane, float(LANES)), axis=-1, keepdims=True)
    e2 = jnp.exp(m2 - m1)
    return lane, i1, i2, 1.0 / (1.0 + e2), e2 / (1.0 + e2)


def _router_logits(hn, wr, br):
    h1, h2 = _split_terms(hn, 2)
    w1, w2 = _split_terms(wr, 2)
    return _dot(h1, w1) + (_dot(h1, w2) + _dot(h2, w1)) + br


def _expert_ranks(lane, i1, i2):
    TM = lane.shape[0]
    sel1 = lane == i1
    sel2 = lane == i2
    sel = jnp.where(sel1, 1.0, jnp.where(sel2, 1.0, 0.0))
    earlier = lax.broadcasted_iota(jnp.int32, (TM, TM), 0) > lax.broadcasted_iota(jnp.int32, (TM, TM), 1)
    rank = _dot(jnp.where(earlier, 1.0, 0.0).astype(bf16), sel.astype(bf16))
    return sel1, sel2, rank, rank[TM - 1:TM, :] + sel[TM - 1:TM, :]


def _seg_pad(c):
    return ((c + (SEG_ALIGN - 1)) // SEG_ALIGN) * SEG_ALIGN


def _moe_count_body(x_ref, g_ref, wr_ref, br_ref, route_ref, cnt_ref):
    i = pl.program_id(0)
    hn = _rmsnorm(x_ref[...], g_ref[...])
    lane, i1, i2, g1, g2 = _top2(_router_logits(hn, wr_ref[...], br_ref[...]))
    route_ref[...] = jnp.where(lane == 0, i1, jnp.where(lane == 1, i2, jnp.where(lane == 2, g1,
                                                                                 jnp.where(lane == 3, g2, 0.0))))
    cnt = jnp.sum(jnp.where(lane == i1, 1.0, jnp.where(lane == i2, 1.0, 0.0)), axis=0, keepdims=True)
    for e in range(N_EXPERTS):
        cnt_ref[i, e] = cnt[0, e].astype(jnp.int32)


def _moe_route_body(cnt_ref, x_ref, g_ref, route_ref, xs_hbm, gs_hbm, rinfo_ref, meta_ref, te_ref, nv_ref,
                    xperm, gperm, pos_smem, sem):
    TM, R = TM_MOE, MOE_ROWS
    i = pl.program_id(0)
    nt = pl.num_programs(0)
    n_tiles = te_ref.shape[0]

    @pl.when(i == 0)
    def _():
        base = jnp.int32(0)
        for e in range(N_EXPERTS):
            rows = lax.fori_loop(0, nt, lambda t, a, e=e: a + _seg_pad(cnt_ref[t, e]), jnp.int32(0))
            first_tile = base // TM
            tiles = (rows + (TM - 1)) // TM

            def mark(k, c, e=e, first_tile=first_tile):
                te_ref[first_tile + k] = e
                return c

            lax.fori_loop(0, tiles, mark, 0)
            pos_smem[e] = base
            base = base + tiles * TM
        nv = base // TM
        nv_ref[0] = nv

        def mark_rest(k, c):
            te_ref[k] = N_EXPERTS - 1
            return c

        lax.fori_loop(nv, n_tiles, mark_rest, 0)

    hn = _rmsnorm(x_ref[...], g_ref[...])
    route = route_ref[...]
    i1, i2, g1, g2 = route[:, 0:1], route[:, 1:2], route[:, 2:3], route[:, 3:4]
    lane = lax.broadcasted_iota(jnp.int32, route.shape, 1).astype(f32)
    sel1, sel2, rank, _ = _expert_ranks(lane, i1, i2)
    seg_rows, seg_off = [], []
    acc = jnp.int32(0)
    for e in range(N_EXPERTS):
        seg_rows.append(_seg_pad(cnt_ref[i, e]))
        seg_off.append(acc)
        acc = acc + seg_rows[e]
    lane1 = lax.broadcasted_iota(jnp.int32, (1, LANES), 1)
    offv = jnp.zeros((1, LANES), f32)
    for e in range(N_EXPERTS):
        offv = jnp.where(lane1 == e, seg_off[e].astype(f32), offv)
    dest = rank + offv
    d1 = jnp.sum(jnp.where(sel1, dest, 0.0), axis=-1, keepdims=True)
    d2 = jnp.sum(jnp.where(sel2, dest, 0.0), axis=-1, keepdims=True)
    rinfo = jnp.where(lane == 0, d1, jnp.where(lane == 1, d2, jnp.where(lane == 2, g1, jnp.where(lane == 3, g2, 0.0))))
    rinfo_ref[...] = rinfo
    rinfo_t = rinfo.T
    d1r, d2r, g1r, g2r = rinfo_t[0:1, :], rinfo_t[1:2, :], rinfo_t[2:3, :], rinfo_t[3:4, :]
    riota = lax.broadcasted_iota(jnp.int32, (R, TM), 0).astype(f32)
    hit1 = riota == d1r
    hit2 = riota == d2r
    perm = jnp.where(hit1, 1.0, jnp.where(hit2, 1.0, 0.0)).astype(bf16)
    xperm[i % 2] = _dot(perm, hn.astype(bf16)).astype(bf16)
    gcol = jnp.sum(jnp.where(hit1, g1r, 0.0) + jnp.where(hit2, g2r, 0.0), axis=1, keepdims=True)
    gperm[i % 2] = jnp.broadcast_to(gcol, (R, LANES))

    for e in range(N_EXPERTS):
        pos = pos_smem[e]
        meta_ref[i, e] = pos
        meta_ref[i, N_EXPERTS + e] = seg_off[e]
        meta_ref[i, 2 * N_EXPERTS + e] = seg_rows[e]
        pos_smem[e] = pos + seg_rows[e]

    def copies(tile, wait):
        for slot in range(2):
            @pl.when(tile % 2 == slot)
            def _(slot=slot):
                for e in range(N_EXPERTS):
                    src_off, dst_off, n = meta_ref[tile, N_EXPERTS + e], meta_ref[tile, e], meta_ref[tile, 2 * N_EXPERTS + e]
                    _seg_dma(xperm.at[slot], src_off, xs_hbm, dst_off, n, sem.at[2 * slot], wait)
                    _seg_dma(gperm.at[slot], src_off, gs_hbm, dst_off, n, sem.at[2 * slot + 1], wait)

    copies(i, False)
    pl.when(i > 0)(lambda: copies(i - 1, True))

    @pl.when(i == nt - 1)
    def _():
        copies(i, True)
        xz, gz = xperm.at[0], gperm.at[0]
        xz[0:TM, :] = jnp.zeros((TM, D_MODEL), bf16)
        gz[0:TM, :] = jnp.zeros((TM, LANES), f32)
        pads = []
        for e in range(N_EXPERTS):
            pos = pos_smem[e]
            pads.append((-pos) & (TM - 1))
            _seg_dma(xz, 0, xs_hbm, pos, pads[e], sem.at[0], False)
            _seg_dma(gz, 0, gs_hbm, pos, pads[e], sem.at[1], False)
        for e in range(N_EXPERTS):
            _seg_dma(xz, 0, xs_hbm, pos_smem[e], pads[e], sem.at[0], True)
            _seg_dma(gz, 0, gs_hbm, pos_smem[e], pads[e], sem.at[1], True)

        def zero_tile(k, c):
            row = pl.multiple_of(k * TM, TM)
            cx = pltpu.make_async_copy(xz.at[pl.ds(0, TM)], xs_hbm.at[pl.ds(row, TM)], sem.at[0])
            cg = pltpu.make_async_copy(gz.at[pl.ds(0, TM)], gs_hbm.at[pl.ds(row, TM)], sem.at[1])
            cx.start()
            cg.start()
            cx.wait()
            cg.wait()
            return c

        lax.fori_loop(nv_ref[0], n_tiles, zero_tile, 0)


def _moe_ffn_body(te_ref, nv_ref, xs_ref, gs_ref, wg_ref, wu_ref, wd_ref, ys_ref):
    i = pl.program_id(0)

    @pl.when(i < nv_ref[0])
    def _():
        ys_ref[...] = (_swiglu(xs_ref[...], wg_ref, wu_ref, wd_ref) * gs_ref[:, 0:1]).astype(bf16)

    @pl.when(i >= nv_ref[0])
    def _():
        ys_ref[...] = jnp.zeros(ys_ref.shape, bf16)


def _moe_combine_body(*refs, final_norm):
    it = iter(refs)
    meta_ref, x_ref, rinfo_ref, ys_hbm = next(it), next(it), next(it), next(it)
    gf_ref = next(it) if final_norm else None
    o_ref, yperm, sem = next(it), next(it), next(it)
    TM, R = TM_MOE, MOE_ROWS
    i = pl.program_id(0)
    nt = pl.num_programs(0)

    def fetch(tile, slot, wait):
        for e in range(N_EXPERTS):
            _seg_dma(ys_hbm, meta_ref[tile, e], yperm.at[slot], meta_ref[tile, N_EXPERTS + e],
                     meta_ref[tile, 2 * N_EXPERTS + e], sem.at[slot], wait)

    def both_slots(tile, wait):
        for slot in range(2):
            pl.when(tile % 2 == slot)(lambda slot=slot: fetch(tile, slot, wait))

    @pl.when(i == 0)
    def _():
        yperm[...] = jnp.zeros(yperm.shape, bf16)
        fetch(0, 0, False)

    @pl.when(i + 1 < nt)
    def _():
        both_slots(i + 1, False)

    rinfo = rinfo_ref[...]
    ciota = lax.broadcasted_iota(jnp.int32, (TM, R), 1).astype(f32)
    unperm = jnp.where(ciota == rinfo[:, 0:1], 1.0, jnp.where(ciota == rinfo[:, 1:2], 1.0, 0.0)).astype(bf16)
    both_slots(i, True)
    out = x_ref[...] + _dot(unperm, yperm[i % 2])
    if final_norm:
        out = _rmsnorm(out, gf_ref[...])
    o_ref[...] = out


def _moe_prompt(x, g, wg, wu, wd, wr, br, final_g):
    N = x.shape[0]
    TM = TM_MOE
    nt = N // TM
    F = wd.shape[1]
    n_tiles = -(-(2 * N + nt * N_EXPERTS * (SEG_ALIGN - 1) + N_EXPERTS * (TM - 1)) // TM)
    rows = n_tiles * TM
    smem = pl.BlockSpec(memory_space=pltpu.SMEM)
    arb = pltpu.CompilerParams(dimension_semantics=("arbitrary",), vmem_limit_bytes=VMEM_LIMIT)
    route, cnt = pl.pallas_call(
        _moe_count_body,
        grid=(nt,),
        in_specs=[pl.BlockSpec((TM, D_MODEL), lambda i: (i, 0)), _full(g.shape), _full(wr.shape), _full(br.shape)],
        out_specs=[pl.BlockSpec((TM, LANES), lambda i: (i, 0)), smem],
        out_shape=[jax.ShapeDtypeStruct((N, LANES), f32), jax.ShapeDtypeStruct((nt, N_EXPERTS), jnp.int32)],
        compiler_params=arb,
        name="moe_count",
    )(x, g, wr, br)

    xs, gs, rinfo, meta, te, nv = pl.pallas_call(
        _moe_route_body,
        grid_spec=pltpu.PrefetchScalarGridSpec(
            num_scalar_prefetch=1,
            grid=(nt,),
            in_specs=[pl.BlockSpec((TM, D_MODEL), lambda i, c: (i, 0)), pl.BlockSpec(g.shape, lambda i, c: (0, 0)),
                      pl.BlockSpec((TM, LANES), lambda i, c: (i, 0))],
            out_specs=[pl.BlockSpec(memory_space=pl.ANY), pl.BlockSpec(memory_space=pl.ANY),
                       pl.BlockSpec((TM, LANES), lambda i, c: (i, 0)), smem, smem, smem],
            scratch_shapes=[pltpu.VMEM((2, MOE_ROWS, D_MODEL), bf16), pltpu.VMEM((2, MOE_ROWS, LANES), f32),
                            pltpu.SMEM((N_EXPERTS,), jnp.int32), pltpu.SemaphoreType.DMA((4,))],
        ),
        out_shape=[jax.ShapeDtypeStruct((rows, D_MODEL), bf16), jax.ShapeDtypeStruct((rows, LANES), f32),
                   jax.ShapeDtypeStruct((N, LANES), f32), jax.ShapeDtypeStruct((nt, META_COLS), jnp.int32),
                   jax.ShapeDtypeStruct((n_tiles,), jnp.int32), jax.ShapeDtypeStruct((1,), jnp.int32)],
        compiler_params=arb,
        name="moe_route",
    )(cnt, x, g, route)

    ys = pl.pallas_call(
        _moe_ffn_body,
        grid_spec=pltpu.PrefetchScalarGridSpec(
            num_scalar_prefetch=2,
            grid=(n_tiles,),
            in_specs=[pl.BlockSpec((TM, D_MODEL), lambda i, te, nv: (i, 0)),
                      pl.BlockSpec((TM, LANES), lambda i, te, nv: (i, 0)),
                      pl.BlockSpec((None, D_MODEL, F), lambda i, te, nv: (te[i], 0, 0)),
                      pl.BlockSpec((None, D_MODEL, F), lambda i, te, nv: (te[i], 0, 0)),
                      pl.BlockSpec((None, F, D_MODEL), lambda i, te, nv: (te[i], 0, 0))],
            out_specs=pl.BlockSpec((TM, D_MODEL), lambda i, te, nv: (i, 0)),
        ),
        out_shape=jax.ShapeDtypeStruct((rows, D_MODEL), bf16),
        compiler_params=arb,
        name="moe_ffn",
    )(te, nv, xs, gs, wg, wu, wd)

    final_norm = final_g is not None
    ins = [meta, x, rinfo, ys] + ([final_g] if final_norm else [])
    in_specs = [pl.BlockSpec((TM, D_MODEL), lambda i, m: (i, 0)), pl.BlockSpec((TM, LANES), lambda i, m: (i, 0)),
                pl.BlockSpec(memory_space=pl.ANY)]
    if final_norm:
        in_specs.append(pl.BlockSpec(final_g.shape, lambda i, m: (0, 0)))
    return pl.pallas_call(
        functools.partial(_moe_combine_body, final_norm=final_norm),
        grid_spec=pltpu.PrefetchScalarGridSpec(
            num_scalar_prefetch=1,
            grid=(nt,),
            in_specs=in_specs,
            out_specs=pl.BlockSpec((TM, D_MODEL), lambda i, m: (i, 0)),
            scratch_shapes=[pltpu.VMEM((2, MOE_ROWS, D_MODEL), bf16), pltpu.SemaphoreType.DMA((2,))],
        ),
        out_shape=jax.ShapeDtypeStruct((N, D_MODEL), f32),
        compiler_params=pltpu.CompilerParams(dimension_semantics=("arbitrary",), vmem_limit_bytes=VMEM_LIMIT),
        name="moe_combine",
    )(*ins)


def _block_diag(w):
    d, h, di, do = w.shape
    return jnp.einsum("lhij,hg->lhigj", w, jnp.eye(h, dtype=w.dtype)).reshape(d, h * di, h * do)


def _mixer_weights(p):
    row = lambda v: v[:, None, :]
    lanes = lambda w: jnp.pad(w, [(0, 0)] * (w.ndim - 1) + [(0, LANES - w.shape[-1])])
    return dict(
        norm_mix_g=row(p["norm_mix_g"]), w_in=p["w_in"].astype(bf16),
        conv_a_w=p["conv_a_w"], conv_a_b=row(p["conv_a_b"]), ln_a_g=row(p["ln_a_g"]), ln_a_b=row(p["ln_a_b"]),
        conv_b_w=p["conv_b_w"], conv_b_b=row(p["conv_b_b"]),
        w_q=p["w_q"].astype(bf16), w_k=p["w_k"].astype(bf16), w_v=p["w_v"].astype(bf16),
        w_gates=jnp.concatenate([lanes(p["w_ig"]), lanes(p["w_fg"])], axis=-1).astype(bf16),
        b_gates=row(jnp.concatenate([lanes(p["b_ig"]), lanes(p["b_fg"])], axis=-1)),
        gn_b_g=row(p["gn_b_g"]), skip_b=row(p["skip_b"]),
        conv_c_w=p["conv_c_w"], conv_c_b=row(p["conv_c_b"]),
        w_rg=jnp.concatenate([_block_diag(p["w_ra"]), _block_diag(p["w_ix"])], axis=-1).astype(bf16),
        b_rg=row(jnp.concatenate([p["b_ra"], p["b_ix"]], axis=-1)), lam=row(p["lam"]),
        w_out=p["w_out"].astype(bf16),
    )


def kernel(x_prompt, x_sample, state_conv_a, state_conv_b, state_mlstm_C, state_mlstm_n, state_mlstm_m, state_conv_c, state_rglru_h, norm_mix_g, w_in, conv_a_w, conv_a_b, ln_a_g, ln_a_b, conv_b_w, conv_b_b, w_q, w_k, w_v, w_ig, b_ig, w_fg, b_fg, gn_b_g, skip_b, conv_c_w, conv_c_b, w_ra, b_ra, w_ix, b_ix, lam, w_out, norm_ffn_g, w_gate, w_up, w_down, w_router, b_router, we_gate, we_up, we_down, norm_final_g):
    p = dict(norm_mix_g=norm_mix_g, w_in=w_in, conv_a_w=conv_a_w, conv_a_b=conv_a_b, ln_a_g=ln_a_g, ln_a_b=ln_a_b,
             conv_b_w=conv_b_w, conv_b_b=conv_b_b, w_q=w_q, w_k=w_k, w_v=w_v, w_ig=w_ig, b_ig=b_ig, w_fg=w_fg,
             b_fg=b_fg, gn_b_g=gn_b_g, skip_b=skip_b, conv_c_w=conv_c_w, conv_c_b=conv_c_b, w_ra=w_ra, b_ra=b_ra,
             w_ix=w_ix, b_ix=b_ix, lam=lam, w_out=w_out)
    depth = w_in.shape[0]
    Bp, S, _ = x_prompt.shape
    Bs = x_sample.shape[0]
    sample_states = (state_conv_a, state_conv_b, state_mlstm_C, state_mlstm_n, state_mlstm_m, state_conv_c,
                     state_rglru_h)
    xp = x_prompt
    xs = x_sample.reshape(Bs, D_MODEL)
    new_p, new_s, mem_ops = [], [], []
    C_s = None
    lw = _mixer_weights(p)
    for l in range(depth):
        xp, sa, sb, sC, sn, sm, sc, sh = _mix_prompt(xp, lw, l)
        new_p.append((sa, sb, sC, sn, sm[:, :, 0], sc, sh.reshape(Bp, W_C)))
        small, mem, post_in = _step_pre(xs, tuple(s for i, s in enumerate(sample_states) if i != 2), lw, l)
        new_s.append(small)
        mem_ops.append(mem)
        if l < depth - 1:
            qc = _step_read(state_mlstm_C, l, mem[0])
        else:
            C_s, qc_all = _step_state(state_mlstm_C, *(jnp.stack([m[i] for m in mem_ops]) for i in range(4)))
            qc = qc_all[l]
        xs = _step_post(xs, post_in, qc, lw, l)
        g = norm_ffn_g[l].reshape(1, -1)
        final_g = norm_final_g.reshape(1, -1) if l == depth - 1 else None
        j = l // 2
        if l % 2 == 0:
            wg, wu, wd = (w[j:j + 1].astype(bf16) for w in (w_gate, w_up, w_down))
            router = None
        else:
            wg, wu, wd = we_gate[j].astype(bf16), we_up[j].astype(bf16), we_down[j].astype(bf16)
            router = (jnp.pad(w_router[j], ((0, 0), (0, LANES - N_EXPERTS))),
                      jnp.pad(b_router[j], (0, LANES - N_EXPERTS)).reshape(1, -1))
        xp2 = xp.reshape(Bp * S, D_MODEL)
        if router is None:
            xp2 = _ffn(xp2, g, wg, wu, wd, None, final_g)
        else:
            xp2 = _moe_prompt(xp2, g, wg, wu, wd, router[0], router[1], final_g)
        xp = xp2.reshape(Bp, S, D_MODEL)
        xs = _ffn(xs, g, wg, wu, wd, router, final_g)
    stack = lambda states, i: jnp.stack([st[i] for st in states])
    out_p = tuple(stack(new_p, i) for i in range(7))
    sa_s, sb_s, n_s, m_s, sc_s, h_s = (stack(new_s, i) for i in range(6))
    sa_s, sb_s, sc_s = (s.transpose(0, 2, 1, 3) for s in (sa_s, sb_s, sc_s))
    return (xp, xs.reshape(Bs, 1, D_MODEL)) + out_p + (sa_s, sb_s, C_s, n_s, m_s, sc_s, h_s)
```

```python
import functools

import jax
import jax.numpy as jnp
from jax import lax
from jax.experimental import pallas as pl
from jax.experimental.pallas import tpu as pltpu

f32 = jnp.float32
bf16 = jnp.bfloat16

D_MODEL = 1024
W_A, W_B, W_C = 256, 512, 256
G_A = 4
CONV_A, CONV_B, CONV_C = 31, 4, 4
H_B, DH_B = 4, 128
RG_C = 8.0
N_EXPERTS = 8
NORM_EPS = 1e-6
LN_EPS = 1e-5

LANES = 128
SUBLANES = 8
MXU_WIDTH = 256
VMEM_LIMIT = 56 * 1024 * 1024

T_MIX = 256
MIX_G = 4
L_CHUNK = 128
CONV_ROWS = 128
A_HIST = 32
S_HIST = 8
TM_FFN = 512
STEP_BB = 32


def _dot(a, b):
    return jnp.dot(a, b, preferred_element_type=f32)


def _dot_nt(a, b):
    return lax.dot_general(a, b, (((1,), (1,)), ((), ())), preferred_element_type=f32)


def _dot_tn(a, b):
    return lax.dot_general(a, b, (((0,), (0,)), ((), ())), preferred_element_type=f32)


def _split_terms(x, terms):
    out = []
    r = x
    for _ in range(terms - 1):
        p = r.astype(bf16)
        out.append(p)
        r = r - p.astype(f32)
    out.append(r.astype(bf16))
    return out


def _dot_split_lhs(x, m, terms):
    acc = None
    for p in _split_terms(x, terms):
        d = _dot(p, m)
        acc = d if acc is None else acc + d
    return acc


def _dot_split_rhs(m, x, terms):
    acc = None
    for p in _split_terms(x, terms):
        d = _dot(m, p)
        acc = d if acc is None else acc + d
    return acc


def _sigmoid(x):
    return 0.5 * (1.0 + jnp.tanh(0.5 * x))


def _rmsnorm(x, g):
    return x * lax.rsqrt(jnp.mean(x * x, axis=-1, keepdims=True) + NORM_EPS) * g


def _group_mask_a():
    r = lax.broadcasted_iota(jnp.int32, (W_A, W_A), 0) // (W_A // G_A)
    c = lax.broadcasted_iota(jnp.int32, (W_A, W_A), 1) // (W_A // G_A)
    return jnp.where(r == c, 1.0, 0.0).astype(bf16)


def _group_layernorm_silu(a, gm, g, b):
    inv = 1.0 / (W_A // G_A)
    mu = _dot_split_lhs(a, gm, 2) * inv
    d = a - mu
    var = _dot((d * d).astype(bf16), gm) * inv
    y = d * lax.rsqrt(var + LN_EPS) * g + b
    return y * _sigmoid(y)


def _head_layernorm(h, g):
    mu = jnp.mean(h, axis=-1, keepdims=True)
    d = h - mu
    var = jnp.mean(d * d, axis=-1, keepdims=True)
    return d * lax.rsqrt(var + LN_EPS) * g


def _rglru_gates(xcv, wrg, brg, lam):
    gates = _dot(xcv.astype(bf16), wrg) + brg
    r = _sigmoid(gates[:, :W_C])
    i = _sigmoid(gates[:, W_C:])
    log_a = RG_C * r * jax.nn.log_sigmoid(lam)
    a = jnp.exp(log_a)
    t = jnp.tanh(log_a)
    one_minus_a2 = -2.0 * t / (1.0 - t)
    u = jnp.sqrt(one_minus_a2) * i * xcv
    return a, u


N_MIX_WEIGHTS = 21


def _mix_prompt_body(*refs):
    x_ref, weights = refs[0], refs[1:1 + N_MIX_WEIGHTS]
    rest = refs[1 + N_MIX_WEIGHTS:]
    tiles = [_mix_tile(x_ref.at[g], *weights, *[r.at[g] for r in rest]) for g in range(MIX_G)]
    done = object()
    while tiles:
        tiles = [t for t in tiles if next(t, done) is not done]


def _mix_tile(x_ref, g_ref, win_ref, caw_ref, cab_ref, lag_ref, lab_ref, cbw_ref, cbb_ref,
              wq_ref, wk_ref, wv_ref, wg_ref, bg_ref, gn_ref, skip_ref, ccw_ref, ccb_ref,
              wrg_ref, brg_ref, lam_ref, wout_ref,
              x1_ref, sa_ref, sb_ref, sC_ref, sn_ref, sm_ref, sc_ref, sh_ref,
              ahist, bhist, chist, conv_scr, C_scr, n_scr, m_scr, h_scr, qkv_scr, mix_scr):
    T, L = T_MIX, L_CHUNK
    s = pl.program_id(1)
    last = pl.num_programs(1) - 1

    @pl.when(s == 0)
    def _():
        ahist[0:A_HIST, :] = jnp.zeros((A_HIST, W_A), f32)
        bhist[0:S_HIST, :] = jnp.zeros((S_HIST, W_B), f32)
        chist[0:S_HIST, :] = jnp.zeros((S_HIST, W_C), f32)
        C_scr[...] = jnp.zeros(C_scr.shape, f32)
        n_scr[...] = jnp.zeros(n_scr.shape, f32)
        m_scr[...] = jnp.zeros(m_scr.shape, f32)
        h_scr[...] = jnp.zeros(h_scr.shape, f32)

    yield
    x = x_ref[...]
    hn = _rmsnorm(x, g_ref[...])
    hn_bf = hn.astype(bf16)
    yield
    ua = _dot(hn_bf, win_ref[:, 0:2 * W_A])
    ub = _dot(hn_bf, win_ref[:, 2 * W_A:2 * W_A + 2 * W_B])
    uc = _dot(hn_bf, win_ref[:, 2 * W_A + 2 * W_B:])
    xa, ga = ua[:, 0:W_A], ua[:, W_A:]
    xb, zb = ub[:, 0:W_B], ub[:, W_B:]
    xc, gc = uc[:, 0:W_C], uc[:, W_C:]
    yield

    bhist[pl.ds(S_HIST, T), :] = xb
    cbw = cbw_ref[...]
    cb = cbb_ref[...] + cbw[3:4, :] * xb
    for j in range(CONV_B - 1):
        cb = cb + cbw[j:j + 1, :] * bhist[pl.ds(S_HIST - (CONV_B - 1) + j, T), :]
    cb = cb * _sigmoid(cb)
    cb_bf = cb.astype(bf16)
    xb_bf = xb.astype(bf16)
    yield
    qs, ks, vs = [], [], []
    for h in range(H_B):
        hs = slice(h * DH_B, (h + 1) * DH_B)
        q = _dot(cb_bf[:, hs], wq_ref[h])
        k = _dot(cb_bf[:, hs], wk_ref[h]) * (DH_B ** -0.5)
        v = _dot(xb_bf[:, hs], wv_ref[h])
        qs.append(q)
        ks.append(k)
        vs.append(v)
        qkv_scr[:, (3 * h) * DH_B:(3 * h + 1) * DH_B] = q.astype(bf16)
        qkv_scr[:, (3 * h + 1) * DH_B:(3 * h + 2) * DH_B] = k.astype(bf16)
        qkv_scr[:, (3 * h + 2) * DH_B:(3 * h + 3) * DH_B] = v.astype(bf16)
    yield
    gates = _dot(qkv_scr[...], wg_ref[...]) + bg_ref[...]
    ig = gates[:, 0:LANES]
    lf = jax.nn.log_sigmoid(gates[:, LANES:2 * LANES])
    yield
    ri = lax.broadcasted_iota(jnp.int32, (L, L), 0)
    ci = lax.broadcasted_iota(jnp.int32, (L, L), 1)
    causal = ri >= ci
    tri = jnp.where(causal, 1.0, 0.0).astype(bf16)
    gn = gn_ref[...]
    skip = skip_ref[...]
    heads = range(H_B)
    stack = lambda parts: jnp.concatenate(parts, axis=0)
    gn_st = stack([jnp.broadcast_to(gn[:, h * DH_B:(h + 1) * DH_B], (L, DH_B)) for h in heads])
    ones_blk = jnp.ones((L, DH_B), bf16)
    C_old = [C_scr[h] for h in heads]
    n_old = [n_scr[h:h + 1, :] for h in heads]
    m_old = [m_scr[h:h + 1, 0:1] for h in heads]
    for c in range(T // L):
        rows = slice(c * L, (c + 1) * L)
        b_all = _dot_split_rhs(tri, lf[rows], 2)
        c_all = ig[rows] - b_all
        c_all_t = c_all.T
        yield
        q_bf = [qs[h][rows].astype(bf16) for h in heads]
        k_bf = [ks[h][rows].astype(bf16) for h in heads]
        v_bf = [vs[h][rows].astype(bf16) for h in heads]
        m_prev = stack([jnp.broadcast_to(m_old[h], (L, 1)) for h in heads])
        b_col = stack([b_all[:, h:h + 1] for h in heads])
        c_col = stack([c_all[:, h:h + 1] for h in heads])
        dm = stack([jnp.where(causal, c_all_t[h:h + 1, :], -jnp.inf) for h in heads])
        mx = jnp.maximum(m_prev, jnp.max(dm, axis=1, keepdims=True))
        yield
        sc = stack([_dot_nt(q_bf[h], k_bf[h]) for h in heads]) * jnp.exp(dm - mx)
        sc_bf = sc.astype(bf16)
        yield
        w_inter = jnp.exp(m_prev - mx)
        sv_aug = [_dot(sc_bf[h * L:(h + 1) * L], jnp.concatenate([v_bf[h], ones_blk], axis=1)) for h in heads]
        sv = stack([a[:, 0:DH_B] for a in sv_aug])
        s_sum = stack([a[:, DH_B:DH_B + 1] for a in sv_aug])
        qC = stack([_dot(q_bf[h], C_old[h].astype(bf16)) for h in heads])
        qn = stack([_dot_nt(q_bf[h], jnp.broadcast_to(n_old[h], (SUBLANES, DH_B)).astype(bf16))[:, 0:1]
                    for h in heads])
        yield
        num = sv + w_inter * qC
        den = s_sum + w_inter * qn
        hb = num / jnp.maximum(jnp.abs(den), jnp.exp(-(b_col + mx)))
        hb = _head_layernorm(hb, gn_st)
        yield
        mx_last = [mx[(h + 1) * L - 1:(h + 1) * L, :] for h in heads]
        g_in = jnp.exp(c_col - stack([jnp.broadcast_to(mx_last[h], (L, 1)) for h in heads]))
        C_new, n_new, m_new = [], [], []
        for h in heads:
            hs = slice(h * DH_B, (h + 1) * DH_B)
            hr = slice(h * L, (h + 1) * L)
            b_out = _sigmoid(zb[rows, hs]) * (hb[hr] + skip[:, hs] * cb[rows, hs])
            mix_scr[rows, hs] = b_out.astype(bf16)
            gk = g_in[hr] * ks[h][rows]
            decay = jnp.exp(m_old[h] - mx_last[h])
            C_new.append(decay * C_old[h] + _dot_tn(gk.astype(bf16), v_bf[h]))
            n_new.append(decay * n_old[h] + jnp.sum(gk, axis=0, keepdims=True))
            m_new.append(b_all[L - 1:L, h:h + 1] + mx_last[h])
        C_old, n_old, m_old = C_new, n_new, m_new
        yield
    for h in heads:
        C_scr[h] = C_old[h]
        n_scr[h:h + 1, :] = n_old[h]
        m_scr[h:h + 1, :] = jnp.broadcast_to(m_old[h], (1, LANES))

    ahist[pl.ds(A_HIST, T), :] = xa * _sigmoid(ga)
    yield
    caw = caw_ref[...]
    cab = cab_ref[...]
    first = A_HIST - (CONV_A - 1)

    for base in range(0, T, CONV_ROWS):
        for lo in range(0, W_A, LANES):
            blk = ahist[pl.ds(base, CONV_ROWS + A_HIST), lo:lo + LANES]
            acc = jnp.broadcast_to(cab[:, lo:lo + LANES], (CONV_ROWS, LANES))
            for rr in range(SUBLANES):
                n = CONV_ROWS if rr == 0 else CONV_ROWS + SUBLANES
                z = None
                for q in range(A_HIST // SUBLANES + 1):
                    j = SUBLANES * q + rr - first
                    if 0 <= j < CONV_A:
                        term = caw[j:j + 1, lo:lo + LANES] * blk[SUBLANES * q:SUBLANES * q + n, :]
                        z = term if z is None else z + term
                acc = acc + (z if rr == 0 else z[rr:rr + CONV_ROWS, :])
            conv_scr[pl.ds(base, CONV_ROWS), lo:lo + LANES] = acc
            yield
    a_out = _group_layernorm_silu(conv_scr[...], _group_mask_a(), lag_ref[...], lab_ref[...])
    y = _dot(a_out.astype(bf16), wout_ref[0:W_A, :])
    yield

    chist[pl.ds(S_HIST, T), :] = xc
    ccw = ccw_ref[...]
    xcv = ccb_ref[...] + ccw[3:4, :] * xc
    for j in range(CONV_C - 1):
        xcv = xcv + ccw[j:j + 1, :] * chist[pl.ds(S_HIST - (CONV_C - 1) + j, T), :]
    yield
    a_t, u_t = _rglru_gates(xcv, wrg_ref[...], brg_ref[...], lam_ref[...])
    yield
    row = lax.broadcasted_iota(jnp.int32, (T, W_C), 0)
    d = 1
    while d < T:
        if d < SUBLANES:
            a_sh = jnp.where(row >= d, pltpu.roll(a_t, d, axis=0), 1.0)
            u_sh = jnp.where(row >= d, pltpu.roll(u_t, d, axis=0), 0.0)
        else:
            a_sh = jnp.concatenate([jnp.ones((d, W_C), f32), a_t[:T - d]], axis=0)
            u_sh = jnp.concatenate([jnp.zeros((d, W_C), f32), u_t[:T - d]], axis=0)
        u_t = a_t * u_sh + u_t
        a_t = a_t * a_sh
        d *= 2
    hc = a_t * h_scr[...] + u_t
    h_scr[...] = hc[T - 1:T, :]
    yield
    y = y + _dot((hc * jax.nn.gelu(gc)).astype(bf16), wout_ref[W_A + W_B:, :])

    yield
    x1_ref[...] = x + (y + _dot(mix_scr[...], wout_ref[W_A:W_A + W_B, :]))
    yield

    ahist[0:A_HIST, :] = ahist[pl.ds(T, A_HIST), :]
    bhist[0:S_HIST, :] = bhist[pl.ds(T, S_HIST), :]
    chist[0:S_HIST, :] = chist[pl.ds(T, S_HIST), :]

    @pl.when(s == last)
    def _():
        sa_ref[...] = ahist[pl.ds(first, CONV_A - 1), :]
        sb_ref[...] = bhist[pl.ds(S_HIST - (CONV_B - 1), CONV_B - 1), :]
        sc_ref[...] = chist[pl.ds(S_HIST - (CONV_C - 1), CONV_C - 1), :]
        sC_ref[...] = C_scr[...]
        sn_ref[...] = n_scr[0:H_B, :]
        sm_ref[...] = m_scr[0:H_B, :]
        sh_ref[...] = h_scr[...]


def _full(shape):
    nd = len(shape)
    return pl.BlockSpec(shape, lambda *_: (0,) * nd)


def _layer(w, l):
    nd = w.ndim - 1
    return pl.BlockSpec((None,) + w.shape[1:], lambda *_: (l,) + (0,) * nd)


def _mix_prompt(x, lw, l):
    B, S, _ = x.shape
    T = T_MIX
    weights = [lw[k] for k in ("norm_mix_g", "w_in", "conv_a_w", "conv_a_b", "ln_a_g", "ln_a_b", "conv_b_w", "conv_b_b",
                               "w_q", "w_k", "w_v", "w_gates", "b_gates", "gn_b_g", "skip_b", "conv_c_w", "conv_c_b",
                               "w_rg", "b_rg", "lam", "w_out")]
    out_shape = (
        jax.ShapeDtypeStruct((B, S, D_MODEL), f32),
        jax.ShapeDtypeStruct((B, CONV_A - 1, W_A), f32),
        jax.ShapeDtypeStruct((B, CONV_B - 1, W_B), f32),
        jax.ShapeDtypeStruct((B, H_B, DH_B, DH_B), f32),
        jax.ShapeDtypeStruct((B, H_B, DH_B), f32),
        jax.ShapeDtypeStruct((B, H_B, LANES), f32),
        jax.ShapeDtypeStruct((B, CONV_C - 1, W_C), f32),
        jax.ShapeDtypeStruct((B, 1, W_C), f32),
    )
    G = MIX_G
    per_b = lambda shp: pl.BlockSpec((G,) + shp, lambda b, s: (b,) + (0,) * len(shp))
    out_specs = (
        pl.BlockSpec((G, T, D_MODEL), lambda b, s: (b, s, 0)),
        per_b((CONV_A - 1, W_A)), per_b((CONV_B - 1, W_B)), per_b((H_B, DH_B, DH_B)), per_b((H_B, DH_B)),
        per_b((H_B, LANES)), per_b((CONV_C - 1, W_C)), per_b((1, W_C)),
    )
    scratch = [
        pltpu.VMEM((G, A_HIST + T, W_A), f32), pltpu.VMEM((G, S_HIST + T, W_B), f32),
        pltpu.VMEM((G, S_HIST + T, W_C), f32), pltpu.VMEM((G, T, W_A), f32),
        pltpu.VMEM((G, H_B, DH_B, DH_B), f32), pltpu.VMEM((G, SUBLANES, DH_B), f32),
        pltpu.VMEM((G, SUBLANES, LANES), f32), pltpu.VMEM((G, 1, W_C), f32),
        pltpu.VMEM((G, T, 3 * W_B), bf16), pltpu.VMEM((G, T, W_B), bf16),
    ]
    assert len(weights) == N_MIX_WEIGHTS and B % G == 0
    return pl.pallas_call(
        _mix_prompt_body,
        grid=(B // G, S // T),
        in_specs=[pl.BlockSpec((G, T, D_MODEL), lambda b, s: (b, s, 0))] + [_layer(w, l) for w in weights],
        out_specs=out_specs,
        out_shape=out_shape,
        scratch_shapes=scratch,
        compiler_params=pltpu.CompilerParams(dimension_semantics=("arbitrary", "arbitrary"),
                                             vmem_limit_bytes=VMEM_LIMIT),
        name="mix_prompt",
    )(x, *weights)


def _step_conv(st_ref, w, bias, x_new, st_out):
    k1 = st_ref.shape[0]
    y = bias + w[k1:k1 + 1, :] * x_new
    for j in range(k1):
        y = y + w[j:j + 1, :] * st_ref[j]
        if j > 0:
            st_out[j - 1] = st_ref[j]
    st_out[k1 - 1] = x_new
    return y


def _step_pre_body(x_ref, g_ref, win_ref, sa_ref, caw_ref, cab_ref, lag_ref, lab_ref, sb_ref, cbw_ref, cbb_ref,
                   wq_ref, wk_ref, wv_ref, wg_ref, bg_ref, n_ref, m_ref, sc_ref, ccw_ref, ccb_ref,
                   wrg_ref, brg_ref, lam_ref, h_ref, skip_ref,
                   sa_o, sb_o, sc_o, n_o, m_o, h_o, q_o, gk_o, v_o, dec_o, den_o, sv_o, zsig_o, skcb_o, ac_o,
                   qkv_scr):
    x = x_ref[...]
    hn = _rmsnorm(x, g_ref[...])
    u = _dot(hn.astype(bf16), win_ref[...])
    b0, c0 = 2 * W_A, 2 * W_A + 2 * W_B
    xa, ga = u[:, 0:W_A], u[:, W_A:b0]
    xb, zb = u[:, b0:b0 + W_B], u[:, b0 + W_B:c0]
    xc, gc = u[:, c0:c0 + W_C], u[:, c0 + W_C:]

    a_new = xa * _sigmoid(ga)
    conv = _step_conv(sa_ref, caw_ref[...], cab_ref[...], a_new, sa_o)
    ac_o[:, 0:W_A] = _group_layernorm_silu(conv, _group_mask_a(), lag_ref[...], lab_ref[...])

    cb = _step_conv(sb_ref, cbw_ref[...], cbb_ref[...], xb, sb_o)
    cb = cb * _sigmoid(cb)
    cb_bf = cb.astype(bf16)
    xb_bf = xb.astype(bf16)
    qs, ks, vs = [], [], []
    for h in range(H_B):
        hs = slice(h * DH_B, (h + 1) * DH_B)
        q = _dot(cb_bf[:, hs], wq_ref[h])
        k = _dot(cb_bf[:, hs], wk_ref[h]) * (DH_B ** -0.5)
        v = _dot(xb_bf[:, hs], wv_ref[h])
        qs.append(q)
        ks.append(k)
        vs.append(v)
        qkv_scr[:, (3 * h) * DH_B:(3 * h + 1) * DH_B] = q.astype(bf16)
        qkv_scr[:, (3 * h + 1) * DH_B:(3 * h + 2) * DH_B] = k.astype(bf16)
        qkv_scr[:, (3 * h + 2) * DH_B:(3 * h + 3) * DH_B] = v.astype(bf16)
    gates = _dot(qkv_scr[...], wg_ref[...]) + bg_ref[...]
    ig = gates[:, 0:LANES]
    lf = jax.nn.log_sigmoid(gates[:, LANES:2 * LANES])
    m0 = m_ref[...]
    m_t = jnp.maximum(lf + m0, ig)
    g_in = jnp.exp(ig - m_t)
    decay = jnp.exp(lf + m0 - m_t)
    lane = lax.broadcasted_iota(jnp.int32, ig.shape, 1)
    qk = jnp.zeros(ig.shape, f32)
    qn = jnp.zeros(ig.shape, f32)
    for h in range(H_B):
        hs = slice(h * DH_B, (h + 1) * DH_B)
        n_h = n_ref[:, h, :]
        qk = jnp.where(lane == h, jnp.sum(qs[h] * ks[h], axis=-1, keepdims=True), qk)
        qn = jnp.where(lane == h, jnp.sum(qs[h] * n_h, axis=-1, keepdims=True), qn)
        gk = g_in[:, h:h + 1] * ks[h]
        n_o[:, h, :] = decay[:, h:h + 1] * n_h + gk
        q_o[:, hs] = qs[h]
        gk_o[:, hs] = gk
        v_o[:, hs] = vs[h]
    s_t = qk * g_in
    den = s_t + decay * qn
    den_o[...] = jnp.maximum(jnp.abs(den), jnp.exp(-m_t))
    dec_o[...] = decay
    m_o[...] = m_t
    for h in range(H_B):
        hs = slice(h * DH_B, (h + 1) * DH_B)
        sv_o[:, hs] = s_t[:, h:h + 1] * vs[h]
    zsig_o[...] = _sigmoid(zb)
    skcb_o[...] = skip_ref[...] * cb

    xcv = _step_conv(sc_ref, ccw_ref[...], ccb_ref[...], xc, sc_o)
    a_t, u_t = _rglru_gates(xcv, wrg_ref[...], brg_ref[...], lam_ref[...])
    hc = a_t * h_ref[...] + u_t
    h_o[...] = hc
    ac_o[:, W_A:] = hc * jax.nn.gelu(gc)


def _step_read_body(C_ref, qT_ref, qc_ref):
    for bb in range(STEP_BB):
        for h in range(H_B):
            hs = slice(h * DH_B, (h + 1) * DH_B)
            qc_ref[bb:bb + 1, hs] = jnp.sum(C_ref[bb, h] * qT_ref[h, :, bb:bb + 1], axis=0, keepdims=True)


def _step_state_body(C_ref, qT_ref, gkT_ref, v_ref, dec_ref, Cn_ref, qc_ref):
    for bb in range(STEP_BB):
        for h in range(H_B):
            hs = slice(h * DH_B, (h + 1) * DH_B)
            C = C_ref[bb, h]
            qc_ref[bb:bb + 1, hs] = jnp.sum(C * qT_ref[h, :, bb:bb + 1], axis=0, keepdims=True)
            Cn_ref[bb, h] = dec_ref[bb:bb + 1, h:h + 1] * C + gkT_ref[h, :, bb:bb + 1] * v_ref[bb:bb + 1, hs]


def _to_cols(t):
    return t.reshape(t.shape[0] // STEP_BB, STEP_BB, H_B, DH_B).transpose(0, 2, 3, 1)


def _step_read(C_all, layer, q):
    Bs = q.shape[0]
    return pl.pallas_call(
        _step_read_body,
        grid=(Bs // STEP_BB,),
        in_specs=[pl.BlockSpec((None, STEP_BB, H_B, DH_B, DH_B), lambda i: (layer, i, 0, 0, 0)),
                  pl.BlockSpec((None, H_B, DH_B, STEP_BB), lambda i: (i, 0, 0, 0))],
        out_specs=pl.BlockSpec((STEP_BB, W_B), lambda i: (i, 0)),
        out_shape=jax.ShapeDtypeStruct((Bs, W_B), f32),
        compiler_params=pltpu.CompilerParams(dimension_semantics=("arbitrary",), vmem_limit_bytes=VMEM_LIMIT),
        name="step_read",
    )(C_all, _to_cols(q))


def _step_state(C_all, q, gk, v, dec):
    depth, Bs = q.shape[0], q.shape[1]
    cols = lambda t: jnp.stack([_to_cols(t[l]) for l in range(depth)])
    blk5 = pl.BlockSpec((None, STEP_BB, H_B, DH_B, DH_B), lambda l, i: (l, i, 0, 0, 0))
    colspec = pl.BlockSpec((None, None, H_B, DH_B, STEP_BB), lambda l, i: (l, i, 0, 0, 0))
    return pl.pallas_call(
        _step_state_body,
        grid=(depth, Bs // STEP_BB),
        in_specs=[blk5, colspec, colspec,
                  pl.BlockSpec((None, STEP_BB, W_B), lambda l, i: (l, i, 0)),
                  pl.BlockSpec((None, STEP_BB, LANES), lambda l, i: (l, i, 0))],
        out_specs=(blk5, pl.BlockSpec((None, STEP_BB, W_B), lambda l, i: (l, i, 0))),
        out_shape=(jax.ShapeDtypeStruct(C_all.shape, f32), jax.ShapeDtypeStruct((depth, Bs, W_B), f32)),
        compiler_params=pltpu.CompilerParams(dimension_semantics=("arbitrary", "arbitrary"),
                                             vmem_limit_bytes=VMEM_LIMIT),
        name="step_state",
    )(C_all, cols(q), cols(gk), v, dec)


def _step_post_body(x_ref, ac_ref, zsig_ref, skcb_ref, sv_ref, qc_ref, dec_ref, den_ref, gn_ref, wout_ref,
                    x1_ref, mix_scr):
    gn = gn_ref[...]
    dec = dec_ref[...]
    den = den_ref[...]
    mix_scr[:, 0:W_A] = ac_ref[:, 0:W_A].astype(bf16)
    mix_scr[:, W_A + W_B:] = ac_ref[:, W_A:].astype(bf16)
    for h in range(H_B):
        hs = slice(h * DH_B, (h + 1) * DH_B)
        num = sv_ref[:, hs] + dec[:, h:h + 1] * qc_ref[:, hs]
        hb = _head_layernorm(num / den[:, h:h + 1], gn[:, hs])
        mix_scr[:, W_A + h * DH_B:W_A + (h + 1) * DH_B] = (zsig_ref[:, hs] * (hb + skcb_ref[:, hs])).astype(bf16)
    x1_ref[...] = x_ref[...] + _dot(mix_scr[...], wout_ref[...])


def _step_pre(x, st, lw, l):
    buf_a, buf_b, n0, m0, buf_c, hc0 = st
    buf_a, buf_b, buf_c = (b.transpose(0, 2, 1, 3) for b in (buf_a, buf_b, buf_c))
    Bs = x.shape[0]
    m_in = jnp.pad(m0, ((0, 0), (0, 0), (0, LANES - H_B)))
    sds = lambda *shape: jax.ShapeDtypeStruct(shape, f32)
    W = lambda k: (lw[k], True)
    L = lambda a: (a, True)
    A = lambda a: (a, False)
    pre_in = [A(x), W("norm_mix_g"), W("w_in"), L(buf_a), W("conv_a_w"), W("conv_a_b"), W("ln_a_g"), W("ln_a_b"),
              L(buf_b), W("conv_b_w"), W("conv_b_b"), W("w_q"), W("w_k"), W("w_v"), W("w_gates"), W("b_gates"),
              L(n0), L(m_in), L(buf_c), W("conv_c_w"), W("conv_c_b"), W("w_rg"), W("b_rg"), W("lam"), L(hc0),
              W("skip_b")]
    out_shape = (sds(*buf_a.shape[1:]), sds(*buf_b.shape[1:]), sds(*buf_c.shape[1:]), sds(*n0.shape[1:]),
                 sds(Bs, LANES), sds(Bs, W_C),
                 sds(Bs, W_B), sds(Bs, W_B), sds(Bs, W_B), sds(Bs, LANES), sds(Bs, LANES), sds(Bs, W_B),
                 sds(Bs, W_B), sds(Bs, W_B), sds(Bs, W_A + W_C))
    (sa_n, sb_n, sc_n, n_n, m_n, h_n, q, gk, v, dec, den, sv, zsig, skcb, ac) = pl.pallas_call(
        _step_pre_body,
        grid=(1,),
        in_specs=[_layer(a, l) if is_w else _full(a.shape) for a, is_w in pre_in],
        out_specs=[_full(o.shape) for o in out_shape],
        out_shape=out_shape,
        scratch_shapes=[pltpu.VMEM((Bs, 3 * W_B), bf16)],
        compiler_params=pltpu.CompilerParams(dimension_semantics=("arbitrary",), vmem_limit_bytes=VMEM_LIMIT),
        name="step_pre",
    )(*[a for a, _ in pre_in])
    new = (sa_n, sb_n, n_n, m_n[:, :H_B], sc_n, h_n)
    return new, (q, gk, v, dec), (ac, zsig, skcb, sv, dec, den)


def _step_post(x, post_in, qc, lw, l):
    ac, zsig, skcb, sv, dec, den = post_in
    acts = [x, ac, zsig, skcb, sv, qc, dec, den]
    return pl.pallas_call(
        _step_post_body,
        grid=(1,),
        in_specs=[_full(a.shape) for a in acts] + [_layer(lw["gn_b_g"], l), _layer(lw["w_out"], l)],
        out_specs=_full(x.shape),
        out_shape=jax.ShapeDtypeStruct(x.shape, f32),
        scratch_shapes=[pltpu.VMEM(x.shape, bf16)],
        compiler_params=pltpu.CompilerParams(dimension_semantics=("arbitrary",), vmem_limit_bytes=VMEM_LIMIT),
        name="step_post",
    )(*acts, lw["gn_b_g"], lw["w_out"])


def _swiglu(hb, wg_ref, wu_ref, wd_ref):
    f = wd_ref.shape[0]
    main = f // MXU_WIDTH * MXU_WIDTH
    glu = lambda gt, up: (gt * _sigmoid(gt) * up).astype(bf16)
    w = lambda ref, cols: ref[:, cols] if ref.dtype == bf16 else ref[:, cols].astype(bf16)
    if main == f:
        act = glu(_dot(hb, w(wg_ref, slice(None))), _dot(hb, w(wu_ref, slice(None))))
    else:
        lo, hi = slice(0, main), slice(main, f)
        tail = _dot(hb, jnp.concatenate([w(wg_ref, hi), w(wu_ref, hi)], axis=1))
        act = jnp.concatenate([glu(_dot(hb, w(wg_ref, lo)), _dot(hb, w(wu_ref, lo))),
                               glu(tail[:, :f - main], tail[:, f - main:])], axis=1)
    return _dot(act, wd_ref[...])


def _ffn_body(*refs, moe, final_norm):
    it = iter(refs)
    x_ref, g_ref, wg_ref, wu_ref, wd_ref = next(it), next(it), next(it), next(it), next(it)
    wr_ref = next(it) if moe else None
    br_ref = next(it) if moe else None
    gf_ref = next(it) if final_norm else None
    o_ref, hn_scr, acc_scr = next(it), next(it), next(it)
    comb_scr = next(it) if moe else None
    j = pl.program_id(1)

    @pl.when(j == 0)
    def _():
        hn = _rmsnorm(x_ref[...], g_ref[...])
        hn_scr[...] = hn.astype(bf16)
        acc_scr[...] = jnp.zeros(acc_scr.shape, f32)
        if moe:
            lane, i1, i2, g1, g2 = _top2(_router_logits(hn, wr_ref[...], br_ref[...]))
            comb_scr[...] = jnp.where(lane == i1, g1, 0.0) + jnp.where(lane == i2, g2, 0.0)

    y = _swiglu(hn_scr[...], wg_ref, wu_ref, wd_ref)
    if moe:
        lane = lax.broadcasted_iota(jnp.int32, comb_scr.shape, 1)
        y = y * jnp.sum(jnp.where(lane == j, comb_scr[...], 0.0), axis=-1, keepdims=True)
    acc_scr[...] += y

    @pl.when(j == pl.num_programs(1) - 1)
    def _():
        out = x_ref[...] + acc_scr[...]
        if final_norm:
            out = _rmsnorm(out, gf_ref[...])
        o_ref[...] = out


def _ffn(x, g, wg, wu, wd, router=None, final_g=None):
    N = x.shape[0]
    E, F = wd.shape[0], wd.shape[1]
    tm = min(TM_FFN, N)
    moe = router is not None
    final_norm = final_g is not None
    ins = [x, g, wg, wu, wd]
    once = dict(pipeline_mode=pl.Buffered(1)) if E == 1 else {}
    in_specs = [pl.BlockSpec((tm, D_MODEL), lambda i, j: (i, 0)), _full(g.shape),
                pl.BlockSpec((None, D_MODEL, F), lambda i, j: (j, 0, 0), **once),
                pl.BlockSpec((None, D_MODEL, F), lambda i, j: (j, 0, 0), **once),
                pl.BlockSpec((None, F, D_MODEL), lambda i, j: (j, 0, 0), **once)]
    scratch = [pltpu.VMEM((tm, D_MODEL), bf16), pltpu.VMEM((tm, D_MODEL), f32)]
    if moe:
        ins += list(router)
        in_specs += [_full(router[0].shape), _full(router[1].shape)]
        scratch.append(pltpu.VMEM((tm, LANES), f32))
    if final_norm:
        ins.append(final_g)
        in_specs.append(_full(final_g.shape))
    return pl.pallas_call(
        functools.partial(_ffn_body, moe=moe, final_norm=final_norm),
        grid=(N // tm, E),
        in_specs=in_specs,
        out_specs=pl.BlockSpec((tm, D_MODEL), lambda i, j: (i, 0)),
        out_shape=jax.ShapeDtypeStruct((N, D_MODEL), f32),
        scratch_shapes=scratch,
        compiler_params=pltpu.CompilerParams(dimension_semantics=("arbitrary", "arbitrary"),
                                             vmem_limit_bytes=VMEM_LIMIT),
        name="ffn_moe" if moe else "ffn_dense",
    )(*ins)


TM_MOE = 512
SEG_ALIGN = 16
SEG_SIZES = (512, 256, 128, 64, 32, 16)
MOE_ROWS = 2 * TM_MOE + N_EXPERTS * SEG_ALIGN
META_COLS = 3 * N_EXPERTS


def _seg_dma(src, src_off, dst, dst_off, nrows, sem, wait):
    done = jnp.int32(0)
    for size in SEG_SIZES:
        take = (nrows & size) != 0

        @pl.when(take)
        def _(done=done, size=size):
            cp = pltpu.make_async_copy(src.at[pl.ds(pl.multiple_of(src_off + done, SEG_ALIGN), size)],
                                       dst.at[pl.ds(pl.multiple_of(dst_off + done, SEG_ALIGN), size)], sem)
            if wait:
                cp.wait()
            else:
                cp.start()

        done = done + jnp.where(take, size, 0)


def _top2(logits):
    lane = lax.broadcasted_iota(jnp.int32, logits.shape, 1).astype(f32)
    lg = jnp.where(lane < N_EXPERTS, logits, -jnp.inf)
    m1 = jnp.max(lg, axis=-1, keepdims=True)
    i1 = jnp.min(jnp.where(lg == m1, lane, float(LANES)), axis=-1, keepdims=True)
    lg2 = jnp.where(lane == i1, -jnp.inf, lg)
    m2 = jnp.max(lg2, axis=-1, keepdims=True)
    i2 = jnp.min(jnp.where(lg2 == m2, lane, float(LANES)), axis=-1, keepdims=True)
    e2 = jnp.exp(m2 - m1)
    return lane, i1, i2, 1.0 / (1.0 + e2), e2 / (1.0 + e2)


def _router_logits(hn, wr, br):
    h1, h2 = _split_terms(hn, 2)
    w1, w2 = _split_terms(wr, 2)
    return _dot(h1, w1) + (_dot(h1, w2) + _dot(h2, w1)) + br


def _expert_ranks(lane, i1, i2):
    TM = lane.shape[0]
    sel1 = lane == i1
    sel2 = lane == i2
    sel = jnp.where(sel1, 1.0, jnp.where(sel2, 1.0, 0.0))
    earlier = lax.broadcasted_iota(jnp.int32, (TM, TM), 0) > lax.broadcasted_iota(jnp.int32, (TM, TM), 1)
    rank = _dot(jnp.where(earlier, 1.0, 0.0).astype(bf16), sel.astype(bf16))
    return sel1, sel2, rank, rank[TM - 1:TM, :] + sel[TM - 1:TM, :]


def _seg_pad(c):
    return ((c + (SEG_ALIGN - 1)) // SEG_ALIGN) * SEG_ALIGN


def _moe_count_body(x_ref, g_ref, wr_ref, br_ref, route_ref, cnt_ref):
    i = pl.program_id(0)
    hn = _rmsnorm(x_ref[...], g_ref[...])
    lane, i1, i2, g1, g2 = _top2(_router_logits(hn, wr_ref[...], br_ref[...]))
    route_ref[...] = jnp.where(lane == 0, i1, jnp.where(lane == 1, i2, jnp.where(lane == 2, g1,
                                                                                 jnp.where(lane == 3, g2, 0.0))))
    cnt = jnp.sum(jnp.where(lane == i1, 1.0, jnp.where(lane == i2, 1.0, 0.0)), axis=0, keepdims=True)
    for e in range(N_EXPERTS):
        cnt_ref[i, e] = cnt[0, e].astype(jnp.int32)


def _moe_route_body(cnt_ref, x_ref, g_ref, route_ref, xs_hbm, gs_hbm, rinfo_ref, meta_ref, te_ref, nv_ref,
                    xperm, gperm, pos_smem, sem):
    TM, R = TM_MOE, MOE_ROWS
    i = pl.program_id(0)
    nt = pl.num_programs(0)
    n_tiles = te_ref.shape[0]

    @pl.when(i == 0)
    def _():
        base = jnp.int32(0)
        for e in range(N_EXPERTS):
            rows = lax.fori_loop(0, nt, lambda t, a, e=e: a + _seg_pad(cnt_ref[t, e]), jnp.int32(0))
            first_tile = base // TM
            tiles = (rows + (TM - 1)) // TM

            def mark(k, c, e=e, first_tile=first_tile):
                te_ref[first_tile + k] = e
                return c

            lax.fori_loop(0, tiles, mark, 0)
            pos_smem[e] = base
            base = base + tiles * TM
        nv = base // TM
        nv_ref[0] = nv

        def mark_rest(k, c):
            te_ref[k] = N_EXPERTS - 1
            return c

        lax.fori_loop(nv, n_tiles, mark_rest, 0)

    hn = _rmsnorm(x_ref[...], g_ref[...])
    route = route_ref[...]
    i1, i2, g1, g2 = route[:, 0:1], route[:, 1:2], route[:, 2:3], route[:, 3:4]
    lane = lax.broadcasted_iota(jnp.int32, route.shape, 1).astype(f32)
    sel1, sel2, rank, _ = _expert_ranks(lane, i1, i2)
    seg_rows, seg_off = [], []
    acc = jnp.int32(0)
    for e in range(N_EXPERTS):
        seg_rows.append(_seg_pad(cnt_ref[i, e]))
        seg_off.append(acc)
        acc = acc + seg_rows[e]
    lane1 = lax.broadcasted_iota(jnp.int32, (1, LANES), 1)
    offv = jnp.zeros((1, LANES), f32)
    for e in range(N_EXPERTS):
        offv = jnp.where(lane1 == e, seg_off[e].astype(f32), offv)
    dest = rank + offv
    d1 = jnp.sum(jnp.where(sel1, dest, 0.0), axis=-1, keepdims=True)
    d2 = jnp.sum(jnp.where(sel2, dest, 0.0), axis=-1, keepdims=True)
    rinfo = jnp.where(lane == 0, d1, jnp.where(lane == 1, d2, jnp.where(lane == 2, g1, jnp.where(lane == 3, g2, 0.0))))
    rinfo_ref[...] = rinfo
    rinfo_t = rinfo.T
    d1r, d2r, g1r, g2r = rinfo_t[0:1, :], rinfo_t[1:2, :], rinfo_t[2:3, :], rinfo_t[3:4, :]
    riota = lax.broadcasted_iota(jnp.int32, (R, TM), 0).astype(f32)
    hit1 = riota == d1r
    hit2 = riota == d2r
    perm = jnp.where(hit1, 1.0, jnp.where(hit2, 1.0, 0.0)).astype(bf16)
    xperm[i % 2] = _dot(perm, hn.astype(bf16)).astype(bf16)
    gcol = jnp.sum(jnp.where(hit1, g1r, 0.0) + jnp.where(hit2, g2r, 0.0), axis=1, keepdims=True)
    gperm[i % 2] = jnp.broadcast_to(gcol, (R, LANES))

    for e in range(N_EXPERTS):
        pos = pos_smem[e]
        meta_ref[i, e] = pos
        meta_ref[i, N_EXPERTS + e] = seg_off[e]
        meta_ref[i, 2 * N_EXPERTS + e] = seg_rows[e]
        pos_smem[e] = pos + seg_rows[e]

    def copies(tile, wait):
        for slot in range(2):
            @pl.when(tile % 2 == slot)
            def _(slot=slot):
                for e in range(N_EXPERTS):
                    src_off, dst_off, n = meta_ref[tile, N_EXPERTS + e], meta_ref[tile, e], meta_ref[tile, 2 * N_EXPERTS + e]
                    _seg_dma(xperm.at[slot], src_off, xs_hbm, dst_off, n, sem.at[2 * slot], wait)
                    _seg_dma(gperm.at[slot], src_off, gs_hbm, dst_off, n, sem.at[2 * slot + 1], wait)

    copies(i, False)
    pl.when(i > 0)(lambda: copies(i - 1, True))

    @pl.when(i == nt - 1)
    def _():
        copies(i, True)
        xz, gz = xperm.at[0], gperm.at[0]
        xz[0:TM, :] = jnp.zeros((TM, D_MODEL), bf16)
        gz[0:TM, :] = jnp.zeros((TM, LANES), f32)
        pads = []
        for e in range(N_EXPERTS):
            pos = pos_smem[e]
            pads.append((-pos) & (TM - 1))
            _seg_dma(xz, 0, xs_hbm, pos, pads[e], sem.at[0], False)
            _seg_dma(gz, 0, gs_hbm, pos, pads[e], sem.at[1], False)
        for e in range(N_EXPERTS):
            _seg_dma(xz, 0, xs_hbm, pos_smem[e], pads[e], sem.at[0], True)
            _seg_dma(gz, 0, gs_hbm, pos_smem[e], pads[e], sem.at[1], True)

        def zero_tile(k, c):
            row = pl.multiple_of(k * TM, TM)
            cx = pltpu.make_async_copy(xz.at[pl.ds(0, TM)], xs_hbm.at[pl.ds(row, TM)], sem.at[0])
            cg = pltpu.make_async_copy(gz.at[pl.ds(0, TM)], gs_hbm.at[pl.ds(row, TM)], sem.at[1])
            cx.start()
            cg.start()
            cx.wait()
            cg.wait()
            return c

        lax.fori_loop(nv_ref[0], n_tiles, zero_tile, 0)


def _moe_ffn_body(te_ref, nv_ref, xs_ref, gs_ref, wg_ref, wu_ref, wd_ref, ys_ref, wg_bf, wu_bf):
    i = pl.program_id(0)

    @pl.when((i == 0) | (te_ref[i] != te_ref[jnp.maximum(i - 1, 0)]))
    def _():
        wg_bf[...] = wg_ref[...].astype(bf16)
        wu_bf[...] = wu_ref[...].astype(bf16)

    @pl.when(i < nv_ref[0])
    def _():
        ys_ref[...] = (_swiglu(xs_ref[...], wg_bf, wu_bf, wd_ref) * gs_ref[:, 0:1]).astype(bf16)

    @pl.when(i >= nv_ref[0])
    def _():
        ys_ref[...] = jnp.zeros(ys_ref.shape, bf16)


def _moe_combine_body(*refs, final_norm):
    it = iter(refs)
    meta_ref, x_ref, rinfo_ref, ys_hbm = next(it), next(it), next(it), next(it)
    gf_ref = next(it) if final_norm else None
    o_ref, yperm, sem = next(it), next(it), next(it)
    TM, R = TM_MOE, MOE_ROWS
    i = pl.program_id(0)
    nt = pl.num_programs(0)

    def fetch(tile, slot, wait):
        for e in range(N_EXPERTS):
            _seg_dma(ys_hbm, meta_ref[tile, e], yperm.at[slot], meta_ref[tile, N_EXPERTS + e],
                     meta_ref[tile, 2 * N_EXPERTS + e], sem.at[slot], wait)

    def both_slots(tile, wait):
        for slot in range(2):
            pl.when(tile % 2 == slot)(lambda slot=slot: fetch(tile, slot, wait))

    @pl.when(i == 0)
    def _():
        yperm[...] = jnp.zeros(yperm.shape, bf16)
        fetch(0, 0, False)

    @pl.when(i + 1 < nt)
    def _():
        both_slots(i + 1, False)

    rinfo = rinfo_ref[...]
    ciota = lax.broadcasted_iota(jnp.int32, (TM, R), 1).astype(f32)
    unperm = jnp.where(ciota == rinfo[:, 0:1], 1.0, jnp.where(ciota == rinfo[:, 1:2], 1.0, 0.0)).astype(bf16)
    both_slots(i, True)
    out = x_ref[...] + _dot(unperm, yperm[i % 2])
    if final_norm:
        out = _rmsnorm(out, gf_ref[...])
    o_ref[...] = out


def _moe_prompt(x, g, wg, wu, wd, wr, br, final_g):
    N = x.shape[0]
    TM = TM_MOE
    nt = N // TM
    F = wd.shape[1]
    n_tiles = -(-(2 * N + nt * N_EXPERTS * (SEG_ALIGN - 1) + N_EXPERTS * (TM - 1)) // TM)
    rows = n_tiles * TM
    smem = pl.BlockSpec(memory_space=pltpu.SMEM)
    arb = pltpu.CompilerParams(dimension_semantics=("arbitrary",), vmem_limit_bytes=VMEM_LIMIT)
    route, cnt = pl.pallas_call(
        _moe_count_body,
        grid=(nt,),
        in_specs=[pl.BlockSpec((TM, D_MODEL), lambda i: (i, 0)), _full(g.shape), _full(wr.shape), _full(br.shape)],
        out_specs=[pl.BlockSpec((TM, LANES), lambda i: (i, 0)), smem],
        out_shape=[jax.ShapeDtypeStruct((N, LANES), f32), jax.ShapeDtypeStruct((nt, N_EXPERTS), jnp.int32)],
        compiler_params=arb,
        name="moe_count",
    )(x, g, wr, br)

    xs, gs, rinfo, meta, te, nv = pl.pallas_call(
        _moe_route_body,
        grid_spec=pltpu.PrefetchScalarGridSpec(
            num_scalar_prefetch=1,
            grid=(nt,),
            in_specs=[pl.BlockSpec((TM, D_MODEL), lambda i, c: (i, 0)), pl.BlockSpec(g.shape, lambda i, c: (0, 0)),
                      pl.BlockSpec((TM, LANES), lambda i, c: (i, 0))],
            out_specs=[pl.BlockSpec(memory_space=pl.ANY), pl.BlockSpec(memory_space=pl.ANY),
                       pl.BlockSpec((TM, LANES), lambda i, c: (i, 0)), smem, smem, smem],
            scratch_shapes=[pltpu.VMEM((2, MOE_ROWS, D_MODEL), bf16), pltpu.VMEM((2, MOE_ROWS, LANES), f32),
                            pltpu.SMEM((N_EXPERTS,), jnp.int32), pltpu.SemaphoreType.DMA((4,))],
        ),
        out_shape=[jax.ShapeDtypeStruct((rows, D_MODEL), bf16), jax.ShapeDtypeStruct((rows, LANES), f32),
                   jax.ShapeDtypeStruct((N, LANES), f32), jax.ShapeDtypeStruct((nt, META_COLS), jnp.int32),
                   jax.ShapeDtypeStruct((n_tiles,), jnp.int32), jax.ShapeDtypeStruct((1,), jnp.int32)],
        compiler_params=arb,
        name="moe_route",
    )(cnt, x, g, route)

    ys = pl.pallas_call(
        _moe_ffn_body,
        grid_spec=pltpu.PrefetchScalarGridSpec(
            num_scalar_prefetch=2,
            grid=(n_tiles,),
            in_specs=[pl.BlockSpec((TM, D_MODEL), lambda i, te, nv: (i, 0)),
                      pl.BlockSpec((TM, LANES), lambda i, te, nv: (i, 0)),
                      pl.BlockSpec((None, D_MODEL, F), lambda i, te, nv: (te[i], 0, 0)),
                      pl.BlockSpec((None, D_MODEL, F), lambda i, te, nv: (te[i], 0, 0)),
                      pl.BlockSpec((None, F, D_MODEL), lambda i, te, nv: (te[i], 0, 0))],
            out_specs=pl.BlockSpec((TM, D_MODEL), lambda i, te, nv: (i, 0)),
            scratch_shapes=[pltpu.VMEM((D_MODEL, F), bf16), pltpu.VMEM((D_MODEL, F), bf16)],
        ),
        out_shape=jax.ShapeDtypeStruct((rows, D_MODEL), bf16),
        compiler_params=arb,
        name="moe_ffn",
    )(te, nv, xs, gs, wg, wu, wd)

    final_norm = final_g is not None
    ins = [meta, x, rinfo, ys] + ([final_g] if final_norm else [])
    in_specs = [pl.BlockSpec((TM, D_MODEL), lambda i, m: (i, 0)), pl.BlockSpec((TM, LANES), lambda i, m: (i, 0)),
                pl.BlockSpec(memory_space=pl.ANY)]
    if final_norm:
        in_specs.append(pl.BlockSpec(final_g.shape, lambda i, m: (0, 0)))
    return pl.pallas_call(
        functools.partial(_moe_combine_body, final_norm=final_norm),
        grid_spec=pltpu.PrefetchScalarGridSpec(
            num_scalar_prefetch=1,
            grid=(nt,),
            in_specs=in_specs,
            out_specs=pl.BlockSpec((TM, D_MODEL), lambda i, m: (i, 0)),
            scratch_shapes=[pltpu.VMEM((2, MOE_ROWS, D_MODEL), bf16), pltpu.SemaphoreType.DMA((2,))],
        ),
        out_shape=jax.ShapeDtypeStruct((N, D_MODEL), f32),
        compiler_params=pltpu.CompilerParams(dimension_semantics=("arbitrary",), vmem_limit_bytes=VMEM_LIMIT),
        name="moe_combine",
    )(*ins)


def _block_diag(w):
    d, h, di, do = w.shape
    return jnp.einsum("lhij,hg->lhigj", w, jnp.eye(h, dtype=w.dtype)).reshape(d, h * di, h * do)


def _mixer_weights(p):
    row = lambda v: v[:, None, :]
    lanes = lambda w: jnp.pad(w, [(0, 0)] * (w.ndim - 1) + [(0, LANES - w.shape[-1])])
    return dict(
        norm_mix_g=row(p["norm_mix_g"]), w_in=p["w_in"].astype(bf16),
        conv_a_w=p["conv_a_w"], conv_a_b=row(p["conv_a_b"]), ln_a_g=row(p["ln_a_g"]), ln_a_b=row(p["ln_a_b"]),
        conv_b_w=p["conv_b_w"], conv_b_b=row(p["conv_b_b"]),
        w_q=p["w_q"].astype(bf16), w_k=p["w_k"].astype(bf16), w_v=p["w_v"].astype(bf16),
        w_gates=jnp.concatenate([lanes(p["w_ig"]), lanes(p["w_fg"])], axis=-1).astype(bf16),
        b_gates=row(jnp.concatenate([lanes(p["b_ig"]), lanes(p["b_fg"])], axis=-1)),
        gn_b_g=row(p["gn_b_g"]), skip_b=row(p["skip_b"]),
        conv_c_w=p["conv_c_w"], conv_c_b=row(p["conv_c_b"]),
        w_rg=jnp.concatenate([_block_diag(p["w_ra"]), _block_diag(p["w_ix"])], axis=-1).astype(bf16),
        b_rg=row(jnp.concatenate([p["b_ra"], p["b_ix"]], axis=-1)), lam=row(p["lam"]),
        w_out=p["w_out"].astype(bf16),
    )


def kernel(x_prompt, x_sample, state_conv_a, state_conv_b, state_mlstm_C, state_mlstm_n, state_mlstm_m, state_conv_c, state_rglru_h, norm_mix_g, w_in, conv_a_w, conv_a_b, ln_a_g, ln_a_b, conv_b_w, conv_b_b, w_q, w_k, w_v, w_ig, b_ig, w_fg, b_fg, gn_b_g, skip_b, conv_c_w, conv_c_b, w_ra, b_ra, w_ix, b_ix, lam, w_out, norm_ffn_g, w_gate, w_up, w_down, w_router, b_router, we_gate, we_up, we_down, norm_final_g):
    p = dict(norm_mix_g=norm_mix_g, w_in=w_in, conv_a_w=conv_a_w, conv_a_b=conv_a_b, ln_a_g=ln_a_g, ln_a_b=ln_a_b,
             conv_b_w=conv_b_w, conv_b_b=conv_b_b, w_q=w_q, w_k=w_k, w_v=w_v, w_ig=w_ig, b_ig=b_ig, w_fg=w_fg,
             b_fg=b_fg, gn_b_g=gn_b_g, skip_b=skip_b, conv_c_w=conv_c_w, conv_c_b=conv_c_b, w_ra=w_ra, b_ra=b_ra,
             w_ix=w_ix, b_ix=b_ix, lam=lam, w_out=w_out)
    depth = w_in.shape[0]
    Bp, S, _ = x_prompt.shape
    Bs = x_sample.shape[0]
    sample_states = (state_conv_a, state_conv_b, state_mlstm_C, state_mlstm_n, state_mlstm_m, state_conv_c,
                     state_rglru_h)
    xp = x_prompt
    xs = x_sample.reshape(Bs, D_MODEL)
    new_p, new_s, mem_ops = [], [], []
    C_s = None
    lw = _mixer_weights(p)
    for l in range(depth):
        xp, sa, sb, sC, sn, sm, sc, sh = _mix_prompt(xp, lw, l)
        new_p.append((sa, sb, sC, sn, sm[:, :, 0], sc, sh.reshape(Bp, W_C)))
        small, mem, post_in = _step_pre(xs, tuple(s for i, s in enumerate(sample_states) if i != 2), lw, l)
        new_s.append(small)
        mem_ops.append(mem)
        if l < depth - 1:
            qc = _step_read(state_mlstm_C, l, mem[0])
        else:
            C_s, qc_all = _step_state(state_mlstm_C, *(jnp.stack([m[i] for m in mem_ops]) for i in range(4)))
            qc = qc_all[l]
        xs = _step_post(xs, post_in, qc, lw, l)
        g = norm_ffn_g[l].reshape(1, -1)
        final_g = norm_final_g.reshape(1, -1) if l == depth - 1 else None
        j = l // 2
        if l % 2 == 0:
            wg, wu, wd = (w[j:j + 1].astype(bf16) for w in (w_gate, w_up, w_down))
            router = None
        else:
            wg, wu, wd = we_gate[j], we_up[j], we_down[j].astype(bf16)
            router = (jnp.pad(w_router[j], ((0, 0), (0, LANES - N_EXPERTS))),
                      jnp.pad(b_router[j], (0, LANES - N_EXPERTS)).reshape(1, -1))
        xp2 = xp.reshape(Bp * S, D_MODEL)
        if router is None:
            xp2 = _ffn(xp2, g, wg, wu, wd, None, final_g)
        else:
            xp2 = _moe_prompt(xp2, g, wg, wu, wd, router[0], router[1], final_g)
        xp = xp2.reshape(Bp, S, D_MODEL)
        xs = _ffn(xs, g, wg, wu, wd, router, final_g)
    stack = lambda states, i: jnp.stack([st[i] for st in states])
    out_p = tuple(stack(new_p, i) for i in range(7))
    sa_s, sb_s, n_s, m_s, sc_s, h_s = (stack(new_s, i) for i in range(6))
    sa_s, sb_s, sc_s = (s.transpose(0, 2, 1, 3) for s in (sa_s, sb_s, sc_s))
    return (xp, xs.reshape(Bs, 1, D_MODEL)) + out_p + (sa_s, sb_s, C_s, n_s, m_s, sc_s, h_s)
```

```python
import functools

import jax
import jax.numpy as jnp
from jax import lax
from jax.experimental import pallas as pl
from jax.experimental.pallas import tpu as pltpu

f32 = jnp.float32
bf16 = jnp.bfloat16

D_MODEL = 1024
W_A, W_B, W_C = 256, 512, 256
G_A = 4
CONV_A, CONV_B, CONV_C = 31, 4, 4
H_B, DH_B = 4, 128
RG_C = 8.0
N_EXPERTS = 8
NORM_EPS = 1e-6
LN_EPS = 1e-5

LANES = 128
SUBLANES = 8
MXU_WIDTH = 256
VMEM_LIMIT = 56 * 1024 * 1024

T_MIX = 256
MIX_G = 4
L_CHUNK = 128
CONV_ROWS = 128
A_HIST = 32
S_HIST = 8
TM_FFN = 512
STEP_BB = 32


def _dot(a, b):
    return jnp.dot(a, b, preferred_element_type=f32)


def _dot_nt(a, b):
    return lax.dot_general(a, b, (((1,), (1,)), ((), ())), preferred_element_type=f32)


def _dot_tn(a, b):
    return lax.dot_general(a, b, (((0,), (0,)), ((), ())), preferred_element_type=f32)


def _split_terms(x, terms):
    out = []
    r = x
    for _ in range(terms - 1):
        p = r.astype(bf16)
        out.append(p)
        r = r - p.astype(f32)
    out.append(r.astype(bf16))
    return out


def _dot_split_lhs(x, m, terms):
    acc = None
    for p in _split_terms(x, terms):
        d = _dot(p, m)
        acc = d if acc is None else acc + d
    return acc


def _dot_split_rhs(m, x, terms):
    acc = None
    for p in _split_terms(x, terms):
        d = _dot(m, p)
        acc = d if acc is None else acc + d
    return acc


def _sigmoid(x):
    return 0.5 * (1.0 + jnp.tanh(0.5 * x))


def _rmsnorm(x, g):
    return x * lax.rsqrt(jnp.mean(x * x, axis=-1, keepdims=True) + NORM_EPS) * g


def _group_mask_a():
    r = lax.broadcasted_iota(jnp.int32, (W_A, W_A), 0) // (W_A // G_A)
    c = lax.broadcasted_iota(jnp.int32, (W_A, W_A), 1) // (W_A // G_A)
    return jnp.where(r == c, 1.0, 0.0).astype(bf16)


def _group_layernorm_silu(a, gm, g, b):
    inv = 1.0 / (W_A // G_A)
    mu = _dot_split_lhs(a, gm, 2) * inv
    d = a - mu
    var = _dot((d * d).astype(bf16), gm) * inv
    y = d * lax.rsqrt(var + LN_EPS) * g + b
    return y * _sigmoid(y)


def _head_layernorm(h, g):
    mu = jnp.mean(h, axis=-1, keepdims=True)
    d = h - mu
    var = jnp.mean(d * d, axis=-1, keepdims=True)
    return d * lax.rsqrt(var + LN_EPS) * g


def _rglru_gates(xcv, wrg, brg, lam):
    gates = _dot(xcv.astype(bf16), wrg) + brg
    r = _sigmoid(gates[:, :W_C])
    i = _sigmoid(gates[:, W_C:])
    log_a = RG_C * r * jax.nn.log_sigmoid(lam)
    a = jnp.exp(log_a)
    t = jnp.tanh(log_a)
    one_minus_a2 = -2.0 * t / (1.0 - t)
    u = jnp.sqrt(one_minus_a2) * i * xcv
    return a, u


N_MIX_WEIGHTS = 21


def _mix_prompt_body(*refs):
    x_ref, weights = refs[0], refs[1:1 + N_MIX_WEIGHTS]
    rest = refs[1 + N_MIX_WEIGHTS:]
    tiles = [_mix_tile(x_ref.at[g], *weights, *[r.at[g] for r in rest]) for g in range(MIX_G)]
    done = object()
    while tiles:
        tiles = [t for t in tiles if next(t, done) is not done]


def _mix_tile(x_ref, g_ref, win_ref, caw_ref, cab_ref, lag_ref, lab_ref, cbw_ref, cbb_ref,
              wq_ref, wk_ref, wv_ref, wg_ref, bg_ref, gn_ref, skip_ref, ccw_ref, ccb_ref,
              wrg_ref, brg_ref, lam_ref, wout_ref,
              x1_ref, sa_ref, sb_ref, sC_ref, sn_ref, sm_ref, sc_ref, sh_ref,
              ahist, bhist, chist, conv_scr, C_scr, n_scr, m_scr, h_scr, qkv_scr, mix_scr):
    T, L = T_MIX, L_CHUNK
    s = pl.program_id(1)
    last = pl.num_programs(1) - 1

    @pl.when(s == 0)
    def _():
        ahist[0:A_HIST, :] = jnp.zeros((A_HIST, W_A), f32)
        bhist[0:S_HIST, :] = jnp.zeros((S_HIST, W_B), f32)
        chist[0:S_HIST, :] = jnp.zeros((S_HIST, W_C), f32)
        C_scr[...] = jnp.zeros(C_scr.shape, f32)
        n_scr[...] = jnp.zeros(n_scr.shape, f32)
        m_scr[...] = jnp.zeros(m_scr.shape, f32)
        h_scr[...] = jnp.zeros(h_scr.shape, f32)

    yield
    x = x_ref[...]
    hn = _rmsnorm(x, g_ref[...])
    hn_bf = hn.astype(bf16)
    yield
    ua = _dot(hn_bf, win_ref[:, 0:2 * W_A])
    ub = _dot(hn_bf, win_ref[:, 2 * W_A:2 * W_A + 2 * W_B])
    uc = _dot(hn_bf, win_ref[:, 2 * W_A + 2 * W_B:])
    xa, ga = ua[:, 0:W_A], ua[:, W_A:]
    xb, zb = ub[:, 0:W_B], ub[:, W_B:]
    xc, gc = uc[:, 0:W_C], uc[:, W_C:]
    yield

    bhist[pl.ds(S_HIST, T), :] = xb
    cbw = cbw_ref[...]
    cb = cbb_ref[...] + cbw[3:4, :] * xb
    for j in range(CONV_B - 1):
        cb = cb + cbw[j:j + 1, :] * bhist[pl.ds(S_HIST - (CONV_B - 1) + j, T), :]
    cb = cb * _sigmoid(cb)
    cb_bf = cb.astype(bf16)
    xb_bf = xb.astype(bf16)
    yield
    qs, ks, vs = [], [], []
    for h in range(H_B):
        hs = slice(h * DH_B, (h + 1) * DH_B)
        q = _dot(cb_bf[:, hs], wq_ref[h])
        k = _dot(cb_bf[:, hs], wk_ref[h]) * (DH_B ** -0.5)
        v = _dot(xb_bf[:, hs], wv_ref[h])
        qs.append(q)
        ks.append(k)
        vs.append(v)
        qkv_scr[:, (3 * h) * DH_B:(3 * h + 1) * DH_B] = q.astype(bf16)
        qkv_scr[:, (3 * h + 1) * DH_B:(3 * h + 2) * DH_B] = k.astype(bf16)
        qkv_scr[:, (3 * h + 2) * DH_B:(3 * h + 3) * DH_B] = v.astype(bf16)
    yield
    gates = _dot(qkv_scr[...], wg_ref[...]) + bg_ref[...]
    ig = gates[:, 0:LANES]
    lf = jax.nn.log_sigmoid(gates[:, LANES:2 * LANES])
    yield
    ri = lax.broadcasted_iota(jnp.int32, (L, L), 0)
    ci = lax.broadcasted_iota(jnp.int32, (L, L), 1)
    causal = ri >= ci
    tri = jnp.where(causal, 1.0, 0.0).astype(bf16)
    gn = gn_ref[...]
    skip = skip_ref[...]
    heads = range(H_B)
    stack = lambda parts: jnp.concatenate(parts, axis=0)
    gn_st = stack([jnp.broadcast_to(gn[:, h * DH_B:(h + 1) * DH_B], (L, DH_B)) for h in heads])
    ones_blk = jnp.ones((L, DH_B), bf16)
    C_old = [C_scr[h] for h in heads]
    n_old = [n_scr[h:h + 1, :] for h in heads]
    m_old = [m_scr[h:h + 1, 0:1] for h in heads]
    for c in range(T // L):
        rows = slice(c * L, (c + 1) * L)
        b_all = _dot_split_rhs(tri, lf[rows], 2)
        c_all = ig[rows] - b_all
        c_all_t = c_all.T
        yield
        q_bf = [qs[h][rows].astype(bf16) for h in heads]
        k_bf = [ks[h][rows].astype(bf16) for h in heads]
        v_bf = [vs[h][rows].astype(bf16) for h in heads]
        m_prev = stack([jnp.broadcast_to(m_old[h], (L, 1)) for h in heads])
        b_col = stack([b_all[:, h:h + 1] for h in heads])
        c_col = stack([c_all[:, h:h + 1] for h in heads])
        dm = stack([jnp.where(causal, c_all_t[h:h + 1, :], -jnp.inf) for h in heads])
        mx = jnp.maximum(m_prev, jnp.max(dm, axis=1, keepdims=True))
        yield
        sc = stack([_dot_nt(q_bf[h], k_bf[h]) for h in heads]) * jnp.exp(dm - mx)
        sc_bf = sc.astype(bf16)
        yield
        w_inter = jnp.exp(m_prev - mx)
        sv_aug = [_dot(sc_bf[h * L:(h + 1) * L], jnp.concatenate([v_bf[h], ones_blk], axis=1)) for h in heads]
        sv = stack([a[:, 0:DH_B] for a in sv_aug])
        s_sum = stack([a[:, DH_B:DH_B + 1] for a in sv_aug])
        qC = stack([_dot(q_bf[h], C_old[h].astype(bf16)) for h in heads])
        qn = stack([_dot_nt(q_bf[h], jnp.broadcast_to(n_old[h], (SUBLANES, DH_B)).astype(bf16))[:, 0:1]
                    for h in heads])
        yield
        num = sv + w_inter * qC
        den = s_sum + w_inter * qn
        hb = num / jnp.maximum(jnp.abs(den), jnp.exp(-(b_col + mx)))
        hb = _head_layernorm(hb, gn_st)
        yield
        mx_last = [mx[(h + 1) * L - 1:(h + 1) * L, :] for h in heads]
        g_in = jnp.exp(c_col - stack([jnp.broadcast_to(mx_last[h], (L, 1)) for h in heads]))
        C_new, n_new, m_new = [], [], []
        for h in heads:
            hs = slice(h * DH_B, (h + 1) * DH_B)
            hr = slice(h * L, (h + 1) * L)
            b_out = _sigmoid(zb[rows, hs]) * (hb[hr] + skip[:, hs] * cb[rows, hs])
            mix_scr[rows, hs] = b_out.astype(bf16)
            gk = g_in[hr] * ks[h][rows]
            decay = jnp.exp(m_old[h] - mx_last[h])
            C_new.append(decay * C_old[h] + _dot_tn(gk.astype(bf16), v_bf[h]))
            n_new.append(decay * n_old[h] + jnp.sum(gk, axis=0, keepdims=True))
            m_new.append(b_all[L - 1:L, h:h + 1] + mx_last[h])
        C_old, n_old, m_old = C_new, n_new, m_new
        yield
    for h in heads:
        C_scr[h] = C_old[h]
        n_scr[h:h + 1, :] = n_old[h]
        m_scr[h:h + 1, :] = jnp.broadcast_to(m_old[h], (1, LANES))

    ahist[pl.ds(A_HIST, T), :] = xa * _sigmoid(ga)
    yield
    caw = caw_ref[...]
    cab = cab_ref[...]
    first = A_HIST - (CONV_A - 1)

    for base in range(0, T, CONV_ROWS):
        for lo in range(0, W_A, LANES):
            blk = ahist[pl.ds(base, CONV_ROWS + A_HIST), lo:lo + LANES]
            acc = jnp.broadcast_to(cab[:, lo:lo + LANES], (CONV_ROWS, LANES))
            for rr in range(SUBLANES):
                n = CONV_ROWS if rr == 0 else CONV_ROWS + SUBLANES
                z = None
                for q in range(A_HIST // SUBLANES + 1):
                    j = SUBLANES * q + rr - first
                    if 0 <= j < CONV_A:
                        term = caw[j:j + 1, lo:lo + LANES] * blk[SUBLANES * q:SUBLANES * q + n, :]
                        z = term if z is None else z + term
                acc = acc + (z if rr == 0 else z[rr:rr + CONV_ROWS, :])
            conv_scr[pl.ds(base, CONV_ROWS), lo:lo + LANES] = acc
            yield
    a_out = _group_layernorm_silu(conv_scr[...], _group_mask_a(), lag_ref[...], lab_ref[...])
    y = _dot(a_out.astype(bf16), wout_ref[0:W_A, :])
    yield

    chist[pl.ds(S_HIST, T), :] = xc
    ccw = ccw_ref[...]
    xcv = ccb_ref[...] + ccw[3:4, :] * xc
    for j in range(CONV_C - 1):
        xcv = xcv + ccw[j:j + 1, :] * chist[pl.ds(S_HIST - (CONV_C - 1) + j, T), :]
    yield
    a_t, u_t = _rglru_gates(xcv, wrg_ref[...], brg_ref[...], lam_ref[...])
    yield
    row = lax.broadcasted_iota(jnp.int32, (T, W_C), 0)
    d = 1
    while d < T:
        if d < SUBLANES:
            a_sh = jnp.where(row >= d, pltpu.roll(a_t, d, axis=0), 1.0)
            u_sh = jnp.where(row >= d, pltpu.roll(u_t, d, axis=0), 0.0)
        else:
            a_sh = jnp.concatenate([jnp.ones((d, W_C), f32), a_t[:T - d]], axis=0)
            u_sh = jnp.concatenate([jnp.zeros((d, W_C), f32), u_t[:T - d]], axis=0)
        u_t = a_t * u_sh + u_t
        a_t = a_t * a_sh
        d *= 2
    hc = a_t * h_scr[...] + u_t
    h_scr[...] = hc[T - 1:T, :]
    yield
    y = y + _dot((hc * jax.nn.gelu(gc)).astype(bf16), wout_ref[W_A + W_B:, :])

    yield
    x1_ref[...] = x + (y + _dot(mix_scr[...], wout_ref[W_A:W_A + W_B, :]))
    yield

    ahist[0:A_HIST, :] = ahist[pl.ds(T, A_HIST), :]
    bhist[0:S_HIST, :] = bhist[pl.ds(T, S_HIST), :]
    chist[0:S_HIST, :] = chist[pl.ds(T, S_HIST), :]

    @pl.when(s == last)
    def _():
        sa_ref[...] = ahist[pl.ds(first, CONV_A - 1), :]
        sb_ref[...] = bhist[pl.ds(S_HIST - (CONV_B - 1), CONV_B - 1), :]
        sc_ref[...] = chist[pl.ds(S_HIST - (CONV_C - 1), CONV_C - 1), :]
        sC_ref[...] = C_scr[...]
        sn_ref[...] = n_scr[0:H_B, :]
        sm_ref[...] = m_scr[0:H_B, :]
        sh_ref[...] = h_scr[...]


def _full(shape):
    nd = len(shape)
    return pl.BlockSpec(shape, lambda *_: (0,) * nd)


def _layer(w, l):
    nd = w.ndim - 1
    return pl.BlockSpec((None,) + w.shape[1:], lambda *_: (l,) + (0,) * nd)


def _mix_prompt(x, lw, l):
    B, S, _ = x.shape
    T = T_MIX
    weights = [lw[k] for k in ("norm_mix_g", "w_in", "conv_a_w", "conv_a_b", "ln_a_g", "ln_a_b", "conv_b_w", "conv_b_b",
                               "w_q", "w_k", "w_v", "w_gates", "b_gates", "gn_b_g", "skip_b", "conv_c_w", "conv_c_b",
                               "w_rg", "b_rg", "lam", "w_out")]
    out_shape = (
        jax.ShapeDtypeStruct((B, S, D_MODEL), f32),
        jax.ShapeDtypeStruct((B, CONV_A - 1, W_A), f32),
        jax.ShapeDtypeStruct((B, CONV_B - 1, W_B), f32),
        jax.ShapeDtypeStruct((B, H_B, DH_B, DH_B), f32),
        jax.ShapeDtypeStruct((B, H_B, DH_B), f32),
        jax.ShapeDtypeStruct((B, H_B, LANES), f32),
        jax.ShapeDtypeStruct((B, CONV_C - 1, W_C), f32),
        jax.ShapeDtypeStruct((B, 1, W_C), f32),
    )
    G = MIX_G
    per_b = lambda shp: pl.BlockSpec((G,) + shp, lambda b, s: (b,) + (0,) * len(shp))
    out_specs = (
        pl.BlockSpec((G, T, D_MODEL), lambda b, s: (b, s, 0)),
        per_b((CONV_A - 1, W_A)), per_b((CONV_B - 1, W_B)), per_b((H_B, DH_B, DH_B)), per_b((H_B, DH_B)),
        per_b((H_B, LANES)), per_b((CONV_C - 1, W_C)), per_b((1, W_C)),
    )
    scratch = [
        pltpu.VMEM((G, A_HIST + T, W_A), f32), pltpu.VMEM((G, S_HIST + T, W_B), f32),
        pltpu.VMEM((G, S_HIST + T, W_C), f32), pltpu.VMEM((G, T, W_A), f32),
        pltpu.VMEM((G, H_B, DH_B, DH_B), f32), pltpu.VMEM((G, SUBLANES, DH_B), f32),
        pltpu.VMEM((G, SUBLANES, LANES), f32), pltpu.VMEM((G, 1, W_C), f32),
        pltpu.VMEM((G, T, 3 * W_B), bf16), pltpu.VMEM((G, T, W_B), bf16),
    ]
    assert len(weights) == N_MIX_WEIGHTS and B % G == 0
    return pl.pallas_call(
        _mix_prompt_body,
        grid=(B // G, S // T),
        in_specs=[pl.BlockSpec((G, T, D_MODEL), lambda b, s: (b, s, 0))] + [_layer(w, l) for w in weights],
        out_specs=out_specs,
        out_shape=out_shape,
        scratch_shapes=scratch,
        compiler_params=pltpu.CompilerParams(dimension_semantics=("arbitrary", "arbitrary"),
                                             vmem_limit_bytes=VMEM_LIMIT),
        name="mix_prompt",
    )(x, *weights)


def _step_conv(st_ref, w, bias, x_new, st_out):
    k1 = st_ref.shape[0]
    y = bias + w[k1:k1 + 1, :] * x_new
    for j in range(k1):
        y = y + w[j:j + 1, :] * st_ref[j]
        if j > 0:
            st_out[j - 1] = st_ref[j]
    st_out[k1 - 1] = x_new
    return y


def _step_pre_body(x_ref, g_ref, win_ref, sa_ref, caw_ref, cab_ref, lag_ref, lab_ref, sb_ref, cbw_ref, cbb_ref,
                   wq_ref, wk_ref, wv_ref, wg_ref, bg_ref, n_ref, m_ref, sc_ref, ccw_ref, ccb_ref,
                   wrg_ref, brg_ref, lam_ref, h_ref, skip_ref,
                   sa_o, sb_o, sc_o, n_o, m_o, h_o, q_o, gk_o, v_o, dec_o, den_o, sv_o, zsig_o, skcb_o, ac_o,
                   qkv_scr):
    x = x_ref[...]
    hn = _rmsnorm(x, g_ref[...])
    u = _dot(hn.astype(bf16), win_ref[...])
    b0, c0 = 2 * W_A, 2 * W_A + 2 * W_B
    xa, ga = u[:, 0:W_A], u[:, W_A:b0]
    xb, zb = u[:, b0:b0 + W_B], u[:, b0 + W_B:c0]
    xc, gc = u[:, c0:c0 + W_C], u[:, c0 + W_C:]

    a_new = xa * _sigmoid(ga)
    conv = _step_conv(sa_ref, caw_ref[...], cab_ref[...], a_new, sa_o)
    ac_o[:, 0:W_A] = _group_layernorm_silu(conv, _group_mask_a(), lag_ref[...], lab_ref[...])

    cb = _step_conv(sb_ref, cbw_ref[...], cbb_ref[...], xb, sb_o)
    cb = cb * _sigmoid(cb)
    cb_bf = cb.astype(bf16)
    xb_bf = xb.astype(bf16)
    qs, ks, vs = [], [], []
    for h in range(H_B):
        hs = slice(h * DH_B, (h + 1) * DH_B)
        q = _dot(cb_bf[:, hs], wq_ref[h])
        k = _dot(cb_bf[:, hs], wk_ref[h]) * (DH_B ** -0.5)
        v = _dot(xb_bf[:, hs], wv_ref[h])
        qs.append(q)
        ks.append(k)
        vs.append(v)
        qkv_scr[:, (3 * h) * DH_B:(3 * h + 1) * DH_B] = q.astype(bf16)
        qkv_scr[:, (3 * h + 1) * DH_B:(3 * h + 2) * DH_B] = k.astype(bf16)
        qkv_scr[:, (3 * h + 2) * DH_B:(3 * h + 3) * DH_B] = v.astype(bf16)
    gates = _dot(qkv_scr[...], wg_ref[...]) + bg_ref[...]
    ig = gates[:, 0:LANES]
    lf = jax.nn.log_sigmoid(gates[:, LANES:2 * LANES])
    m0 = m_ref[...]
    m_t = jnp.maximum(lf + m0, ig)
    g_in = jnp.exp(ig - m_t)
    decay = jnp.exp(lf + m0 - m_t)
    lane = lax.broadcasted_iota(jnp.int32, ig.shape, 1)
    qk = jnp.zeros(ig.shape, f32)
    qn = jnp.zeros(ig.shape, f32)
    for h in range(H_B):
        hs = slice(h * DH_B, (h + 1) * DH_B)
        n_h = n_ref[:, h, :]
        qk = jnp.where(lane == h, jnp.sum(qs[h] * ks[h], axis=-1, keepdims=True), qk)
        qn = jnp.where(lane == h, jnp.sum(qs[h] * n_h, axis=-1, keepdims=True), qn)
        gk = g_in[:, h:h + 1] * ks[h]
        n_o[:, h, :] = decay[:, h:h + 1] * n_h + gk
        q_o[:, hs] = qs[h]
        gk_o[:, hs] = gk
        v_o[:, hs] = vs[h]
    s_t = qk * g_in
    den = s_t + decay * qn
    den_o[...] = jnp.maximum(jnp.abs(den), jnp.exp(-m_t))
    dec_o[...] = decay
    m_o[...] = m_t
    for h in range(H_B):
        hs = slice(h * DH_B, (h + 1) * DH_B)
        sv_o[:, hs] = s_t[:, h:h + 1] * vs[h]
    zsig_o[...] = _sigmoid(zb)
    skcb_o[...] = skip_ref[...] * cb

    xcv = _step_conv(sc_ref, ccw_ref[...], ccb_ref[...], xc, sc_o)
    a_t, u_t = _rglru_gates(xcv, wrg_ref[...], brg_ref[...], lam_ref[...])
    hc = a_t * h_ref[...] + u_t
    h_o[...] = hc
    ac_o[:, W_A:] = hc * jax.nn.gelu(gc)


def _step_read_body(C_ref, qT_ref, qc_ref):
    for bb in range(STEP_BB):
        for h in range(H_B):
            hs = slice(h * DH_B, (h + 1) * DH_B)
            qc_ref[bb:bb + 1, hs] = jnp.sum(C_ref[bb, h] * qT_ref[h, :, bb:bb + 1], axis=0, keepdims=True)


def _step_state_body(C_ref, qT_ref, gkT_ref, v_ref, dec_ref, Cn_ref, qc_ref):
    for bb in range(STEP_BB):
        for h in range(H_B):
            hs = slice(h * DH_B, (h + 1) * DH_B)
            C = C_ref[bb, h]
            qc_ref[bb:bb + 1, hs] = jnp.sum(C * qT_ref[h, :, bb:bb + 1], axis=0, keepdims=True)
            Cn_ref[bb, h] = dec_ref[bb:bb + 1, h:h + 1] * C + gkT_ref[h, :, bb:bb + 1] * v_ref[bb:bb + 1, hs]


def _to_cols(t):
    return t.reshape(t.shape[0] // STEP_BB, STEP_BB, H_B, DH_B).transpose(0, 2, 3, 1)


def _step_read(C_all, layer, q):
    Bs = q.shape[0]
    return pl.pallas_call(
        _step_read_body,
        grid=(Bs // STEP_BB,),
        in_specs=[pl.BlockSpec((None, STEP_BB, H_B, DH_B, DH_B), lambda i: (layer, i, 0, 0, 0)),
                  pl.BlockSpec((None, H_B, DH_B, STEP_BB), lambda i: (i, 0, 0, 0))],
        out_specs=pl.BlockSpec((STEP_BB, W_B), lambda i: (i, 0)),
        out_shape=jax.ShapeDtypeStruct((Bs, W_B), f32),
        compiler_params=pltpu.CompilerParams(dimension_semantics=("arbitrary",), vmem_limit_bytes=VMEM_LIMIT),
        name="step_read",
    )(C_all, _to_cols(q))


def _step_state(C_all, q, gk, v, dec):
    depth, Bs = q.shape[0], q.shape[1]
    cols = lambda t: jnp.stack([_to_cols(t[l]) for l in range(depth)])
    blk5 = pl.BlockSpec((None, STEP_BB, H_B, DH_B, DH_B), lambda l, i: (l, i, 0, 0, 0))
    colspec = pl.BlockSpec((None, None, H_B, DH_B, STEP_BB), lambda l, i: (l, i, 0, 0, 0))
    return pl.pallas_call(
        _step_state_body,
        grid=(depth, Bs // STEP_BB),
        in_specs=[blk5, colspec, colspec,
                  pl.BlockSpec((None, STEP_BB, W_B), lambda l, i: (l, i, 0)),
                  pl.BlockSpec((None, STEP_BB, LANES), lambda l, i: (l, i, 0))],
        out_specs=(blk5, pl.BlockSpec((None, STEP_BB, W_B), lambda l, i: (l, i, 0))),
        out_shape=(jax.ShapeDtypeStruct(C_all.shape, f32), jax.ShapeDtypeStruct((depth, Bs, W_B), f32)),
        compiler_params=pltpu.CompilerParams(dimension_semantics=("arbitrary", "arbitrary"),
                                             vmem_limit_bytes=VMEM_LIMIT),
        name="step_state",
    )(C_all, cols(q), cols(gk), v, dec)


def _step_post_body(x_ref, ac_ref, zsig_ref, skcb_ref, sv_ref, qc_ref, dec_ref, den_ref, gn_ref, wout_ref,
                    x1_ref, mix_scr):
    gn = gn_ref[...]
    dec = dec_ref[...]
    den = den_ref[...]
    mix_scr[:, 0:W_A] = ac_ref[:, 0:W_A].astype(bf16)
    mix_scr[:, W_A + W_B:] = ac_ref[:, W_A:].astype(bf16)
    for h in range(H_B):
        hs = slice(h * DH_B, (h + 1) * DH_B)
        num = sv_ref[:, hs] + dec[:, h:h + 1] * qc_ref[:, hs]
        hb = _head_layernorm(num / den[:, h:h + 1], gn[:, hs])
        mix_scr[:, W_A + h * DH_B:W_A + (h + 1) * DH_B] = (zsig_ref[:, hs] * (hb + skcb_ref[:, hs])).astype(bf16)
    x1_ref[...] = x_ref[...] + _dot(mix_scr[...], wout_ref[...])


def _step_pre(x, st, lw, l):
    buf_a, buf_b, n0, m0, buf_c, hc0 = st
    buf_a, buf_b, buf_c = (b.transpose(0, 2, 1, 3) for b in (buf_a, buf_b, buf_c))
    Bs = x.shape[0]
    m_in = jnp.pad(m0, ((0, 0), (0, 0), (0, LANES - H_B)))
    sds = lambda *shape: jax.ShapeDtypeStruct(shape, f32)
    W = lambda k: (lw[k], True)
    L = lambda a: (a, True)
    A = lambda a: (a, False)
    pre_in = [A(x), W("norm_mix_g"), W("w_in"), L(buf_a), W("conv_a_w"), W("conv_a_b"), W("ln_a_g"), W("ln_a_b"),
              L(buf_b), W("conv_b_w"), W("conv_b_b"), W("w_q"), W("w_k"), W("w_v"), W("w_gates"), W("b_gates"),
              L(n0), L(m_in), L(buf_c), W("conv_c_w"), W("conv_c_b"), W("w_rg"), W("b_rg"), W("lam"), L(hc0),
              W("skip_b")]
    out_shape = (sds(*buf_a.shape[1:]), sds(*buf_b.shape[1:]), sds(*buf_c.shape[1:]), sds(*n0.shape[1:]),
                 sds(Bs, LANES), sds(Bs, W_C),
                 sds(Bs, W_B), sds(Bs, W_B), sds(Bs, W_B), sds(Bs, LANES), sds(Bs, LANES), sds(Bs, W_B),
                 sds(Bs, W_B), sds(Bs, W_B), sds(Bs, W_A + W_C))
    (sa_n, sb_n, sc_n, n_n, m_n, h_n, q, gk, v, dec, den, sv, zsig, skcb, ac) = pl.pallas_call(
        _step_pre_body,
        grid=(1,),
        in_specs=[_layer(a, l) if is_w else _full(a.shape) for a, is_w in pre_in],
        out_specs=[_full(o.shape) for o in out_shape],
        out_shape=out_shape,
        scratch_shapes=[pltpu.VMEM((Bs, 3 * W_B), bf16)],
        compiler_params=pltpu.CompilerParams(dimension_semantics=("arbitrary",), vmem_limit_bytes=VMEM_LIMIT),
        name="step_pre",
    )(*[a for a, _ in pre_in])
    new = (sa_n, sb_n, n_n, m_n[:, :H_B], sc_n, h_n)
    return new, (q, gk, v, dec), (ac, zsig, skcb, sv, dec, den)


def _step_post(x, post_in, qc, lw, l):
    ac, zsig, skcb, sv, dec, den = post_in
    acts = [x, ac, zsig, skcb, sv, qc, dec, den]
    return pl.pallas_call(
        _step_post_body,
        grid=(1,),
        in_specs=[_full(a.shape) for a in acts] + [_layer(lw["gn_b_g"], l), _layer(lw["w_out"], l)],
        out_specs=_full(x.shape),
        out_shape=jax.ShapeDtypeStruct(x.shape, f32),
        scratch_shapes=[pltpu.VMEM(x.shape, bf16)],
        compiler_params=pltpu.CompilerParams(dimension_semantics=("arbitrary",), vmem_limit_bytes=VMEM_LIMIT),
        name="step_post",
    )(*acts, lw["gn_b_g"], lw["w_out"])


def _swiglu(hb, wg_ref, wu_ref, wd_ref):
    f = wd_ref.shape[0]
    main = f // MXU_WIDTH * MXU_WIDTH
    glu = lambda gt, up: (gt * _sigmoid(gt) * up).astype(bf16)
    w = lambda ref, cols: ref[:, cols] if ref.dtype == bf16 else ref[:, cols].astype(bf16)
    if main == f:
        act = glu(_dot(hb, w(wg_ref, slice(None))), _dot(hb, w(wu_ref, slice(None))))
    else:
        lo, hi = slice(0, main), slice(main, f)
        tail = _dot(hb, jnp.concatenate([w(wg_ref, hi), w(wu_ref, hi)], axis=1))
        act = jnp.concatenate([glu(_dot(hb, w(wg_ref, lo)), _dot(hb, w(wu_ref, lo))),
                               glu(tail[:, :f - main], tail[:, f - main:])], axis=1)
    return _dot(act, wd_ref[...])


def _ffn_body(*refs, moe, final_norm):
    it = iter(refs)
    x_ref, g_ref, wg_ref, wu_ref, wd_ref = next(it), next(it), next(it), next(it), next(it)
    wr_ref = next(it) if moe else None
    br_ref = next(it) if moe else None
    gf_ref = next(it) if final_norm else None
    o_ref, hn_scr, acc_scr = next(it), next(it), next(it)
    comb_scr = next(it) if moe else None
    j = pl.program_id(1)

    @pl.when(j == 0)
    def _():
        hn = _rmsnorm(x_ref[...], g_ref[...])
        hn_scr[...] = hn.astype(bf16)
        acc_scr[...] = jnp.zeros(acc_scr.shape, f32)
        if moe:
            lane, i1, i2, g1, g2 = _top2(_router_logits(hn, wr_ref[...], br_ref[...]))
            comb_scr[...] = jnp.where(lane == i1, g1, 0.0) + jnp.where(lane == i2, g2, 0.0)

    y = _swiglu(hn_scr[...], wg_ref, wu_ref, wd_ref)
    if moe:
        lane = lax.broadcasted_iota(jnp.int32, comb_scr.shape, 1)
        y = y * jnp.sum(jnp.where(lane == j, comb_scr[...], 0.0), axis=-1, keepdims=True)
    acc_scr[...] += y

    @pl.when(j == pl.num_programs(1) - 1)
    def _():
        out = x_ref[...] + acc_scr[...]
        if final_norm:
            out = _rmsnorm(out, gf_ref[...])
        o_ref[...] = out


def _ffn(x, g, wg, wu, wd, router=None, final_g=None):
    N = x.shape[0]
    E, F = wd.shape[0], wd.shape[1]
    tm = min(TM_FFN, N)
    moe = router is not None
    final_norm = final_g is not None
    ins = [x, g, wg, wu, wd]
    once = dict(pipeline_mode=pl.Buffered(1)) if E == 1 else {}
    in_specs = [pl.BlockSpec((tm, D_MODEL), lambda i, j: (i, 0)), _full(g.shape),
                pl.BlockSpec((None, D_MODEL, F), lambda i, j: (j, 0, 0), **once),
                pl.BlockSpec((None, D_MODEL, F), lambda i, j: (j, 0, 0), **once),
                pl.BlockSpec((None, F, D_MODEL), lambda i, j: (j, 0, 0), **once)]
    scratch = [pltpu.VMEM((tm, D_MODEL), bf16), pltpu.VMEM((tm, D_MODEL), f32)]
    if moe:
        ins += list(router)
        in_specs += [_full(router[0].shape), _full(router[1].shape)]
        scratch.append(pltpu.VMEM((tm, LANES), f32))
    if final_norm:
        ins.append(final_g)
        in_specs.append(_full(final_g.shape))
    return pl.pallas_call(
        functools.partial(_ffn_body, moe=moe, final_norm=final_norm),
        grid=(N // tm, E),
        in_specs=in_specs,
        out_specs=pl.BlockSpec((tm, D_MODEL), lambda i, j: (i, 0)),
        out_shape=jax.ShapeDtypeStruct((N, D_MODEL), f32),
        scratch_shapes=scratch,
        compiler_params=pltpu.CompilerParams(dimension_semantics=("arbitrary", "arbitrary"),
                                             vmem_limit_bytes=VMEM_LIMIT),
        name="ffn_moe" if moe else "ffn_dense",
    )(*ins)


TM_MOE = 512
SEG_ALIGN = 16
SEG_SIZES = (512, 256, 128, 64, 32, 16)
MOE_ROWS = 2 * TM_MOE + N_EXPERTS * SEG_ALIGN
META_COLS = 3 * N_EXPERTS


def _seg_dma(src, src_off, dst, dst_off, nrows, sem, wait):
    done = jnp.int32(0)
    for size in SEG_SIZES:
        take = (nrows & size) != 0

        @pl.when(take)
        def _(done=done, size=size):
            cp = pltpu.make_async_copy(src.at[pl.ds(pl.multiple_of(src_off + done, SEG_ALIGN), size)],
                                       dst.at[pl.ds(pl.multiple_of(dst_off + done, SEG_ALIGN), size)], sem)
            if wait:
                cp.wait()
            else:
                cp.start()

        done = done + jnp.where(take, size, 0)


def _top2(logits):
    lane = lax.broadcasted_iota(jnp.int32, logits.shape, 1).astype(f32)
    lg = jnp.where(lane < N_EXPERTS, logits, -jnp.inf)
    m1 = jnp.max(lg, axis=-1, keepdims=True)
    i1 = jnp.min(jnp.where(lg == m1, lane, float(LANES)), axis=-1, keepdims=True)
    lg2 = jnp.where(lane == i1, -jnp.inf, lg)
    m2 = jnp.max(lg2, axis=-1, keepdims=True)
    i2 = jnp.min(jnp.where(lg2 == m2, lane, float(LANES)), axis=-1, keepdims=True)
    e2 = jnp.exp(m2 - m1)
    return lane, i1, i2, 1.0 / (1.0 + e2), e2 / (1.0 + e2)


def _router_logits(hn, wr, br):
    h1, h2 = _split_terms(hn, 2)
    w1, w2 = _split_terms(wr, 2)
    return _dot(h1, w1) + (_dot(h1, w2) + _dot(h2, w1)) + br


def _expert_ranks(lane, i1, i2):
    TM = lane.shape[0]
    sel1 = lane == i1
    sel2 = lane == i2
    sel = jnp.where(sel1, 1.0, jnp.where(sel2, 1.0, 0.0))
    earlier = lax.broadcasted_iota(jnp.int32, (TM, TM), 0) > lax.broadcasted_iota(jnp.int32, (TM, TM), 1)
    rank = _dot(jnp.where(earlier, 1.0, 0.0).astype(bf16), sel.astype(bf16))
    return sel1, sel2, rank, rank[TM - 1:TM, :] + sel[TM - 1:TM, :]


def _seg_pad(c):
    return ((c + (SEG_ALIGN - 1)) // SEG_ALIGN) * SEG_ALIGN


COUNT_G = 4


def _moe_count_body(x_ref, g_ref, wr_ref, br_ref, route_ref, cnt_ref):
    tiles = [_moe_count_tile(x_ref.at[pl.ds(t * TM_MOE, TM_MOE)], g_ref, wr_ref, br_ref,
                             route_ref.at[pl.ds(t * TM_MOE, TM_MOE)], cnt_ref, pl.program_id(0) * COUNT_G + t)
             for t in range(COUNT_G)]
    done = object()
    while tiles:
        tiles = [t for t in tiles if next(t, done) is not done]


def _moe_count_tile(x_ref, g_ref, wr_ref, br_ref, route_ref, cnt_ref, i):
    hn = _rmsnorm(x_ref[...], g_ref[...])
    yield
    logits = _router_logits(hn, wr_ref[...], br_ref[...])
    yield
    lane, i1, i2, g1, g2 = _top2(logits)
    yield
    route_ref[...] = jnp.where(lane == 0, i1, jnp.where(lane == 1, i2, jnp.where(lane == 2, g1,
                                                                                 jnp.where(lane == 3, g2, 0.0))))
    cnt = jnp.sum(jnp.where(lane == i1, 1.0, jnp.where(lane == i2, 1.0, 0.0)), axis=0, keepdims=True)
    yield
    for e in range(N_EXPERTS):
        cnt_ref[i, e] = cnt[0, e].astype(jnp.int32)


def _moe_route_body(cnt_ref, x_ref, g_ref, route_ref, xs_hbm, gs_hbm, rinfo_ref, meta_ref, te_ref, nv_ref,
                    xperm, gperm, pos_smem, sem):
    TM, R = TM_MOE, MOE_ROWS
    i = pl.program_id(0)
    nt = pl.num_programs(0)
    n_tiles = te_ref.shape[0]

    @pl.when(i == 0)
    def _():
        base = jnp.int32(0)
        for e in range(N_EXPERTS):
            rows = lax.fori_loop(0, nt, lambda t, a, e=e: a + _seg_pad(cnt_ref[t, e]), jnp.int32(0))
            first_tile = base // TM
            tiles = (rows + (TM - 1)) // TM

            def mark(k, c, e=e, first_tile=first_tile):
                te_ref[first_tile + k] = e
                return c

            lax.fori_loop(0, tiles, mark, 0)
            pos_smem[e] = base
            base = base + tiles * TM
        nv = base // TM
        nv_ref[0] = nv

        def mark_rest(k, c):
            te_ref[k] = N_EXPERTS - 1
            return c

        lax.fori_loop(nv, n_tiles, mark_rest, 0)

    hn = _rmsnorm(x_ref[...], g_ref[...])
    route = route_ref[...]
    i1, i2, g1, g2 = route[:, 0:1], route[:, 1:2], route[:, 2:3], route[:, 3:4]
    lane = lax.broadcasted_iota(jnp.int32, route.shape, 1).astype(f32)
    sel1, sel2, rank, _ = _expert_ranks(lane, i1, i2)
    seg_rows, seg_off = [], []
    acc = jnp.int32(0)
    for e in range(N_EXPERTS):
        seg_rows.append(_seg_pad(cnt_ref[i, e]))
        seg_off.append(acc)
        acc = acc + seg_rows[e]
    lane1 = lax.broadcasted_iota(jnp.int32, (1, LANES), 1)
    offv = jnp.zeros((1, LANES), f32)
    for e in range(N_EXPERTS):
        offv = jnp.where(lane1 == e, seg_off[e].astype(f32), offv)
    dest = rank + offv
    d1 = jnp.sum(jnp.where(sel1, dest, 0.0), axis=-1, keepdims=True)
    d2 = jnp.sum(jnp.where(sel2, dest, 0.0), axis=-1, keepdims=True)
    rinfo = jnp.where(lane == 0, d1, jnp.where(lane == 1, d2, jnp.where(lane == 2, g1, jnp.where(lane == 3, g2, 0.0))))
    rinfo_ref[...] = rinfo
    rinfo_t = rinfo.T
    d1r, d2r, g1r, g2r = rinfo_t[0:1, :], rinfo_t[1:2, :], rinfo_t[2:3, :], rinfo_t[3:4, :]
    riota = lax.broadcasted_iota(jnp.int32, (R, TM), 0).astype(f32)
    hit1 = riota == d1r
    hit2 = riota == d2r
    perm = jnp.where(hit1, 1.0, jnp.where(hit2, 1.0, 0.0)).astype(bf16)
    xperm[i % 2] = _dot(perm, hn.astype(bf16)).astype(bf16)
    gcol = jnp.sum(jnp.where(hit1, g1r, 0.0) + jnp.where(hit2, g2r, 0.0), axis=1, keepdims=True)
    gperm[i % 2] = jnp.broadcast_to(gcol, (R, LANES))

    for e in range(N_EXPERTS):
        pos = pos_smem[e]
        meta_ref[i, e] = pos
        meta_ref[i, N_EXPERTS + e] = seg_off[e]
        meta_ref[i, 2 * N_EXPERTS + e] = seg_rows[e]
        pos_smem[e] = pos + seg_rows[e]

    def copies(tile, wait):
        for slot in range(2):
            @pl.when(tile % 2 == slot)
            def _(slot=slot):
                for e in range(N_EXPERTS):
                    src_off, dst_off, n = meta_ref[tile, N_EXPERTS + e], meta_ref[tile, e], meta_ref[tile, 2 * N_EXPERTS + e]
                    _seg_dma(xperm.at[slot], src_off, xs_hbm, dst_off, n, sem.at[2 * slot], wait)
                    _seg_dma(gperm.at[slot], src_off, gs_hbm, dst_off, n, sem.at[2 * slot + 1], wait)

    copies(i, False)
    pl.when(i > 0)(lambda: copies(i - 1, True))

    @pl.when(i == nt - 1)
    def _():
        copies(i, True)
        xz, gz = xperm.at[0], gperm.at[0]
        xz[0:TM, :] = jnp.zeros((TM, D_MODEL), bf16)
        gz[0:TM, :] = jnp.zeros((TM, LANES), f32)
        pads = []
        for e in range(N_EXPERTS):
            pos = pos_smem[e]
            pads.append((-pos) & (TM - 1))
            _seg_dma(xz, 0, xs_hbm, pos, pads[e], sem.at[0], False)
            _seg_dma(gz, 0, gs_hbm, pos, pads[e], sem.at[1], False)
        for e in range(N_EXPERTS):
            _seg_dma(xz, 0, xs_hbm, pos_smem[e], pads[e], sem.at[0], True)
            _seg_dma(gz, 0, gs_hbm, pos_smem[e], pads[e], sem.at[1], True)

        def zero_tile(k, c):
            row = pl.multiple_of(k * TM, TM)
            cx = pltpu.make_async_copy(xz.at[pl.ds(0, TM)], xs_hbm.at[pl.ds(row, TM)], sem.at[0])
            cg = pltpu.make_async_copy(gz.at[pl.ds(0, TM)], gs_hbm.at[pl.ds(row, TM)], sem.at[1])
            cx.start()
            cg.start()
            cx.wait()
            cg.wait()
            return c

        lax.fori_loop(nv_ref[0], n_tiles, zero_tile, 0)


def _moe_ffn_body(te_ref, nv_ref, xs_ref, gs_ref, wg_ref, wu_ref, wd_ref, ys_ref, wg_bf, wu_bf):
    i = pl.program_id(0)

    @pl.when((i == 0) | (te_ref[i] != te_ref[jnp.maximum(i - 1, 0)]))
    def _():
        wg_bf[...] = wg_ref[...].astype(bf16)
        wu_bf[...] = wu_ref[...].astype(bf16)

    @pl.when(i < nv_ref[0])
    def _():
        ys_ref[...] = (_swiglu(xs_ref[...], wg_bf, wu_bf, wd_ref) * gs_ref[:, 0:1]).astype(bf16)

    @pl.when(i >= nv_ref[0])
    def _():
        ys_ref[...] = jnp.zeros(ys_ref.shape, bf16)


def _moe_combine_body(*refs, final_norm):
    it = iter(refs)
    meta_ref, x_ref, rinfo_ref, ys_hbm = next(it), next(it), next(it), next(it)
    gf_ref = next(it) if final_norm else None
    o_ref, yperm, sem = next(it), next(it), next(it)
    TM, R = TM_MOE, MOE_ROWS
    i = pl.program_id(0)
    nt = pl.num_programs(0)

    def fetch(tile, slot, wait):
        for e in range(N_EXPERTS):
            _seg_dma(ys_hbm, meta_ref[tile, e], yperm.at[slot], meta_ref[tile, N_EXPERTS + e],
                     meta_ref[tile, 2 * N_EXPERTS + e], sem.at[slot], wait)

    def both_slots(tile, wait):
        for slot in range(2):
            pl.when(tile % 2 == slot)(lambda slot=slot: fetch(tile, slot, wait))

    @pl.when(i == 0)
    def _():
        yperm[...] = jnp.zeros(yperm.shape, bf16)
        fetch(0, 0, False)

    @pl.when(i + 1 < nt)
    def _():
        both_slots(i + 1, False)

    rinfo = rinfo_ref[...]
    ciota = lax.broadcasted_iota(jnp.int32, (TM, R), 1).astype(f32)
    unperm = jnp.where(ciota == rinfo[:, 0:1], 1.0, jnp.where(ciota == rinfo[:, 1:2], 1.0, 0.0)).astype(bf16)
    both_slots(i, True)
    out = x_ref[...] + _dot(unperm, yperm[i % 2])
    if final_norm:
        out = _rmsnorm(out, gf_ref[...])
    o_ref[...] = out


def _moe_prompt(x, g, wg, wu, wd, wr, br, final_g):
    N = x.shape[0]
    TM = TM_MOE
    nt = N // TM
    F = wd.shape[1]
    n_tiles = -(-(2 * N + nt * N_EXPERTS * (SEG_ALIGN - 1) + N_EXPERTS * (TM - 1)) // TM)
    rows = n_tiles * TM
    smem = pl.BlockSpec(memory_space=pltpu.SMEM)
    arb = pltpu.CompilerParams(dimension_semantics=("arbitrary",), vmem_limit_bytes=VMEM_LIMIT)
    route, cnt = pl.pallas_call(
        _moe_count_body,
        grid=(nt // COUNT_G,),
        in_specs=[pl.BlockSpec((COUNT_G * TM, D_MODEL), lambda i: (i, 0)), _full(g.shape), _full(wr.shape),
                  _full(br.shape)],
        out_specs=[pl.BlockSpec((COUNT_G * TM, LANES), lambda i: (i, 0)), smem],
        out_shape=[jax.ShapeDtypeStruct((N, LANES), f32), jax.ShapeDtypeStruct((nt, N_EXPERTS), jnp.int32)],
        compiler_params=arb,
        name="moe_count",
    )(x, g, wr, br)

    xs, gs, rinfo, meta, te, nv = pl.pallas_call(
        _moe_route_body,
        grid_spec=pltpu.PrefetchScalarGridSpec(
            num_scalar_prefetch=1,
            grid=(nt,),
            in_specs=[pl.BlockSpec((TM, D_MODEL), lambda i, c: (i, 0)), pl.BlockSpec(g.shape, lambda i, c: (0, 0)),
                      pl.BlockSpec((TM, LANES), lambda i, c: (i, 0))],
            out_specs=[pl.BlockSpec(memory_space=pl.ANY), pl.BlockSpec(memory_space=pl.ANY),
                       pl.BlockSpec((TM, LANES), lambda i, c: (i, 0)), smem, smem, smem],
            scratch_shapes=[pltpu.VMEM((2, MOE_ROWS, D_MODEL), bf16), pltpu.VMEM((2, MOE_ROWS, LANES), f32),
                            pltpu.SMEM((N_EXPERTS,), jnp.int32), pltpu.SemaphoreType.DMA((4,))],
        ),
        out_shape=[jax.ShapeDtypeStruct((rows, D_MODEL), bf16), jax.ShapeDtypeStruct((rows, LANES), f32),
                   jax.ShapeDtypeStruct((N, LANES), f32), jax.ShapeDtypeStruct((nt, META_COLS), jnp.int32),
                   jax.ShapeDtypeStruct((n_tiles,), jnp.int32), jax.ShapeDtypeStruct((1,), jnp.int32)],
        compiler_params=arb,
        name="moe_route",
    )(cnt, x, g, route)

    ys = pl.pallas_call(
        _moe_ffn_body,
        grid_spec=pltpu.PrefetchScalarGridSpec(
            num_scalar_prefetch=2,
            grid=(n_tiles,),
            in_specs=[pl.BlockSpec((TM, D_MODEL), lambda i, te, nv: (i, 0)),
                      pl.BlockSpec((TM, LANES), lambda i, te, nv: (i, 0)),
                      pl.BlockSpec((None, D_MODEL, F), lambda i, te, nv: (te[i], 0, 0)),
                      pl.BlockSpec((None, D_MODEL, F), lambda i, te, nv: (te[i], 0, 0)),
                      pl.BlockSpec((None, F, D_MODEL), lambda i, te, nv: (te[i], 0, 0))],
            out_specs=pl.BlockSpec((TM, D_MODEL), lambda i, te, nv: (i, 0)),
            scratch_shapes=[pltpu.VMEM((D_MODEL, F), bf16), pltpu.VMEM((D_MODEL, F), bf16)],
        ),
        out_shape=jax.ShapeDtypeStruct((rows, D_MODEL), bf16),
        compiler_params=arb,
        name="moe_ffn",
    )(te, nv, xs, gs, wg, wu, wd)

    final_norm = final_g is not None
    ins = [meta, x, rinfo, ys] + ([final_g] if final_norm else [])
    in_specs = [pl.BlockSpec((TM, D_MODEL), lambda i, m: (i, 0)), pl.BlockSpec((TM, LANES), lambda i, m: (i, 0)),
                pl.BlockSpec(memory_space=pl.ANY)]
    if final_norm:
        in_specs.append(pl.BlockSpec(final_g.shape, lambda i, m: (0, 0)))
    return pl.pallas_call(
        functools.partial(_moe_combine_body, final_norm=final_norm),
        grid_spec=pltpu.PrefetchScalarGridSpec(
            num_scalar_prefetch=1,
            grid=(nt,),
            in_specs=in_specs,
            out_specs=pl.BlockSpec((TM, D_MODEL), lambda i, m: (i, 0)),
            scratch_shapes=[pltpu.VMEM((2, MOE_ROWS, D_MODEL), bf16), pltpu.SemaphoreType.DMA((2,))],
        ),
        out_shape=jax.ShapeDtypeStruct((N, D_MODEL), f32),
        compiler_params=pltpu.CompilerParams(dimension_semantics=("arbitrary",), vmem_limit_bytes=VMEM_LIMIT),
        name="moe_combine",
    )(*ins)


def _block_diag(w):
    d, h, di, do = w.shape
    return jnp.einsum("lhij,hg->lhigj", w, jnp.eye(h, dtype=w.dtype)).reshape(d, h * di, h * do)


def _mixer_weights(p):
    row = lambda v: v[:, None, :]
    lanes = lambda w: jnp.pad(w, [(0, 0)] * (w.ndim - 1) + [(0, LANES - w.shape[-1])])
    return dict(
        norm_mix_g=row(p["norm_mix_g"]), w_in=p["w_in"].astype(bf16),
        conv_a_w=p["conv_a_w"], conv_a_b=row(p["conv_a_b"]), ln_a_g=row(p["ln_a_g"]), ln_a_b=row(p["ln_a_b"]),
        conv_b_w=p["conv_b_w"], conv_b_b=row(p["conv_b_b"]),
        w_q=p["w_q"].astype(bf16), w_k=p["w_k"].astype(bf16), w_v=p["w_v"].astype(bf16),
        w_gates=jnp.concatenate([lanes(p["w_ig"]), lanes(p["w_fg"])], axis=-1).astype(bf16),
        b_gates=row(jnp.concatenate([lanes(p["b_ig"]), lanes(p["b_fg"])], axis=-1)),
        gn_b_g=row(p["gn_b_g"]), skip_b=row(p["skip_b"]),
        conv_c_w=p["conv_c_w"], conv_c_b=row(p["conv_c_b"]),
        w_rg=jnp.concatenate([_block_diag(p["w_ra"]), _block_diag(p["w_ix"])], axis=-1).astype(bf16),
        b_rg=row(jnp.concatenate([p["b_ra"], p["b_ix"]], axis=-1)), lam=row(p["lam"]),
        w_out=p["w_out"].astype(bf16),
    )


def kernel(x_prompt, x_sample, state_conv_a, state_conv_b, state_mlstm_C, state_mlstm_n, state_mlstm_m, state_conv_c, state_rglru_h, norm_mix_g, w_in, conv_a_w, conv_a_b, ln_a_g, ln_a_b, conv_b_w, conv_b_b, w_q, w_k, w_v, w_ig, b_ig, w_fg, b_fg, gn_b_g, skip_b, conv_c_w, conv_c_b, w_ra, b_ra, w_ix, b_ix, lam, w_out, norm_ffn_g, w_gate, w_up, w_down, w_router, b_router, we_gate, we_up, we_down, norm_final_g):
    p = dict(norm_mix_g=norm_mix_g, w_in=w_in, conv_a_w=conv_a_w, conv_a_b=conv_a_b, ln_a_g=ln_a_g, ln_a_b=ln_a_b,
             conv_b_w=conv_b_w, conv_b_b=conv_b_b, w_q=w_q, w_k=w_k, w_v=w_v, w_ig=w_ig, b_ig=b_ig, w_fg=w_fg,
             b_fg=b_fg, gn_b_g=gn_b_g, skip_b=skip_b, conv_c_w=conv_c_w, conv_c_b=conv_c_b, w_ra=w_ra, b_ra=b_ra,
             w_ix=w_ix, b_ix=b_ix, lam=lam, w_out=w_out)
    depth = w_in.shape[0]
    Bp, S, _ = x_prompt.shape
    Bs = x_sample.shape[0]
    sample_states = (state_conv_a, state_conv_b, state_mlstm_C, state_mlstm_n, state_mlstm_m, state_conv_c,
                     state_rglru_h)
    xp = x_prompt
    xs = x_sample.reshape(Bs, D_MODEL)
    new_p, new_s, mem_ops = [], [], []
    C_s = None
    lw = _mixer_weights(p)
    for l in range(depth):
        xp, sa, sb, sC, sn, sm, sc, sh = _mix_prompt(xp, lw, l)
        new_p.append((sa, sb, sC, sn, sm[:, :, 0], sc, sh.reshape(Bp, W_C)))
        small, mem, post_in = _step_pre(xs, tuple(s for i, s in enumerate(sample_states) if i != 2), lw, l)
        new_s.append(small)
        mem_ops.append(mem)
        if l < depth - 1:
            qc = _step_read(state_mlstm_C, l, mem[0])
        else:
            C_s, qc_all = _step_state(state_mlstm_C, *(jnp.stack([m[i] for m in mem_ops]) for i in range(4)))
            qc = qc_all[l]
        xs = _step_post(xs, post_in, qc, lw, l)
        g = norm_ffn_g[l].reshape(1, -1)
        final_g = norm_final_g.reshape(1, -1) if l == depth - 1 else None
        j = l // 2
        if l % 2 == 0:
            wg, wu, wd = (w[j:j + 1].astype(bf16) for w in (w_gate, w_up, w_down))
            router = None
        else:
            wg, wu, wd = we_gate[j], we_up[j], we_down[j].astype(bf16)
            router = (jnp.pad(w_router[j], ((0, 0), (0, LANES - N_EXPERTS))),
                      jnp.pad(b_router[j], (0, LANES - N_EXPERTS)).reshape(1, -1))
        xp2 = xp.reshape(Bp * S, D_MODEL)
        if router is None:
            xp2 = _ffn(xp2, g, wg, wu, wd, None, final_g)
        else:
            xp2 = _moe_prompt(xp2, g, wg, wu, wd, router[0], router[1], final_g)
        xp = xp2.reshape(Bp, S, D_MODEL)
        xs = _ffn(xs, g, wg, wu, wd, router, final_g)
    stack = lambda states, i: jnp.stack([st[i] for st in states])
    out_p = tuple(stack(new_p, i) for i in range(7))
    sa_s, sb_s, n_s, m_s, sc_s, h_s = (stack(new_s, i) for i in range(6))
    sa_s, sb_s, sc_s = (s.transpose(0, 2, 1, 3) for s in (sa_s, sb_s, sc_s))
    return (xp, xs.reshape(Bs, 1, D_MODEL)) + out_p + (sa_s, sb_s, C_s, n_s, m_s, sc_s, h_s)
```
